```python
import math
import jax, jax.numpy as jnp
from jax import lax
import numpy as np

D_MODEL = 1024
BATCH = 2
SEQ = 8192
DEPTH = 1

N_DIFF_HEADS = 8
DIFF_HEAD_DIM = 64
DIFF_V_DIM = 2 * DIFF_HEAD_DIM
ATTN_WIDTH = N_DIFF_HEADS * 2 * DIFF_HEAD_DIM
CONV_WIDTH = 1024
CONV_K = 3
SPLIT_SIZES = (ATTN_WIDTH, ATTN_WIDTH, ATTN_WIDTH, CONV_WIDTH, CONV_WIDTH, CONV_WIDTH, D_MODEL, D_MODEL)
SPLIT_POINTS = tuple(int(v) for v in np.cumsum(SPLIT_SIZES)[:-1])
IN_WIDTH = int(sum(SPLIT_SIZES))
N_EXPERTS = 32
TOP_K = 4
D_FF = 1024
SWIGLU_LIMIT = 7.0
SWIGLU_ALPHA = 1.702
ROPE_THETA = 10000.0
RMS_EPS = 1e-6
SUBLN_EPS = 1e-5
Q_BLOCK = 128
MOE_BLOCK = 128

kernel_name = "hybrid_diffattn_shortconv_moe_encoder_block"


def rmsnorm(x, g, eps=RMS_EPS):
    xf = x.astype(jnp.float32)
    y = xf * lax.rsqrt(jnp.mean(xf * xf, axis=-1, keepdims=True) + eps)
    return (y * g.astype(jnp.float32)).astype(x.dtype)


def rope(t, seq_len):
    dh = t.shape[-1]
    inv_freq = ROPE_THETA ** (-jnp.arange(0, dh, 2, dtype=jnp.float32) / dh)
    pos = jnp.arange(seq_len, dtype=jnp.float32)
    ang = pos[:, None] * inv_freq[None, :]
    emb = jnp.concatenate([ang, ang], axis=-1)[None, :, None, None, :]
    cos, sin = jnp.cos(emb), jnp.sin(emb)
    tf = t.astype(jnp.float32)
    t1, t2 = jnp.split(tf, 2, axis=-1)
    rot = jnp.concatenate([-t2, t1], axis=-1)
    return (tf * cos + rot * sin).astype(t.dtype)


def diff_attention(q, k, v, lam):
    b, s, h, _, dh = q.shape
    nq = s // Q_BLOCK
    qb = q.reshape(b, nq, Q_BLOCK, h, 2, dh).transpose(1, 0, 2, 3, 4, 5)
    scale = 1.0 / math.sqrt(dh)

    def one_block(q_blk):
        sc = jnp.einsum('bqhcd,bkhcd->bhcqk', q_blk, k).astype(jnp.float32) * scale
        p = jax.nn.softmax(sc, axis=-1)
        a = p[:, :, 0] - lam * p[:, :, 1]
        return jnp.einsum('bhqk,bkhd->bqhd', a.astype(v.dtype), v)

    o = lax.map(one_block, qb)
    return o.transpose(1, 0, 2, 3, 4).reshape(b, s, h, v.shape[-1])


def short_conv(z, w):
    return lax.conv_general_dilated(
        z, w[:, None, :].astype(z.dtype), window_strides=(1,),
        padding=[(CONV_K // 2, CONV_K // 2)],
        dimension_numbers=('NWC', 'WIO', 'NWC'),
        feature_group_count=z.shape[-1])


def moe(h, w_router, b_router, w_gate_up, b_gate_up, w_down, b_down):
    n, d = h.shape
    logits = h.astype(jnp.float32) @ w_router.astype(jnp.float32) + b_router.astype(jnp.float32)
    top_vals, top_idx = lax.top_k(logits, TOP_K)
    top_w = jax.nn.softmax(top_vals, axis=-1)
    nk = n * TOP_K
    flat_e = top_idx.reshape(-1)
    flat_tok = jnp.broadcast_to(jnp.arange(n, dtype=jnp.int32)[:, None], (n, TOP_K)).reshape(-1)
    flat_w = top_w.reshape(-1)
    order = jnp.argsort(flat_e)
    sorted_e = flat_e[order]
    counts = jnp.bincount(flat_e, length=N_EXPERTS)
    padded = ((counts + MOE_BLOCK - 1) // MOE_BLOCK) * MOE_BLOCK
    ends_pad = jnp.cumsum(padded)
    start_pad = ends_pad - padded
    start = jnp.cumsum(counts) - counts
    dest = start_pad[sorted_e] + (jnp.arange(nk) - start[sorted_e])
    p_rows = nk + N_EXPERTS * MOE_BLOCK
    n_blocks = p_rows // MOE_BLOCK
    row_tok = jnp.full((p_rows,), n, dtype=jnp.int32).at[dest].set(flat_tok[order])
    row_w = jnp.zeros((p_rows,), jnp.float32).at[dest].set(flat_w[order])
    block_e = jnp.clip(jnp.searchsorted(ends_pad, jnp.arange(n_blocks) * MOE_BLOCK, side='right'),
                       0, N_EXPERTS - 1)
    h_pad = jnp.concatenate([h, jnp.zeros((1, d), h.dtype)], axis=0)
    xs = h_pad[row_tok].reshape(n_blocks, MOE_BLOCK, d)

    def expert_block(args):
        xb, e = args
        gu = xb @ w_gate_up[e] + b_gate_up[e]
        gate = jnp.minimum(gu[..., ::2], SWIGLU_LIMIT)
        up = jnp.clip(gu[..., 1::2], -SWIGLU_LIMIT, SWIGLU_LIMIT)
        glu = gate * jax.nn.sigmoid(gate * SWIGLU_ALPHA)
        return ((up + 1.0) * glu) @ w_down[e] + b_down[e]

    ys = lax.map(expert_block, (xs, block_e)).reshape(p_rows, d)
    ys = (ys.astype(jnp.float32) * row_w[:, None]).astype(h.dtype)
    out = jnp.zeros((n + 1, d), h.dtype).at[row_tok].add(ys)
    return out[:n]


def setup_inputs(seed: int = 0) -> dict:
    key = jax.random.key(seed)
    ks = jax.random.split(key, 32)
    f32 = jnp.float32
    L, D, H, dh = DEPTH, D_MODEL, N_DIFF_HEADS, DIFF_HEAD_DIM
    nrm = lambda k, shape, s: jax.random.normal(k, shape, f32) * s
    return {
        "x": nrm(ks[0], (BATCH, SEQ, D), 1.0),
        "c": nrm(ks[1], (BATCH, D), 1.0),
        "w_ada": nrm(ks[2], (L, D, 6 * D), 0.5 * D ** -0.5),
        "b_ada": nrm(ks[3], (L, 6 * D), 0.01),
        "norm1_w": 1.0 + nrm(ks[4], (L, D), 0.02),
        "w_in": nrm(ks[5], (L, D, IN_WIDTH), D ** -0.5),
        "q_norm_w": 1.0 + nrm(ks[6], (L, dh), 0.02),
        "k_norm_w": 1.0 + nrm(ks[7], (L, dh), 0.02),
        "lambda_q1": nrm(ks[8], (L, dh), 0.1),
        "lambda_k1": nrm(ks[9], (L, dh), 0.1),
        "lambda_q2": nrm(ks[10], (L, dh), 0.1),
        "lambda_k2": nrm(ks[11], (L, dh), 0.1),
        "subln_w": 1.0 + nrm(ks[12], (L, DIFF_V_DIM), 0.02),
        "w_attn_o": nrm(ks[13], (L, H * DIFF_V_DIM, D), (H * DIFF_V_DIM) ** -0.5),
        "conv_w": nrm(ks[14], (L, CONV_K, CONV_WIDTH), CONV_K ** -0.5),
        "w_conv_o": nrm(ks[15], (L, CONV_WIDTH, D), CONV_WIDTH ** -0.5),
        "w_out": nrm(ks[16], (L, D, D), D ** -0.5),
        "norm2_w": 1.0 + nrm(ks[17], (L, D), 0.02),
        "w_router": nrm(ks[18], (L, D, N_EXPERTS), D ** -0.5),
        "b_router": nrm(ks[19], (L, N_EXPERTS), 0.01),
        "w_gate_up": nrm(ks[20], (L, N_EXPERTS, D, 2 * D_FF), D ** -0.5),
        "b_gate_up": nrm(ks[21], (L, N_EXPERTS, 2 * D_FF), 0.01),
        "w_down": nrm(ks[22], (L, N_EXPERTS, D_FF, D), D_FF ** -0.5),
        "b_down": nrm(ks[23], (L, N_EXPERTS, D), 0.01),
    }


def reference(x, c, w_ada, b_ada, norm1_w, w_in, q_norm_w, k_norm_w, lambda_q1, lambda_k1,
              lambda_q2, lambda_k2, subln_w, w_attn_o, conv_w, w_conv_o, w_out, norm2_w,
              w_router, b_router, w_gate_up, b_gate_up, w_down, b_down):
    b, s, d = x.shape
    H, dh = N_DIFF_HEADS, DIFF_HEAD_DIM
    for l in range(DEPTH):
        lambda_init = 0.8 - 0.6 * math.exp(-0.3 * l)
        mod = jax.nn.silu(c) @ w_ada[l] + b_ada[l]
        sh1, sc1, g1, sh2, sc2, g2 = jnp.split(mod[:, None, :], 6, axis=-1)

        h = rmsnorm(x, norm1_w[l]) * (1.0 + sc1) + sh1
        proj = h @ w_in[l]
        q, k, v, cb, cc, cx, ga, gc = jnp.split(proj, SPLIT_POINTS, axis=-1)

        q = rope(rmsnorm(q.reshape(b, s, H, 2, dh), q_norm_w[l]), s)
        k = rope(rmsnorm(k.reshape(b, s, H, 2, dh), k_norm_w[l]), s)
        v = v.reshape(b, s, H, DIFF_V_DIM)
        lam = (jnp.exp(jnp.sum(lambda_q1[l].astype(jnp.float32) * lambda_k1[l].astype(jnp.float32)))
               - jnp.exp(jnp.sum(lambda_q2[l].astype(jnp.float32) * lambda_k2[l].astype(jnp.float32)))
               + lambda_init)
        o = diff_attention(q, k, v, lam)
        o = rmsnorm(o, subln_w[l], SUBLN_EPS) * (1.0 - lambda_init)
        y_attn = o.reshape(b, s, H * DIFF_V_DIM) @ w_attn_o[l]

        y_conv = (cb * short_conv(cc * cx, conv_w[l])) @ w_conv_o[l]

        m = jax.nn.sigmoid(ga) * y_attn + jax.nn.sigmoid(gc) * y_conv
        x = x + g1 * (m @ w_out[l])

        h2 = rmsnorm(x, norm2_w[l]) * (1.0 + sc2) + sh2
        y_moe = moe(h2.reshape(b * s, d), w_router[l], b_router[l], w_gate_up[l], b_gate_up[l],
                    w_down[l], b_down[l]).reshape(b, s, d)
        x = x + g2 * y_moe
    return x
```

```python
import functools
import math

import jax
import jax.numpy as jnp
from jax import lax
from jax.experimental import pallas as pl
from jax.experimental.pallas import tpu as pltpu

N_HEADS = 8
HEAD_DIM = 64
V_DIM = 2 * HEAD_DIM
N_EXPERTS = 32
TOP_K = 4
SWIGLU_LIMIT = 7.0
SWIGLU_ALPHA = 1.702
ROPE_THETA = 10000.0
RMS_EPS = 1e-6
SUBLN_EPS = 1e-5
LANES = 128
MOE_ROWS = 256
NEG_BIG = -1e30
VMEM_LIMIT = 56 * 1024 * 1024

F32 = jnp.float32
BF16 = jnp.bfloat16


def _dot(a, b):
    return jnp.dot(a, b, preferred_element_type=F32)


def _ada_kernel(ct_ref, w_ref, b_ref, o_ref):
    ct = ct_ref[...]
    s = ct * jax.nn.sigmoid(ct)
    w = w_ref[...]
    for b in range(ct.shape[1]):
        o_ref[b:b + 1, :] = jnp.sum(w * s[:, b:b + 1], axis=0, keepdims=True) + b_ref[...]


def _ada_mod(c, w_ada, b_ada):
    bsz, d = c.shape
    n = w_ada.shape[1]
    tn = min(n, 1536)
    return pl.pallas_call(
        _ada_kernel,
        out_shape=jax.ShapeDtypeStruct((bsz, n), F32),
        grid=(n // tn,),
        in_specs=[pl.BlockSpec((d, bsz), lambda j: (0, 0)),
                  pl.BlockSpec((d, tn), lambda j: (0, j)),
                  pl.BlockSpec((1, tn), lambda j: (0, j))],
        out_specs=pl.BlockSpec((bsz, tn), lambda j: (0, j)),
        compiler_params=pltpu.CompilerParams(dimension_semantics=("arbitrary",)),
        name="ada_mod",
    )(c.T, w_ada, b_ada.reshape(1, n))


def _qk_norm_rope_t(y, g_col, cos_t, sin_t, scale):
    tm, w = y.shape
    yt = y.T.reshape(w // HEAD_DIM, HEAD_DIM, tm)
    ms = jnp.mean(yt * yt, axis=1, keepdims=True)
    yn = yt * lax.rsqrt(ms + RMS_EPS) * g_col[None]
    half = HEAD_DIM // 2
    swapped = jnp.concatenate([yn[:, half:, :], yn[:, :half, :]], axis=1)
    out = yn * cos_t[None] + swapped * sin_t[None]
    if scale != 1.0:
        out = out * scale
    return out.reshape(w, tm)


def _in_proj_kernel(x_ref, n1_ref, sc_ref, sh_ref, w_ref, gq_ref, gk_ref, cos_ref, sin_ref,
                    qt_ref, k_ref, vt_ref, cb_ref, z_ref, sga_ref, sgc_ref,
                    h_scr, cc_scr):
    j = pl.program_id(1)

    @pl.when(j == 0)
    def _():
        x = x_ref[...]
        xn = x * lax.rsqrt(jnp.mean(x * x, axis=-1, keepdims=True) + RMS_EPS) * n1_ref[...]
        h_scr[...] = (xn * (1.0 + sc_ref[...]) + sh_ref[...]).astype(BF16)

    y = _dot(h_scr[...], w_ref[...])

    @pl.when(j == 0)
    def _():
        qt_ref[...] = _qk_norm_rope_t(y, gq_ref[...], cos_ref[...], sin_ref[...],
                                      1.0 / math.sqrt(HEAD_DIM)).astype(BF16)

    @pl.when(j == 1)
    def _():
        kt = _qk_norm_rope_t(y, gk_ref[...], cos_ref[...], sin_ref[...], 1.0)
        k_ref[...] = kt.T.astype(BF16)

    @pl.when(j == 2)
    def _():
        vt_ref[...] = y.T.astype(BF16)

    @pl.when(j == 3)
    def _():
        cb_ref[...] = y.astype(BF16)

    @pl.when(j == 4)
    def _():
        cc_scr[...] = y

    @pl.when(j == 5)
    def _():
        z_ref[...] = (cc_scr[...] * y).astype(BF16)

    @pl.when(j == 6)
    def _():
        sga_ref[...] = jax.nn.sigmoid(y).astype(BF16)

    @pl.when(j == 7)
    def _():
        sgc_ref[...] = jax.nn.sigmoid(y).astype(BF16)


def _in_proj(x2, n1, sc1, sh1, w_in_bf, gq, gk, cos_t, sin_t, bsz, seq):
    n, d = x2.shape
    tm = min(seq, 512)
    tpb = seq // tm
    wcol = d
    nj = w_in_bf.shape[1] // wcol
    assert nj == 8
    row = lambda i, j: (i, 0)
    tcol = lambda i, j: (i // tpb, 0, i % tpb)
    mod = lambda i, j: (i // tpb, 0, 0)
    nat = jax.ShapeDtypeStruct((n, d), BF16)
    tr = jax.ShapeDtypeStruct((bsz, d, seq), BF16)
    return pl.pallas_call(
        _in_proj_kernel,
        out_shape=(tr, nat, tr, nat, nat, nat, nat),
        grid=(n // tm, nj),
        in_specs=[pl.BlockSpec((tm, d), row),
                  pl.BlockSpec((1, d), lambda i, j: (0, 0)),
                  pl.BlockSpec((None, 1, d), mod),
                  pl.BlockSpec((None, 1, d), mod),
                  pl.BlockSpec((d, wcol), lambda i, j: (0, j)),
                  pl.BlockSpec((HEAD_DIM, 1), lambda i, j: (0, 0)),
                  pl.BlockSpec((HEAD_DIM, 1), lambda i, j: (0, 0)),
                  pl.BlockSpec((HEAD_DIM, tm), lambda i, j: (0, i % tpb)),
                  pl.BlockSpec((HEAD_DIM, tm), lambda i, j: (0, i % tpb))],
        out_specs=(pl.BlockSpec((None, d, tm), tcol),
                   pl.BlockSpec((tm, d), row),
                   pl.BlockSpec((None, d, tm), tcol),
                   pl.BlockSpec((tm, d), row),
                   pl.BlockSpec((tm, d), row),
                   pl.BlockSpec((tm, d), row),
                   pl.BlockSpec((tm, d), row)),
        scratch_shapes=[pltpu.VMEM((tm, d), BF16), pltpu.VMEM((tm, d), F32)],
        compiler_params=pltpu.CompilerParams(
            dimension_semantics=("arbitrary", "arbitrary"), vmem_limit_bytes=VMEM_LIMIT),
        name="in_proj",
    )(x2, n1, sc1, sh1, w_in_bf, gq, gk, cos_t, sin_t)


def _diff_attn_kernel(qt_ref, k_ref, vt_ref, lam_ref, sw_ref, o_ref, acc0, acc1, *, tk, lambda_init):
    seq = k_ref.shape[0]
    tq = qt_ref.shape[1]
    qt = qt_ref[...]
    first = lax.broadcasted_iota(jnp.int32, qt.shape, 0) < HEAD_DIM
    zero = jnp.zeros_like(qt)
    qz = (jnp.where(first, qt, zero), jnp.where(first, zero, qt))
    accs = (acc0, acc1)
    acc0[...] = jnp.zeros_like(acc0)
    acc1[...] = jnp.zeros_like(acc1)

    def body(j, carry):
        off = pl.multiple_of(j * tk, tk)
        kk = k_ref[pl.ds(off, tk), :]
        vt = vt_ref[:, pl.ds(off, tk)]
        new = []
        for c in range(2):
            m, l = carry[2 * c], carry[2 * c + 1]
            s = _dot(kk, qz[c])
            m_new = jnp.maximum(m, jnp.max(s, axis=0, keepdims=True))
            alpha = jnp.exp(m - m_new)
            p = jnp.exp(s - m_new)
            l = alpha * l + jnp.sum(p, axis=0, keepdims=True)
            accs[c][...] = alpha * accs[c][...] + _dot(vt, p.astype(BF16))
            new += [m_new, l]
        return tuple(new)

    m_init = jnp.full((1, tq), NEG_BIG, F32)
    l_init = jnp.zeros((1, tq), F32)
    _, l0, _, l1 = lax.fori_loop(0, seq // tk, body, (m_init, l_init, m_init, l_init))

    lq = lam_ref[...]
    lam = (jnp.exp(jnp.sum(lq[0:1] * lq[1:2], axis=-1, keepdims=True))
           - jnp.exp(jnp.sum(lq[2:3] * lq[3:4], axis=-1, keepdims=True)) + lambda_init)
    o = acc0[...] / l0 - lam * (acc1[...] / l1)
    o = o * lax.rsqrt(jnp.mean(o * o, axis=0, keepdims=True) + SUBLN_EPS)
    o = o * sw_ref[...] * (1.0 - lambda_init)
    o_ref[...] = o.T.astype(BF16)


def _diff_attn(qt, k3, vt, lam_vecs, subln_col, lambda_init):
    bsz, d, seq = qt.shape
    tq = min(seq, 512)
    tk = min(seq, 512)
    kern = functools.partial(_diff_attn_kernel, tk=tk, lambda_init=lambda_init)
    return pl.pallas_call(
        kern,
        out_shape=jax.ShapeDtypeStruct((bsz, seq, d), BF16),
        grid=(bsz, N_HEADS, seq // tq),
        in_specs=[pl.BlockSpec((None, V_DIM, tq), lambda b, h, i: (b, h, i)),
                  pl.BlockSpec((None, seq, V_DIM), lambda b, h, i: (b, 0, h)),
                  pl.BlockSpec((None, V_DIM, seq), lambda b, h, i: (b, h, 0)),
                  pl.BlockSpec((4, HEAD_DIM), lambda b, h, i: (0, 0)),
                  pl.BlockSpec((V_DIM, 1), lambda b, h, i: (0, 0))],
        out_specs=pl.BlockSpec((None, tq, V_DIM), lambda b, h, i: (b, i, h)),
        scratch_shapes=[pltpu.VMEM((V_DIM, tq), F32), pltpu.VMEM((V_DIM, tq), F32)],
        compiler_params=pltpu.CompilerParams(
            dimension_semantics=("arbitrary", "arbitrary", "arbitrary"),
            vmem_limit_bytes=VMEM_LIMIT),
        name="diff_attn",
    )(qt, k3, vt, lam_vecs, subln_col)


def _post_mix_kernel(o_ref, cb_ref, z_ref, zp_ref, zn_ref, sga_ref, sgc_ref, x_ref,
                     g1_ref, sc2_ref, sh2_ref, n2_ref, cw_ref, wao_ref, wco_ref, wout_ref,
                     wrh_ref, wrl_ref, br_ref,
                     x1_ref, h2_ref, route_ref, cnt_ref, *, tpb):
    i = pl.program_id(0)
    tm = x_ref.shape[0]

    @pl.when(i == 0)
    def _():
        cnt_ref[...] = jnp.zeros_like(cnt_ref)

    z = z_ref[...].astype(F32)
    rows = lax.broadcasted_iota(jnp.int32, z.shape, 0)
    halo_rows = zp_ref.shape[0]
    prev_row = zp_ref[halo_rows - 1:halo_rows, :].astype(F32)
    next_row = zn_ref[0:1, :].astype(F32)
    prev_row = jnp.where(i % tpb == 0, jnp.zeros_like(prev_row), prev_row)
    next_row = jnp.where(i % tpb == tpb - 1, jnp.zeros_like(next_row), next_row)
    z_m1 = jnp.where(rows == 0, prev_row, pltpu.roll(z, 1, 0))
    z_p1 = jnp.where(rows == tm - 1, next_row, pltpu.roll(z, tm - 1, 0))
    cw = cw_ref[...]
    conv = z_m1 * cw[0:1] + z * cw[1:2] + z_p1 * cw[2:3]
    u = (cb_ref[...].astype(F32) * conv).astype(BF16)

    y_attn = _dot(o_ref[...], wao_ref[...])
    y_conv = _dot(u, wco_ref[...])
    m = sga_ref[...].astype(F32) * y_attn + sgc_ref[...].astype(F32) * y_conv
    x1 = x_ref[...] + g1_ref[...] * _dot(m.astype(BF16), wout_ref[...])
    x1_ref[...] = x1

    h2 = x1 * lax.rsqrt(jnp.mean(x1 * x1, axis=-1, keepdims=True) + RMS_EPS) * n2_ref[...]
    h2 = h2 * (1.0 + sc2_ref[...]) + sh2_ref[...]
    h2_ref[...] = h2

    h_hi = h2.astype(BF16)
    h_lo = (h2 - h_hi.astype(F32)).astype(BF16)
    logits = (_dot(h_hi, wrh_ref[...]) + _dot(h_lo, wrh_ref[...]) + _dot(h_hi, wrl_ref[...])
              + br_ref[...])

    lane = lax.broadcasted_iota(jnp.int32, logits.shape, 1)
    work = logits
    vals, idxs = [], []
    for _ in range(TOP_K):
        mx = jnp.max(work, axis=-1, keepdims=True)
        ix = jnp.min(jnp.where(work == mx, lane, LANES), axis=-1, keepdims=True)
        vals.append(mx)
        idxs.append(ix)
        work = jnp.where(lane == ix, 2.0 * NEG_BIG, work)
    exps = [jnp.exp(v - vals[0]) for v in vals]
    den = exps[0] + exps[1] + exps[2] + exps[3]

    sel = (work == 2.0 * NEG_BIG).astype(BF16)
    r_i = lax.broadcasted_iota(jnp.int32, (tm, tm), 0)
    c_i = lax.broadcasted_iota(jnp.int32, (tm, tm), 1)
    lower = (r_i > c_i).astype(BF16)
    before = _dot(lower, sel) + cnt_ref[0:1, :]
    cnt_ref[...] = cnt_ref[...] + jnp.sum(sel.astype(F32), axis=0, keepdims=True)

    route = jnp.zeros(logits.shape, F32)
    for k in range(TOP_K):
        rank = jnp.sum(jnp.where(lane == idxs[k], before, 0.0), axis=-1, keepdims=True)
        route = jnp.where(lane == k, exps[k] / den, route)
        route = jnp.where(lane == TOP_K + k, idxs[k].astype(F32), route)
        route = jnp.where(lane == 2 * TOP_K + k, rank, route)
    route_ref[...] = route


def _post_mix(o2, cb, z, sga, sgc, x2, g1, sc2, sh2, n2, conv_w, wao, wco, wout,
              wr_hi, wr_lo, br_pad, seq):
    n, d = x2.shape
    tm = min(seq, 512)
    tpb = seq // tm
    halo = 16
    hb = tm // halo
    last_hb = n // halo - 1
    row = lambda i: (i, 0)
    const = lambda i: (0, 0)
    mod = lambda i: (i // tpb, 0, 0)
    wspec = pl.BlockSpec((d, d), const)
    kern = functools.partial(_post_mix_kernel, tpb=tpb)
    return pl.pallas_call(
        kern,
        out_shape=(jax.ShapeDtypeStruct((n, d), F32), jax.ShapeDtypeStruct((n, d), F32),
                   jax.ShapeDtypeStruct((n, LANES), F32), jax.ShapeDtypeStruct((8, LANES), F32)),
        grid=(n // tm,),
        in_specs=[pl.BlockSpec((tm, d), row), pl.BlockSpec((tm, d), row), pl.BlockSpec((tm, d), row),
                  pl.BlockSpec((halo, d), lambda i: (jnp.maximum(i * hb - 1, 0), 0)),
                  pl.BlockSpec((halo, d), lambda i: (jnp.minimum((i + 1) * hb, last_hb), 0)),
                  pl.BlockSpec((tm, d), row), pl.BlockSpec((tm, d), row), pl.BlockSpec((tm, d), row),
                  pl.BlockSpec((None, 1, d), mod), pl.BlockSpec((None, 1, d), mod),
                  pl.BlockSpec((None, 1, d), mod),
                  pl.BlockSpec((1, d), const), pl.BlockSpec((3, d), const),
                  wspec, wspec, wspec,
                  pl.BlockSpec((d, LANES), const), pl.BlockSpec((d, LANES), const),
                  pl.BlockSpec((1, LANES), const)],
        out_specs=(pl.BlockSpec((tm, d), row), pl.BlockSpec((tm, d), row),
                   pl.BlockSpec((tm, LANES), row), pl.BlockSpec((8, LANES), const)),
        compiler_params=pltpu.CompilerParams(
            dimension_semantics=("arbitrary",), vmem_limit_bytes=VMEM_LIMIT),
        name="post_mix",
    )(o2, cb, z, z, z, sga, sgc, x2, g1, sc2, sh2, n2, conv_w, wao, wco, wout, wr_hi, wr_lo, br_pad)


def _row_copy(src_hbm, dst_hbm, s, t, sem):
    return pltpu.make_async_copy(src_hbm.at[pl.ds(s, 1)], dst_hbm.at[pl.ds(t, 1)], sem)


def _dispatch_kernel(dest_ref, h2_hbm, xs_in_hbm, xs_hbm, sem, *, tc):
    del xs_in_hbm
    base = pl.program_id(0) * tc

    def issue(t, carry):
        tok = base + t
        for k in range(TOP_K):
            _row_copy(h2_hbm, xs_hbm, tok, dest_ref[tok * TOP_K + k], sem).start()
        return carry

    lax.fori_loop(0, tc, issue, 0)

    def drain(t, carry):
        _row_copy(h2_hbm, xs_hbm, 0, 0, sem).wait()
        return carry

    lax.fori_loop(0, tc * TOP_K, drain, 0)


def _moe_dispatch(dest_flat, h2, p_rows):
    n, d = h2.shape
    tc = min(n, 512)
    xs0 = jnp.zeros((p_rows, d), F32)
    kern = functools.partial(_dispatch_kernel, tc=tc)
    return pl.pallas_call(
        kern,
        out_shape=jax.ShapeDtypeStruct((p_rows, d), F32),
        grid_spec=pltpu.PrefetchScalarGridSpec(
            num_scalar_prefetch=1, grid=(n // tc,),
            in_specs=[pl.BlockSpec(memory_space=pl.ANY), pl.BlockSpec(memory_space=pl.ANY)],
            out_specs=pl.BlockSpec(memory_space=pl.ANY),
            scratch_shapes=[pltpu.SemaphoreType.DMA]),
        input_output_aliases={2: 0},
        compiler_params=pltpu.CompilerParams(dimension_semantics=("arbitrary",)),
        name="moe_dispatch",
    )(dest_flat, h2, xs0)


def _ffn_kernel(be_ref, nb_ref, xs_ref, wg_ref, wu_ref, wd_ref, bg_ref, bu_ref, bd_ref, ys_ref):
    del be_ref
    used = pl.program_id(0) < nb_ref[0]

    @pl.when(jnp.logical_not(used))
    def _():
        ys_ref[...] = jnp.zeros_like(ys_ref)

    @pl.when(used)
    def _():
        x = xs_ref[...].astype(BF16)
        gate = jnp.minimum(_dot(x, wg_ref[...]) + bg_ref[...], SWIGLU_LIMIT)
        up = jnp.clip(_dot(x, wu_ref[...]) + bu_ref[...], -SWIGLU_LIMIT, SWIGLU_LIMIT)
        glu = gate * jax.nn.sigmoid(gate * SWIGLU_ALPHA)
        mid = ((up + 1.0) * glu).astype(BF16)
        ys_ref[...] = _dot(mid, wd_ref[...]) + bd_ref[...]


def _moe_ffn(block_e, n_used, xs, wg, wu, wd, bg, bu, bd):
    p_rows, d = xs.shape
    f = wg.shape[2]
    nblk = p_rows // MOE_ROWS
    rows = lambda i, be, nb: (jnp.minimum(i, nb[0] - 1), 0)
    wmap = lambda i, be, nb: (be[i], 0, 0)
    return pl.pallas_call(
        _ffn_kernel,
        out_shape=jax.ShapeDtypeStruct((p_rows, d), F32),
        grid_spec=pltpu.PrefetchScalarGridSpec(
            num_scalar_prefetch=2, grid=(nblk,),
            in_specs=[pl.BlockSpec((MOE_ROWS, d), rows),
                      pl.BlockSpec((None, d, f), wmap), pl.BlockSpec((None, d, f), wmap),
                      pl.BlockSpec((None, f, d), wmap),
                      pl.BlockSpec((None, 1, f), wmap), pl.BlockSpec((None, 1, f), wmap),
                      pl.BlockSpec((None, 1, d), wmap)],
            out_specs=pl.BlockSpec((MOE_ROWS, d), lambda i, be, nb: (i, 0))),
        compiler_params=pltpu.CompilerParams(
            dimension_semantics=("arbitrary",), vmem_limit_bytes=VMEM_LIMIT),
        name="moe_ffn",
    )(block_e, n_used, xs, wg, wu, wd, bg, bu, bd)


def _gather_copy(ys_hbm, buf, s, k, t, sem):
    return pltpu.make_async_copy(ys_hbm.at[pl.ds(s, 1)], buf.at[k, pl.ds(t, 1)], sem)


def _combine_kernel(dest_ref, ys_hbm, x1_ref, route_ref, g2_ref, o_ref, buf, sem, *, tc):
    base = pl.program_id(0) * tc

    def issue(t, carry):
        for k in range(TOP_K):
            _gather_copy(ys_hbm, buf, dest_ref[(base + t) * TOP_K + k], k, t, sem).start()
        return carry

    lax.fori_loop(0, tc, issue, 0)

    def drain(t, carry):
        _gather_copy(ys_hbm, buf, 0, 0, 0, sem).wait()
        return carry

    lax.fori_loop(0, tc * TOP_K, drain, 0)

    route = route_ref[...]
    y = buf[0] * route[:, 0:1]
    for k in range(1, TOP_K):
        y = y + buf[k] * route[:, k:k + 1]
    o_ref[...] = x1_ref[...] + g2_ref[...] * y


def _moe_combine(dest_flat, ys, x1, route, g2, seq):
    n, d = x1.shape
    tc = min(seq, 256)
    tpb = seq // tc
    kern = functools.partial(_combine_kernel, tc=tc)
    return pl.pallas_call(
        kern,
        out_shape=jax.ShapeDtypeStruct((n, d), F32),
        grid_spec=pltpu.PrefetchScalarGridSpec(
            num_scalar_prefetch=1, grid=(n // tc,),
            in_specs=[pl.BlockSpec(memory_space=pl.ANY),
                      pl.BlockSpec((tc, d), lambda i, dr: (i, 0)),
                      pl.BlockSpec((tc, LANES), lambda i, dr: (i, 0)),
                      pl.BlockSpec((None, 1, d), lambda i, dr: (i // tpb, 0, 0))],
            out_specs=pl.BlockSpec((tc, d), lambda i, dr: (i, 0)),
            scratch_shapes=[pltpu.VMEM((TOP_K, tc, d), F32), pltpu.SemaphoreType.DMA]),
        compiler_params=pltpu.CompilerParams(
            dimension_semantics=("arbitrary",), vmem_limit_bytes=VMEM_LIMIT),
        name="moe_combine",
    )(dest_flat, ys, x1, route, g2)


def _rope_tables(seq):
    inv_freq = ROPE_THETA ** (-jnp.arange(0, HEAD_DIM, 2, dtype=F32) / HEAD_DIM)
    ang = inv_freq[:, None] * jnp.arange(seq, dtype=F32)[None, :]
    cos_t = jnp.concatenate([jnp.cos(ang), jnp.cos(ang)], axis=0)
    sin_t = jnp.concatenate([-jnp.sin(ang), jnp.sin(ang)], axis=0)
    return cos_t, sin_t


def _layer(x, c, l, lambda_init, w_ada, b_ada, norm1_w, w_in, q_norm_w, k_norm_w, lambda_q1,
           lambda_k1, lambda_q2, lambda_k2, subln_w, w_attn_o, conv_w, w_conv_o, w_out, norm2_w,
           w_router, b_router, w_gate_up, b_gate_up, w_down, b_down):
    bsz, seq, d = x.shape
    n = bsz * seq
    x2 = x.reshape(n, d)

    mod = _ada_mod(c, w_ada[l], b_ada[l])
    sh1, sc1, g1, sh2, sc2, g2 = [m.reshape(bsz, 1, d) for m in jnp.split(mod, 6, axis=-1)]

    cos_t, sin_t = _rope_tables(seq)
    qt, k, vt, cb, z, sga, sgc = _in_proj(
        x2, norm1_w[l].reshape(1, d), sc1, sh1, w_in[l].astype(BF16),
        q_norm_w[l].reshape(HEAD_DIM, 1), k_norm_w[l].reshape(HEAD_DIM, 1), cos_t, sin_t, bsz, seq)

    lam_vecs = jnp.stack([lambda_q1[l], lambda_k1[l], lambda_q2[l], lambda_k2[l]]).astype(F32)
    o = _diff_attn(qt, k.reshape(bsz, seq, d), vt, lam_vecs, subln_w[l].reshape(V_DIM, 1), lambda_init)

    wr = jnp.zeros((d, LANES), F32).at[:, :N_EXPERTS].set(w_router[l].astype(F32))
    wr_hi = wr.astype(BF16)
    wr_lo = (wr - wr_hi.astype(F32)).astype(BF16)
    br_pad = jnp.full((1, LANES), NEG_BIG, F32).at[0, :N_EXPERTS].set(b_router[l].astype(F32))
    x1, h2, route, cnt = _post_mix(
        o.reshape(n, d), cb, z, sga, sgc, x2, g1, sc2, sh2, norm2_w[l].reshape(1, d), conv_w[l],
        w_attn_o[l].astype(BF16), w_conv_o[l].astype(BF16), w_out[l].astype(BF16),
        wr_hi, wr_lo, br_pad, seq)

    counts = cnt[0, :N_EXPERTS].astype(jnp.int32)
    padded = ((counts + MOE_ROWS - 1) // MOE_ROWS) * MOE_ROWS
    ends = jnp.cumsum(padded)
    start = ends - padded
    p_rows = n * TOP_K + N_EXPERTS * MOE_ROWS
    nblk = p_rows // MOE_ROWS
    top_e = route[:, TOP_K:2 * TOP_K].astype(jnp.int32)
    rank = route[:, 2 * TOP_K:3 * TOP_K].astype(jnp.int32)
    dest = (start[top_e] + rank).reshape(-1)
    block_e = jnp.clip(jnp.searchsorted(ends, jnp.arange(nblk, dtype=jnp.int32) * MOE_ROWS,
                                        side='right'), 0, N_EXPERTS - 1).astype(jnp.int32)
    n_used = (ends[-1:] // MOE_ROWS).astype(jnp.int32)

    xs = _moe_dispatch(dest, h2, p_rows)
    wgu = w_gate_up[l]
    ys = _moe_ffn(block_e, n_used, xs,
                  wgu[:, :, 0::2].astype(BF16), wgu[:, :, 1::2].astype(BF16), w_down[l].astype(BF16),
                  b_gate_up[l][:, None, 0::2], b_gate_up[l][:, None, 1::2], b_down[l][:, None, :])
    out = _moe_combine(dest, ys, x1, route, g2, seq)
    return out.reshape(bsz, seq, d)


def kernel(x, c, w_ada, b_ada, norm1_w, w_in, q_norm_w, k_norm_w, lambda_q1, lambda_k1, lambda_q2,
           lambda_k2, subln_w, w_attn_o, conv_w, w_conv_o, w_out, norm2_w, w_router, b_router,
           w_gate_up, b_gate_up, w_down, b_down):
    depth = w_ada.shape[0]
    for l in range(depth):
        lambda_init = 0.8 - 0.6 * math.exp(-0.3 * l)
        x = _layer(x, c, l, lambda_init, w_ada, b_ada, norm1_w, w_in, q_norm_w, k_norm_w,
                   lambda_q1, lambda_k1, lambda_q2, lambda_k2, subln_w, w_attn_o, conv_w,
                   w_conv_o, w_out, norm2_w, w_router, b_router, w_gate_up, b_gate_up,
                   w_down, b_down)
    return x
```

```python
import functools
import math

import jax
import jax.numpy as jnp
from jax import lax
from jax.experimental import pallas as pl
from jax.experimental.pallas import tpu as pltpu

N_HEADS = 8
HEAD_DIM = 64
V_DIM = 2 * HEAD_DIM
N_EXPERTS = 32
TOP_K = 4
SWIGLU_LIMIT = 7.0
SWIGLU_ALPHA = 1.702
ROPE_THETA = 10000.0
RMS_EPS = 1e-6
SUBLN_EPS = 1e-5
LANES = 128
MOE_ROWS = 256
NEG_BIG = -1e30
LOG2E = 1.4426950408889634
SAFE_EXP2_BOUND = 80.0
VMEM_LIMIT = 56 * 1024 * 1024

F32 = jnp.float32
BF16 = jnp.bfloat16


def _dot(a, b):
    return jnp.dot(a, b, preferred_element_type=F32)


def _ada_kernel(ct_ref, w_ref, b_ref, o_ref):
    ct = ct_ref[...]
    s = ct * jax.nn.sigmoid(ct)
    w = w_ref[...]
    for b in range(ct.shape[1]):
        o_ref[b:b + 1, :] = jnp.sum(w * s[:, b:b + 1], axis=0, keepdims=True) + b_ref[...]


def _ada_mod(c, w_ada, b_ada):
    bsz, d = c.shape
    n = w_ada.shape[1]
    tn = min(n, 1536)
    return pl.pallas_call(
        _ada_kernel,
        out_shape=jax.ShapeDtypeStruct((bsz, n), F32),
        grid=(n // tn,),
        in_specs=[pl.BlockSpec((d, bsz), lambda j: (0, 0)),
                  pl.BlockSpec((d, tn), lambda j: (0, j)),
                  pl.BlockSpec((1, tn), lambda j: (0, j))],
        out_specs=pl.BlockSpec((bsz, tn), lambda j: (0, j)),
        compiler_params=pltpu.CompilerParams(dimension_semantics=("arbitrary",)),
        name="ada_mod",
    )(c.T, w_ada, b_ada.reshape(1, n))


def _qk_norm_rope_t(y, g_col, cos_t, sin_t, scale):
    tm, w = y.shape
    yt = y.T.reshape(w // HEAD_DIM, HEAD_DIM, tm)
    ms = jnp.mean(yt * yt, axis=1, keepdims=True)
    yn = yt * lax.rsqrt(ms + RMS_EPS) * g_col[None]
    half = HEAD_DIM // 2
    swapped = jnp.concatenate([yn[:, half:, :], yn[:, :half, :]], axis=1)
    out = yn * cos_t[None] + swapped * sin_t[None]
    if scale != 1.0:
        out = out * scale
    norm2 = jnp.sum(out * out, axis=1).reshape(N_HEADS, 2, tm)
    return out.reshape(w, tm), norm2


def _in_proj_kernel(x_ref, n1_ref, sc_ref, sh_ref, w_ref, gq_ref, gk_ref, cos_ref, sin_ref,
                    qt_ref, k_ref, vt_ref, cb_ref, z_ref, sga_ref, sgc_ref, qn_ref, kn_ref,
                    h_scr, cc_scr):
    j = pl.program_id(1)

    @pl.when(j == 0)
    def _():
        x = x_ref[...]
        xn = x * lax.rsqrt(jnp.mean(x * x, axis=-1, keepdims=True) + RMS_EPS) * n1_ref[...]
        h_scr[...] = (xn * (1.0 + sc_ref[...]) + sh_ref[...]).astype(BF16)

    y = _dot(h_scr[...], w_ref[...])

    @pl.when(j == 0)
    def _():
        qt, qn = _qk_norm_rope_t(y, gq_ref[...], cos_ref[...], sin_ref[...],
                                 LOG2E / math.sqrt(HEAD_DIM))
        qt_ref[...] = qt.astype(BF16)
        qn_ref[...] = qn

    @pl.when(j == 1)
    def _():
        kt, kn = _qk_norm_rope_t(y, gk_ref[...], cos_ref[...], sin_ref[...], 1.0)
        k_ref[...] = kt.T.astype(BF16)
        kn_ref[...] = kn

    @pl.when(j == 2)
    def _():
        vt_ref[...] = y.T.astype(BF16)

    @pl.when(j == 3)
    def _():
        cb_ref[...] = y.astype(BF16)

    @pl.when(j == 4)
    def _():
        cc_scr[...] = y

    @pl.when(j == 5)
    def _():
        z_ref[...] = (cc_scr[...] * y).astype(BF16)

    @pl.when(j == 6)
    def _():
        sga_ref[...] = jax.nn.sigmoid(y).astype(BF16)

    @pl.when(j == 7)
    def _():
        sgc_ref[...] = jax.nn.sigmoid(y).astype(BF16)


def _in_proj(x2, n1, sc1, sh1, w_in_bf, gq, gk, cos_t, sin_t, bsz, seq):
    n, d = x2.shape
    tm = min(seq, 512)
    tpb = seq // tm
    wcol = d
    nj = w_in_bf.shape[1] // wcol
    assert nj == 8
    row = lambda i, j: (i, 0)
    tcol = lambda i, j: (i // tpb, 0, i % tpb)
    mod = lambda i, j: (i // tpb, 0, 0)
    nat = jax.ShapeDtypeStruct((n, d), BF16)
    tr = jax.ShapeDtypeStruct((bsz, d, seq), BF16)
    nrm = jax.ShapeDtypeStruct((bsz, N_HEADS, 2, seq), F32)
    nrm_spec = pl.BlockSpec((None, N_HEADS, 2, tm), lambda i, j: (i // tpb, 0, 0, i % tpb))
    return pl.pallas_call(
        _in_proj_kernel,
        out_shape=(tr, nat, tr, nat, nat, nat, nat, nrm, nrm),
        grid=(n // tm, nj),
        in_specs=[pl.BlockSpec((tm, d), row),
                  pl.BlockSpec((1, d), lambda i, j: (0, 0)),
                  pl.BlockSpec((None, 1, d), mod),
                  pl.BlockSpec((None, 1, d), mod),
                  pl.BlockSpec((d, wcol), lambda i, j: (0, j)),
                  pl.BlockSpec((HEAD_DIM, 1), lambda i, j: (0, 0)),
                  pl.BlockSpec((HEAD_DIM, 1), lambda i, j: (0, 0)),
                  pl.BlockSpec((HEAD_DIM, tm), lambda i, j: (0, i % tpb)),
                  pl.BlockSpec((HEAD_DIM, tm), lambda i, j: (0, i % tpb))],
        out_specs=(pl.BlockSpec((None, d, tm), tcol),
                   pl.BlockSpec((tm, d), row),
                   pl.BlockSpec((None, d, tm), tcol),
                   pl.BlockSpec((tm, d), row),
                   pl.BlockSpec((tm, d), row),
                   pl.BlockSpec((tm, d), row),
                   pl.BlockSpec((tm, d), row),
                   nrm_spec, nrm_spec),
        scratch_shapes=[pltpu.VMEM((tm, d), BF16), pltpu.VMEM((tm, d), F32)],
        compiler_params=pltpu.CompilerParams(
            dimension_semantics=("arbitrary", "arbitrary"), vmem_limit_bytes=VMEM_LIMIT),
        name="in_proj",
    )(x2, n1, sc1, sh1, w_in_bf, gq, gk, cos_t, sin_t)


def _diff_attn_kernel(qt_ref, k_ref, vt_ref, qn_ref, kn_ref, lam_ref, sw_ref, o_ref, acc0, acc1,
                      *, tk, lambda_init):
    seq = k_ref.shape[0]
    tq = qt_ref.shape[1]
    qt = qt_ref[...]
    first = lax.broadcasted_iota(jnp.int32, qt.shape, 0) < HEAD_DIM
    zero = jnp.zeros_like(qt)
    qz = (jnp.where(first, qt, zero), jnp.where(first, zero, qt))
    accs = (acc0, acc1)
    acc0[...] = jnp.zeros_like(acc0)
    acc1[...] = jnp.zeros_like(acc1)
    n_chunks = seq // tk

    def load(j):
        off = pl.multiple_of(j * tk, tk)
        return k_ref[pl.ds(off, tk), :], vt_ref[:, pl.ds(off, tk)]

    def plain_body(j, carry):
        kk, vt = load(j)
        new = []
        for c in range(2):
            p = jnp.exp2(_dot(kk, qz[c]))
            new.append(carry[c] + jnp.sum(p, axis=0, keepdims=True))
            accs[c][...] += _dot(vt, p.astype(BF16))
        return tuple(new)

    def online_body(j, carry):
        kk, vt = load(j)
        new = []
        for c in range(2):
            m, l = carry[2 * c], carry[2 * c + 1]
            s = _dot(kk, qz[c])
            m_new = jnp.maximum(m, jnp.max(s, axis=0, keepdims=True))
            alpha = jnp.exp2(m - m_new)
            p = jnp.exp2(s - m_new)
            l = alpha * l + jnp.sum(p, axis=0, keepdims=True)
            accs[c][...] = alpha * accs[c][...] + _dot(vt, p.astype(BF16))
            new += [m_new, l]
        return tuple(new)

    m_init = jnp.full((1, tq), NEG_BIG, F32)
    l_init = jnp.zeros((1, tq), F32)

    def plain():
        return lax.fori_loop(0, n_chunks, plain_body, (l_init, l_init))

    def online():
        _, l0, _, l1 = lax.fori_loop(0, n_chunks, online_body, (m_init, l_init, m_init, l_init))
        return l0, l1

    bound2 = jnp.max(jnp.max(qn_ref[...], axis=-1, keepdims=True)
                     * jnp.max(kn_ref[...], axis=-1, keepdims=True))
    l0, l1 = lax.cond(bound2 <= SAFE_EXP2_BOUND * SAFE_EXP2_BOUND, plain, online)

    lq = lam_ref[...]
    lam = (jnp.exp(jnp.sum(lq[0:1] * lq[1:2], axis=-1, keepdims=True))
           - jnp.exp(jnp.sum(lq[2:3] * lq[3:4], axis=-1, keepdims=True)) + lambda_init)
    o = acc0[...] / l0 - lam * (acc1[...] / l1)
    o = o * lax.rsqrt(jnp.mean(o * o, axis=0, keepdims=True) + SUBLN_EPS)
    o = o * sw_ref[...] * (1.0 - lambda_init)
    o_ref[...] = o.T.astype(BF16)


def _diff_attn(qt, k3, vt, qn, kn, lam_vecs, subln_col, lambda_init):
    bsz, d, seq = qt.shape
    tq = min(seq, 512)
    tk = min(seq, 2048)
    kern = functools.partial(_diff_attn_kernel, tk=tk, lambda_init=lambda_init)
    return pl.pallas_call(
        kern,
        out_shape=jax.ShapeDtypeStruct((bsz, seq, d), BF16),
        grid=(bsz, N_HEADS, seq // tq),
        in_specs=[pl.BlockSpec((None, V_DIM, tq), lambda b, h, i: (b, h, i)),
                  pl.BlockSpec((None, seq, V_DIM), lambda b, h, i: (b, 0, h)),
                  pl.BlockSpec((None, V_DIM, seq), lambda b, h, i: (b, h, 0)),
                  pl.BlockSpec((None, None, 2, tq), lambda b, h, i: (b, h, 0, i)),
                  pl.BlockSpec((None, None, 2, seq), lambda b, h, i: (b, h, 0, 0)),
                  pl.BlockSpec((4, HEAD_DIM), lambda b, h, i: (0, 0)),
                  pl.BlockSpec((V_DIM, 1), lambda b, h, i: (0, 0))],
        out_specs=pl.BlockSpec((None, tq, V_DIM), lambda b, h, i: (b, i, h)),
        scratch_shapes=[pltpu.VMEM((V_DIM, tq), F32), pltpu.VMEM((V_DIM, tq), F32)],
        compiler_params=pltpu.CompilerParams(
            dimension_semantics=("arbitrary", "arbitrary", "arbitrary"),
            vmem_limit_bytes=VMEM_LIMIT),
        name="diff_attn",
    )(qt, k3, vt, qn, kn, lam_vecs, subln_col)


def _post_mix_kernel(o_ref, cb_ref, z_ref, zp_ref, zn_ref, sga_ref, sgc_ref, x_ref,
                     g1_ref, sc2_ref, sh2_ref, n2_ref, cw_ref, wao_ref, wco_ref, wout_ref,
                     wrh_ref, wrl_ref, br_ref,
                     x1_ref, h2_ref, route_ref, cnt_ref, *, tpb):
    i = pl.program_id(0)
    tm = x_ref.shape[0]

    @pl.when(i == 0)
    def _():
        cnt_ref[...] = jnp.zeros_like(cnt_ref)

    z = z_ref[...].astype(F32)
    rows = lax.broadcasted_iota(jnp.int32, z.shape, 0)
    halo_rows = zp_ref.shape[0]
    prev_row = zp_ref[halo_rows - 1:halo_rows, :].astype(F32)
    next_row = zn_ref[0:1, :].astype(F32)
    prev_row = jnp.where(i % tpb == 0, jnp.zeros_like(prev_row), prev_row)
    next_row = jnp.where(i % tpb == tpb - 1, jnp.zeros_like(next_row), next_row)
    z_m1 = jnp.where(rows == 0, prev_row, pltpu.roll(z, 1, 0))
    z_p1 = jnp.where(rows == tm - 1, next_row, pltpu.roll(z, tm - 1, 0))
    cw = cw_ref[...]
    conv = z_m1 * cw[0:1] + z * cw[1:2] + z_p1 * cw[2:3]
    u = (cb_ref[...].astype(F32) * conv).astype(BF16)

    y_attn = _dot(o_ref[...], wao_ref[...])
    y_conv = _dot(u, wco_ref[...])
    m = sga_ref[...].astype(F32) * y_attn + sgc_ref[...].astype(F32) * y_conv
    x1 = x_ref[...] + g1_ref[...] * _dot(m.astype(BF16), wout_ref[...])
    x1_ref[...] = x1

    h2 = x1 * lax.rsqrt(jnp.mean(x1 * x1, axis=-1, keepdims=True) + RMS_EPS) * n2_ref[...]
    h2 = h2 * (1.0 + sc2_ref[...]) + sh2_ref[...]
    h2_ref[...] = h2

    h_hi = h2.astype(BF16)
    h_lo = (h2 - h_hi.astype(F32)).astype(BF16)
    logits = (_dot(h_hi, wrh_ref[...]) + _dot(h_lo, wrh_ref[...]) + _dot(h_hi, wrl_ref[...])
              + br_ref[...])

    lane = lax.broadcasted_iota(jnp.int32, logits.shape, 1)
    work = logits
    vals, idxs = [], []
    for _ in range(TOP_K):
        mx = jnp.max(work, axis=-1, keepdims=True)
        ix = jnp.min(jnp.where(work == mx, lane, LANES), axis=-1, keepdims=True)
        vals.append(mx)
        idxs.append(ix)
        work = jnp.where(lane == ix, 2.0 * NEG_BIG, work)
    exps = [jnp.exp(v - vals[0]) for v in vals]
    den = exps[0] + exps[1] + exps[2] + exps[3]

    sel = (work == 2.0 * NEG_BIG).astype(BF16)
    r_i = lax.broadcasted_iota(jnp.int32, (tm, tm), 0)
    c_i = lax.broadcasted_iota(jnp.int32, (tm, tm), 1)
    lower = (r_i > c_i).astype(BF16)
    before = _dot(lower, sel) + cnt_ref[0:1, :]
    cnt_ref[...] = cnt_ref[...] + jnp.sum(sel.astype(F32), axis=0, keepdims=True)

    route = jnp.zeros(logits.shape, F32)
    for k in range(TOP_K):
        rank = jnp.sum(jnp.where(lane == idxs[k], before, 0.0), axis=-1, keepdims=True)
        route = jnp.where(lane == k, exps[k] / den, route)
        route = jnp.where(lane == TOP_K + k, idxs[k].astype(F32), route)
        route = jnp.where(lane == 2 * TOP_K + k, rank, route)
    route_ref[...] = route


def _post_mix(o2, cb, z, sga, sgc, x2, g1, sc2, sh2, n2, conv_w, wao, wco, wout,
              wr_hi, wr_lo, br_pad, seq):
    n, d = x2.shape
    tm = min(seq, 512)
    tpb = seq // tm
    halo = 16
    hb = tm // halo
    last_hb = n // halo - 1
    row = lambda i: (i, 0)
    const = lambda i: (0, 0)
    mod = lambda i: (i // tpb, 0, 0)
    wspec = pl.BlockSpec((d, d), const)
    kern = functools.partial(_post_mix_kernel, tpb=tpb)
    return pl.pallas_call(
        kern,
        out_shape=(jax.ShapeDtypeStruct((n, d), F32), jax.ShapeDtypeStruct((n, d), F32),
                   jax.ShapeDtypeStruct((n, LANES), F32), jax.ShapeDtypeStruct((8, LANES), F32)),
        grid=(n // tm,),
        in_specs=[pl.BlockSpec((tm, d), row), pl.BlockSpec((tm, d), row), pl.BlockSpec((tm, d), row),
                  pl.BlockSpec((halo, d), lambda i: (jnp.maximum(i * hb - 1, 0), 0)),
                  pl.BlockSpec((halo, d), lambda i: (jnp.minimum((i + 1) * hb, last_hb), 0)),
                  pl.BlockSpec((tm, d), row), pl.BlockSpec((tm, d), row), pl.BlockSpec((tm, d), row),
                  pl.BlockSpec((None, 1, d), mod), pl.BlockSpec((None, 1, d), mod),
                  pl.BlockSpec((None, 1, d), mod),
                  pl.BlockSpec((1, d), const), pl.BlockSpec((3, d), const),
                  wspec, wspec, wspec,
                  pl.BlockSpec((d, LANES), const), pl.BlockSpec((d, LANES), const),
                  pl.BlockSpec((1, LANES), const)],
        out_specs=(pl.BlockSpec((tm, d), row), pl.BlockSpec((tm, d), row),
                   pl.BlockSpec((tm, LANES), row), pl.BlockSpec((8, LANES), const)),
        compiler_params=pltpu.CompilerParams(
            dimension_semantics=("arbitrary",), vmem_limit_bytes=VMEM_LIMIT),
        name="post_mix",
    )(o2, cb, z, z, z, sga, sgc, x2, g1, sc2, sh2, n2, conv_w, wao, wco, wout, wr_hi, wr_lo, br_pad)


def _row_copy(src_hbm, dst_hbm, s, t, sem):
    return pltpu.make_async_copy(src_hbm.at[pl.ds(s, 1)], dst_hbm.at[pl.ds(t, 1)], sem)


def _dispatch_kernel(dest_ref, h2_ref, xs_in_hbm, xs_hbm, sem, *, tc):
    del xs_in_hbm
    base = pl.program_id(0) * tc

    def issue(t, carry):
        for k in range(TOP_K):
            _row_copy(h2_ref, xs_hbm, t, dest_ref[(base + t) * TOP_K + k], sem).start()
        return carry

    lax.fori_loop(0, tc, issue, 0)

    for _ in range(TOP_K):
        pltpu.make_async_copy(h2_ref, xs_hbm.at[pl.ds(0, tc)], sem).wait()


def _moe_dispatch(dest_flat, h2, p_rows):
    n, d = h2.shape
    tc = min(n, 512)
    xs0 = jnp.zeros((p_rows, d), F32)
    kern = functools.partial(_dispatch_kernel, tc=tc)
    return pl.pallas_call(
        kern,
        out_shape=jax.ShapeDtypeStruct((p_rows, d), F32),
        grid_spec=pltpu.PrefetchScalarGridSpec(
            num_scalar_prefetch=1, grid=(n // tc,),
            in_specs=[pl.BlockSpec((tc, d), lambda i, dr: (i, 0)), pl.BlockSpec(memory_space=pl.ANY)],
            out_specs=pl.BlockSpec(memory_space=pl.ANY),
            scratch_shapes=[pltpu.SemaphoreType.DMA]),
        input_output_aliases={2: 0},
        compiler_params=pltpu.CompilerParams(dimension_semantics=("arbitrary",)),
        name="moe_dispatch",
    )(dest_flat, h2, xs0)


_NT = (((1,), (1,)), ((), ()))


def _ffn_kernel(be_ref, nb_ref, xs_ref, wgu_ref, wd_ref, bg_ref, bu_ref, bd_ref, ys_ref,
                wt_scr, wg_scr, wu_scr, wd_scr):
    i = pl.program_id(0)
    used = i < nb_ref[0]
    fresh = jnp.logical_or(i == 0, be_ref[i] != be_ref[jnp.maximum(i - 1, 0)])

    @pl.when(jnp.logical_not(used))
    def _():
        ys_ref[...] = jnp.zeros_like(ys_ref)

    @pl.when(jnp.logical_and(used, fresh))
    def _():
        d, f2 = wgu_ref.shape
        for c in range(d // LANES):
            cols = slice(c * LANES, (c + 1) * LANES)
            wt_scr[c] = wgu_ref[cols, :].T
            wg_scr[:, cols] = wt_scr[c, pl.ds(0, f2 // 2, stride=2), :].astype(BF16)
            wu_scr[:, cols] = wt_scr[c, pl.ds(1, f2 // 2, stride=2), :].astype(BF16)
        wd_scr[...] = wd_ref[...].astype(BF16)

    @pl.when(used)
    def _():
        x = xs_ref[...].astype(BF16)
        gate = lax.dot_general(x, wg_scr[...], _NT, preferred_element_type=F32) + bg_ref[...]
        up = lax.dot_general(x, wu_scr[...], _NT, preferred_element_type=F32) + bu_ref[...]
        gate = jnp.minimum(gate, SWIGLU_LIMIT)
        up = jnp.clip(up, -SWIGLU_LIMIT, SWIGLU_LIMIT)
        glu = gate * jax.nn.sigmoid(gate * SWIGLU_ALPHA)
        mid = ((up + 1.0) * glu).astype(BF16)
        ys_ref[...] = _dot(mid, wd_scr[...]) + bd_ref[...]


def _moe_ffn(block_e, n_used, xs, wgu, wd, bg, bu, bd):
    p_rows, d = xs.shape
    f2 = wgu.shape[2]
    f = f2 // 2
    nblk = p_rows // MOE_ROWS
    rows = lambda i, be, nb: (jnp.minimum(i, nb[0] - 1), 0)
    wmap = lambda i, be, nb: (be[i], 0, 0)
    return pl.pallas_call(
        _ffn_kernel,
        out_shape=jax.ShapeDtypeStruct((p_rows, d), F32),
        grid_spec=pltpu.PrefetchScalarGridSpec(
            num_scalar_prefetch=2, grid=(nblk,),
            in_specs=[pl.BlockSpec((MOE_ROWS, d), rows),
                      pl.BlockSpec((None, d, f2), wmap),
                      pl.BlockSpec((None, f, d), wmap),
                      pl.BlockSpec((None, 1, f), wmap), pl.BlockSpec((None, 1, f), wmap),
                      pl.BlockSpec((None, 1, d), wmap)],
            out_specs=pl.BlockSpec((MOE_ROWS, d), lambda i, be, nb: (i, 0)),
            scratch_shapes=[pltpu.VMEM((d // LANES, f2, LANES), F32), pltpu.VMEM((f, d), BF16),
                            pltpu.VMEM((f, d), BF16), pltpu.VMEM((f, d), BF16)]),
        compiler_params=pltpu.CompilerParams(
            dimension_semantics=("arbitrary",), vmem_limit_bytes=VMEM_LIMIT),
        name="moe_ffn",
    )(block_e, n_used, xs, wgu, wd, bg, bu, bd)


def _gather_copy(ys_hbm, buf, s, k, t, sem):
    return pltpu.make_async_copy(ys_hbm.at[pl.ds(s, 1)], buf.at[k, pl.ds(t, 1)], sem)


def _combine_kernel(dest_ref, ys_hbm, x1_ref, route_ref, g2_ref, o_ref, buf, sem, *, tc):
    base = pl.program_id(0) * tc

    def issue(t, carry):
        for k in range(TOP_K):
            _gather_copy(ys_hbm, buf, dest_ref[(base + t) * TOP_K + k], k, t, sem).start()
        return carry

    lax.fori_loop(0, tc, issue, 0)

    for k in range(TOP_K):
        pltpu.make_async_copy(ys_hbm.at[pl.ds(0, tc)], buf.at[k], sem).wait()

    route = route_ref[...]
    y = buf[0] * route[:, 0:1]
    for k in range(1, TOP_K):
        y = y + buf[k] * route[:, k:k + 1]
    o_ref[...] = x1_ref[...] + g2_ref[...] * y


def _moe_combine(dest_flat, ys, x1, route, g2, seq):
    n, d = x1.shape
    tc = min(seq, 256)
    tpb = seq // tc
    kern = functools.partial(_combine_kernel, tc=tc)
    return pl.pallas_call(
        kern,
        out_shape=jax.ShapeDtypeStruct((n, d), F32),
        grid_spec=pltpu.PrefetchScalarGridSpec(
            num_scalar_prefetch=1, grid=(n // tc,),
            in_specs=[pl.BlockSpec(memory_space=pl.ANY),
                      pl.BlockSpec((tc, d), lambda i, dr: (i, 0)),
                      pl.BlockSpec((tc, LANES), lambda i, dr: (i, 0)),
                      pl.BlockSpec((None, 1, d), lambda i, dr: (i // tpb, 0, 0))],
            out_specs=pl.BlockSpec((tc, d), lambda i, dr: (i, 0)),
            scratch_shapes=[pltpu.VMEM((TOP_K, tc, d), F32), pltpu.SemaphoreType.DMA]),
        compiler_params=pltpu.CompilerParams(
            dimension_semantics=("arbitrary",), vmem_limit_bytes=VMEM_LIMIT),
        name="moe_combine",
    )(dest_flat, ys, x1, route, g2)


def _rope_tables(seq):
    inv_freq = ROPE_THETA ** (-jnp.arange(0, HEAD_DIM, 2, dtype=F32) / HEAD_DIM)
    ang = inv_freq[:, None] * jnp.arange(seq, dtype=F32)[None, :]
    cos_t = jnp.concatenate([jnp.cos(ang), jnp.cos(ang)], axis=0)
    sin_t = jnp.concatenate([-jnp.sin(ang), jnp.sin(ang)], axis=0)
    return cos_t, sin_t


def _layer(x, c, l, lambda_init, w_ada, b_ada, norm1_w, w_in, q_norm_w, k_norm_w, lambda_q1,
           lambda_k1, lambda_q2, lambda_k2, subln_w, w_attn_o, conv_w, w_conv_o, w_out, norm2_w,
           w_router, b_router, w_gate_up, b_gate_up, w_down, b_down):
    bsz, seq, d = x.shape
    n = bsz * seq
    x2 = x.reshape(n, d)

    mod = _ada_mod(c, w_ada[l], b_ada[l])
    sh1, sc1, g1, sh2, sc2, g2 = [m.reshape(bsz, 1, d) for m in jnp.split(mod, 6, axis=-1)]

    cos_t, sin_t = _rope_tables(seq)
    qt, k, vt, cb, z, sga, sgc, qn, kn = _in_proj(
        x2, norm1_w[l].reshape(1, d), sc1, sh1, w_in[l].astype(BF16),
        q_norm_w[l].reshape(HEAD_DIM, 1), k_norm_w[l].reshape(HEAD_DIM, 1), cos_t, sin_t, bsz, seq)

    lam_vecs = jnp.stack([lambda_q1[l], lambda_k1[l], lambda_q2[l], lambda_k2[l]]).astype(F32)
    o = _diff_attn(qt, k.reshape(bsz, seq, d), vt, qn, kn, lam_vecs, subln_w[l].reshape(V_DIM, 1),
                   lambda_init)

    wr = jnp.zeros((d, LANES), F32).at[:, :N_EXPERTS].set(w_router[l].astype(F32))
    wr_hi = wr.astype(BF16)
    wr_lo = (wr - wr_hi.astype(F32)).astype(BF16)
    br_pad = jnp.full((1, LANES), NEG_BIG, F32).at[0, :N_EXPERTS].set(b_router[l].astype(F32))
    x1, h2, route, cnt = _post_mix(
        o.reshape(n, d), cb, z, sga, sgc, x2, g1, sc2, sh2, norm2_w[l].reshape(1, d), conv_w[l],
        w_attn_o[l].astype(BF16), w_conv_o[l].astype(BF16), w_out[l].astype(BF16),
        wr_hi, wr_lo, br_pad, seq)

    counts = cnt[0, :N_EXPERTS].astype(jnp.int32)
    padded = ((counts + MOE_ROWS - 1) // MOE_ROWS) * MOE_ROWS
    ends = jnp.cumsum(padded)
    start = ends - padded
    p_rows = n * TOP_K + N_EXPERTS * MOE_ROWS
    nblk = p_rows // MOE_ROWS
    top_e = route[:, TOP_K:2 * TOP_K].astype(jnp.int32)
    rank = route[:, 2 * TOP_K:3 * TOP_K].astype(jnp.int32)
    dest = (start[top_e] + rank).reshape(-1)
    block_start = jnp.arange(nblk, dtype=jnp.int32) * MOE_ROWS
    block_e = jnp.sum((ends[None, :] <= block_start[:, None]).astype(jnp.int32), axis=1)
    block_e = jnp.minimum(block_e, N_EXPERTS - 1)
    n_used = (ends[-1:] // MOE_ROWS).astype(jnp.int32)

    xs = _moe_dispatch(dest, h2, p_rows)
    ys = _moe_ffn(block_e, n_used, xs, w_gate_up[l], w_down[l],
                  b_gate_up[l][:, None, 0::2], b_gate_up[l][:, None, 1::2], b_down[l][:, None, :])
    out = _moe_combine(dest, ys, x1, route, g2, seq)
    return out.reshape(bsz, seq, d)


def kernel(x, c, w_ada, b_ada, norm1_w, w_in, q_norm_w, k_norm_w, lambda_q1, lambda_k1, lambda_q2,
           lambda_k2, subln_w, w_attn_o, conv_w, w_conv_o, w_out, norm2_w, w_router, b_router,
           w_gate_up, b_gate_up, w_down, b_down):
    depth = w_ada.shape[0]
    for l in range(depth):
        lambda_init = 0.8 - 0.6 * math.exp(-0.3 * l)
        x = _layer(x, c, l, lambda_init, w_ada, b_ada, norm1_w, w_in, q_norm_w, k_norm_w,
                   lambda_q1, lambda_k1, lambda_q2, lambda_k2, subln_w, w_attn_o, conv_w,
                   w_conv_o, w_out, norm2_w, w_router, b_router, w_gate_up, b_gate_up,
                   w_down, b_down)
    return x
```

```python
import functools
import math

import jax
import jax.numpy as jnp
from jax import lax
from jax.experimental import pallas as pl
from jax.experimental.pallas import tpu as pltpu

N_HEADS = 8
HEAD_DIM = 64
V_DIM = 2 * HEAD_DIM
N_EXPERTS = 32
TOP_K = 4
SWIGLU_LIMIT = 7.0
SWIGLU_ALPHA = 1.702
ROPE_THETA = 10000.0
RMS_EPS = 1e-6
SUBLN_EPS = 1e-5
LANES = 128
MOE_ROWS = 256
NEG_BIG = -1e30
LOG2E = 1.4426950408889634
Q_SCALE = LOG2E / math.sqrt(HEAD_DIM)
SAFE_EXP2_BOUND = 80.0
VMEM_LIMIT = 56 * 1024 * 1024

F32 = jnp.float32
BF16 = jnp.bfloat16


def _dot(a, b):
    return jnp.dot(a, b, preferred_element_type=F32)


def _ada_kernel(ct_ref, w_ref, b_ref, o_ref):
    ct = ct_ref[...]
    s = ct * jax.nn.sigmoid(ct)
    w = w_ref[...]
    for b in range(ct.shape[1]):
        o_ref[b:b + 1, :] = jnp.sum(w * s[:, b:b + 1], axis=0, keepdims=True) + b_ref[...]


def _ada_mod(c, w_ada, b_ada):
    bsz, d = c.shape
    n = w_ada.shape[1]
    tn = min(n, 1536)
    return pl.pallas_call(
        _ada_kernel,
        out_shape=jax.ShapeDtypeStruct((bsz, n), F32),
        grid=(n // tn,),
        in_specs=[pl.BlockSpec((d, bsz), lambda j: (0, 0)),
                  pl.BlockSpec((d, tn), lambda j: (0, j)),
                  pl.BlockSpec((1, tn), lambda j: (0, j))],
        out_specs=pl.BlockSpec((bsz, tn), lambda j: (0, j)),
        compiler_params=pltpu.CompilerParams(dimension_semantics=("arbitrary",)),
        name="ada_mod",
    )(c.T, w_ada, b_ada.reshape(1, n))


def _qk_norm_rope_t(y, g_col, cos_t, sin_t):
    tm, w = y.shape
    yt = y.T.reshape(w // HEAD_DIM, HEAD_DIM, tm)
    ms = jnp.mean(yt * yt, axis=1, keepdims=True)
    yn = yt * lax.rsqrt(ms + RMS_EPS) * g_col[None]
    half = HEAD_DIM // 2
    swapped = jnp.concatenate([yn[:, half:, :], yn[:, :half, :]], axis=1)
    out = yn * cos_t[None] + swapped * sin_t[None]
    norm2 = jnp.sum(out * out, axis=1).reshape(N_HEADS, 2, tm)
    return out.reshape(w, tm), norm2


def _in_proj_kernel(x_ref, n1_ref, sc_ref, sh_ref, w_ref, gq_ref, gk_ref, cos_ref, sin_ref,
                    qt_ref, k_ref, vt_ref, cb_ref, z_ref, sga_ref, sgc_ref, qn_ref, kn_ref,
                    h_scr, y_a, y_b):
    j = pl.program_id(1)
    y_scr = (y_a, y_b)

    def finish_q(y):
        qt, qn = _qk_norm_rope_t(y, gq_ref[...], cos_ref[...], sin_ref[...])
        qt_ref[...] = qt.astype(BF16)
        qn_ref[...] = qn

    def finish_k(y):
        kt, kn = _qk_norm_rope_t(y, gk_ref[...], cos_ref[...], sin_ref[...])
        k_ref[...] = kt.T.astype(BF16)
        kn_ref[...] = kn

    def finish_v(y):
        vt_ref[...] = y.T.astype(BF16)

    def finish_cb(y):
        cb_ref[...] = y.astype(BF16)

    def finish_z(y):
        z_ref[...] = (y_scr[0][...] * y).astype(BF16)

    def finish_ga(y):
        sga_ref[...] = jax.nn.sigmoid(y).astype(BF16)

    def finish_gc(y):
        sgc_ref[...] = jax.nn.sigmoid(y).astype(BF16)

    finish = (finish_q, finish_k, finish_v, finish_cb, None, finish_z, finish_ga, finish_gc)
    n_blocks = len(finish)

    for c in range(n_blocks + 1):
        @pl.when(j == c)
        def _(c=c):
            if c == 0:
                x = x_ref[...]
                xn = x * lax.rsqrt(jnp.mean(x * x, axis=-1, keepdims=True) + RMS_EPS) * n1_ref[...]
                h_scr[...] = (xn * (1.0 + sc_ref[...]) + sh_ref[...]).astype(BF16)
            if c >= 1 and finish[c - 1] is not None:
                finish[c - 1](y_scr[(c - 1) % 2][...])
            if c < n_blocks:
                y_scr[c % 2][...] = _dot(h_scr[...], w_ref[...])


def _in_proj(x2, n1, sc1, sh1, w_in_bf, gq, gk, cos_t, sin_t, bsz, seq):
    n, d = x2.shape
    tm = min(seq, 512)
    tpb = seq // tm
    wcol = d
    nj = w_in_bf.shape[1] // wcol
    assert nj == 8
    row = lambda i, j: (i, 0)
    tcol = lambda i, j: (i // tpb, 0, i % tpb)
    mod = lambda i, j: (i // tpb, 0, 0)
    nat = jax.ShapeDtypeStruct((n, d), BF16)
    tr = jax.ShapeDtypeStruct((bsz, d, seq), BF16)
    nrm = jax.ShapeDtypeStruct((bsz, N_HEADS, 2, seq), F32)
    nrm_spec = pl.BlockSpec((None, N_HEADS, 2, tm), lambda i, j: (i // tpb, 0, 0, i % tpb))
    return pl.pallas_call(
        _in_proj_kernel,
        out_shape=(tr, nat, tr, nat, nat, nat, nat, nrm, nrm),
        grid=(n // tm, nj + 1),
        in_specs=[pl.BlockSpec((tm, d), row),
                  pl.BlockSpec((1, d), lambda i, j: (0, 0)),
                  pl.BlockSpec((None, 1, d), mod),
                  pl.BlockSpec((None, 1, d), mod),
                  pl.BlockSpec((d, wcol), lambda i, j: (0, jnp.minimum(j, nj - 1))),
                  pl.BlockSpec((HEAD_DIM, 1), lambda i, j: (0, 0)),
                  pl.BlockSpec((HEAD_DIM, 1), lambda i, j: (0, 0)),
                  pl.BlockSpec((HEAD_DIM, tm), lambda i, j: (0, i % tpb)),
                  pl.BlockSpec((HEAD_DIM, tm), lambda i, j: (0, i % tpb))],
        out_specs=(pl.BlockSpec((None, d, tm), tcol),
                   pl.BlockSpec((tm, d), row),
                   pl.BlockSpec((None, d, tm), tcol),
                   pl.BlockSpec((tm, d), row),
                   pl.BlockSpec((tm, d), row),
                   pl.BlockSpec((tm, d), row),
                   pl.BlockSpec((tm, d), row),
                   nrm_spec, nrm_spec),
        scratch_shapes=[pltpu.VMEM((tm, d), BF16), pltpu.VMEM((tm, d), F32),
                        pltpu.VMEM((tm, d), F32)],
        compiler_params=pltpu.CompilerParams(
            dimension_semantics=("arbitrary", "arbitrary"), vmem_limit_bytes=VMEM_LIMIT),
        name="in_proj",
    )(x2, n1, sc1, sh1, w_in_bf, gq, gk, cos_t, sin_t)


def _diff_attn_kernel(qt_ref, k_ref, vt_ref, qn_ref, kn_ref, lam_ref, sw_ref, o_ref, acc0, acc1,
                      *, tk, lambda_init):
    seq = k_ref.shape[0]
    tq = qt_ref.shape[1]
    qt = qt_ref[...]
    first = lax.broadcasted_iota(jnp.int32, qt.shape, 0) < HEAD_DIM
    zero = jnp.zeros_like(qt)
    qz = (jnp.where(first, qt, zero), jnp.where(first, zero, qt))
    accs = (acc0, acc1)
    acc0[...] = jnp.zeros_like(acc0)
    acc1[...] = jnp.zeros_like(acc1)
    n_chunks = seq // tk

    def load(j):
        off = pl.multiple_of(j * tk, tk)
        return k_ref[pl.ds(off, tk), :], vt_ref[:, pl.ds(off, tk)]

    def plain_body(j, carry):
        kk, vt = load(j)
        new = []
        for c in range(2):
            p = jnp.exp2(_dot(kk, qz[c]))
            new.append(carry[c] + jnp.sum(p, axis=0, keepdims=True))
            accs[c][...] += _dot(vt, p.astype(BF16))
        return tuple(new)

    def online_body(j, carry):
        kk, vt = load(j)
        new = []
        for c in range(2):
            m, l = carry[2 * c], carry[2 * c + 1]
            s = _dot(kk, qz[c])
            m_new = jnp.maximum(m, jnp.max(s, axis=0, keepdims=True))
            alpha = jnp.exp2(m - m_new)
            p = jnp.exp2(s - m_new)
            l = alpha * l + jnp.sum(p, axis=0, keepdims=True)
            accs[c][...] = alpha * accs[c][...] + _dot(vt, p.astype(BF16))
            new += [m_new, l]
        return tuple(new)

    m_init = jnp.full((1, tq), NEG_BIG, F32)
    l_init = jnp.zeros((1, tq), F32)

    def plain():
        return lax.fori_loop(0, n_chunks, plain_body, (l_init, l_init))

    def online():
        _, l0, _, l1 = lax.fori_loop(0, n_chunks, online_body, (m_init, l_init, m_init, l_init))
        return l0, l1

    bound2 = jnp.max(jnp.max(qn_ref[...], axis=-1, keepdims=True)
                     * jnp.max(kn_ref[...], axis=-1, keepdims=True))
    l0, l1 = lax.cond(bound2 <= SAFE_EXP2_BOUND * SAFE_EXP2_BOUND, plain, online)

    lq = lam_ref[...]
    lam = (jnp.exp(jnp.sum(lq[0:1] * lq[1:2], axis=-1, keepdims=True))
           - jnp.exp(jnp.sum(lq[2:3] * lq[3:4], axis=-1, keepdims=True)) + lambda_init)
    o = acc0[...] / l0 - lam * (acc1[...] / l1)
    o = o * lax.rsqrt(jnp.mean(o * o, axis=0, keepdims=True) + SUBLN_EPS)
    o = o * sw_ref[...] * (1.0 - lambda_init)
    o_ref[...] = o.T.astype(BF16)


def _diff_attn(qt, k3, vt, qn, kn, lam_vecs, subln_col, lambda_init):
    bsz, d, seq = qt.shape
    tq = min(seq, 1024)
    tk = min(seq, 4096)
    kern = functools.partial(_diff_attn_kernel, tk=tk, lambda_init=lambda_init)
    return pl.pallas_call(
        kern,
        out_shape=jax.ShapeDtypeStruct((bsz, seq, d), BF16),
        grid=(bsz, N_HEADS, seq // tq),
        in_specs=[pl.BlockSpec((None, V_DIM, tq), lambda b, h, i: (b, h, i)),
                  pl.BlockSpec((None, seq, V_DIM), lambda b, h, i: (b, 0, h)),
                  pl.BlockSpec((None, V_DIM, seq), lambda b, h, i: (b, h, 0)),
                  pl.BlockSpec((None, None, 2, tq), lambda b, h, i: (b, h, 0, i)),
                  pl.BlockSpec((None, None, 2, seq), lambda b, h, i: (b, h, 0, 0)),
                  pl.BlockSpec((4, HEAD_DIM), lambda b, h, i: (0, 0)),
                  pl.BlockSpec((V_DIM, 1), lambda b, h, i: (0, 0))],
        out_specs=pl.BlockSpec((None, tq, V_DIM), lambda b, h, i: (b, i, h)),
        scratch_shapes=[pltpu.VMEM((V_DIM, tq), F32), pltpu.VMEM((V_DIM, tq), F32)],
        compiler_params=pltpu.CompilerParams(
            dimension_semantics=("arbitrary", "arbitrary", "arbitrary"),
            vmem_limit_bytes=VMEM_LIMIT),
        name="diff_attn",
    )(qt, k3, vt, qn, kn, lam_vecs, subln_col)


def _post_mix_kernel(o_ref, cb_ref, z_ref, zp_ref, zn_ref, sga_ref, sgc_ref, x_ref,
                     g1_ref, sc2_ref, sh2_ref, n2_ref, cw_ref, wao_ref, wco_ref, wout_ref,
                     wrh_ref, wrl_ref, br_ref,
                     x1_ref, h2_ref, route_ref, cnt_ref, *, tpb):
    i = pl.program_id(0)
    tm = x_ref.shape[0]

    @pl.when(i == 0)
    def _():
        cnt_ref[...] = jnp.zeros_like(cnt_ref)

    z = z_ref[...].astype(F32)
    rows = lax.broadcasted_iota(jnp.int32, z.shape, 0)
    halo_rows = zp_ref.shape[0]
    prev_row = zp_ref[halo_rows - 1:halo_rows, :].astype(F32)
    next_row = zn_ref[0:1, :].astype(F32)
    prev_row = jnp.where(i % tpb == 0, jnp.zeros_like(prev_row), prev_row)
    next_row = jnp.where(i % tpb == tpb - 1, jnp.zeros_like(next_row), next_row)
    z_m1 = jnp.where(rows == 0, prev_row, pltpu.roll(z, 1, 0))
    z_p1 = jnp.where(rows == tm - 1, next_row, pltpu.roll(z, tm - 1, 0))
    cw = cw_ref[...]
    conv = z_m1 * cw[0:1] + z * cw[1:2] + z_p1 * cw[2:3]
    u = (cb_ref[...].astype(F32) * conv).astype(BF16)

    y_attn = _dot(o_ref[...], wao_ref[...])
    y_conv = _dot(u, wco_ref[...])
    m = sga_ref[...].astype(F32) * y_attn + sgc_ref[...].astype(F32) * y_conv
    x1 = x_ref[...] + g1_ref[...] * _dot(m.astype(BF16), wout_ref[...])
    x1_ref[...] = x1

    h2 = x1 * lax.rsqrt(jnp.mean(x1 * x1, axis=-1, keepdims=True) + RMS_EPS) * n2_ref[...]
    h2 = h2 * (1.0 + sc2_ref[...]) + sh2_ref[...]
    h2_ref[...] = h2

    h_hi = h2.astype(BF16)
    h_lo = (h2 - h_hi.astype(F32)).astype(BF16)
    logits = (_dot(h_hi, wrh_ref[...]) + _dot(h_lo, wrh_ref[...]) + _dot(h_hi, wrl_ref[...])
              + br_ref[...])

    lane = lax.broadcasted_iota(jnp.int32, logits.shape, 1)
    work = logits
    vals, idxs = [], []
    for _ in range(TOP_K):
        mx = jnp.max(work, axis=-1, keepdims=True)
        ix = jnp.min(jnp.where(work == mx, lane, LANES), axis=-1, keepdims=True)
        vals.append(mx)
        idxs.append(ix)
        work = jnp.where(lane == ix, 2.0 * NEG_BIG, work)
    exps = [jnp.exp(v - vals[0]) for v in vals]
    den = exps[0] + exps[1] + exps[2] + exps[3]

    sel = (work == 2.0 * NEG_BIG).astype(BF16)
    r_i = lax.broadcasted_iota(jnp.int32, (tm, tm), 0)
    c_i = lax.broadcasted_iota(jnp.int32, (tm, tm), 1)
    lower = (r_i > c_i).astype(BF16)
    before = _dot(lower, sel) + cnt_ref[0:1, :]
    cnt_ref[...] = cnt_ref[...] + jnp.sum(sel.astype(F32), axis=0, keepdims=True)

    route = jnp.zeros(logits.shape, F32)
    for k in range(TOP_K):
        rank = jnp.sum(jnp.where(lane == idxs[k], before, 0.0), axis=-1, keepdims=True)
        route = jnp.where(lane == k, exps[k] / den, route)
        route = jnp.where(lane == TOP_K + k, idxs[k].astype(F32), route)
        route = jnp.where(lane == 2 * TOP_K + k, rank, route)
    route_ref[...] = route


def _post_mix(o2, cb, z, sga, sgc, x2, g1, sc2, sh2, n2, conv_w, wao, wco, wout,
              wr_hi, wr_lo, br_pad, seq):
    n, d = x2.shape
    tm = min(seq, 512)
    tpb = seq // tm
    halo = 16
    hb = tm // halo
    last_hb = n // halo - 1
    row = lambda i: (i, 0)
    const = lambda i: (0, 0)
    mod = lambda i: (i // tpb, 0, 0)
    wspec = pl.BlockSpec((d, d), const)
    kern = functools.partial(_post_mix_kernel, tpb=tpb)
    return pl.pallas_call(
        kern,
        out_shape=(jax.ShapeDtypeStruct((n, d), F32), jax.ShapeDtypeStruct((n, d), F32),
                   jax.ShapeDtypeStruct((n, LANES), F32), jax.ShapeDtypeStruct((8, LANES), F32)),
        grid=(n // tm,),
        in_specs=[pl.BlockSpec((tm, d), row), pl.BlockSpec((tm, d), row), pl.BlockSpec((tm, d), row),
                  pl.BlockSpec((halo, d), lambda i: (jnp.maximum(i * hb - 1, 0), 0)),
                  pl.BlockSpec((halo, d), lambda i: (jnp.minimum((i + 1) * hb, last_hb), 0)),
                  pl.BlockSpec((tm, d), row), pl.BlockSpec((tm, d), row), pl.BlockSpec((tm, d), row),
                  pl.BlockSpec((None, 1, d), mod), pl.BlockSpec((None, 1, d), mod),
                  pl.BlockSpec((None, 1, d), mod),
                  pl.BlockSpec((1, d), const), pl.BlockSpec((3, d), const),
                  wspec, wspec, wspec,
                  pl.BlockSpec((d, LANES), const), pl.BlockSpec((d, LANES), const),
                  pl.BlockSpec((1, LANES), const)],
        out_specs=(pl.BlockSpec((tm, d), row), pl.BlockSpec((tm, d), row),
                   pl.BlockSpec((tm, LANES), row), pl.BlockSpec((8, LANES), const)),
        compiler_params=pltpu.CompilerParams(
            dimension_semantics=("arbitrary",), vmem_limit_bytes=VMEM_LIMIT),
        name="post_mix",
    )(o2, cb, z, z, z, sga, sgc, x2, g1, sc2, sh2, n2, conv_w, wao, wco, wout, wr_hi, wr_lo, br_pad)


def _row_copy(src_hbm, dst_hbm, s, t, sem):
    return pltpu.make_async_copy(src_hbm.at[pl.ds(s, 1)], dst_hbm.at[pl.ds(t, 1)], sem)


def _dispatch_kernel(dest_ref, ends_ref, h2_ref, xs_hbm, zbuf, sem, zsem, *, tc, nblk):
    base = pl.program_id(0) * tc

    @pl.when(pl.program_id(0) == 0)
    def _():
        zbuf[...] = jnp.zeros_like(zbuf)

        def zero_block(row):
            row = pl.multiple_of(row, MOE_ROWS)
            return pltpu.make_async_copy(zbuf, xs_hbm.at[pl.ds(row, MOE_ROWS)], zsem)

        def nonempty(e):
            return ends_ref[e] > (ends_ref[e - 1] if e else 0)

        total = ends_ref[N_EXPERTS - 1]
        n_tail = nblk - total // MOE_ROWS

        def tail_start(b, carry):
            zero_block(total + b * MOE_ROWS).start()
            return carry

        def tail_wait(b, carry):
            zero_block(0).wait()
            return carry

        for e in range(N_EXPERTS):
            @pl.when(nonempty(e))
            def _(e=e):
                zero_block(ends_ref[e] - MOE_ROWS).start()
        lax.fori_loop(0, n_tail, tail_start, 0)
        for e in range(N_EXPERTS):
            @pl.when(nonempty(e))
            def _():
                zero_block(0).wait()
        lax.fori_loop(0, n_tail, tail_wait, 0)

    def issue(t, carry):
        for k in range(TOP_K):
            _row_copy(h2_ref, xs_hbm, t, dest_ref[(base + t) * TOP_K + k], sem).start()
        return carry

    lax.fori_loop(0, tc, issue, 0, unroll=4)

    for _ in range(TOP_K):
        pltpu.make_async_copy(h2_ref, xs_hbm.at[pl.ds(0, tc)], sem).wait()


def _moe_dispatch(dest_flat, ends, h2, p_rows):
    n, d = h2.shape
    tc = min(n, 512)
    kern = functools.partial(_dispatch_kernel, tc=tc, nblk=p_rows // MOE_ROWS)
    return pl.pallas_call(
        kern,
        out_shape=jax.ShapeDtypeStruct((p_rows, d), F32),
        grid_spec=pltpu.PrefetchScalarGridSpec(
            num_scalar_prefetch=2, grid=(n // tc,),
            in_specs=[pl.BlockSpec((tc, d), lambda i, dr, en: (i, 0))],
            out_specs=pl.BlockSpec(memory_space=pl.ANY),
            scratch_shapes=[pltpu.VMEM((MOE_ROWS, d), F32), pltpu.SemaphoreType.DMA,
                            pltpu.SemaphoreType.DMA]),
        compiler_params=pltpu.CompilerParams(dimension_semantics=("arbitrary",)),
        name="moe_dispatch",
    )(dest_flat, ends, h2)


_NT = (((1,), (1,)), ((), ()))


def _ffn_kernel(be_ref, nb_ref, xs_ref, wgu_ref, wd_ref, bg_ref, bu_ref, bd_ref, ys_ref,
                wt_scr, wg_scr, wu_scr, wd_scr):
    i = pl.program_id(0)
    used = i < nb_ref[0]
    fresh = jnp.logical_or(i == 0, be_ref[i] != be_ref[jnp.maximum(i - 1, 0)])

    @pl.when(jnp.logical_not(used))
    def _():
        ys_ref[...] = jnp.zeros_like(ys_ref)

    @pl.when(jnp.logical_and(used, fresh))
    def _():
        d, f2 = wgu_ref.shape
        for c in range(d // LANES):
            cols = slice(c * LANES, (c + 1) * LANES)
            wt_scr[c] = wgu_ref[cols, :].T
            wg_scr[:, cols] = wt_scr[c, pl.ds(0, f2 // 2, stride=2), :].astype(BF16)
            wu_scr[:, cols] = wt_scr[c, pl.ds(1, f2 // 2, stride=2), :].astype(BF16)
        wd_scr[...] = wd_ref[...].astype(BF16)

    @pl.when(used)
    def _():
        x = xs_ref[...].astype(BF16)
        gate = lax.dot_general(x, wg_scr[...], _NT, preferred_element_type=F32) + bg_ref[...]
        up = lax.dot_general(x, wu_scr[...], _NT, preferred_element_type=F32) + bu_ref[...]
        gate = jnp.minimum(gate, SWIGLU_LIMIT)
        up = jnp.clip(up, -SWIGLU_LIMIT, SWIGLU_LIMIT)
        glu = gate * jax.nn.sigmoid(gate * SWIGLU_ALPHA)
        mid = ((up + 1.0) * glu).astype(BF16)
        ys_ref[...] = _dot(mid, wd_scr[...]) + bd_ref[...]


def _moe_ffn(block_e, n_used, xs, wgu, wd, bg, bu, bd):
    p_rows, d = xs.shape
    f2 = wgu.shape[2]
    f = f2 // 2
    nblk = p_rows // MOE_ROWS
    rows = lambda i, be, nb: (jnp.minimum(i, nb[0] - 1), 0)
    wmap = lambda i, be, nb: (be[i], 0, 0)
    return pl.pallas_call(
        _ffn_kernel,
        out_shape=jax.ShapeDtypeStruct((p_rows, d), F32),
        grid_spec=pltpu.PrefetchScalarGridSpec(
            num_scalar_prefetch=2, grid=(nblk,),
            in_specs=[pl.BlockSpec((MOE_ROWS, d), rows),
                      pl.BlockSpec((None, d, f2), wmap),
                      pl.BlockSpec((None, f, d), wmap),
                      pl.BlockSpec((None, 1, f), wmap), pl.BlockSpec((None, 1, f), wmap),
                      pl.BlockSpec((None, 1, d), wmap)],
            out_specs=pl.BlockSpec((MOE_ROWS, d), lambda i, be, nb: (i, 0)),
            scratch_shapes=[pltpu.VMEM((d // LANES, f2, LANES), F32), pltpu.VMEM((f, d), BF16),
                            pltpu.VMEM((f, d), BF16), pltpu.VMEM((f, d), BF16)]),
        compiler_params=pltpu.CompilerParams(
            dimension_semantics=("arbitrary",), vmem_limit_bytes=VMEM_LIMIT),
        name="moe_ffn",
    )(block_e, n_used, xs, wgu, wd, bg, bu, bd)


def _gather_copy(ys_hbm, buf, s, k, t, sem):
    return pltpu.make_async_copy(ys_hbm.at[pl.ds(s, 1)], buf.at[k, pl.ds(t, 1)], sem)


def _combine_kernel(dest_ref, ys_hbm, x1_ref, route_ref, g2_ref, o_ref, buf, sem, *, tc):
    base = pl.program_id(0) * tc

    def issue(t, carry):
        for k in range(TOP_K):
            _gather_copy(ys_hbm, buf, dest_ref[(base + t) * TOP_K + k], k, t, sem).start()
        return carry

    lax.fori_loop(0, tc, issue, 0, unroll=4)

    for k in range(TOP_K):
        pltpu.make_async_copy(ys_hbm.at[pl.ds(0, tc)], buf.at[k], sem).wait()

    route = route_ref[...]
    y = buf[0] * route[:, 0:1]
    for k in range(1, TOP_K):
        y = y + buf[k] * route[:, k:k + 1]
    o_ref[...] = x1_ref[...] + g2_ref[...] * y


def _moe_combine(dest_flat, ys, x1, route, g2, seq):
    n, d = x1.shape
    tc = min(seq, 256)
    tpb = seq // tc
    kern = functools.partial(_combine_kernel, tc=tc)
    return pl.pallas_call(
        kern,
        out_shape=jax.ShapeDtypeStruct((n, d), F32),
        grid_spec=pltpu.PrefetchScalarGridSpec(
            num_scalar_prefetch=1, grid=(n // tc,),
            in_specs=[pl.BlockSpec(memory_space=pl.ANY),
                      pl.BlockSpec((tc, d), lambda i, dr: (i, 0)),
                      pl.BlockSpec((tc, LANES), lambda i, dr: (i, 0)),
                      pl.BlockSpec((None, 1, d), lambda i, dr: (i // tpb, 0, 0))],
            out_specs=pl.BlockSpec((tc, d), lambda i, dr: (i, 0)),
            scratch_shapes=[pltpu.VMEM((TOP_K, tc, d), F32), pltpu.SemaphoreType.DMA]),
        compiler_params=pltpu.CompilerParams(
            dimension_semantics=("arbitrary",), vmem_limit_bytes=VMEM_LIMIT),
        name="moe_combine",
    )(dest_flat, ys, x1, route, g2)


def _rope_tables(seq):
    inv_freq = ROPE_THETA ** (-jnp.arange(0, HEAD_DIM, 2, dtype=F32) / HEAD_DIM)
    ang = inv_freq[:, None] * jnp.arange(seq, dtype=F32)[None, :]
    cos_t = jnp.concatenate([jnp.cos(ang), jnp.cos(ang)], axis=0)
    sin_t = jnp.concatenate([-jnp.sin(ang), jnp.sin(ang)], axis=0)
    return cos_t, sin_t


def _layer(x, c, l, lambda_init, w_ada, b_ada, norm1_w, w_in, q_norm_w, k_norm_w, lambda_q1,
           lambda_k1, lambda_q2, lambda_k2, subln_w, w_attn_o, conv_w, w_conv_o, w_out, norm2_w,
           w_router, b_router, w_gate_up, b_gate_up, w_down, b_down):
    bsz, seq, d = x.shape
    n = bsz * seq
    x2 = x.reshape(n, d)

    mod = _ada_mod(c, w_ada[l], b_ada[l])
    sh1, sc1, g1, sh2, sc2, g2 = [m.reshape(bsz, 1, d) for m in jnp.split(mod, 6, axis=-1)]

    cos_t, sin_t = _rope_tables(seq)
    qt, k, vt, cb, z, sga, sgc, qn, kn = _in_proj(
        x2, norm1_w[l].reshape(1, d), sc1, sh1, w_in[l].astype(BF16),
        q_norm_w[l].reshape(HEAD_DIM, 1) * Q_SCALE, k_norm_w[l].reshape(HEAD_DIM, 1),
        cos_t, sin_t, bsz, seq)

    lam_vecs = jnp.stack([lambda_q1[l], lambda_k1[l], lambda_q2[l], lambda_k2[l]]).astype(F32)
    o = _diff_attn(qt, k.reshape(bsz, seq, d), vt, qn, kn, lam_vecs, subln_w[l].reshape(V_DIM, 1),
                   lambda_init)

    wr = jnp.zeros((d, LANES), F32).at[:, :N_EXPERTS].set(w_router[l].astype(F32))
    wr_hi = wr.astype(BF16)
    wr_lo = (wr - wr_hi.astype(F32)).astype(BF16)
    br_pad = jnp.full((1, LANES), NEG_BIG, F32).at[0, :N_EXPERTS].set(b_router[l].astype(F32))
    x1, h2, route, cnt = _post_mix(
        o.reshape(n, d), cb, z, sga, sgc, x2, g1, sc2, sh2, norm2_w[l].reshape(1, d), conv_w[l],
        w_attn_o[l].astype(BF16), w_conv_o[l].astype(BF16), w_out[l].astype(BF16),
        wr_hi, wr_lo, br_pad, seq)

    counts = cnt[0, :N_EXPERTS].astype(jnp.int32)
    padded = ((counts + MOE_ROWS - 1) // MOE_ROWS) * MOE_ROWS
    ends = jnp.cumsum(padded)
    start = ends - padded
    p_rows = n * TOP_K + N_EXPERTS * MOE_ROWS
    nblk = p_rows // MOE_ROWS
    top_e = route[:, TOP_K:2 * TOP_K].astype(jnp.int32)
    rank = route[:, 2 * TOP_K:3 * TOP_K].astype(jnp.int32)
    dest = (start[top_e] + rank).reshape(-1)
    block_start = jnp.arange(nblk, dtype=jnp.int32) * MOE_ROWS
    block_e = jnp.sum((ends[None, :] <= block_start[:, None]).astype(jnp.int32), axis=1)
    block_e = jnp.minimum(block_e, N_EXPERTS - 1)
    n_used = (ends[-1:] // MOE_ROWS).astype(jnp.int32)

    xs = _moe_dispatch(dest, ends.astype(jnp.int32), h2, p_rows)
    ys = _moe_ffn(block_e, n_used, xs, w_gate_up[l], w_down[l],
                  b_gate_up[l][:, None, 0::2], b_gate_up[l][:, None, 1::2], b_down[l][:, None, :])
    out = _moe_combine(dest, ys, x1, route, g2, seq)
    return out.reshape(bsz, seq, d)


def kernel(x, c, w_ada, b_ada, norm1_w, w_in, q_norm_w, k_norm_w, lambda_q1, lambda_k1, lambda_q2,
           lambda_k2, subln_w, w_attn_o, conv_w, w_conv_o, w_out, norm2_w, w_router, b_router,
           w_gate_up, b_gate_up, w_down, b_down):
    depth = w_ada.shape[0]
    for l in range(depth):
        lambda_init = 0.8 - 0.6 * math.exp(-0.3 * l)
        x = _layer(x, c, l, lambda_init, w_ada, b_ada, norm1_w, w_in, q_norm_w, k_norm_w,
                   lambda_q1, lambda_k1, lambda_q2, lambda_k2, subln_w, w_attn_o, conv_w,
                   w_conv_o, w_out, norm2_w, w_router, b_router, w_gate_up, b_gate_up,
                   w_down, b_down)
    return x
```

```python
import functools
import math

import jax
import jax.numpy as jnp
from jax import lax
from jax.experimental import pallas as pl
from jax.experimental.pallas import tpu as pltpu

N_HEADS = 8
HEAD_DIM = 64
V_DIM = 2 * HEAD_DIM
N_EXPERTS = 32
TOP_K = 4
SWIGLU_LIMIT = 7.0
SWIGLU_ALPHA = 1.702
ROPE_THETA = 10000.0
RMS_EPS = 1e-6
SUBLN_EPS = 1e-5
LANES = 128
MOE_ROWS = 256
NEG_BIG = -1e30
LOG2E = 1.4426950408889634
Q_SCALE = LOG2E / math.sqrt(HEAD_DIM)
SAFE_EXP2_BOUND = 80.0
VMEM_LIMIT = 56 * 1024 * 1024

F32 = jnp.float32
BF16 = jnp.bfloat16


def _dot(a, b):
    return jnp.dot(a, b, preferred_element_type=F32)


def _ada_kernel(ct_ref, w_ref, b_ref, o_ref):
    ct = ct_ref[...]
    s = ct * jax.nn.sigmoid(ct)
    w = w_ref[...]
    for b in range(ct.shape[1]):
        o_ref[b:b + 1, :] = jnp.sum(w * s[:, b:b + 1], axis=0, keepdims=True) + b_ref[...]


def _ada_mod(c, w_ada, b_ada):
    bsz, d = c.shape
    n = w_ada.shape[1]
    tn = min(n, 1536)
    return pl.pallas_call(
        _ada_kernel,
        out_shape=jax.ShapeDtypeStruct((bsz, n), F32),
        grid=(n // tn,),
        in_specs=[pl.BlockSpec((d, bsz), lambda j: (0, 0)),
                  pl.BlockSpec((d, tn), lambda j: (0, j)),
                  pl.BlockSpec((1, tn), lambda j: (0, j))],
        out_specs=pl.BlockSpec((bsz, tn), lambda j: (0, j)),
        compiler_params=pltpu.CompilerParams(dimension_semantics=("arbitrary",)),
        name="ada_mod",
    )(c.T, w_ada, b_ada.reshape(1, n))


def _qk_norm_rope_t(y, g_col, cos_t, sin_t):
    tm, w = y.shape
    yt = y.T.reshape(w // HEAD_DIM, HEAD_DIM, tm)
    ms = jnp.mean(yt * yt, axis=1, keepdims=True)
    yn = yt * lax.rsqrt(ms + RMS_EPS) * g_col[None]
    half = HEAD_DIM // 2
    swapped = jnp.concatenate([yn[:, half:, :], yn[:, :half, :]], axis=1)
    out = yn * cos_t[None] + swapped * sin_t[None]
    norm2 = jnp.sum(out * out, axis=1).reshape(N_HEADS, 2, tm)
    return out.reshape(w, tm), norm2


def _in_proj_kernel(x_ref, n1_ref, sc_ref, sh_ref, w_ref, gq_ref, gk_ref, cos_ref, sin_ref,
                    qt_ref, k_ref, vt_ref, cb_ref, z_ref, sga_ref, sgc_ref, qn_ref, kn_ref,
                    h_scr, y_a, y_b):
    j = pl.program_id(1)
    y_scr = (y_a, y_b)

    def finish_q(y):
        qt, qn = _qk_norm_rope_t(y, gq_ref[...], cos_ref[...], sin_ref[...])
        qt_ref[...] = qt.astype(BF16)
        qn_ref[...] = qn

    def finish_k(y):
        kt, kn = _qk_norm_rope_t(y, gk_ref[...], cos_ref[...], sin_ref[...])
        k_ref[...] = kt.T.astype(BF16)
        kn_ref[...] = kn

    def finish_v(y):
        vt_ref[...] = y.T.astype(BF16)

    def finish_cb(y):
        cb_ref[...] = y.astype(BF16)

    def finish_z(y):
        z_ref[...] = (y_scr[0][...] * y).astype(BF16)

    def finish_ga(y):
        sga_ref[...] = jax.nn.sigmoid(y).astype(BF16)

    def finish_gc(y):
        sgc_ref[...] = jax.nn.sigmoid(y).astype(BF16)

    finish = (finish_q, finish_k, finish_v, finish_cb, None, finish_z, finish_ga, finish_gc)
    n_blocks = len(finish)

    for c in range(n_blocks + 1):
        @pl.when(j == c)
        def _(c=c):
            if c == 0:
                x = x_ref[...]
                xn = x * lax.rsqrt(jnp.mean(x * x, axis=-1, keepdims=True) + RMS_EPS) * n1_ref[...]
                h_scr[...] = (xn * (1.0 + sc_ref[...]) + sh_ref[...]).astype(BF16)
            if c >= 1 and finish[c - 1] is not None:
                finish[c - 1](y_scr[(c - 1) % 2][...])
            if c < n_blocks:
                y_scr[c % 2][...] = _dot(h_scr[...], w_ref[...])


def _in_proj(x2, n1, sc1, sh1, w_in_bf, gq, gk, cos_t, sin_t, bsz, seq):
    n, d = x2.shape
    tm = min(seq, 512)
    tpb = seq // tm
    wcol = d
    nj = w_in_bf.shape[1] // wcol
    assert nj == 8
    n_row = n // tm
    mod = lambda i, j: (i // tpb, 0, 0)
    nat = jax.ShapeDtypeStruct((n, d), BF16)
    tr = jax.ShapeDtypeStruct((bsz, d, seq), BF16)
    nrm = jax.ShapeDtypeStruct((bsz, N_HEADS, 2, seq), F32)

    def tile_at(w):
        return lambda i, j: jnp.where(j >= w, i, jnp.maximum(i - 1, 0))

    def nat_spec(w):
        t = tile_at(w)
        return pl.BlockSpec((tm, d), lambda i, j: (t(i, j), 0))

    def tr_spec(w):
        t = tile_at(w)
        return pl.BlockSpec((None, d, tm), lambda i, j: (t(i, j) // tpb, 0, t(i, j) % tpb))

    def nrm_spec(w):
        t = tile_at(w)
        return pl.BlockSpec((None, N_HEADS, 2, tm),
                            lambda i, j: (t(i, j) // tpb, 0, 0, t(i, j) % tpb))

    x_spec = pl.BlockSpec((tm, d), lambda i, j: (jnp.where(j == 0, i, jnp.minimum(i + 1, n_row - 1)), 0))
    return pl.pallas_call(
        _in_proj_kernel,
        out_shape=(tr, nat, tr, nat, nat, nat, nat, nrm, nrm),
        grid=(n_row, nj + 1),
        in_specs=[x_spec,
                  pl.BlockSpec((1, d), lambda i, j: (0, 0)),
                  pl.BlockSpec((None, 1, d), mod),
                  pl.BlockSpec((None, 1, d), mod),
                  pl.BlockSpec((d, wcol), lambda i, j: (0, jnp.minimum(j, nj - 1))),
                  pl.BlockSpec((HEAD_DIM, 1), lambda i, j: (0, 0)),
                  pl.BlockSpec((HEAD_DIM, 1), lambda i, j: (0, 0)),
                  pl.BlockSpec((HEAD_DIM, tm), lambda i, j: (0, i % tpb)),
                  pl.BlockSpec((HEAD_DIM, tm), lambda i, j: (0, i % tpb))],
        out_specs=(tr_spec(1), nat_spec(2), tr_spec(3), nat_spec(4), nat_spec(6), nat_spec(7),
                   nat_spec(8), nrm_spec(1), nrm_spec(2)),
        scratch_shapes=[pltpu.VMEM((tm, d), BF16), pltpu.VMEM((tm, d), F32),
                        pltpu.VMEM((tm, d), F32)],
        compiler_params=pltpu.CompilerParams(
            dimension_semantics=("arbitrary", "arbitrary"), vmem_limit_bytes=VMEM_LIMIT),
        name="in_proj",
    )(x2, n1, sc1, sh1, w_in_bf, gq, gk, cos_t, sin_t)


def _diff_attn_kernel(qt_ref, k_ref, vt_ref, qn_ref, kn_ref, lam_ref, sw_ref, o_ref, acc0, acc1,
                      *, tk, lambda_init):
    seq = k_ref.shape[0]
    tq = qt_ref.shape[1]
    qt = qt_ref[...]
    first = lax.broadcasted_iota(jnp.int32, qt.shape, 0) < HEAD_DIM
    zero = jnp.zeros_like(qt)
    qz = (jnp.where(first, qt, zero), jnp.where(first, zero, qt))
    accs = (acc0, acc1)
    acc0[...] = jnp.zeros_like(acc0)
    acc1[...] = jnp.zeros_like(acc1)
    n_chunks = seq // tk

    def load(j):
        off = pl.multiple_of(j * tk, tk)
        return k_ref[pl.ds(off, tk), :], vt_ref[:, pl.ds(off, tk)]

    def plain_body(j, carry):
        kk, vt = load(j)
        new = []
        for c in range(2):
            p = jnp.exp2(_dot(kk, qz[c]))
            new.append(carry[c] + jnp.sum(p, axis=0, keepdims=True))
            accs[c][...] += _dot(vt, p.astype(BF16))
        return tuple(new)

    def online_body(j, carry):
        kk, vt = load(j)
        new = []
        for c in range(2):
            m, l = carry[2 * c], carry[2 * c + 1]
            s = _dot(kk, qz[c])
            m_new = jnp.maximum(m, jnp.max(s, axis=0, keepdims=True))
            alpha = jnp.exp2(m - m_new)
            p = jnp.exp2(s - m_new)
            l = alpha * l + jnp.sum(p, axis=0, keepdims=True)
            accs[c][...] = alpha * accs[c][...] + _dot(vt, p.astype(BF16))
            new += [m_new, l]
        return tuple(new)

    m_init = jnp.full((1, tq), NEG_BIG, F32)
    l_init = jnp.zeros((1, tq), F32)

    def plain():
        return lax.fori_loop(0, n_chunks, plain_body, (l_init, l_init))

    def online():
        _, l0, _, l1 = lax.fori_loop(0, n_chunks, online_body, (m_init, l_init, m_init, l_init))
        return l0, l1

    bound2 = jnp.max(jnp.max(qn_ref[...], axis=-1, keepdims=True)
                     * jnp.max(kn_ref[...], axis=-1, keepdims=True))
    l0, l1 = lax.cond(bound2 <= SAFE_EXP2_BOUND * SAFE_EXP2_BOUND, plain, online)

    lq = lam_ref[...]
    lam = (jnp.exp(jnp.sum(lq[0:1] * lq[1:2], axis=-1, keepdims=True))
           - jnp.exp(jnp.sum(lq[2:3] * lq[3:4], axis=-1, keepdims=True)) + lambda_init)
    o = acc0[...] / l0 - lam * (acc1[...] / l1)
    o = o * lax.rsqrt(jnp.mean(o * o, axis=0, keepdims=True) + SUBLN_EPS)
    o = o * sw_ref[...] * (1.0 - lambda_init)
    o_ref[...] = o.T.astype(BF16)


def _diff_attn(qt, k3, vt, qn, kn, lam_vecs, subln_col, lambda_init):
    bsz, d, seq = qt.shape
    tq = min(seq, 1024)
    tk = min(seq, 4096)
    kern = functools.partial(_diff_attn_kernel, tk=tk, lambda_init=lambda_init)
    return pl.pallas_call(
        kern,
        out_shape=jax.ShapeDtypeStruct((bsz, seq, d), BF16),
        grid=(bsz, N_HEADS, seq // tq),
        in_specs=[pl.BlockSpec((None, V_DIM, tq), lambda b, h, i: (b, h, i)),
                  pl.BlockSpec((None, seq, V_DIM), lambda b, h, i: (b, 0, h)),
                  pl.BlockSpec((None, V_DIM, seq), lambda b, h, i: (b, h, 0)),
                  pl.BlockSpec((None, None, 2, tq), lambda b, h, i: (b, h, 0, i)),
                  pl.BlockSpec((None, None, 2, seq), lambda b, h, i: (b, h, 0, 0)),
                  pl.BlockSpec((4, HEAD_DIM), lambda b, h, i: (0, 0)),
                  pl.BlockSpec((V_DIM, 1), lambda b, h, i: (0, 0))],
        out_specs=pl.BlockSpec((None, tq, V_DIM), lambda b, h, i: (b, i, h)),
        scratch_shapes=[pltpu.VMEM((V_DIM, tq), F32), pltpu.VMEM((V_DIM, tq), F32)],
        compiler_params=pltpu.CompilerParams(
            dimension_semantics=("arbitrary", "arbitrary", "arbitrary"),
            vmem_limit_bytes=VMEM_LIMIT),
        name="diff_attn",
    )(qt, k3, vt, qn, kn, lam_vecs, subln_col)


def _post_mix_kernel(o_ref, cb_ref, z_ref, zp_ref, zn_ref, sga_ref, sgc_ref, x_ref,
                     g1_ref, sc2_ref, sh2_ref, n2_ref, cw_ref, wao_ref, wco_ref, wout_ref,
                     wrh_ref, wrl_ref, br_ref,
                     x1_ref, h2_ref, route_ref, cnt_ref, u_scr, *, tpb, n_sub):
    i = pl.program_id(0)
    tm = x_ref.shape[0]

    @pl.when(i == 0)
    def _():
        cnt_ref[...] = jnp.zeros_like(cnt_ref)

    z = z_ref[...].astype(F32)
    rows = lax.broadcasted_iota(jnp.int32, z.shape, 0)
    halo_rows = zp_ref.shape[0]
    prev_row = zp_ref[halo_rows - 1:halo_rows, :].astype(F32)
    next_row = zn_ref[0:1, :].astype(F32)
    prev_row = jnp.where(i % tpb == 0, jnp.zeros_like(prev_row), prev_row)
    next_row = jnp.where(i % tpb == tpb - 1, jnp.zeros_like(next_row), next_row)
    z_m1 = jnp.where(rows == 0, prev_row, pltpu.roll(z, 1, 0))
    z_p1 = jnp.where(rows == tm - 1, next_row, pltpu.roll(z, tm - 1, 0))
    cw = cw_ref[...]
    conv = z_m1 * cw[0:1] + z * cw[1:2] + z_p1 * cw[2:3]
    u_scr[...] = (cb_ref[...].astype(F32) * conv).astype(BF16)

    ts = tm // n_sub
    lane = lax.broadcasted_iota(jnp.int32, (ts, LANES), 1)
    r_i = lax.broadcasted_iota(jnp.int32, (ts, ts), 0)
    c_i = lax.broadcasted_iota(jnp.int32, (ts, ts), 1)
    lower = (r_i > c_i).astype(BF16)
    counts = cnt_ref[0:1, :]
    for s in range(n_sub):
        rs = slice(s * ts, (s + 1) * ts)
        y_attn = _dot(o_ref[rs, :], wao_ref[...])
        y_conv = _dot(u_scr[rs, :], wco_ref[...])
        m = sga_ref[rs, :].astype(F32) * y_attn + sgc_ref[rs, :].astype(F32) * y_conv
        x1 = x_ref[rs, :] + g1_ref[...] * _dot(m.astype(BF16), wout_ref[...])
        x1_ref[rs, :] = x1

        h2 = x1 * lax.rsqrt(jnp.mean(x1 * x1, axis=-1, keepdims=True) + RMS_EPS) * n2_ref[...]
        h2 = h2 * (1.0 + sc2_ref[...]) + sh2_ref[...]
        h2_ref[rs, :] = h2

        h_hi = h2.astype(BF16)
        h_lo = (h2 - h_hi.astype(F32)).astype(BF16)
        logits = (_dot(h_hi, wrh_ref[...]) + _dot(h_lo, wrh_ref[...]) + _dot(h_hi, wrl_ref[...])
                  + br_ref[...])

        work = logits
        vals, idxs = [], []
        for _ in range(TOP_K):
            mx = jnp.max(work, axis=-1, keepdims=True)
            ix = jnp.min(jnp.where(work == mx, lane, LANES), axis=-1, keepdims=True)
            vals.append(mx)
            idxs.append(ix)
            work = jnp.where(lane == ix, 2.0 * NEG_BIG, work)
        exps = [jnp.exp(v - vals[0]) for v in vals]
        den = exps[0] + exps[1] + exps[2] + exps[3]

        sel = (work == 2.0 * NEG_BIG).astype(BF16)
        before = _dot(lower, sel) + counts
        counts = counts + jnp.sum(sel.astype(F32), axis=0, keepdims=True)

        route = jnp.zeros(logits.shape, F32)
        for k in range(TOP_K):
            rank = jnp.sum(jnp.where(lane == idxs[k], before, 0.0), axis=-1, keepdims=True)
            route = jnp.where(lane == k, exps[k] / den, route)
            route = jnp.where(lane == TOP_K + k, idxs[k].astype(F32), route)
            route = jnp.where(lane == 2 * TOP_K + k, rank, route)
        route_ref[rs, :] = route
    cnt_ref[...] = jnp.broadcast_to(counts, cnt_ref.shape)


def _post_mix(o2, cb, z, sga, sgc, x2, g1, sc2, sh2, n2, conv_w, wao, wco, wout,
              wr_hi, wr_lo, br_pad, seq):
    n, d = x2.shape
    tm = min(seq, 512)
    tpb = seq // tm
    halo = 16
    hb = tm // halo
    last_hb = n // halo - 1
    row = lambda i: (i, 0)
    const = lambda i: (0, 0)
    mod = lambda i: (i // tpb, 0, 0)
    wspec = pl.BlockSpec((d, d), const)
    kern = functools.partial(_post_mix_kernel, tpb=tpb, n_sub=1)
    return pl.pallas_call(
        kern,
        out_shape=(jax.ShapeDtypeStruct((n, d), F32), jax.ShapeDtypeStruct((n, d), F32),
                   jax.ShapeDtypeStruct((n, LANES), F32), jax.ShapeDtypeStruct((8, LANES), F32)),
        grid=(n // tm,),
        in_specs=[pl.BlockSpec((tm, d), row), pl.BlockSpec((tm, d), row), pl.BlockSpec((tm, d), row),
                  pl.BlockSpec((halo, d), lambda i: (jnp.maximum(i * hb - 1, 0), 0)),
                  pl.BlockSpec((halo, d), lambda i: (jnp.minimum((i + 1) * hb, last_hb), 0)),
                  pl.BlockSpec((tm, d), row), pl.BlockSpec((tm, d), row), pl.BlockSpec((tm, d), row),
                  pl.BlockSpec((None, 1, d), mod), pl.BlockSpec((None, 1, d), mod),
                  pl.BlockSpec((None, 1, d), mod),
                  pl.BlockSpec((1, d), const), pl.BlockSpec((3, d), const),
                  wspec, wspec, wspec,
                  pl.BlockSpec((d, LANES), const), pl.BlockSpec((d, LANES), const),
                  pl.BlockSpec((1, LANES), const)],
        out_specs=(pl.BlockSpec((tm, d), row), pl.BlockSpec((tm, d), row),
                   pl.BlockSpec((tm, LANES), row), pl.BlockSpec((8, LANES), const)),
        scratch_shapes=[pltpu.VMEM((tm, d), BF16)],
        compiler_params=pltpu.CompilerParams(
            dimension_semantics=("arbitrary",), vmem_limit_bytes=VMEM_LIMIT),
        name="post_mix",
    )(o2, cb, z, z, z, sga, sgc, x2, g1, sc2, sh2, n2, conv_w, wao, wco, wout, wr_hi, wr_lo, br_pad)


def _row_copy(src_hbm, dst_hbm, s, t, sem):
    return pltpu.make_async_copy(src_hbm.at[pl.ds(s, 1)], dst_hbm.at[pl.ds(t, 1)], sem)


def _dispatch_kernel(dest_ref, ends_ref, h2_ref, xs_hbm, zbuf, sem, zsem, *, tc, nblk):
    base = pl.program_id(0) * tc

    @pl.when(pl.program_id(0) == 0)
    def _():
        zbuf[...] = jnp.zeros_like(zbuf)

        def zero_block(row):
            row = pl.multiple_of(row, MOE_ROWS)
            return pltpu.make_async_copy(zbuf, xs_hbm.at[pl.ds(row, MOE_ROWS)], zsem)

        def nonempty(e):
            return ends_ref[e] > (ends_ref[e - 1] if e else 0)

        total = ends_ref[N_EXPERTS - 1]
        n_tail = nblk - total // MOE_ROWS

        def tail_start(b, carry):
            zero_block(total + b * MOE_ROWS).start()
            return carry

        def tail_wait(b, carry):
            zero_block(0).wait()
            return carry

        for e in range(N_EXPERTS):
            @pl.when(nonempty(e))
            def _(e=e):
                zero_block(ends_ref[e] - MOE_ROWS).start()
        lax.fori_loop(0, n_tail, tail_start, 0)
        for e in range(N_EXPERTS):
            @pl.when(nonempty(e))
            def _():
                zero_block(0).wait()
        lax.fori_loop(0, n_tail, tail_wait, 0)

    def issue(t, carry):
        for k in range(TOP_K):
            _row_copy(h2_ref, xs_hbm, t, dest_ref[(base + t) * TOP_K + k], sem).start()
        return carry

    lax.fori_loop(0, tc, issue, 0, unroll=4)

    for _ in range(TOP_K):
        pltpu.make_async_copy(h2_ref, xs_hbm.at[pl.ds(0, tc)], sem).wait()


def _moe_dispatch(dest_flat, ends, h2, p_rows):
    n, d = h2.shape
    tc = min(n, 512)
    kern = functools.partial(_dispatch_kernel, tc=tc, nblk=p_rows // MOE_ROWS)
    return pl.pallas_call(
        kern,
        out_shape=jax.ShapeDtypeStruct((p_rows, d), F32),
        grid_spec=pltpu.PrefetchScalarGridSpec(
            num_scalar_prefetch=2, grid=(n // tc,),
            in_specs=[pl.BlockSpec((tc, d), lambda i, dr, en: (i, 0))],
            out_specs=pl.BlockSpec(memory_space=pl.ANY),
            scratch_shapes=[pltpu.VMEM((MOE_ROWS, d), F32), pltpu.SemaphoreType.DMA,
                            pltpu.SemaphoreType.DMA]),
        compiler_params=pltpu.CompilerParams(dimension_semantics=("arbitrary",)),
        name="moe_dispatch",
    )(dest_flat, ends, h2)


_NT = (((1,), (1,)), ((), ()))


def _ffn_kernel(be_ref, nb_ref, fresh_ref, slot_ref, next_ref,
                xs_ref, wgu_hbm, wd_hbm, bg_ref, bu_ref, bd_ref, ys_ref,
                wgu_buf, wd_buf, wt_scr, wg_scr, wu_scr, wd_scr, sem):
    i = pl.program_id(0)
    used = i < nb_ref[0]
    fresh = fresh_ref[i] == 1
    slot = slot_ref[i]

    def weight_copies(e, s):
        return (pltpu.make_async_copy(wgu_hbm.at[e], wgu_buf.at[s], sem.at[0, s]),
                pltpu.make_async_copy(wd_hbm.at[e], wd_buf.at[s], sem.at[1, s]))

    @pl.when(jnp.logical_not(used))
    def _():
        ys_ref[...] = jnp.zeros_like(ys_ref)

    @pl.when(i == 0)
    def _():
        for cp in weight_copies(be_ref[0], 0):
            cp.start()

    @pl.when(jnp.logical_and(used, fresh))
    def _():
        for cp in weight_copies(be_ref[i], slot):
            cp.wait()

        @pl.when(next_ref[i] >= 0)
        def _():
            for cp in weight_copies(next_ref[i], 1 - slot):
                cp.start()

        _, d, f2 = wgu_buf.shape
        for c in range(d // LANES):
            cols = slice(c * LANES, (c + 1) * LANES)
            wt_scr[c] = wgu_buf[slot, cols, :].T
            wg_scr[:, cols] = wt_scr[c, pl.ds(0, f2 // 2, stride=2), :].astype(BF16)
            wu_scr[:, cols] = wt_scr[c, pl.ds(1, f2 // 2, stride=2), :].astype(BF16)
        wd_scr[...] = wd_buf[slot].astype(BF16)

    @pl.when(used)
    def _():
        x = xs_ref[...].astype(BF16)
        gate = lax.dot_general(x, wg_scr[...], _NT, preferred_element_type=F32) + bg_ref[...]
        up = lax.dot_general(x, wu_scr[...], _NT, preferred_element_type=F32) + bu_ref[...]
        gate = jnp.minimum(gate, SWIGLU_LIMIT)
        up = jnp.clip(up, -SWIGLU_LIMIT, SWIGLU_LIMIT)
        glu = gate * jax.nn.sigmoid(gate * SWIGLU_ALPHA)
        mid = ((up + 1.0) * glu).astype(BF16)
        ys_ref[...] = _dot(mid, wd_scr[...]) + bd_ref[...]


def _moe_ffn(block_e, n_used, fresh, slot, next_e, xs, wgu, wd, bg, bu, bd):
    p_rows, d = xs.shape
    f2 = wgu.shape[2]
    f = f2 // 2
    nblk = p_rows // MOE_ROWS
    rows = lambda i, be, nb, *_: (jnp.minimum(i, nb[0] - 1), 0)
    bmap = lambda i, be, *_: (be[i], 0, 0)
    return pl.pallas_call(
        _ffn_kernel,
        out_shape=jax.ShapeDtypeStruct((p_rows, d), F32),
        grid_spec=pltpu.PrefetchScalarGridSpec(
            num_scalar_prefetch=5, grid=(nblk,),
            in_specs=[pl.BlockSpec((MOE_ROWS, d), rows),
                      pl.BlockSpec(memory_space=pl.ANY), pl.BlockSpec(memory_space=pl.ANY),
                      pl.BlockSpec((None, 1, f), bmap), pl.BlockSpec((None, 1, f), bmap),
                      pl.BlockSpec((None, 1, d), bmap)],
            out_specs=pl.BlockSpec((MOE_ROWS, d), lambda i, *_: (i, 0)),
            scratch_shapes=[pltpu.VMEM((2, d, f2), F32), pltpu.VMEM((2, f, d), F32),
                            pltpu.VMEM((d // LANES, f2, LANES), F32), pltpu.VMEM((f, d), BF16),
                            pltpu.VMEM((f, d), BF16), pltpu.VMEM((f, d), BF16),
                            pltpu.SemaphoreType.DMA((2, 2))]),
        compiler_params=pltpu.CompilerParams(
            dimension_semantics=("arbitrary",), vmem_limit_bytes=VMEM_LIMIT),
        name="moe_ffn",
    )(block_e, n_used, fresh, slot, next_e, xs, wgu, wd, bg, bu, bd)


def _gather_copy(ys_hbm, buf, s, k, t, sem):
    return pltpu.make_async_copy(ys_hbm.at[pl.ds(s, 1)], buf.at[k, pl.ds(t, 1)], sem)


def _combine_kernel(dest_ref, ys_hbm, x1_ref, route_ref, g2_ref, o_ref, buf, sem, *, tc):
    base = pl.program_id(0) * tc

    def issue(t, carry):
        for k in range(TOP_K):
            _gather_copy(ys_hbm, buf, dest_ref[(base + t) * TOP_K + k], k, t, sem).start()
        return carry

    lax.fori_loop(0, tc, issue, 0, unroll=4)

    for k in range(TOP_K):
        pltpu.make_async_copy(ys_hbm.at[pl.ds(0, tc)], buf.at[k], sem).wait()

    route = route_ref[...]
    y = buf[0] * route[:, 0:1]
    for k in range(1, TOP_K):
        y = y + buf[k] * route[:, k:k + 1]
    o_ref[...] = x1_ref[...] + g2_ref[...] * y


def _moe_combine(dest_flat, ys, x1, route, g2, seq):
    n, d = x1.shape
    tc = min(seq, 256)
    tpb = seq // tc
    kern = functools.partial(_combine_kernel, tc=tc)
    return pl.pallas_call(
        kern,
        out_shape=jax.ShapeDtypeStruct((n, d), F32),
        grid_spec=pltpu.PrefetchScalarGridSpec(
            num_scalar_prefetch=1, grid=(n // tc,),
            in_specs=[pl.BlockSpec(memory_space=pl.ANY),
                      pl.BlockSpec((tc, d), lambda i, dr: (i, 0)),
                      pl.BlockSpec((tc, LANES), lambda i, dr: (i, 0)),
                      pl.BlockSpec((None, 1, d), lambda i, dr: (i // tpb, 0, 0))],
            out_specs=pl.BlockSpec((tc, d), lambda i, dr: (i, 0)),
            scratch_shapes=[pltpu.VMEM((TOP_K, tc, d), F32), pltpu.SemaphoreType.DMA]),
        compiler_params=pltpu.CompilerParams(
            dimension_semantics=("arbitrary",), vmem_limit_bytes=VMEM_LIMIT),
        name="moe_combine",
    )(dest_flat, ys, x1, route, g2)


def _rope_tables(seq):
    inv_freq = ROPE_THETA ** (-jnp.arange(0, HEAD_DIM, 2, dtype=F32) / HEAD_DIM)
    ang = inv_freq[:, None] * jnp.arange(seq, dtype=F32)[None, :]
    cos_t = jnp.concatenate([jnp.cos(ang), jnp.cos(ang)], axis=0)
    sin_t = jnp.concatenate([-jnp.sin(ang), jnp.sin(ang)], axis=0)
    return cos_t, sin_t


def _layer(x, c, l, lambda_init, w_ada, b_ada, norm1_w, w_in, q_norm_w, k_norm_w, lambda_q1,
           lambda_k1, lambda_q2, lambda_k2, subln_w, w_attn_o, conv_w, w_conv_o, w_out, norm2_w,
           w_router, b_router, w_gate_up, b_gate_up, w_down, b_down):
    bsz, seq, d = x.shape
    n = bsz * seq
    x2 = x.reshape(n, d)

    mod = _ada_mod(c, w_ada[l], b_ada[l])
    sh1, sc1, g1, sh2, sc2, g2 = [m.reshape(bsz, 1, d) for m in jnp.split(mod, 6, axis=-1)]

    cos_t, sin_t = _rope_tables(seq)
    qt, k, vt, cb, z, sga, sgc, qn, kn = _in_proj(
        x2, norm1_w[l].reshape(1, d), sc1, sh1, w_in[l].astype(BF16),
        q_norm_w[l].reshape(HEAD_DIM, 1) * Q_SCALE, k_norm_w[l].reshape(HEAD_DIM, 1),
        cos_t, sin_t, bsz, seq)

    lam_vecs = jnp.stack([lambda_q1[l], lambda_k1[l], lambda_q2[l], lambda_k2[l]]).astype(F32)
    o = _diff_attn(qt, k.reshape(bsz, seq, d), vt, qn, kn, lam_vecs, subln_w[l].reshape(V_DIM, 1),
                   lambda_init)

    wr = jnp.zeros((d, LANES), F32).at[:, :N_EXPERTS].set(w_router[l].astype(F32))
    wr_hi = wr.astype(BF16)
    wr_lo = (wr - wr_hi.astype(F32)).astype(BF16)
    br_pad = jnp.full((1, LANES), NEG_BIG, F32).at[0, :N_EXPERTS].set(b_router[l].astype(F32))
    x1, h2, route, cnt = _post_mix(
        o.reshape(n, d), cb, z, sga, sgc, x2, g1, sc2, sh2, norm2_w[l].reshape(1, d), conv_w[l],
        w_attn_o[l].astype(BF16), w_conv_o[l].astype(BF16), w_out[l].astype(BF16),
        wr_hi, wr_lo, br_pad, seq)

    counts = cnt[0, :N_EXPERTS].astype(jnp.int32)
    padded = ((counts + MOE_ROWS - 1) // MOE_ROWS) * MOE_ROWS
    ends = jnp.cumsum(padded)
    start = ends - padded
    p_rows = n * TOP_K + N_EXPERTS * MOE_ROWS
    nblk = p_rows // MOE_ROWS
    top_e = route[:, TOP_K:2 * TOP_K].astype(jnp.int32)
    rank = route[:, 2 * TOP_K:3 * TOP_K].astype(jnp.int32)
    dest = (start[top_e] + rank).reshape(-1)
    block_start = jnp.arange(nblk, dtype=jnp.int32) * MOE_ROWS
    block_e = jnp.sum((ends[None, :] <= block_start[:, None]).astype(jnp.int32), axis=1)
    block_e = jnp.minimum(block_e, N_EXPERTS - 1)
    n_used = (ends[-1:] // MOE_ROWS).astype(jnp.int32)
    experts = jnp.arange(N_EXPERTS, dtype=jnp.int32)
    nonempty = padded > 0
    later = jnp.logical_and(nonempty[None, :], experts[None, :] > experts[:, None])
    next_of = jnp.min(jnp.where(later, experts[None, :], N_EXPERTS), axis=1)
    next_of = jnp.where(next_of == N_EXPERTS, -1, next_of).astype(jnp.int32)
    slot_of = ((jnp.cumsum(nonempty.astype(jnp.int32)) - 1) % 2).astype(jnp.int32)
    fresh = jnp.concatenate([jnp.ones((1,), jnp.int32),
                             (block_e[1:] != block_e[:-1]).astype(jnp.int32)])
    slot = slot_of[block_e]
    next_e = next_of[block_e]

    xs = _moe_dispatch(dest, ends.astype(jnp.int32), h2, p_rows)
    ys = _moe_ffn(block_e, n_used, fresh, slot, next_e, xs, w_gate_up[l], w_down[l],
                  b_gate_up[l][:, None, 0::2], b_gate_up[l][:, None, 1::2], b_down[l][:, None, :])
    out = _moe_combine(dest, ys, x1, route, g2, seq)
    return out.reshape(bsz, seq, d)


def kernel(x, c, w_ada, b_ada, norm1_w, w_in, q_norm_w, k_norm_w, lambda_q1, lambda_k1, lambda_q2,
           lambda_k2, subln_w, w_attn_o, conv_w, w_conv_o, w_out, norm2_w, w_router, b_router,
           w_gate_up, b_gate_up, w_down, b_down):
    depth = w_ada.shape[0]
    for l in range(depth):
        lambda_init = 0.8 - 0.6 * math.exp(-0.3 * l)
        x = _layer(x, c, l, lambda_init, w_ada, b_ada, norm1_w, w_in, q_norm_w, k_norm_w,
                   lambda_q1, lambda_k1, lambda_q2, lambda_k2, subln_w, w_attn_o, conv_w,
                   w_conv_o, w_out, norm2_w, w_router, b_router, w_gate_up, b_gate_up,
                   w_down, b_down)
    return x
```

```python
import functools
import math

import jax
import jax.numpy as jnp
from jax import lax
from jax.experimental import pallas as pl
from jax.experimental.pallas import tpu as pltpu

N_HEADS = 8
HEAD_DIM = 64
V_DIM = 2 * HEAD_DIM
N_EXPERTS = 32
TOP_K = 4
SWIGLU_LIMIT = 7.0
SWIGLU_ALPHA = 1.702
ROPE_THETA = 10000.0
RMS_EPS = 1e-6
SUBLN_EPS = 1e-5
LANES = 128
TILE_ROW = 8
MOE_ROWS = 256
NEG_BIG = -1e30
LOG2E = 1.4426950408889634
Q_SCALE = LOG2E / math.sqrt(HEAD_DIM)
SAFE_EXP2_BOUND = 80.0
VMEM_LIMIT = 56 * 1024 * 1024

F32 = jnp.float32
BF16 = jnp.bfloat16


def _dot(a, b):
    return jnp.dot(a, b, preferred_element_type=F32)


def _store_tile_rows(ref, row0, val):
    rows, d = val.shape
    chunks = d // LANES
    for c in range(chunks):
        ref[pl.ds(row0 * chunks + c, rows, stride=chunks), :] = val[:, c * LANES:(c + 1) * LANES]


def _load_tile_rows(ref, chunks):
    rows = ref.shape[0] // chunks
    return [ref[pl.ds(c, rows, stride=chunks), :] for c in range(chunks)]


def _ada_kernel(ct_ref, w_ref, b_ref, o_ref):
    ct = ct_ref[...]
    s = ct * jax.nn.sigmoid(ct)
    w = w_ref[...]
    for b in range(ct.shape[1]):
        o_ref[b:b + 1, :] = jnp.sum(w * s[:, b:b + 1], axis=0, keepdims=True) + b_ref[...]


def _ada_mod(c, w_ada, b_ada):
    bsz, d = c.shape
    n = w_ada.shape[1]
    tn = min(n, 1536)
    return pl.pallas_call(
        _ada_kernel,
        out_shape=jax.ShapeDtypeStruct((bsz, n), F32),
        grid=(n // tn,),
        in_specs=[pl.BlockSpec((d, bsz), lambda j: (0, 0)),
                  pl.BlockSpec((d, tn), lambda j: (0, j)),
                  pl.BlockSpec((1, tn), lambda j: (0, j))],
        out_specs=pl.BlockSpec((bsz, tn), lambda j: (0, j)),
        compiler_params=pltpu.CompilerParams(dimension_semantics=("arbitrary",)),
        name="ada_mod",
    )(c.T, w_ada, b_ada.reshape(1, n))


def _qk_norm_rope_t(y, g_col, cos_t, sin_t):
    tm, w = y.shape
    yt = y.T.reshape(w // HEAD_DIM, HEAD_DIM, tm)
    ms = jnp.mean(yt * yt, axis=1, keepdims=True)
    yn = yt * lax.rsqrt(ms + RMS_EPS) * g_col[None]
    half = HEAD_DIM // 2
    swapped = jnp.concatenate([yn[:, half:, :], yn[:, :half, :]], axis=1)
    out = yn * cos_t[None] + swapped * sin_t[None]
    norm2 = jnp.sum(out * out, axis=1).reshape(N_HEADS, 2, tm)
    return out.reshape(w, tm), norm2


def _in_proj_kernel(x_ref, n1_ref, sc_ref, sh_ref, w_ref, gq_ref, gk_ref, cos_ref, sin_ref,
                    qt_ref, k_ref, vt_ref, cb_ref, z_ref, sga_ref, sgc_ref, qn_ref, kn_ref,
                    h_scr, y_a, y_b):
    j = pl.program_id(1)
    y_scr = (y_a, y_b)

    def finish_q(y):
        qt, qn = _qk_norm_rope_t(y, gq_ref[...], cos_ref[...], sin_ref[...])
        qt_ref[...] = qt.astype(BF16)
        qn_ref[...] = qn

    def finish_k(y):
        kt, kn = _qk_norm_rope_t(y, gk_ref[...], cos_ref[...], sin_ref[...])
        k_ref[...] = kt.T.astype(BF16)
        kn_ref[...] = kn

    def finish_v(y):
        vt_ref[...] = y.T.astype(BF16)

    def finish_cb(y):
        cb_ref[...] = y.astype(BF16)

    def finish_z(y):
        z_ref[...] = (y_scr[0][...] * y).astype(BF16)

    def finish_ga(y):
        sga_ref[...] = jax.nn.sigmoid(y).astype(BF16)

    def finish_gc(y):
        sgc_ref[...] = jax.nn.sigmoid(y).astype(BF16)

    finish = (finish_q, finish_k, finish_v, finish_cb, None, finish_z, finish_ga, finish_gc)
    n_blocks = len(finish)

    for c in range(n_blocks + 1):
        @pl.when(j == c)
        def _(c=c):
            if c == 0:
                x = x_ref[...]
                xn = x * lax.rsqrt(jnp.mean(x * x, axis=-1, keepdims=True) + RMS_EPS) * n1_ref[...]
                h_scr[...] = (xn * (1.0 + sc_ref[...]) + sh_ref[...]).astype(BF16)
            if c >= 1 and finish[c - 1] is not None:
                finish[c - 1](y_scr[(c - 1) % 2][...])
            if c < n_blocks:
                wcol = w_ref.shape[1] // n_blocks
                y_scr[c % 2][...] = _dot(h_scr[...], w_ref[:, c * wcol:(c + 1) * wcol])


def _in_proj(x2, n1, sc1, sh1, w_in_bf, gq, gk, cos_t, sin_t, bsz, seq):
    n, d = x2.shape
    tm = min(seq, 512)
    tpb = seq // tm
    wcol = d
    nj = w_in_bf.shape[1] // wcol
    assert nj == 8
    n_row = n // tm
    mod = lambda i, j: (i // tpb, 0, 0)
    nat = jax.ShapeDtypeStruct((n, d), BF16)
    tr = jax.ShapeDtypeStruct((bsz, d, seq), BF16)
    nrm = jax.ShapeDtypeStruct((bsz, N_HEADS, 2, seq), F32)

    def tile_at(w):
        return lambda i, j: jnp.where(j >= w, i, jnp.maximum(i - 1, 0))

    def nat_spec(w):
        t = tile_at(w)
        return pl.BlockSpec((tm, d), lambda i, j: (t(i, j), 0))

    def tr_spec(w):
        t = tile_at(w)
        return pl.BlockSpec((None, d, tm), lambda i, j: (t(i, j) // tpb, 0, t(i, j) % tpb))

    def nrm_spec(w):
        t = tile_at(w)
        return pl.BlockSpec((None, N_HEADS, 2, tm),
                            lambda i, j: (t(i, j) // tpb, 0, 0, t(i, j) % tpb))

    x_spec = pl.BlockSpec((tm, d), lambda i, j: (jnp.where(j == 0, i, jnp.minimum(i + 1, n_row - 1)), 0))
    return pl.pallas_call(
        _in_proj_kernel,
        out_shape=(tr, nat, tr, nat, nat, nat, nat, nrm, nrm),
        grid=(n_row, nj + 1),
        in_specs=[x_spec,
                  pl.BlockSpec((1, d), lambda i, j: (0, 0)),
                  pl.BlockSpec((None, 1, d), mod),
                  pl.BlockSpec((None, 1, d), mod),
                  pl.BlockSpec((d, nj * wcol), lambda i, j: (0, 0)),
                  pl.BlockSpec((HEAD_DIM, 1), lambda i, j: (0, 0)),
                  pl.BlockSpec((HEAD_DIM, 1), lambda i, j: (0, 0)),
                  pl.BlockSpec((HEAD_DIM, tm), lambda i, j: (0, i % tpb)),
                  pl.BlockSpec((HEAD_DIM, tm), lambda i, j: (0, i % tpb))],
        out_specs=(tr_spec(1), nat_spec(2), tr_spec(3), nat_spec(4), nat_spec(6), nat_spec(7),
                   nat_spec(8), nrm_spec(1), nrm_spec(2)),
        scratch_shapes=[pltpu.VMEM((tm, d), BF16), pltpu.VMEM((tm, d), F32),
                        pltpu.VMEM((tm, d), F32)],
        compiler_params=pltpu.CompilerParams(
            dimension_semantics=("arbitrary", "arbitrary"), vmem_limit_bytes=VMEM_LIMIT),
        name="in_proj",
    )(x2, n1, sc1, sh1, w_in_bf, gq, gk, cos_t, sin_t)


def _diff_attn_kernel(qt_ref, k_ref, vt_ref, qn_ref, kn_ref, lam_ref, sw_ref, o_ref, acc0, acc1,
                      *, tk, lambda_init):
    seq = k_ref.shape[0]
    tq = qt_ref.shape[1]
    qt = qt_ref[...]
    first = lax.broadcasted_iota(jnp.int32, qt.shape, 0) < HEAD_DIM
    zero = jnp.zeros_like(qt)
    qz = (jnp.where(first, qt, zero), jnp.where(first, zero, qt))
    accs = (acc0, acc1)
    acc0[...] = jnp.zeros_like(acc0)
    acc1[...] = jnp.zeros_like(acc1)
    n_chunks = seq // tk

    def load(j):
        off = pl.multiple_of(j * tk, tk)
        return k_ref[pl.ds(off, tk), :], vt_ref[:, pl.ds(off, tk)]

    def plain_body(j, carry):
        kk, vt = load(j)
        new = []
        for c in range(2):
            p = jnp.exp2(_dot(kk, qz[c]))
            new.append(carry[c] + jnp.sum(p, axis=0, keepdims=True))
            accs[c][...] += _dot(vt, p.astype(BF16))
        return tuple(new)

    def online_body(j, carry):
        kk, vt = load(j)
        new = []
        for c in range(2):
            m, l = carry[2 * c], carry[2 * c + 1]
            s = _dot(kk, qz[c])
            m_new = jnp.maximum(m, jnp.max(s, axis=0, keepdims=True))
            alpha = jnp.exp2(m - m_new)
            p = jnp.exp2(s - m_new)
            l = alpha * l + jnp.sum(p, axis=0, keepdims=True)
            accs[c][...] = alpha * accs[c][...] + _dot(vt, p.astype(BF16))
            new += [m_new, l]
        return tuple(new)

    m_init = jnp.full((1, tq), NEG_BIG, F32)
    l_init = jnp.zeros((1, tq), F32)

    def plain():
        return lax.fori_loop(0, n_chunks, plain_body, (l_init, l_init))

    def online():
        _, l0, _, l1 = lax.fori_loop(0, n_chunks, online_body, (m_init, l_init, m_init, l_init))
        return l0, l1

    bound2 = jnp.max(jnp.max(qn_ref[...], axis=-1, keepdims=True)
                     * jnp.max(kn_ref[...], axis=-1, keepdims=True))
    l0, l1 = lax.cond(bound2 <= SAFE_EXP2_BOUND * SAFE_EXP2_BOUND, plain, online)

    lq = lam_ref[...]
    lam = (jnp.exp(jnp.sum(lq[0:1] * lq[1:2], axis=-1, keepdims=True))
           - jnp.exp(jnp.sum(lq[2:3] * lq[3:4], axis=-1, keepdims=True)) + lambda_init)
    o = acc0[...] / l0 - lam * (acc1[...] / l1)
    o = o * lax.rsqrt(jnp.mean(o * o, axis=0, keepdims=True) + SUBLN_EPS)
    o = o * sw_ref[...] * (1.0 - lambda_init)
    o_ref[...] = o.T.astype(BF16)


def _diff_attn(qt, k3, vt, qn, kn, lam_vecs, subln_col, lambda_init):
    bsz, d, seq = qt.shape
    tq = min(seq, 1024)
    tk = min(seq, 4096)
    kern = functools.partial(_diff_attn_kernel, tk=tk, lambda_init=lambda_init)
    return pl.pallas_call(
        kern,
        out_shape=jax.ShapeDtypeStruct((bsz, seq, d), BF16),
        grid=(bsz, N_HEADS, seq // tq),
        in_specs=[pl.BlockSpec((None, V_DIM, tq), lambda b, h, i: (b, h, i)),
                  pl.BlockSpec((None, seq, V_DIM), lambda b, h, i: (b, 0, h)),
                  pl.BlockSpec((None, V_DIM, seq), lambda b, h, i: (b, h, 0)),
                  pl.BlockSpec((None, None, 2, tq), lambda b, h, i: (b, h, 0, i)),
                  pl.BlockSpec((None, None, 2, seq), lambda b, h, i: (b, h, 0, 0)),
                  pl.BlockSpec((4, HEAD_DIM), lambda b, h, i: (0, 0)),
                  pl.BlockSpec((V_DIM, 1), lambda b, h, i: (0, 0))],
        out_specs=pl.BlockSpec((None, tq, V_DIM), lambda b, h, i: (b, i, h)),
        scratch_shapes=[pltpu.VMEM((V_DIM, tq), F32), pltpu.VMEM((V_DIM, tq), F32)],
        compiler_params=pltpu.CompilerParams(
            dimension_semantics=("arbitrary", "arbitrary", "arbitrary"),
            vmem_limit_bytes=VMEM_LIMIT),
        name="diff_attn",
    )(qt, k3, vt, qn, kn, lam_vecs, subln_col)


def _post_mix_kernel(o_ref, cb_ref, z_ref, zp_ref, zn_ref, sga_ref, sgc_ref, x_ref,
                     g1_ref, sc2_ref, sh2_ref, n2_ref, cw_ref, wao_ref, wco_ref, wout_ref,
                     wrh_ref, wrl_ref, br_ref,
                     x1_ref, h2_ref, route_ref, cnt_ref, u_scr, *, tpb, n_sub):
    i = pl.program_id(0)
    tm = x_ref.shape[0]

    @pl.when(i == 0)
    def _():
        cnt_ref[...] = jnp.zeros_like(cnt_ref)

    z = z_ref[...].astype(F32)
    rows = lax.broadcasted_iota(jnp.int32, z.shape, 0)
    halo_rows = zp_ref.shape[0]
    prev_row = zp_ref[halo_rows - 1:halo_rows, :].astype(F32)
    next_row = zn_ref[0:1, :].astype(F32)
    prev_row = jnp.where(i % tpb == 0, jnp.zeros_like(prev_row), prev_row)
    next_row = jnp.where(i % tpb == tpb - 1, jnp.zeros_like(next_row), next_row)
    z_m1 = jnp.where(rows == 0, prev_row, pltpu.roll(z, 1, 0))
    z_p1 = jnp.where(rows == tm - 1, next_row, pltpu.roll(z, tm - 1, 0))
    cw = cw_ref[...]
    conv = z_m1 * cw[0:1] + z * cw[1:2] + z_p1 * cw[2:3]
    u_scr[...] = (cb_ref[...].astype(F32) * conv).astype(BF16)

    ts = tm // n_sub
    lane = lax.broadcasted_iota(jnp.int32, (ts, LANES), 1)
    r_i = lax.broadcasted_iota(jnp.int32, (ts, ts), 0)
    c_i = lax.broadcasted_iota(jnp.int32, (ts, ts), 1)
    lower = (r_i > c_i).astype(BF16)
    counts = cnt_ref[0:1, :]
    for s in range(n_sub):
        rs = slice(s * ts, (s + 1) * ts)
        y_attn = _dot(o_ref[rs, :], wao_ref[...])
        y_conv = _dot(u_scr[rs, :], wco_ref[...])
        m = sga_ref[rs, :].astype(F32) * y_attn + sgc_ref[rs, :].astype(F32) * y_conv
        x1 = x_ref[rs, :] + g1_ref[...] * _dot(m.astype(BF16), wout_ref[...])
        x1_ref[rs, :] = x1

        h2 = x1 * lax.rsqrt(jnp.mean(x1 * x1, axis=-1, keepdims=True) + RMS_EPS) * n2_ref[...]
        h2 = h2 * (1.0 + sc2_ref[...]) + sh2_ref[...]
        _store_tile_rows(h2_ref, s * ts, h2)

        h_hi = h2.astype(BF16)
        h_lo = (h2 - h_hi.astype(F32)).astype(BF16)
        logits = (_dot(h_hi, wrh_ref[...]) + _dot(h_lo, wrh_ref[...]) + _dot(h_hi, wrl_ref[...])
                  + br_ref[...])

        work = logits
        vals, idxs = [], []
        for _ in range(TOP_K):
            mx = jnp.max(work, axis=-1, keepdims=True)
            ix = jnp.min(jnp.where(work == mx, lane, LANES), axis=-1, keepdims=True)
            vals.append(mx)
            idxs.append(ix)
            work = jnp.where(lane == ix, 2.0 * NEG_BIG, work)
        exps = [jnp.exp(v - vals[0]) for v in vals]
        den = exps[0] + exps[1] + exps[2] + exps[3]

        sel = (work == 2.0 * NEG_BIG).astype(BF16)
        before = _dot(lower, sel) + counts
        counts = counts + jnp.sum(sel.astype(F32), axis=0, keepdims=True)

        route = jnp.zeros(logits.shape, F32)
        for k in range(TOP_K):
            rank = jnp.sum(jnp.where(lane == idxs[k], before, 0.0), axis=-1, keepdims=True)
            route = jnp.where(lane == k, exps[k] / den, route)
            route = jnp.where(lane == TOP_K + k, idxs[k].astype(F32), route)
            route = jnp.where(lane == 2 * TOP_K + k, rank, route)
        route_ref[rs, :] = route
    cnt_ref[...] = jnp.broadcast_to(counts, cnt_ref.shape)


def _post_mix(o2, cb, z, sga, sgc, x2, g1, sc2, sh2, n2, conv_w, wao, wco, wout,
              wr_hi, wr_lo, br_pad, seq):
    n, d = x2.shape
    tm = min(seq, 512)
    tpb = seq // tm
    halo = 16
    hb = tm // halo
    last_hb = n // halo - 1
    row = lambda i: (i, 0)
    const = lambda i: (0, 0)
    mod = lambda i: (i // tpb, 0, 0)
    wspec = pl.BlockSpec((d, d), const)
    kern = functools.partial(_post_mix_kernel, tpb=tpb, n_sub=1)
    return pl.pallas_call(
        kern,
        out_shape=(jax.ShapeDtypeStruct((n, d), F32),
                   jax.ShapeDtypeStruct((n * (d // LANES), LANES), F32),
                   jax.ShapeDtypeStruct((n, LANES), F32), jax.ShapeDtypeStruct((8, LANES), F32)),
        grid=(n // tm,),
        in_specs=[pl.BlockSpec((tm, d), row), pl.BlockSpec((tm, d), row), pl.BlockSpec((tm, d), row),
                  pl.BlockSpec((halo, d), lambda i: (jnp.maximum(i * hb - 1, 0), 0)),
                  pl.BlockSpec((halo, d), lambda i: (jnp.minimum((i + 1) * hb, last_hb), 0)),
                  pl.BlockSpec((tm, d), row), pl.BlockSpec((tm, d), row), pl.BlockSpec((tm, d), row),
                  pl.BlockSpec((None, 1, d), mod), pl.BlockSpec((None, 1, d), mod),
                  pl.BlockSpec((None, 1, d), mod),
                  pl.BlockSpec((1, d), const), pl.BlockSpec((3, d), const),
                  wspec, wspec, wspec,
                  pl.BlockSpec((d, LANES), const), pl.BlockSpec((d, LANES), const),
                  pl.BlockSpec((1, LANES), const)],
        out_specs=(pl.BlockSpec((tm, d), row), pl.BlockSpec((tm * (d // LANES), LANES), row),
                   pl.BlockSpec((tm, LANES), row), pl.BlockSpec((8, LANES), const)),
        scratch_shapes=[pltpu.VMEM((tm, d), BF16)],
        compiler_params=pltpu.CompilerParams(
            dimension_semantics=("arbitrary",), vmem_limit_bytes=VMEM_LIMIT),
        name="post_mix",
    )(o2, cb, z, z, z, sga, sgc, x2, g1, sc2, sh2, n2, conv_w, wao, wco, wout, wr_hi, wr_lo, br_pad)


def _tile_rows(ref, row, n_rows=1):
    start = pl.multiple_of(row * TILE_ROW, TILE_ROW)
    return ref.at[pl.ds(start, n_rows * TILE_ROW)]


def _row_copy(src, dst, s, t, sem):
    return pltpu.make_async_copy(_tile_rows(src, s), _tile_rows(dst, t), sem)


def _dispatch_kernel(dest_ref, ends_ref, h2_ref, xs_hbm, zbuf, sem, zsem, *, tc, nblk):
    base = pl.program_id(0) * tc

    @pl.when(pl.program_id(0) == 0)
    def _():
        zbuf[...] = jnp.zeros_like(zbuf)

        def zero_block(row):
            return pltpu.make_async_copy(zbuf, _tile_rows(xs_hbm, row, MOE_ROWS), zsem)

        def nonempty(e):
            return ends_ref[e] > (ends_ref[e - 1] if e else 0)

        total = ends_ref[N_EXPERTS - 1]
        n_tail = nblk - total // MOE_ROWS

        def tail_start(b, carry):
            zero_block(total + b * MOE_ROWS).start()
            return carry

        def tail_wait(b, carry):
            zero_block(0).wait()
            return carry

        for e in range(N_EXPERTS):
            @pl.when(nonempty(e))
            def _(e=e):
                zero_block(ends_ref[e] - MOE_ROWS).start()
        lax.fori_loop(0, n_tail, tail_start, 0)
        for e in range(N_EXPERTS):
            @pl.when(nonempty(e))
            def _():
                zero_block(0).wait()
        lax.fori_loop(0, n_tail, tail_wait, 0)

    def issue(t, carry):
        for k in range(TOP_K):
            _row_copy(h2_ref, xs_hbm, t, dest_ref[(base + t) * TOP_K + k], sem).start()
        return carry

    lax.fori_loop(0, tc, issue, 0, unroll=4)

    for _ in range(TOP_K):
        pltpu.make_async_copy(h2_ref, _tile_rows(xs_hbm, 0, tc), sem).wait()


def _moe_dispatch(dest_flat, ends, h2_tiles, n, p_rows):
    tc = min(n, 512)
    kern = functools.partial(_dispatch_kernel, tc=tc, nblk=p_rows // MOE_ROWS)
    return pl.pallas_call(
        kern,
        out_shape=jax.ShapeDtypeStruct((p_rows * TILE_ROW, LANES), F32),
        grid_spec=pltpu.PrefetchScalarGridSpec(
            num_scalar_prefetch=2, grid=(n // tc,),
            in_specs=[pl.BlockSpec((tc * TILE_ROW, LANES), lambda i, dr, en: (i, 0))],
            out_specs=pl.BlockSpec(memory_space=pl.ANY),
            scratch_shapes=[pltpu.VMEM((MOE_ROWS * TILE_ROW, LANES), F32),
                            pltpu.SemaphoreType.DMA, pltpu.SemaphoreType.DMA]),
        compiler_params=pltpu.CompilerParams(dimension_semantics=("arbitrary",)),
        name="moe_dispatch",
    )(dest_flat, ends, h2_tiles)


_NT = (((1,), (1,)), ((), ()))


def _ffn_kernel(be_ref, nb_ref, fresh_ref, slot_ref, next_ref,
                xs_ref, wgu_hbm, wd_hbm, bg_ref, bu_ref, bd_ref, ys_ref,
                wgu_buf, wd_buf, wt_scr, wg_scr, wu_scr, wd_scr, sem):
    i = pl.program_id(0)
    used = i < nb_ref[0]
    fresh = fresh_ref[i] == 1
    slot = slot_ref[i]

    def weight_copies(e, s):
        return (pltpu.make_async_copy(wgu_hbm.at[e], wgu_buf.at[s], sem.at[0, s]),
                pltpu.make_async_copy(wd_hbm.at[e], wd_buf.at[s], sem.at[1, s]))

    @pl.when(jnp.logical_not(used))
    def _():
        ys_ref[...] = jnp.zeros_like(ys_ref)

    @pl.when(i == 0)
    def _():
        for cp in weight_copies(be_ref[0], 0):
            cp.start()

    @pl.when(jnp.logical_and(used, fresh))
    def _():
        for cp in weight_copies(be_ref[i], slot):
            cp.wait()

        @pl.when(next_ref[i] >= 0)
        def _():
            for cp in weight_copies(next_ref[i], 1 - slot):
                cp.start()

        _, d, f2 = wgu_buf.shape
        for c in range(d // LANES):
            cols = slice(c * LANES, (c + 1) * LANES)
            wt_scr[c] = wgu_buf[slot, cols, :].T
            wg_scr[:, cols] = wt_scr[c, pl.ds(0, f2 // 2, stride=2), :].astype(BF16)
            wu_scr[:, cols] = wt_scr[c, pl.ds(1, f2 // 2, stride=2), :].astype(BF16)
        wd_scr[...] = wd_buf[slot].astype(BF16)

    @pl.when(used)
    def _():
        x = jnp.concatenate(_load_tile_rows(xs_ref, TILE_ROW), axis=1).astype(BF16)
        gate = lax.dot_general(x, wg_scr[...], _NT, preferred_element_type=F32) + bg_ref[...]
        up = lax.dot_general(x, wu_scr[...], _NT, preferred_element_type=F32) + bu_ref[...]
        gate = jnp.minimum(gate, SWIGLU_LIMIT)
        up = jnp.clip(up, -SWIGLU_LIMIT, SWIGLU_LIMIT)
        glu = gate * jax.nn.sigmoid(gate * SWIGLU_ALPHA)
        mid = ((up + 1.0) * glu).astype(BF16)
        _store_tile_rows(ys_ref, 0, _dot(mid, wd_scr[...]) + bd_ref[...])


def _moe_ffn(block_e, n_used, fresh, slot, next_e, xs, wgu, wd, bg, bu, bd):
    p_rows = xs.shape[0] // TILE_ROW
    _, d, f2 = wgu.shape
    assert d == TILE_ROW * LANES
    f = f2 // 2
    nblk = p_rows // MOE_ROWS
    rows = lambda i, be, nb, *_: (jnp.minimum(i, nb[0] - 1), 0)
    bmap = lambda i, be, *_: (be[i], 0, 0)
    return pl.pallas_call(
        _ffn_kernel,
        out_shape=jax.ShapeDtypeStruct((p_rows * TILE_ROW, LANES), F32),
        grid_spec=pltpu.PrefetchScalarGridSpec(
            num_scalar_prefetch=5, grid=(nblk,),
            in_specs=[pl.BlockSpec((MOE_ROWS * TILE_ROW, LANES), rows),
                      pl.BlockSpec(memory_space=pl.ANY), pl.BlockSpec(memory_space=pl.ANY),
                      pl.BlockSpec((None, 1, f), bmap), pl.BlockSpec((None, 1, f), bmap),
                      pl.BlockSpec((None, 1, d), bmap)],
            out_specs=pl.BlockSpec((MOE_ROWS * TILE_ROW, LANES), lambda i, *_: (i, 0)),
            scratch_shapes=[pltpu.VMEM((2, d, f2), F32), pltpu.VMEM((2, f, d), F32),
                            pltpu.VMEM((d // LANES, f2, LANES), F32), pltpu.VMEM((f, d), BF16),
                            pltpu.VMEM((f, d), BF16), pltpu.VMEM((f, d), BF16),
                            pltpu.SemaphoreType.DMA((2, 2))]),
        compiler_params=pltpu.CompilerParams(
            dimension_semantics=("arbitrary",), vmem_limit_bytes=VMEM_LIMIT),
        name="moe_ffn",
    )(block_e, n_used, fresh, slot, next_e, xs, wgu, wd, bg, bu, bd)


def _combine_kernel(dest_ref, ys_hbm, x1_ref, route_ref, g2_ref, o_ref, buf, sem, *, tc):
    base = pl.program_id(0) * tc

    def issue(t, carry):
        for k in range(TOP_K):
            _row_copy(ys_hbm, buf.at[k], dest_ref[(base + t) * TOP_K + k], t, sem).start()
        return carry

    lax.fori_loop(0, tc, issue, 0, unroll=4)

    for k in range(TOP_K):
        pltpu.make_async_copy(_tile_rows(ys_hbm, 0, tc), buf.at[k], sem).wait()

    route = route_ref[...]
    for c in range(TILE_ROW):
        cols = slice(c * LANES, (c + 1) * LANES)
        y = buf[0, pl.ds(c, tc, stride=TILE_ROW), :] * route[:, 0:1]
        for k in range(1, TOP_K):
            y = y + buf[k, pl.ds(c, tc, stride=TILE_ROW), :] * route[:, k:k + 1]
        o_ref[:, cols] = x1_ref[:, cols] + g2_ref[:, cols] * y


def _moe_combine(dest_flat, ys, x1, route, g2, seq):
    n, d = x1.shape
    assert d == TILE_ROW * LANES
    tc = min(seq, 256)
    tpb = seq // tc
    kern = functools.partial(_combine_kernel, tc=tc)
    return pl.pallas_call(
        kern,
        out_shape=jax.ShapeDtypeStruct((n, d), F32),
        grid_spec=pltpu.PrefetchScalarGridSpec(
            num_scalar_prefetch=1, grid=(n // tc,),
            in_specs=[pl.BlockSpec(memory_space=pl.ANY),
                      pl.BlockSpec((tc, d), lambda i, dr: (i, 0)),
                      pl.BlockSpec((tc, LANES), lambda i, dr: (i, 0)),
                      pl.BlockSpec((None, 1, d), lambda i, dr: (i // tpb, 0, 0))],
            out_specs=pl.BlockSpec((tc, d), lambda i, dr: (i, 0)),
            scratch_shapes=[pltpu.VMEM((TOP_K, tc * TILE_ROW, LANES), F32),
                            pltpu.SemaphoreType.DMA]),
        compiler_params=pltpu.CompilerParams(
            dimension_semantics=("arbitrary",), vmem_limit_bytes=VMEM_LIMIT),
        name="moe_combine",
    )(dest_flat, ys, x1, route, g2)


def _rope_tables(seq):
    inv_freq = ROPE_THETA ** (-jnp.arange(0, HEAD_DIM, 2, dtype=F32) / HEAD_DIM)
    ang = inv_freq[:, None] * jnp.arange(seq, dtype=F32)[None, :]
    cos_t = jnp.concatenate([jnp.cos(ang), jnp.cos(ang)], axis=0)
    sin_t = jnp.concatenate([-jnp.sin(ang), jnp.sin(ang)], axis=0)
    return cos_t, sin_t


def _layer(x, c, l, lambda_init, w_ada, b_ada, norm1_w, w_in, q_norm_w, k_norm_w, lambda_q1,
           lambda_k1, lambda_q2, lambda_k2, subln_w, w_attn_o, conv_w, w_conv_o, w_out, norm2_w,
           w_router, b_router, w_gate_up, b_gate_up, w_down, b_down):
    bsz, seq, d = x.shape
    n = bsz * seq
    x2 = x.reshape(n, d)

    mod = _ada_mod(c, w_ada[l], b_ada[l])
    sh1, sc1, g1, sh2, sc2, g2 = [m.reshape(bsz, 1, d) for m in jnp.split(mod, 6, axis=-1)]

    cos_t, sin_t = _rope_tables(seq)
    qt, k, vt, cb, z, sga, sgc, qn, kn = _in_proj(
        x2, norm1_w[l].reshape(1, d), sc1, sh1, w_in[l].astype(BF16),
        q_norm_w[l].reshape(HEAD_DIM, 1) * Q_SCALE, k_norm_w[l].reshape(HEAD_DIM, 1),
        cos_t, sin_t, bsz, seq)

    lam_vecs = jnp.stack([lambda_q1[l], lambda_k1[l], lambda_q2[l], lambda_k2[l]]).astype(F32)
    o = _diff_attn(qt, k.reshape(bsz, seq, d), vt, qn, kn, lam_vecs, subln_w[l].reshape(V_DIM, 1),
                   lambda_init)

    wr = jnp.zeros((d, LANES), F32).at[:, :N_EXPERTS].set(w_router[l].astype(F32))
    wr_hi = wr.astype(BF16)
    wr_lo = (wr - wr_hi.astype(F32)).astype(BF16)
    br_pad = jnp.full((1, LANES), NEG_BIG, F32).at[0, :N_EXPERTS].set(b_router[l].astype(F32))
    x1, h2, route, cnt = _post_mix(
        o.reshape(n, d), cb, z, sga, sgc, x2, g1, sc2, sh2, norm2_w[l].reshape(1, d), conv_w[l],
        w_attn_o[l].astype(BF16), w_conv_o[l].astype(BF16), w_out[l].astype(BF16),
        wr_hi, wr_lo, br_pad, seq)

    counts = cnt[0, :N_EXPERTS].astype(jnp.int32)
    padded = ((counts + MOE_ROWS - 1) // MOE_ROWS) * MOE_ROWS
    ends = jnp.cumsum(padded)
    start = ends - padded
    p_rows = n * TOP_K + N_EXPERTS * MOE_ROWS
    nblk = p_rows // MOE_ROWS
    top_e = route[:, TOP_K:2 * TOP_K].astype(jnp.int32)
    rank = route[:, 2 * TOP_K:3 * TOP_K].astype(jnp.int32)
    dest = (start[top_e] + rank).reshape(-1)
    block_start = jnp.arange(nblk, dtype=jnp.int32) * MOE_ROWS
    block_e = jnp.sum((ends[None, :] <= block_start[:, None]).astype(jnp.int32), axis=1)
    block_e = jnp.minimum(block_e, N_EXPERTS - 1)
    n_used = (ends[-1:] // MOE_ROWS).astype(jnp.int32)
    experts = jnp.arange(N_EXPERTS, dtype=jnp.int32)
    nonempty = padded > 0
    later = jnp.logical_and(nonempty[None, :], experts[None, :] > experts[:, None])
    next_of = jnp.min(jnp.where(later, experts[None, :], N_EXPERTS), axis=1)
    next_of = jnp.where(next_of == N_EXPERTS, -1, next_of).astype(jnp.int32)
    slot_of = ((jnp.cumsum(nonempty.astype(jnp.int32)) - 1) % 2).astype(jnp.int32)
    fresh = jnp.concatenate([jnp.ones((1,), jnp.int32),
                             (block_e[1:] != block_e[:-1]).astype(jnp.int32)])
    slot = slot_of[block_e]
    next_e = next_of[block_e]

    xs = _moe_dispatch(dest, ends.astype(jnp.int32), h2, n, p_rows)
    ys = _moe_ffn(block_e, n_used, fresh, slot, next_e, xs, w_gate_up[l], w_down[l],
                  b_gate_up[l][:, None, 0::2], b_gate_up[l][:, None, 1::2], b_down[l][:, None, :])
    out = _moe_combine(dest, ys, x1, route, g2, seq)
    return out.reshape(bsz, seq, d)


def kernel(x, c, w_ada, b_ada, norm1_w, w_in, q_norm_w, k_norm_w, lambda_q1, lambda_k1, lambda_q2,
           lambda_k2, subln_w, w_attn_o, conv_w, w_conv_o, w_out, norm2_w, w_router, b_router,
           w_gate_up, b_gate_up, w_down, b_down):
    depth = w_ada.shape[0]
    for l in range(depth):
        lambda_init = 0.8 - 0.6 * math.exp(-0.3 * l)
        x = _layer(x, c, l, lambda_init, w_ada, b_ada, norm1_w, w_in, q_norm_w, k_norm_w,
                   lambda_q1, lambda_k1, lambda_q2, lambda_k2, subln_w, w_attn_o, conv_w,
                   w_conv_o, w_out, norm2_w, w_router, b_router, w_gate_up, b_gate_up,
                   w_down, b_down)
    return x
```

```python
import functools
import math

import jax
import jax.numpy as jnp
from jax import lax
from jax.experimental import pallas as pl
from jax.experimental.pallas import tpu as pltpu

N_HEADS = 8
HEAD_DIM = 64
V_DIM = 2 * HEAD_DIM
N_EXPERTS = 32
TOP_K = 4
SWIGLU_LIMIT = 7.0
SWIGLU_ALPHA = 1.702
ROPE_THETA = 10000.0
RMS_EPS = 1e-6
SUBLN_EPS = 1e-5
LANES = 128
TILE_ROW = 8
MOE_ROWS = 256
NEG_BIG = -1e30
LOG2E = 1.4426950408889634
Q_SCALE = LOG2E / math.sqrt(HEAD_DIM)
SAFE_EXP2_BOUND = 80.0
VMEM_LIMIT = 56 * 1024 * 1024

F32 = jnp.float32
BF16 = jnp.bfloat16


def _dot(a, b):
    return jnp.dot(a, b, preferred_element_type=F32)


def _store_tile_rows(ref, row0, val):
    rows, d = val.shape
    chunks = d // LANES
    for c in range(chunks):
        ref[pl.ds(row0 * chunks + c, rows, stride=chunks), :] = val[:, c * LANES:(c + 1) * LANES]


def _load_tile_rows(ref, chunks):
    rows = ref.shape[0] // chunks
    return [ref[pl.ds(c, rows, stride=chunks), :] for c in range(chunks)]


def _ada_kernel(ct_ref, w_ref, b_ref, o_ref):
    ct = ct_ref[...]
    s = ct * jax.nn.sigmoid(ct)
    w = w_ref[...]
    for b in range(ct.shape[1]):
        o_ref[b:b + 1, :] = jnp.sum(w * s[:, b:b + 1], axis=0, keepdims=True) + b_ref[...]


def _ada_mod(c, w_ada, b_ada):
    bsz, d = c.shape
    n = w_ada.shape[1]
    tn = min(n, 1536)
    return pl.pallas_call(
        _ada_kernel,
        out_shape=jax.ShapeDtypeStruct((bsz, n), F32),
        grid=(n // tn,),
        in_specs=[pl.BlockSpec((d, bsz), lambda j: (0, 0)),
                  pl.BlockSpec((d, tn), lambda j: (0, j)),
                  pl.BlockSpec((1, tn), lambda j: (0, j))],
        out_specs=pl.BlockSpec((bsz, tn), lambda j: (0, j)),
        compiler_params=pltpu.CompilerParams(dimension_semantics=("arbitrary",)),
        name="ada_mod",
    )(c.T, w_ada, b_ada.reshape(1, n))


def _qk_norm_rope_t(y, g_col, cos_t, sin_t):
    tm, w = y.shape
    yt = y.T.reshape(w // HEAD_DIM, HEAD_DIM, tm)
    ms = jnp.mean(yt * yt, axis=1, keepdims=True)
    yn = yt * lax.rsqrt(ms + RMS_EPS) * g_col[None]
    half = HEAD_DIM // 2
    swapped = jnp.concatenate([yn[:, half:, :], yn[:, :half, :]], axis=1)
    out = yn * cos_t[None] + swapped * sin_t[None]
    norm2 = jnp.sum(out * out, axis=1).reshape(N_HEADS, 2, tm)
    return out.reshape(w, tm), norm2


def _in_proj_kernel(x_ref, n1_ref, sc_ref, sh_ref, w_ref, gq_ref, gk_ref, cos_ref, sin_ref,
                    qt_ref, k_ref, vt_ref, cb_ref, z_ref, sga_ref, sgc_ref, qn_ref, kn_ref,
                    h_scr, y_a, y_b):
    j = pl.program_id(1)
    y_scr = (y_a, y_b)

    def finish_q(y):
        qt, qn = _qk_norm_rope_t(y, gq_ref[...], cos_ref[...], sin_ref[...])
        qt_ref[...] = qt.astype(BF16)
        qn_ref[...] = qn

    def finish_k(y):
        kt, kn = _qk_norm_rope_t(y, gk_ref[...], cos_ref[...], sin_ref[...])
        k_ref[...] = kt.T.astype(BF16)
        kn_ref[...] = kn

    def finish_v(y):
        vt_ref[...] = y.T.astype(BF16)

    def finish_cb(y):
        cb_ref[...] = y.astype(BF16)

    def finish_z(y):
        z_ref[...] = (y_scr[0][...] * y).astype(BF16)

    def finish_ga(y):
        sga_ref[...] = jax.nn.sigmoid(y).astype(BF16)

    def finish_gc(y):
        sgc_ref[...] = jax.nn.sigmoid(y).astype(BF16)

    finish = (finish_q, finish_k, finish_v, finish_cb, None, finish_z, finish_ga, finish_gc)
    n_blocks = len(finish)

    for c in range(n_blocks + 1):
        @pl.when(j == c)
        def _(c=c):
            if c == 0:
                x = x_ref[...]
                xn = x * lax.rsqrt(jnp.mean(x * x, axis=-1, keepdims=True) + RMS_EPS) * n1_ref[...]
                h_scr[...] = (xn * (1.0 + sc_ref[...]) + sh_ref[...]).astype(BF16)
            if c >= 1 and finish[c - 1] is not None:
                finish[c - 1](y_scr[(c - 1) % 2][...])
            if c < n_blocks:
                wcol = w_ref.shape[1] // n_blocks
                y_scr[c % 2][...] = _dot(h_scr[...], w_ref[:, c * wcol:(c + 1) * wcol])


def _in_proj(x2, n1, sc1, sh1, w_in_bf, gq, gk, cos_t, sin_t, bsz, seq):
    n, d = x2.shape
    tm = min(seq, 512)
    tpb = seq // tm
    wcol = d
    nj = w_in_bf.shape[1] // wcol
    assert nj == 8
    n_row = n // tm
    mod = lambda i, j: (i // tpb, 0, 0)
    nat = jax.ShapeDtypeStruct((n, d), BF16)
    tr = jax.ShapeDtypeStruct((bsz, d, seq), BF16)
    nrm = jax.ShapeDtypeStruct((bsz, N_HEADS, 2, seq), F32)

    def tile_at(w):
        return lambda i, j: jnp.where(j >= w, i, jnp.maximum(i - 1, 0))

    def nat_spec(w):
        t = tile_at(w)
        return pl.BlockSpec((tm, d), lambda i, j: (t(i, j), 0))

    def tr_spec(w):
        t = tile_at(w)
        return pl.BlockSpec((None, d, tm), lambda i, j: (t(i, j) // tpb, 0, t(i, j) % tpb))

    def nrm_spec(w):
        t = tile_at(w)
        return pl.BlockSpec((None, N_HEADS, 2, tm),
                            lambda i, j: (t(i, j) // tpb, 0, 0, t(i, j) % tpb))

    x_spec = pl.BlockSpec((tm, d), lambda i, j: (jnp.where(j == 0, i, jnp.minimum(i + 1, n_row - 1)), 0))
    return pl.pallas_call(
        _in_proj_kernel,
        out_shape=(tr, nat, tr, nat, nat, nat, nat, nrm, nrm),
        grid=(n_row, nj + 1),
        in_specs=[x_spec,
                  pl.BlockSpec((1, d), lambda i, j: (0, 0)),
                  pl.BlockSpec((None, 1, d), mod),
                  pl.BlockSpec((None, 1, d), mod),
                  pl.BlockSpec((d, nj * wcol), lambda i, j: (0, 0)),
                  pl.BlockSpec((HEAD_DIM, 1), lambda i, j: (0, 0)),
                  pl.BlockSpec((HEAD_DIM, 1), lambda i, j: (0, 0)),
                  pl.BlockSpec((HEAD_DIM, tm), lambda i, j: (0, i % tpb)),
                  pl.BlockSpec((HEAD_DIM, tm), lambda i, j: (0, i % tpb))],
        out_specs=(tr_spec(1), nat_spec(2), tr_spec(3), nat_spec(4), nat_spec(6), nat_spec(7),
                   nat_spec(8), nrm_spec(1), nrm_spec(2)),
        scratch_shapes=[pltpu.VMEM((tm, d), BF16), pltpu.VMEM((tm, d), F32),
                        pltpu.VMEM((tm, d), F32)],
        compiler_params=pltpu.CompilerParams(
            dimension_semantics=("arbitrary", "arbitrary"), vmem_limit_bytes=VMEM_LIMIT),
        name="in_proj",
    )(x2, n1, sc1, sh1, w_in_bf, gq, gk, cos_t, sin_t)


def _diff_attn_kernel(qt_ref, k_ref, vt_ref, qn_ref, kn_ref, lam_ref, sw_ref, o_ref, acc0, acc1,
                      *, tk, lambda_init):
    seq = k_ref.shape[0]
    tq = qt_ref.shape[1]
    qt = qt_ref[...]
    first = lax.broadcasted_iota(jnp.int32, qt.shape, 0) < HEAD_DIM
    zero = jnp.zeros_like(qt)
    qz = (jnp.where(first, qt, zero), jnp.where(first, zero, qt))
    accs = (acc0, acc1)
    acc0[...] = jnp.zeros_like(acc0)
    acc1[...] = jnp.zeros_like(acc1)
    n_chunks = seq // tk

    def load(j):
        off = pl.multiple_of(j * tk, tk)
        return k_ref[pl.ds(off, tk), :], vt_ref[:, pl.ds(off, tk)]

    def plain_body(j, carry):
        kk, vt = load(j)
        new = []
        for c in range(2):
            p = jnp.exp2(_dot(kk, qz[c]))
            new.append(carry[c] + jnp.sum(p, axis=0, keepdims=True))
            accs[c][...] += _dot(vt, p.astype(BF16))
        return tuple(new)

    def online_body(j, carry):
        kk, vt = load(j)
        new = []
        for c in range(2):
            m, l = carry[2 * c], carry[2 * c + 1]
            s = _dot(kk, qz[c])
            m_new = jnp.maximum(m, jnp.max(s, axis=0, keepdims=True))
            alpha = jnp.exp2(m - m_new)
            p = jnp.exp2(s - m_new)
            l = alpha * l + jnp.sum(p, axis=0, keepdims=True)
            accs[c][...] = alpha * accs[c][...] + _dot(vt, p.astype(BF16))
            new += [m_new, l]
        return tuple(new)

    m_init = jnp.full((1, tq), NEG_BIG, F32)
    l_init = jnp.zeros((1, tq), F32)

    def plain():
        return lax.fori_loop(0, n_chunks, plain_body, (l_init, l_init))

    def online():
        _, l0, _, l1 = lax.fori_loop(0, n_chunks, online_body, (m_init, l_init, m_init, l_init))
        return l0, l1

    bound2 = jnp.max(jnp.max(qn_ref[...], axis=-1, keepdims=True)
                     * jnp.max(kn_ref[...], axis=-1, keepdims=True))
    l0, l1 = lax.cond(bound2 <= SAFE_EXP2_BOUND * SAFE_EXP2_BOUND, plain, online)

    lq = lam_ref[...]
    lam = (jnp.exp(jnp.sum(lq[0:1] * lq[1:2], axis=-1, keepdims=True))
           - jnp.exp(jnp.sum(lq[2:3] * lq[3:4], axis=-1, keepdims=True)) + lambda_init)
    o = acc0[...] / l0 - lam * (acc1[...] / l1)
    o = o * lax.rsqrt(jnp.mean(o * o, axis=0, keepdims=True) + SUBLN_EPS)
    o = o * sw_ref[...] * (1.0 - lambda_init)
    o_ref[...] = o.T.astype(BF16)


def _diff_attn(qt, k3, vt, qn, kn, lam_vecs, subln_col, lambda_init):
    bsz, d, seq = qt.shape
    tq = min(seq, 1024)
    tk = min(seq, 4096)
    kern = functools.partial(_diff_attn_kernel, tk=tk, lambda_init=lambda_init)
    return pl.pallas_call(
        kern,
        out_shape=jax.ShapeDtypeStruct((bsz, seq, d), BF16),
        grid=(bsz, N_HEADS, seq // tq),
        in_specs=[pl.BlockSpec((None, V_DIM, tq), lambda b, h, i: (b, h, i)),
                  pl.BlockSpec((None, seq, V_DIM), lambda b, h, i: (b, 0, h)),
                  pl.BlockSpec((None, V_DIM, seq), lambda b, h, i: (b, h, 0)),
                  pl.BlockSpec((None, None, 2, tq), lambda b, h, i: (b, h, 0, i)),
                  pl.BlockSpec((None, None, 2, seq), lambda b, h, i: (b, h, 0, 0)),
                  pl.BlockSpec((4, HEAD_DIM), lambda b, h, i: (0, 0)),
                  pl.BlockSpec((V_DIM, 1), lambda b, h, i: (0, 0))],
        out_specs=pl.BlockSpec((None, tq, V_DIM), lambda b, h, i: (b, i, h)),
        scratch_shapes=[pltpu.VMEM((V_DIM, tq), F32), pltpu.VMEM((V_DIM, tq), F32)],
        compiler_params=pltpu.CompilerParams(
            dimension_semantics=("arbitrary", "arbitrary", "arbitrary"),
            vmem_limit_bytes=VMEM_LIMIT),
        name="diff_attn",
    )(qt, k3, vt, qn, kn, lam_vecs, subln_col)


def _post_mix_kernel(o_ref, cb_ref, z_ref, zp_ref, zn_ref, sga_ref, sgc_ref, x_ref,
                     g1_ref, sc2_ref, sh2_ref, n2_ref, cw_ref, wao_ref, wco_ref, wout_ref,
                     wrh_ref, wrl_ref, br_ref,
                     x1_ref, h2_ref, route_ref, cnt_ref, u_scr, *, tpb, n_sub):
    i = pl.program_id(0)
    tm = x_ref.shape[0]

    @pl.when(i == 0)
    def _():
        cnt_ref[...] = jnp.zeros_like(cnt_ref)

    z = z_ref[...].astype(F32)
    rows = lax.broadcasted_iota(jnp.int32, z.shape, 0)
    halo_rows = zp_ref.shape[0]
    prev_row = zp_ref[halo_rows - 1:halo_rows, :].astype(F32)
    next_row = zn_ref[0:1, :].astype(F32)
    prev_row = jnp.where(i % tpb == 0, jnp.zeros_like(prev_row), prev_row)
    next_row = jnp.where(i % tpb == tpb - 1, jnp.zeros_like(next_row), next_row)
    z_m1 = jnp.where(rows == 0, prev_row, pltpu.roll(z, 1, 0))
    z_p1 = jnp.where(rows == tm - 1, next_row, pltpu.roll(z, tm - 1, 0))
    cw = cw_ref[...]
    conv = z_m1 * cw[0:1] + z * cw[1:2] + z_p1 * cw[2:3]
    u_scr[...] = (cb_ref[...].astype(F32) * conv).astype(BF16)

    ts = tm // n_sub
    lane = lax.broadcasted_iota(jnp.int32, (ts, LANES), 1)
    r_i = lax.broadcasted_iota(jnp.int32, (ts, ts), 0)
    c_i = lax.broadcasted_iota(jnp.int32, (ts, ts), 1)
    lower = (r_i > c_i).astype(BF16)
    counts = cnt_ref[0:1, :]
    for s in range(n_sub):
        rs = slice(s * ts, (s + 1) * ts)
        y_attn = _dot(o_ref[rs, :], wao_ref[...])
        y_conv = _dot(u_scr[rs, :], wco_ref[...])
        m = sga_ref[rs, :].astype(F32) * y_attn + sgc_ref[rs, :].astype(F32) * y_conv
        x1 = x_ref[rs, :] + g1_ref[...] * _dot(m.astype(BF16), wout_ref[...])
        x1_ref[rs, :] = x1

        h2 = x1 * lax.rsqrt(jnp.mean(x1 * x1, axis=-1, keepdims=True) + RMS_EPS) * n2_ref[...]
        h2 = h2 * (1.0 + sc2_ref[...]) + sh2_ref[...]
        _store_tile_rows(h2_ref, s * ts, h2)

        h_hi = h2.astype(BF16)
        h_lo = (h2 - h_hi.astype(F32)).astype(BF16)
        logits = (_dot(h_hi, wrh_ref[...]) + _dot(h_lo, wrh_ref[...]) + _dot(h_hi, wrl_ref[...])
                  + br_ref[...])

        work = logits
        vals, idxs = [], []
        for _ in range(TOP_K):
            mx = jnp.max(work, axis=-1, keepdims=True)
            ix = jnp.min(jnp.where(work == mx, lane, LANES), axis=-1, keepdims=True)
            vals.append(mx)
            idxs.append(ix)
            work = jnp.where(lane == ix, 2.0 * NEG_BIG, work)
        exps = [jnp.exp(v - vals[0]) for v in vals]
        den = exps[0] + exps[1] + exps[2] + exps[3]

        sel = (work == 2.0 * NEG_BIG).astype(BF16)
        before = _dot(lower, sel) + counts
        counts = counts + jnp.sum(sel.astype(F32), axis=0, keepdims=True)

        route = jnp.zeros(logits.shape, F32)
        for k in range(TOP_K):
            rank = jnp.sum(jnp.where(lane == idxs[k], before, 0.0), axis=-1, keepdims=True)
            route = jnp.where(lane == k, exps[k] / den, route)
            route = jnp.where(lane == TOP_K + k, idxs[k].astype(F32), route)
            route = jnp.where(lane == 2 * TOP_K + k, rank, route)
        route_ref[rs, :] = route
    cnt_ref[...] = jnp.broadcast_to(counts, cnt_ref.shape)


def _post_mix(o2, cb, z, sga, sgc, x2, g1, sc2, sh2, n2, conv_w, wao, wco, wout,
              wr_hi, wr_lo, br_pad, seq):
    n, d = x2.shape
    tm = min(seq, 512)
    tpb = seq // tm
    halo = 16
    hb = tm // halo
    last_hb = n // halo - 1
    row = lambda i: (i, 0)
    const = lambda i: (0, 0)
    mod = lambda i: (i // tpb, 0, 0)
    wspec = pl.BlockSpec((d, d), const)
    kern = functools.partial(_post_mix_kernel, tpb=tpb, n_sub=1)
    return pl.pallas_call(
        kern,
        out_shape=(jax.ShapeDtypeStruct((n, d), F32),
                   jax.ShapeDtypeStruct((n * (d // LANES), LANES), F32),
                   jax.ShapeDtypeStruct((n, LANES), F32), jax.ShapeDtypeStruct((8, LANES), F32)),
        grid=(n // tm,),
        in_specs=[pl.BlockSpec((tm, d), row), pl.BlockSpec((tm, d), row), pl.BlockSpec((tm, d), row),
                  pl.BlockSpec((halo, d), lambda i: (jnp.maximum(i * hb - 1, 0), 0)),
                  pl.BlockSpec((halo, d), lambda i: (jnp.minimum((i + 1) * hb, last_hb), 0)),
                  pl.BlockSpec((tm, d), row), pl.BlockSpec((tm, d), row), pl.BlockSpec((tm, d), row),
                  pl.BlockSpec((None, 1, d), mod), pl.BlockSpec((None, 1, d), mod),
                  pl.BlockSpec((None, 1, d), mod),
                  pl.BlockSpec((1, d), const), pl.BlockSpec((3, d), const),
                  wspec, wspec, wspec,
                  pl.BlockSpec((d, LANES), const), pl.BlockSpec((d, LANES), const),
                  pl.BlockSpec((1, LANES), const)],
        out_specs=(pl.BlockSpec((tm, d), row), pl.BlockSpec((tm * (d // LANES), LANES), row),
                   pl.BlockSpec((tm, LANES), row), pl.BlockSpec((8, LANES), const)),
        scratch_shapes=[pltpu.VMEM((tm, d), BF16)],
        compiler_params=pltpu.CompilerParams(
            dimension_semantics=("arbitrary",), vmem_limit_bytes=VMEM_LIMIT),
        name="post_mix",
    )(o2, cb, z, z, z, sga, sgc, x2, g1, sc2, sh2, n2, conv_w, wao, wco, wout, wr_hi, wr_lo, br_pad)


def _tile_rows(ref, row, n_rows=1):
    start = pl.multiple_of(row * TILE_ROW, TILE_ROW)
    return ref.at[pl.ds(start, n_rows * TILE_ROW)]


def _row_copy(src, dst, s, t, sem):
    return pltpu.make_async_copy(_tile_rows(src, s), _tile_rows(dst, t), sem)


def _dispatch_kernel(dest_ref, ends_ref, h2_ref, xs_hbm, zbuf, sem, zsem, *, tc, nblk):
    base = pl.program_id(0) * tc

    @pl.when(pl.program_id(0) == 0)
    def _():
        zbuf[...] = jnp.zeros_like(zbuf)

        def zero_block(row):
            return pltpu.make_async_copy(zbuf, _tile_rows(xs_hbm, row, MOE_ROWS), zsem)

        def nonempty(e):
            return ends_ref[e] > (ends_ref[e - 1] if e else 0)

        total = ends_ref[N_EXPERTS - 1]
        n_tail = nblk - total // MOE_ROWS

        def tail_start(b, carry):
            zero_block(total + b * MOE_ROWS).start()
            return carry

        def tail_wait(b, carry):
            zero_block(0).wait()
            return carry

        for e in range(N_EXPERTS):
            @pl.when(nonempty(e))
            def _(e=e):
                zero_block(ends_ref[e] - MOE_ROWS).start()
        lax.fori_loop(0, n_tail, tail_start, 0)
        for e in range(N_EXPERTS):
            @pl.when(nonempty(e))
            def _():
                zero_block(0).wait()
        lax.fori_loop(0, n_tail, tail_wait, 0)

    def issue(t, carry):
        for k in range(TOP_K):
            _row_copy(h2_ref, xs_hbm, t, dest_ref[(base + t) * TOP_K + k], sem).start(priority=k % 2)
        return carry

    lax.fori_loop(0, tc, issue, 0, unroll=4)

    for _ in range(TOP_K):
        pltpu.make_async_copy(h2_ref, _tile_rows(xs_hbm, 0, tc), sem).wait()


def _moe_dispatch(dest_flat, ends, h2_tiles, n, p_rows):
    tc = min(n, 512)
    kern = functools.partial(_dispatch_kernel, tc=tc, nblk=p_rows // MOE_ROWS)
    return pl.pallas_call(
        kern,
        out_shape=jax.ShapeDtypeStruct((p_rows * TILE_ROW, LANES), F32),
        grid_spec=pltpu.PrefetchScalarGridSpec(
            num_scalar_prefetch=2, grid=(n // tc,),
            in_specs=[pl.BlockSpec((tc * TILE_ROW, LANES), lambda i, dr, en: (i, 0))],
            out_specs=pl.BlockSpec(memory_space=pl.ANY),
            scratch_shapes=[pltpu.VMEM((MOE_ROWS * TILE_ROW, LANES), F32),
                            pltpu.SemaphoreType.DMA, pltpu.SemaphoreType.DMA]),
        compiler_params=pltpu.CompilerParams(dimension_semantics=("arbitrary",)),
        name="moe_dispatch",
    )(dest_flat, ends, h2_tiles)


_NT = (((1,), (1,)), ((), ()))


def _ffn_kernel(be_ref, nb_ref, fresh_ref, slot_ref, next_ref,
                xs_ref, wgu_hbm, wd_hbm, bg_ref, bu_ref, bd_ref, ys_ref,
                wgu_buf, wd_buf, wt_scr, wg_scr, wu_scr, wd_scr, sem):
    i = pl.program_id(0)
    used = i < nb_ref[0]
    fresh = fresh_ref[i] == 1
    slot = slot_ref[i]

    def weight_copies(e, s):
        return (pltpu.make_async_copy(wgu_hbm.at[e], wgu_buf.at[s], sem.at[0, s]),
                pltpu.make_async_copy(wd_hbm.at[e], wd_buf.at[s], sem.at[1, s]))

    @pl.when(jnp.logical_not(used))
    def _():
        ys_ref[...] = jnp.zeros_like(ys_ref)

    @pl.when(i == 0)
    def _():
        for cp in weight_copies(be_ref[0], 0):
            cp.start()

    @pl.when(jnp.logical_and(used, fresh))
    def _():
        for cp in weight_copies(be_ref[i], slot):
            cp.wait()

        @pl.when(next_ref[i] >= 0)
        def _():
            for cp in weight_copies(next_ref[i], 1 - slot):
                cp.start()

        _, d, f2 = wgu_buf.shape
        for c in range(d // LANES):
            cols = slice(c * LANES, (c + 1) * LANES)
            wt_scr[c] = wgu_buf[slot, cols, :].T
            wg_scr[:, cols] = wt_scr[c, pl.ds(0, f2 // 2, stride=2), :].astype(BF16)
            wu_scr[:, cols] = wt_scr[c, pl.ds(1, f2 // 2, stride=2), :].astype(BF16)
        wd_scr[...] = wd_buf[slot].astype(BF16)

    @pl.when(used)
    def _():
        x = jnp.concatenate(_load_tile_rows(xs_ref, TILE_ROW), axis=1).astype(BF16)
        gate = lax.dot_general(x, wg_scr[...], _NT, preferred_element_type=F32) + bg_ref[...]
        up = lax.dot_general(x, wu_scr[...], _NT, preferred_element_type=F32) + bu_ref[...]
        gate = jnp.minimum(gate, SWIGLU_LIMIT)
        up = jnp.clip(up, -SWIGLU_LIMIT, SWIGLU_LIMIT)
        glu = gate * jax.nn.sigmoid(gate * SWIGLU_ALPHA)
        mid = ((up + 1.0) * glu).astype(BF16)
        _store_tile_rows(ys_ref, 0, _dot(mid, wd_scr[...]) + bd_ref[...])


def _moe_ffn(block_e, n_used, fresh, slot, next_e, xs, wgu, wd, bg, bu, bd):
    p_rows = xs.shape[0] // TILE_ROW
    _, d, f2 = wgu.shape
    assert d == TILE_ROW * LANES
    f = f2 // 2
    nblk = p_rows // MOE_ROWS
    rows = lambda i, be, nb, *_: (jnp.minimum(i, nb[0] - 1), 0)
    bmap = lambda i, be, *_: (be[i], 0, 0)
    return pl.pallas_call(
        _ffn_kernel,
        out_shape=jax.ShapeDtypeStruct((p_rows * TILE_ROW, LANES), F32),
        grid_spec=pltpu.PrefetchScalarGridSpec(
            num_scalar_prefetch=5, grid=(nblk,),
            in_specs=[pl.BlockSpec((MOE_ROWS * TILE_ROW, LANES), rows),
                      pl.BlockSpec(memory_space=pl.ANY), pl.BlockSpec(memory_space=pl.ANY),
                      pl.BlockSpec((None, 1, f), bmap), pl.BlockSpec((None, 1, f), bmap),
                      pl.BlockSpec((None, 1, d), bmap)],
            out_specs=pl.BlockSpec((MOE_ROWS * TILE_ROW, LANES), lambda i, *_: (i, 0)),
            scratch_shapes=[pltpu.VMEM((2, d, f2), F32), pltpu.VMEM((2, f, d), F32),
                            pltpu.VMEM((d // LANES, f2, LANES), F32), pltpu.VMEM((f, d), BF16),
                            pltpu.VMEM((f, d), BF16), pltpu.VMEM((f, d), BF16),
                            pltpu.SemaphoreType.DMA((2, 2))]),
        compiler_params=pltpu.CompilerParams(
            dimension_semantics=("arbitrary",), vmem_limit_bytes=VMEM_LIMIT),
        name="moe_ffn",
    )(block_e, n_used, fresh, slot, next_e, xs, wgu, wd, bg, bu, bd)


def _combine_kernel(dest_ref, ys_hbm, x1_ref, route_ref, g2_ref, o_ref, buf, sem, *, tc):
    base = pl.program_id(0) * tc

    def issue(t, carry):
        for k in range(TOP_K):
            _row_copy(ys_hbm, buf.at[k], dest_ref[(base + t) * TOP_K + k], t, sem).start(priority=k % 2)
        return carry

    lax.fori_loop(0, tc, issue, 0, unroll=4)

    for k in range(TOP_K):
        pltpu.make_async_copy(_tile_rows(ys_hbm, 0, tc), buf.at[k], sem).wait()

    route = route_ref[...]
    for c in range(TILE_ROW):
        cols = slice(c * LANES, (c + 1) * LANES)
        y = buf[0, pl.ds(c, tc, stride=TILE_ROW), :] * route[:, 0:1]
        for k in range(1, TOP_K):
            y = y + buf[k, pl.ds(c, tc, stride=TILE_ROW), :] * route[:, k:k + 1]
        o_ref[:, cols] = x1_ref[:, cols] + g2_ref[:, cols] * y


def _moe_combine(dest_flat, ys, x1, route, g2, seq):
    n, d = x1.shape
    assert d == TILE_ROW * LANES
    tc = min(seq, 256)
    tpb = seq // tc
    kern = functools.partial(_combine_kernel, tc=tc)
    return pl.pallas_call(
        kern,
        out_shape=jax.ShapeDtypeStruct((n, d), F32),
        grid_spec=pltpu.PrefetchScalarGridSpec(
            num_scalar_prefetch=1, grid=(n // tc,),
            in_specs=[pl.BlockSpec(memory_space=pl.ANY),
                      pl.BlockSpec((tc, d), lambda i, dr: (i, 0)),
                      pl.BlockSpec((tc, LANES), lambda i, dr: (i, 0)),
                      pl.BlockSpec((None, 1, d), lambda i, dr: (i // tpb, 0, 0))],
            out_specs=pl.BlockSpec((tc, d), lambda i, dr: (i, 0)),
            scratch_shapes=[pltpu.VMEM((TOP_K, tc * TILE_ROW, LANES), F32),
                            pltpu.SemaphoreType.DMA]),
        compiler_params=pltpu.CompilerParams(
            dimension_semantics=("arbitrary",), vmem_limit_bytes=VMEM_LIMIT),
        name="moe_combine",
    )(dest_flat, ys, x1, route, g2)


def _rope_tables(seq):
    inv_freq = ROPE_THETA ** (-jnp.arange(0, HEAD_DIM, 2, dtype=F32) / HEAD_DIM)
    ang = inv_freq[:, None] * jnp.arange(seq, dtype=F32)[None, :]
    cos_t = jnp.concatenate([jnp.cos(ang), jnp.cos(ang)], axis=0)
    sin_t = jnp.concatenate([-jnp.sin(ang), jnp.sin(ang)], axis=0)
    return cos_t, sin_t


def _layer(x, c, l, lambda_init, w_ada, b_ada, norm1_w, w_in, q_norm_w, k_norm_w, lambda_q1,
           lambda_k1, lambda_q2, lambda_k2, subln_w, w_attn_o, conv_w, w_conv_o, w_out, norm2_w,
           w_router, b_router, w_gate_up, b_gate_up, w_down, b_down):
    bsz, seq, d = x.shape
    n = bsz * seq
    x2 = x.reshape(n, d)

    mod = _ada_mod(c, w_ada[l], b_ada[l])
    sh1, sc1, g1, sh2, sc2, g2 = [m.reshape(bsz, 1, d) for m in jnp.split(mod, 6, axis=-1)]

    cos_t, sin_t = _rope_tables(seq)
    qt, k, vt, cb, z, sga, sgc, qn, kn = _in_proj(
        x2, norm1_w[l].reshape(1, d), sc1, sh1, w_in[l].astype(BF16),
        q_norm_w[l].reshape(HEAD_DIM, 1) * Q_SCALE, k_norm_w[l].reshape(HEAD_DIM, 1),
        cos_t, sin_t, bsz, seq)

    lam_vecs = jnp.stack([lambda_q1[l], lambda_k1[l], lambda_q2[l], lambda_k2[l]]).astype(F32)
    o = _diff_attn(qt, k.reshape(bsz, seq, d), vt, qn, kn, lam_vecs, subln_w[l].reshape(V_DIM, 1),
                   lambda_init)

    wr = jnp.zeros((d, LANES), F32).at[:, :N_EXPERTS].set(w_router[l].astype(F32))
    wr_hi = wr.astype(BF16)
    wr_lo = (wr - wr_hi.astype(F32)).astype(BF16)
    br_pad = jnp.full((1, LANES), NEG_BIG, F32).at[0, :N_EXPERTS].set(b_router[l].astype(F32))
    x1, h2, route, cnt = _post_mix(
        o.reshape(n, d), cb, z, sga, sgc, x2, g1, sc2, sh2, norm2_w[l].reshape(1, d), conv_w[l],
        w_attn_o[l].astype(BF16), w_conv_o[l].astype(BF16), w_out[l].astype(BF16),
        wr_hi, wr_lo, br_pad, seq)

    counts = cnt[0, :N_EXPERTS].astype(jnp.int32)
    padded = ((counts + MOE_ROWS - 1) // MOE_ROWS) * MOE_ROWS
    ends = jnp.cumsum(padded)
    start = ends - padded
    p_rows = n * TOP_K + N_EXPERTS * MOE_ROWS
    nblk = p_rows // MOE_ROWS
    top_e = route[:, TOP_K:2 * TOP_K].astype(jnp.int32)
    rank = route[:, 2 * TOP_K:3 * TOP_K].astype(jnp.int32)
    experts = jnp.arange(N_EXPERTS, dtype=jnp.int32)
    seg_start = jnp.sum(jnp.where(top_e[..., None] == experts, start, 0), axis=-1)
    dest = (seg_start + rank).reshape(-1)
    block_start = jnp.arange(nblk, dtype=jnp.int32) * MOE_ROWS
    block_e = jnp.sum((ends[None, :] <= block_start[:, None]).astype(jnp.int32), axis=1)
    block_e = jnp.minimum(block_e, N_EXPERTS - 1)
    n_used = (ends[-1:] // MOE_ROWS).astype(jnp.int32)
    nonempty = padded > 0
    later = jnp.logical_and(nonempty[None, :], experts[None, :] > experts[:, None])
    next_of = jnp.min(jnp.where(later, experts[None, :], N_EXPERTS), axis=1)
    next_of = jnp.where(next_of == N_EXPERTS, -1, next_of).astype(jnp.int32)
    slot_of = ((jnp.cumsum(nonempty.astype(jnp.int32)) - 1) % 2).astype(jnp.int32)
    fresh = jnp.concatenate([jnp.ones((1,), jnp.int32),
                             (block_e[1:] != block_e[:-1]).astype(jnp.int32)])
    slot = slot_of[block_e]
    next_e = next_of[block_e]

    xs = _moe_dispatch(dest, ends.astype(jnp.int32), h2, n, p_rows)
    ys = _moe_ffn(block_e, n_used, fresh, slot, next_e, xs, w_gate_up[l], w_down[l],
                  b_gate_up[l][:, None, 0::2], b_gate_up[l][:, None, 1::2], b_down[l][:, None, :])
    out = _moe_combine(dest, ys, x1, route, g2, seq)
    return out.reshape(bsz, seq, d)


def kernel(x, c, w_ada, b_ada, norm1_w, w_in, q_norm_w, k_norm_w, lambda_q1, lambda_k1, lambda_q2,
           lambda_k2, subln_w, w_attn_o, conv_w, w_conv_o, w_out, norm2_w, w_router, b_router,
           w_gate_up, b_gate_up, w_down, b_down):
    depth = w_ada.shape[0]
    for l in range(depth):
        lambda_init = 0.8 - 0.6 * math.exp(-0.3 * l)
        x = _layer(x, c, l, lambda_init, w_ada, b_ada, norm1_w, w_in, q_norm_w, k_norm_w,
                   lambda_q1, lambda_k1, lambda_q2, lambda_k2, subln_w, w_attn_o, conv_w,
                   w_conv_o, w_out, norm2_w, w_router, b_router, w_gate_up, b_gate_up,
                   w_down, b_down)
    return x
```

```python
import functools
import math

import jax
import jax.numpy as jnp
from jax import lax
from jax.experimental import pallas as pl
from jax.experimental.pallas import tpu as pltpu

N_HEADS = 8
HEAD_DIM = 64
V_DIM = 2 * HEAD_DIM
IN_BLOCKS = ("q", "k", "v", "conv_b", "conv_c", "conv_x", "gate_attn", "gate_conv")
N_EXPERTS = 32
TOP_K = 4
SWIGLU_LIMIT = 7.0
SWIGLU_ALPHA = 1.702
ROPE_THETA = 10000.0
RMS_EPS = 1e-6
SUBLN_EPS = 1e-5
LANES = 128
TILE_ROW = 8
MOE_ROWS = 256
NEG_BIG = -1e30
LOG2E = 1.4426950408889634
Q_SCALE = LOG2E / math.sqrt(HEAD_DIM)
SAFE_EXP2_BOUND = 80.0
VMEM_LIMIT = 56 * 1024 * 1024

F32 = jnp.float32
BF16 = jnp.bfloat16


def _dot(a, b):
    return jnp.dot(a, b, preferred_element_type=F32)


def _store_tile_rows(ref, row0, val):
    rows, d = val.shape
    chunks = d // LANES
    for c in range(chunks):
        ref[pl.ds(row0 * chunks + c, rows, stride=chunks), :] = val[:, c * LANES:(c + 1) * LANES]


def _load_tile_rows(ref, chunks):
    rows = ref.shape[0] // chunks
    return [ref[pl.ds(c, rows, stride=chunks), :] for c in range(chunks)]


def _ada_kernel(ct_ref, w_ref, b_ref, o_ref):
    ct = ct_ref[...]
    s = ct * jax.nn.sigmoid(ct)
    w = w_ref[...]
    for b in range(ct.shape[1]):
        o_ref[b:b + 1, :] = jnp.sum(w * s[:, b:b + 1], axis=0, keepdims=True) + b_ref[...]


def _ada_mod(c, w_ada, b_ada):
    bsz, d = c.shape
    n = w_ada.shape[1]
    tn = min(n, 1536)
    return pl.pallas_call(
        _ada_kernel,
        out_shape=jax.ShapeDtypeStruct((bsz, n), F32),
        grid=(n // tn,),
        in_specs=[pl.BlockSpec((d, bsz), lambda j: (0, 0)),
                  pl.BlockSpec((d, tn), lambda j: (0, j)),
                  pl.BlockSpec((1, tn), lambda j: (0, j))],
        out_specs=pl.BlockSpec((bsz, tn), lambda j: (0, j)),
        compiler_params=pltpu.CompilerParams(dimension_semantics=("arbitrary",)),
        name="ada_mod",
    )(c.T, w_ada, b_ada.reshape(1, n))


def _qk_norm_rope_t(y, g_col, cos_t, sin_t):
    tm, w = y.shape
    yt = y.T.reshape(w // HEAD_DIM, HEAD_DIM, tm)
    ms = jnp.mean(yt * yt, axis=1, keepdims=True)
    yn = yt * lax.rsqrt(ms + RMS_EPS) * g_col[None]
    half = HEAD_DIM // 2
    swapped = jnp.concatenate([yn[:, half:, :], yn[:, :half, :]], axis=1)
    out = yn * cos_t[None] + swapped * sin_t[None]
    norm2 = jnp.sum(out * out, axis=1).reshape(N_HEADS, 2, tm)
    return out.reshape(w, tm), norm2


def _in_proj_kernel(x_ref, n1_ref, sc_ref, sh_ref, w_ref, gq_ref, gk_ref, cos_ref, sin_ref,
                    qt_ref, k_ref, vt_ref, cb_ref, z_ref, sga_ref, sgc_ref, qn_ref, kn_ref, h_scr):
    x = x_ref[...]
    xn = x * lax.rsqrt(jnp.mean(x * x, axis=-1, keepdims=True) + RMS_EPS) * n1_ref[...]
    h_scr[...] = (xn * (1.0 + sc_ref[...]) + sh_ref[...]).astype(BF16)
    wcol = w_ref.shape[1] // len(IN_BLOCKS)

    def proj(name):
        c = IN_BLOCKS.index(name)
        return _dot(h_scr[...], w_ref[:, c * wcol:(c + 1) * wcol])

    qt, qn = _qk_norm_rope_t(proj("q"), gq_ref[...], cos_ref[...], sin_ref[...])
    qt_ref[...] = qt.astype(BF16)
    qn_ref[...] = qn
    kt, kn = _qk_norm_rope_t(proj("k"), gk_ref[...], cos_ref[...], sin_ref[...])
    k_ref[...] = kt.T.astype(BF16)
    kn_ref[...] = kn
    vt_ref[...] = proj("v").T.astype(BF16)
    cb_ref[...] = proj("conv_b").astype(BF16)
    z_ref[...] = (proj("conv_c") * proj("conv_x")).astype(BF16)
    sga_ref[...] = jax.nn.sigmoid(proj("gate_attn")).astype(BF16)
    sgc_ref[...] = jax.nn.sigmoid(proj("gate_conv")).astype(BF16)


def _in_proj(x2, n1, sc1, sh1, w_in_bf, gq, gk, cos_t, sin_t, bsz, seq):
    n, d = x2.shape
    tm = min(seq, 512)
    tpb = seq // tm
    assert w_in_bf.shape[1] == len(IN_BLOCKS) * d
    row = lambda i: (i, 0)
    const = lambda i: (0, 0)
    tcol = lambda i: (i // tpb, 0, i % tpb)
    mod = lambda i: (i // tpb, 0, 0)
    nat = jax.ShapeDtypeStruct((n, d), BF16)
    tr = jax.ShapeDtypeStruct((bsz, d, seq), BF16)
    nrm = jax.ShapeDtypeStruct((bsz, N_HEADS, 2, seq), F32)
    nat_spec = pl.BlockSpec((tm, d), row)
    tr_spec = pl.BlockSpec((None, d, tm), tcol)
    nrm_spec = pl.BlockSpec((None, N_HEADS, 2, tm), lambda i: (i // tpb, 0, 0, i % tpb))
    return pl.pallas_call(
        _in_proj_kernel,
        out_shape=(tr, nat, tr, nat, nat, nat, nat, nrm, nrm),
        grid=(n // tm,),
        in_specs=[pl.BlockSpec((tm, d), row),
                  pl.BlockSpec((1, d), const),
                  pl.BlockSpec((None, 1, d), mod),
                  pl.BlockSpec((None, 1, d), mod),
                  pl.BlockSpec(w_in_bf.shape, const),
                  pl.BlockSpec((HEAD_DIM, 1), const),
                  pl.BlockSpec((HEAD_DIM, 1), const),
                  pl.BlockSpec((HEAD_DIM, tm), lambda i: (0, i % tpb)),
                  pl.BlockSpec((HEAD_DIM, tm), lambda i: (0, i % tpb))],
        out_specs=(tr_spec, nat_spec, tr_spec, nat_spec, nat_spec, nat_spec, nat_spec,
                   nrm_spec, nrm_spec),
        scratch_shapes=[pltpu.VMEM((tm, d), BF16)],
        compiler_params=pltpu.CompilerParams(
            dimension_semantics=("arbitrary",), vmem_limit_bytes=VMEM_LIMIT),
        name="in_proj",
    )(x2, n1, sc1, sh1, w_in_bf, gq, gk, cos_t, sin_t)


def _diff_attn_kernel(qt_ref, k_ref, vt_ref, qn_ref, kn_ref, lam_ref, sw_ref, o_ref, acc0, acc1,
                      *, tk, lambda_init):
    seq = k_ref.shape[0]
    tq = qt_ref.shape[1]
    qt = qt_ref[...]
    first = lax.broadcasted_iota(jnp.int32, qt.shape, 0) < HEAD_DIM
    zero = jnp.zeros_like(qt)
    qz = (jnp.where(first, qt, zero), jnp.where(first, zero, qt))
    accs = (acc0, acc1)
    acc0[...] = jnp.zeros_like(acc0)
    acc1[...] = jnp.zeros_like(acc1)
    n_chunks = seq // tk

    def load(j):
        off = pl.multiple_of(j * tk, tk)
        return k_ref[pl.ds(off, tk), :], vt_ref[:, pl.ds(off, tk)]

    def plain_body(j, carry):
        kk, vt = load(j)
        new = []
        for c in range(2):
            p = jnp.exp2(_dot(kk, qz[c]))
            new.append(carry[c] + jnp.sum(p, axis=0, keepdims=True))
            accs[c][...] += _dot(vt, p.astype(BF16))
        return tuple(new)

    def online_body(j, carry):
        kk, vt = load(j)
        new = []
        for c in range(2):
            m, l = carry[2 * c], carry[2 * c + 1]
            s = _dot(kk, qz[c])
            m_new = jnp.maximum(m, jnp.max(s, axis=0, keepdims=True))
            alpha = jnp.exp2(m - m_new)
            p = jnp.exp2(s - m_new)
            l = alpha * l + jnp.sum(p, axis=0, keepdims=True)
            accs[c][...] = alpha * accs[c][...] + _dot(vt, p.astype(BF16))
            new += [m_new, l]
        return tuple(new)

    m_init = jnp.full((1, tq), NEG_BIG, F32)
    l_init = jnp.zeros((1, tq), F32)

    def plain():
        return lax.fori_loop(0, n_chunks, plain_body, (l_init, l_init))

    def online():
        _, l0, _, l1 = lax.fori_loop(0, n_chunks, online_body, (m_init, l_init, m_init, l_init))
        return l0, l1

    bound2 = jnp.max(jnp.max(qn_ref[...], axis=-1, keepdims=True)
                     * jnp.max(kn_ref[...], axis=-1, keepdims=True))
    l0, l1 = lax.cond(bound2 <= SAFE_EXP2_BOUND * SAFE_EXP2_BOUND, plain, online)

    lq = lam_ref[...]
    lam = (jnp.exp(jnp.sum(lq[0:1] * lq[1:2], axis=-1, keepdims=True))
           - jnp.exp(jnp.sum(lq[2:3] * lq[3:4], axis=-1, keepdims=True)) + lambda_init)
    o = acc0[...] / l0 - lam * (acc1[...] / l1)
    o = o * lax.rsqrt(jnp.mean(o * o, axis=0, keepdims=True) + SUBLN_EPS)
    o = o * sw_ref[...] * (1.0 - lambda_init)
    o_ref[...] = o.T.astype(BF16)


def _diff_attn(qt, k3, vt, qn, kn, lam_vecs, subln_col, lambda_init):
    bsz, d, seq = qt.shape
    tq = min(seq, 1024)
    tk = min(seq, 4096)
    kern = functools.partial(_diff_attn_kernel, tk=tk, lambda_init=lambda_init)
    return pl.pallas_call(
        kern,
        out_shape=jax.ShapeDtypeStruct((bsz, seq, d), BF16),
        grid=(bsz, N_HEADS, seq // tq),
        in_specs=[pl.BlockSpec((None, V_DIM, tq), lambda b, h, i: (b, h, i)),
                  pl.BlockSpec((None, seq, V_DIM), lambda b, h, i: (b, 0, h)),
                  pl.BlockSpec((None, V_DIM, seq), lambda b, h, i: (b, h, 0)),
                  pl.BlockSpec((None, None, 2, tq), lambda b, h, i: (b, h, 0, i)),
                  pl.BlockSpec((None, None, 2, seq), lambda b, h, i: (b, h, 0, 0)),
                  pl.BlockSpec((4, HEAD_DIM), lambda b, h, i: (0, 0)),
                  pl.BlockSpec((V_DIM, 1), lambda b, h, i: (0, 0))],
        out_specs=pl.BlockSpec((None, tq, V_DIM), lambda b, h, i: (b, i, h)),
        scratch_shapes=[pltpu.VMEM((V_DIM, tq), F32), pltpu.VMEM((V_DIM, tq), F32)],
        compiler_params=pltpu.CompilerParams(
            dimension_semantics=("arbitrary", "arbitrary", "arbitrary"),
            vmem_limit_bytes=VMEM_LIMIT),
        name="diff_attn",
    )(qt, k3, vt, qn, kn, lam_vecs, subln_col)


def _post_mix_kernel(o_ref, cb_ref, z_ref, zp_ref, zn_ref, sga_ref, sgc_ref, x_ref,
                     g1_ref, sc2_ref, sh2_ref, n2_ref, cw_ref, wao_ref, wco_ref, wout_ref,
                     wrh_ref, wrl_ref, br_ref,
                     x1_ref, h2_ref, route_ref, cnt_ref, u_scr, *, tpb, n_sub):
    i = pl.program_id(0)
    tm = x_ref.shape[0]

    @pl.when(i == 0)
    def _():
        cnt_ref[...] = jnp.zeros_like(cnt_ref)

    z = z_ref[...].astype(F32)
    rows = lax.broadcasted_iota(jnp.int32, z.shape, 0)
    halo_rows = zp_ref.shape[0]
    prev_row = zp_ref[halo_rows - 1:halo_rows, :].astype(F32)
    next_row = zn_ref[0:1, :].astype(F32)
    prev_row = jnp.where(i % tpb == 0, jnp.zeros_like(prev_row), prev_row)
    next_row = jnp.where(i % tpb == tpb - 1, jnp.zeros_like(next_row), next_row)
    z_m1 = jnp.where(rows == 0, prev_row, pltpu.roll(z, 1, 0))
    z_p1 = jnp.where(rows == tm - 1, next_row, pltpu.roll(z, tm - 1, 0))
    cw = cw_ref[...]
    conv = z_m1 * cw[0:1] + z * cw[1:2] + z_p1 * cw[2:3]
    u_scr[...] = (cb_ref[...].astype(F32) * conv).astype(BF16)

    ts = tm // n_sub
    lane = lax.broadcasted_iota(jnp.int32, (ts, LANES), 1)
    r_i = lax.broadcasted_iota(jnp.int32, (ts, ts), 0)
    c_i = lax.broadcasted_iota(jnp.int32, (ts, ts), 1)
    lower = (r_i > c_i).astype(BF16)
    counts = cnt_ref[0:1, :]
    for s in range(n_sub):
        rs = slice(s * ts, (s + 1) * ts)
        y_attn = _dot(o_ref[rs, :], wao_ref[...])
        y_conv = _dot(u_scr[rs, :], wco_ref[...])
        m = sga_ref[rs, :].astype(F32) * y_attn + sgc_ref[rs, :].astype(F32) * y_conv
        x1 = x_ref[rs, :] + g1_ref[...] * _dot(m.astype(BF16), wout_ref[...])
        x1_ref[rs, :] = x1

        h2 = x1 * lax.rsqrt(jnp.mean(x1 * x1, axis=-1, keepdims=True) + RMS_EPS) * n2_ref[...]
        h2 = h2 * (1.0 + sc2_ref[...]) + sh2_ref[...]
        _store_tile_rows(h2_ref, s * ts, h2)

        h_hi = h2.astype(BF16)
        h_lo = (h2 - h_hi.astype(F32)).astype(BF16)
        logits = (_dot(h_hi, wrh_ref[...]) + _dot(h_lo, wrh_ref[...]) + _dot(h_hi, wrl_ref[...])
                  + br_ref[...])

        work = logits
        vals, idxs = [], []
        for _ in range(TOP_K):
            mx = jnp.max(work, axis=-1, keepdims=True)
            ix = jnp.min(jnp.where(work == mx, lane, LANES), axis=-1, keepdims=True)
            vals.append(mx)
            idxs.append(ix)
            work = jnp.where(lane == ix, 2.0 * NEG_BIG, work)
        exps = [jnp.exp(v - vals[0]) for v in vals]
        den = exps[0] + exps[1] + exps[2] + exps[3]

        sel = (work == 2.0 * NEG_BIG).astype(BF16)
        before = _dot(lower, sel) + counts
        counts = counts + jnp.sum(sel.astype(F32), axis=0, keepdims=True)

        route = jnp.zeros(logits.shape, F32)
        for k in range(TOP_K):
            rank = jnp.sum(jnp.where(lane == idxs[k], before, 0.0), axis=-1, keepdims=True)
            route = jnp.where(lane == k, exps[k] / den, route)
            route = jnp.where(lane == TOP_K + k, idxs[k].astype(F32), route)
            route = jnp.where(lane == 2 * TOP_K + k, rank, route)
        route_ref[rs, :] = route
    cnt_ref[...] = jnp.broadcast_to(counts, cnt_ref.shape)


def _post_mix(o2, cb, z, sga, sgc, x2, g1, sc2, sh2, n2, conv_w, wao, wco, wout,
              wr_hi, wr_lo, br_pad, seq):
    n, d = x2.shape
    tm = min(seq, 512)
    tpb = seq // tm
    halo = 16
    hb = tm // halo
    last_hb = n // halo - 1
    row = lambda i: (i, 0)
    const = lambda i: (0, 0)
    mod = lambda i: (i // tpb, 0, 0)
    wspec = pl.BlockSpec((d, d), const)
    kern = functools.partial(_post_mix_kernel, tpb=tpb, n_sub=1)
    return pl.pallas_call(
        kern,
        out_shape=(jax.ShapeDtypeStruct((n, d), F32),
                   jax.ShapeDtypeStruct((n * (d // LANES), LANES), F32),
                   jax.ShapeDtypeStruct((n, LANES), F32), jax.ShapeDtypeStruct((8, LANES), F32)),
        grid=(n // tm,),
        in_specs=[pl.BlockSpec((tm, d), row), pl.BlockSpec((tm, d), row), pl.BlockSpec((tm, d), row),
                  pl.BlockSpec((halo, d), lambda i: (jnp.maximum(i * hb - 1, 0), 0)),
                  pl.BlockSpec((halo, d), lambda i: (jnp.minimum((i + 1) * hb, last_hb), 0)),
                  pl.BlockSpec((tm, d), row), pl.BlockSpec((tm, d), row), pl.BlockSpec((tm, d), row),
                  pl.BlockSpec((None, 1, d), mod), pl.BlockSpec((None, 1, d), mod),
                  pl.BlockSpec((None, 1, d), mod),
                  pl.BlockSpec((1, d), const), pl.BlockSpec((3, d), const),
                  wspec, wspec, wspec,
                  pl.BlockSpec((d, LANES), const), pl.BlockSpec((d, LANES), const),
                  pl.BlockSpec((1, LANES), const)],
        out_specs=(pl.BlockSpec((tm, d), row), pl.BlockSpec((tm * (d // LANES), LANES), row),
                   pl.BlockSpec((tm, LANES), row), pl.BlockSpec((8, LANES), const)),
        scratch_shapes=[pltpu.VMEM((tm, d), BF16)],
        compiler_params=pltpu.CompilerParams(
            dimension_semantics=("arbitrary",), vmem_limit_bytes=VMEM_LIMIT),
        name="post_mix",
    )(o2, cb, z, z, z, sga, sgc, x2, g1, sc2, sh2, n2, conv_w, wao, wco, wout, wr_hi, wr_lo, br_pad)


def _tile_rows(ref, row, n_rows=1):
    start = pl.multiple_of(row * TILE_ROW, TILE_ROW)
    return ref.at[pl.ds(start, n_rows * TILE_ROW)]


def _row_copy(src, dst, s, t, sem):
    return pltpu.make_async_copy(_tile_rows(src, s), _tile_rows(dst, t), sem)


def _dispatch_kernel(dest_ref, ends_ref, h2_ref, xs_hbm, zbuf, sem, zsem, *, tc, nblk):
    base = pl.program_id(0) * tc

    @pl.when(pl.program_id(0) == 0)
    def _():
        zbuf[...] = jnp.zeros_like(zbuf)

        def zero_block(row):
            return pltpu.make_async_copy(zbuf, _tile_rows(xs_hbm, row, MOE_ROWS), zsem)

        def nonempty(e):
            return ends_ref[e] > (ends_ref[e - 1] if e else 0)

        total = ends_ref[N_EXPERTS - 1]
        n_tail = nblk - total // MOE_ROWS

        def tail_start(b, carry):
            zero_block(total + b * MOE_ROWS).start()
            return carry

        def tail_wait(b, carry):
            zero_block(0).wait()
            return carry

        for e in range(N_EXPERTS):
            @pl.when(nonempty(e))
            def _(e=e):
                zero_block(ends_ref[e] - MOE_ROWS).start()
        lax.fori_loop(0, n_tail, tail_start, 0)
        for e in range(N_EXPERTS):
            @pl.when(nonempty(e))
            def _():
                zero_block(0).wait()
        lax.fori_loop(0, n_tail, tail_wait, 0)

    def issue(t, carry):
        for k in range(TOP_K):
            _row_copy(h2_ref, xs_hbm, t, dest_ref[(base + t) * TOP_K + k], sem).start(priority=k % 2)
        return carry

    lax.fori_loop(0, tc, issue, 0, unroll=4)

    for _ in range(TOP_K):
        pltpu.make_async_copy(h2_ref, _tile_rows(xs_hbm, 0, tc), sem).wait()


def _moe_dispatch(dest_flat, ends, h2_tiles, n, p_rows):
    tc = min(n, 512)
    kern = functools.partial(_dispatch_kernel, tc=tc, nblk=p_rows // MOE_ROWS)
    return pl.pallas_call(
        kern,
        out_shape=jax.ShapeDtypeStruct((p_rows * TILE_ROW, LANES), F32),
        grid_spec=pltpu.PrefetchScalarGridSpec(
            num_scalar_prefetch=2, grid=(n // tc,),
            in_specs=[pl.BlockSpec((tc * TILE_ROW, LANES), lambda i, dr, en: (i, 0))],
            out_specs=pl.BlockSpec(memory_space=pl.ANY),
            scratch_shapes=[pltpu.VMEM((MOE_ROWS * TILE_ROW, LANES), F32),
                            pltpu.SemaphoreType.DMA, pltpu.SemaphoreType.DMA]),
        compiler_params=pltpu.CompilerParams(dimension_semantics=("arbitrary",)),
        name="moe_dispatch",
    )(dest_flat, ends, h2_tiles)


_NT = (((1,), (1,)), ((), ()))


def _ffn_kernel(be_ref, nb_ref, fresh_ref, slot_ref, next_ref,
                xs_ref, wgu_hbm, wd_hbm, bg_ref, bu_ref, bd_ref, ys_ref,
                wgu_buf, wd_buf, wt_scr, wg_scr, wu_scr, wd_scr, sem):
    i = pl.program_id(0)
    used = i < nb_ref[0]
    fresh = fresh_ref[i] == 1
    slot = slot_ref[i]

    def weight_copies(e, s):
        return (pltpu.make_async_copy(wgu_hbm.at[e], wgu_buf.at[s], sem.at[0, s]),
                pltpu.make_async_copy(wd_hbm.at[e], wd_buf.at[s], sem.at[1, s]))

    @pl.when(jnp.logical_not(used))
    def _():
        ys_ref[...] = jnp.zeros_like(ys_ref)

    @pl.when(i == 0)
    def _():
        for cp in weight_copies(be_ref[0], 0):
            cp.start()

    @pl.when(jnp.logical_and(used, fresh))
    def _():
        for cp in weight_copies(be_ref[i], slot):
            cp.wait()

        @pl.when(next_ref[i] >= 0)
        def _():
            for cp in weight_copies(next_ref[i], 1 - slot):
                cp.start()

        _, d, f2 = wgu_buf.shape
        for c in range(d // LANES):
            cols = slice(c * LANES, (c + 1) * LANES)
            wt_scr[c] = wgu_buf[slot, cols, :].T
            wg_scr[:, cols] = wt_scr[c, pl.ds(0, f2 // 2, stride=2), :].astype(BF16)
            wu_scr[:, cols] = wt_scr[c, pl.ds(1, f2 // 2, stride=2), :].astype(BF16)
        wd_scr[...] = wd_buf[slot].astype(BF16)

    @pl.when(used)
    def _():
        x = jnp.concatenate(_load_tile_rows(xs_ref, TILE_ROW), axis=1).astype(BF16)
        gate = lax.dot_general(x, wg_scr[...], _NT, preferred_element_type=F32) + bg_ref[...]
        up = lax.dot_general(x, wu_scr[...], _NT, preferred_element_type=F32) + bu_ref[...]
        gate = jnp.minimum(gate, SWIGLU_LIMIT)
        up = jnp.clip(up, -SWIGLU_LIMIT, SWIGLU_LIMIT)
        glu = gate * jax.nn.sigmoid(gate * SWIGLU_ALPHA)
        mid = ((up + 1.0) * glu).astype(BF16)
        _store_tile_rows(ys_ref, 0, _dot(mid, wd_scr[...]) + bd_ref[...])


def _moe_ffn(block_e, n_used, fresh, slot, next_e, xs, wgu, wd, bg, bu, bd):
    p_rows = xs.shape[0] // TILE_ROW
    _, d, f2 = wgu.shape
    assert d == TILE_ROW * LANES
    f = f2 // 2
    nblk = p_rows // MOE_ROWS
    rows = lambda i, be, nb, *_: (jnp.minimum(i, nb[0] - 1), 0)
    bmap = lambda i, be, *_: (be[i], 0, 0)
    return pl.pallas_call(
        _ffn_kernel,
        out_shape=jax.ShapeDtypeStruct((p_rows * TILE_ROW, LANES), F32),
        grid_spec=pltpu.PrefetchScalarGridSpec(
            num_scalar_prefetch=5, grid=(nblk,),
            in_specs=[pl.BlockSpec((MOE_ROWS * TILE_ROW, LANES), rows),
                      pl.BlockSpec(memory_space=pl.ANY), pl.BlockSpec(memory_space=pl.ANY),
                      pl.BlockSpec((None, 1, f), bmap), pl.BlockSpec((None, 1, f), bmap),
                      pl.BlockSpec((None, 1, d), bmap)],
            out_specs=pl.BlockSpec((MOE_ROWS * TILE_ROW, LANES), lambda i, *_: (i, 0)),
            scratch_shapes=[pltpu.VMEM((2, d, f2), F32), pltpu.VMEM((2, f, d), F32),
                            pltpu.VMEM((d // LANES, f2, LANES), F32), pltpu.VMEM((f, d), BF16),
                            pltpu.VMEM((f, d), BF16), pltpu.VMEM((f, d), BF16),
                            pltpu.SemaphoreType.DMA((2, 2))]),
        compiler_params=pltpu.CompilerParams(
            dimension_semantics=("arbitrary",), vmem_limit_bytes=VMEM_LIMIT),
        name="moe_ffn",
    )(block_e, n_used, fresh, slot, next_e, xs, wgu, wd, bg, bu, bd)


def _combine_kernel(dest_ref, ys_hbm, x1_ref, route_ref, g2_ref, o_ref, buf, sem, *, tc, n_steps):
    i = pl.program_id(0)
    slot = i % 2

    def gather(step, s):
        def issue(t, carry):
            for k in range(TOP_K):
                row = dest_ref[(step * tc + t) * TOP_K + k]
                _row_copy(ys_hbm, buf.at[s, k], row, t, sem.at[s]).start(priority=k % 2)
            return carry

        lax.fori_loop(0, tc, issue, 0, unroll=4)

    @pl.when(i == 0)
    def _():
        gather(0, 0)

    @pl.when(i + 1 < n_steps)
    def _():
        gather(i + 1, 1 - slot)

    for k in range(TOP_K):
        pltpu.make_async_copy(_tile_rows(ys_hbm, 0, tc), buf.at[slot, k], sem.at[slot]).wait()

    route = route_ref[...]
    for c in range(TILE_ROW):
        cols = slice(c * LANES, (c + 1) * LANES)
        y = buf[slot, 0, pl.ds(c, tc, stride=TILE_ROW), :] * route[:, 0:1]
        for k in range(1, TOP_K):
            y = y + buf[slot, k, pl.ds(c, tc, stride=TILE_ROW), :] * route[:, k:k + 1]
        o_ref[:, cols] = x1_ref[:, cols] + g2_ref[:, cols] * y


def _moe_combine(dest_flat, ys, x1, route, g2, seq):
    n, d = x1.shape
    assert d == TILE_ROW * LANES
    tc = min(seq, 256)
    tpb = seq // tc
    kern = functools.partial(_combine_kernel, tc=tc, n_steps=n // tc)
    return pl.pallas_call(
        kern,
        out_shape=jax.ShapeDtypeStruct((n, d), F32),
        grid_spec=pltpu.PrefetchScalarGridSpec(
            num_scalar_prefetch=1, grid=(n // tc,),
            in_specs=[pl.BlockSpec(memory_space=pl.ANY),
                      pl.BlockSpec((tc, d), lambda i, dr: (i, 0)),
                      pl.BlockSpec((tc, LANES), lambda i, dr: (i, 0)),
                      pl.BlockSpec((None, 1, d), lambda i, dr: (i // tpb, 0, 0))],
            out_specs=pl.BlockSpec((tc, d), lambda i, dr: (i, 0)),
            scratch_shapes=[pltpu.VMEM((2, TOP_K, tc * TILE_ROW, LANES), F32),
                            pltpu.SemaphoreType.DMA((2,))]),
        compiler_params=pltpu.CompilerParams(
            dimension_semantics=("arbitrary",), vmem_limit_bytes=VMEM_LIMIT),
        name="moe_combine",
    )(dest_flat, ys, x1, route, g2)


def _rope_tables(seq):
    inv_freq = ROPE_THETA ** (-jnp.arange(0, HEAD_DIM, 2, dtype=F32) / HEAD_DIM)
    ang = inv_freq[:, None] * jnp.arange(seq, dtype=F32)[None, :]
    cos_t = jnp.concatenate([jnp.cos(ang), jnp.cos(ang)], axis=0)
    sin_t = jnp.concatenate([-jnp.sin(ang), jnp.sin(ang)], axis=0)
    return cos_t, sin_t


def _layer(x, c, l, lambda_init, w_ada, b_ada, norm1_w, w_in, q_norm_w, k_norm_w, lambda_q1,
           lambda_k1, lambda_q2, lambda_k2, subln_w, w_attn_o, conv_w, w_conv_o, w_out, norm2_w,
           w_router, b_router, w_gate_up, b_gate_up, w_down, b_down):
    bsz, seq, d = x.shape
    n = bsz * seq
    x2 = x.reshape(n, d)

    mod = _ada_mod(c, w_ada[l], b_ada[l])
    sh1, sc1, g1, sh2, sc2, g2 = [m.reshape(bsz, 1, d) for m in jnp.split(mod, 6, axis=-1)]

    cos_t, sin_t = _rope_tables(seq)
    qt, k, vt, cb, z, sga, sgc, qn, kn = _in_proj(
        x2, norm1_w[l].reshape(1, d), sc1, sh1, w_in[l].astype(BF16),
        q_norm_w[l].reshape(HEAD_DIM, 1) * Q_SCALE, k_norm_w[l].reshape(HEAD_DIM, 1),
        cos_t, sin_t, bsz, seq)

    lam_vecs = jnp.stack([lambda_q1[l], lambda_k1[l], lambda_q2[l], lambda_k2[l]]).astype(F32)
    o = _diff_attn(qt, k.reshape(bsz, seq, d), vt, qn, kn, lam_vecs, subln_w[l].reshape(V_DIM, 1),
                   lambda_init)

    wr = jnp.zeros((d, LANES), F32).at[:, :N_EXPERTS].set(w_router[l].astype(F32))
    wr_hi = wr.astype(BF16)
    wr_lo = (wr - wr_hi.astype(F32)).astype(BF16)
    br_pad = jnp.full((1, LANES), NEG_BIG, F32).at[0, :N_EXPERTS].set(b_router[l].astype(F32))
    x1, h2, route, cnt = _post_mix(
        o.reshape(n, d), cb, z, sga, sgc, x2, g1, sc2, sh2, norm2_w[l].reshape(1, d), conv_w[l],
        w_attn_o[l].astype(BF16), w_conv_o[l].astype(BF16), w_out[l].astype(BF16),
        wr_hi, wr_lo, br_pad, seq)

    counts = cnt[0, :N_EXPERTS].astype(jnp.int32)
    padded = ((counts + MOE_ROWS - 1) // MOE_ROWS) * MOE_ROWS
    ends = jnp.cumsum(padded)
    start = ends - padded
    p_rows = n * TOP_K + N_EXPERTS * MOE_ROWS
    nblk = p_rows // MOE_ROWS
    top_e = route[:, TOP_K:2 * TOP_K].astype(jnp.int32)
    rank = route[:, 2 * TOP_K:3 * TOP_K].astype(jnp.int32)
    experts = jnp.arange(N_EXPERTS, dtype=jnp.int32)
    seg_start = jnp.sum(jnp.where(top_e[..., None] == experts, start, 0), axis=-1)
    dest = (seg_start + rank).reshape(-1)
    block_start = jnp.arange(nblk, dtype=jnp.int32) * MOE_ROWS
    block_e = jnp.sum((ends[None, :] <= block_start[:, None]).astype(jnp.int32), axis=1)
    block_e = jnp.minimum(block_e, N_EXPERTS - 1)
    n_used = (ends[-1:] // MOE_ROWS).astype(jnp.int32)
    nonempty = padded > 0
    later = jnp.logical_and(nonempty[None, :], experts[None, :] > experts[:, None])
    next_of = jnp.min(jnp.where(later, experts[None, :], N_EXPERTS), axis=1)
    next_of = jnp.where(next_of == N_EXPERTS, -1, next_of).astype(jnp.int32)
    slot_of = ((jnp.cumsum(nonempty.astype(jnp.int32)) - 1) % 2).astype(jnp.int32)
    fresh = jnp.concatenate([jnp.ones((1,), jnp.int32),
                             (block_e[1:] != block_e[:-1]).astype(jnp.int32)])
    slot = slot_of[block_e]
    next_e = next_of[block_e]

    xs = _moe_dispatch(dest, ends.astype(jnp.int32), h2, n, p_rows)
    ys = _moe_ffn(block_e, n_used, fresh, slot, next_e, xs, w_gate_up[l], w_down[l],
                  b_gate_up[l][:, None, 0::2], b_gate_up[l][:, None, 1::2], b_down[l][:, None, :])
    out = _moe_combine(dest, ys, x1, route, g2, seq)
    return out.reshape(bsz, seq, d)


def kernel(x, c, w_ada, b_ada, norm1_w, w_in, q_norm_w, k_norm_w, lambda_q1, lambda_k1, lambda_q2,
           lambda_k2, subln_w, w_attn_o, conv_w, w_conv_o, w_out, norm2_w, w_router, b_router,
           w_gate_up, b_gate_up, w_down, b_down):
    depth = w_ada.shape[0]
    for l in range(depth):
        lambda_init = 0.8 - 0.6 * math.exp(-0.3 * l)
        x = _layer(x, c, l, lambda_init, w_ada, b_ada, norm1_w, w_in, q_norm_w, k_norm_w,
                   lambda_q1, lambda_k1, lambda_q2, lambda_k2, subln_w, w_attn_o, conv_w,
                   w_conv_o, w_out, norm2_w, w_router, b_router, w_gate_up, b_gate_up,
                   w_down, b_down)
    return x
```

```python
import functools
import math

import jax
import jax.numpy as jnp
import numpy as np
from jax import lax
from jax.experimental import pallas as pl
from jax.experimental.pallas import tpu as pltpu

N_HEADS = 8
HEAD_DIM = 64
V_DIM = 2 * HEAD_DIM
IN_BLOCKS = ("q", "k", "v", "conv_b", "conv_c", "conv_x", "gate_attn", "gate_conv")
N_EXPERTS = 32
TOP_K = 4
SWIGLU_LIMIT = 7.0
SWIGLU_ALPHA = 1.702
ROPE_THETA = 10000.0
RMS_EPS = 1e-6
SUBLN_EPS = 1e-5
LANES = 128
TILE_ROW = 8
MOE_ROWS = 256
NEG_BIG = -1e30
LOG2E = 1.4426950408889634
Q_SCALE = LOG2E / math.sqrt(HEAD_DIM)
SAFE_EXP2_BOUND = 80.0
VMEM_LIMIT = 56 * 1024 * 1024

F32 = jnp.float32
BF16 = jnp.bfloat16


def _dot(a, b):
    return jnp.dot(a, b, preferred_element_type=F32)


def _store_tile_rows(ref, row0, val):
    rows, d = val.shape
    chunks = d // LANES
    for c in range(chunks):
        ref[pl.ds(row0 * chunks + c, rows, stride=chunks), :] = val[:, c * LANES:(c + 1) * LANES]


def _load_tile_rows(ref, chunks):
    rows = ref.shape[0] // chunks
    return [ref[pl.ds(c, rows, stride=chunks), :] for c in range(chunks)]


def _ada_kernel(ct_ref, w_ref, b_ref, o_ref):
    ct = ct_ref[...]
    s = ct * jax.nn.sigmoid(ct)
    w = w_ref[...]
    for b in range(ct.shape[1]):
        o_ref[b:b + 1, :] = jnp.sum(w * s[:, b:b + 1], axis=0, keepdims=True) + b_ref[...]


def _ada_mod(c, w_ada, b_ada):
    bsz, d = c.shape
    n = w_ada.shape[1]
    tn = min(n, 1536)
    return pl.pallas_call(
        _ada_kernel,
        out_shape=jax.ShapeDtypeStruct((bsz, n), F32),
        grid=(n // tn,),
        in_specs=[pl.BlockSpec((d, bsz), lambda j: (0, 0)),
                  pl.BlockSpec((d, tn), lambda j: (0, j)),
                  pl.BlockSpec((1, tn), lambda j: (0, j))],
        out_specs=pl.BlockSpec((bsz, tn), lambda j: (0, j)),
        compiler_params=pltpu.CompilerParams(dimension_semantics=("arbitrary",)),
        name="ada_mod",
    )(c.T, w_ada, b_ada.reshape(1, n))


def _qk_norm_rope_t(y, g_col, cos_t, sin_t):
    tm, w = y.shape
    yt = y.T.reshape(w // HEAD_DIM, HEAD_DIM, tm)
    ms = jnp.mean(yt * yt, axis=1, keepdims=True)
    yn = yt * lax.rsqrt(ms + RMS_EPS) * g_col[None]
    half = HEAD_DIM // 2
    swapped = jnp.concatenate([yn[:, half:, :], yn[:, :half, :]], axis=1)
    out = yn * cos_t[None] + swapped * sin_t[None]
    norm2 = jnp.sum(out * out, axis=1).reshape(N_HEADS, 2, tm)
    return out.reshape(w, tm), norm2


def _in_proj_kernel(x_ref, n1_ref, sc_ref, sh_ref, w_ref, gq_ref, gk_ref, cos_ref, sin_ref,
                    qt_ref, k_ref, vt_ref, cb_ref, z_ref, sga_ref, sgc_ref, qn_ref, kn_ref, h_scr):
    x = x_ref[...]
    xn = x * lax.rsqrt(jnp.mean(x * x, axis=-1, keepdims=True) + RMS_EPS) * n1_ref[...]
    h_scr[...] = (xn * (1.0 + sc_ref[...]) + sh_ref[...]).astype(BF16)
    wcol = w_ref.shape[1] // len(IN_BLOCKS)

    def proj(name):
        c = IN_BLOCKS.index(name)
        return _dot(h_scr[...], w_ref[:, c * wcol:(c + 1) * wcol])

    qt, qn = _qk_norm_rope_t(proj("q"), gq_ref[...], cos_ref[...], sin_ref[...])
    qt_ref[...] = qt.astype(BF16)
    qn_ref[...] = qn
    kt, kn = _qk_norm_rope_t(proj("k"), gk_ref[...], cos_ref[...], sin_ref[...])
    k_ref[...] = kt.T.astype(BF16)
    kn_ref[...] = kn
    vt_ref[...] = proj("v").T.astype(BF16)
    cb_ref[...] = proj("conv_b").astype(BF16)
    z_ref[...] = (proj("conv_c") * proj("conv_x")).astype(BF16)
    sga_ref[...] = jax.nn.sigmoid(proj("gate_attn")).astype(BF16)
    sgc_ref[...] = jax.nn.sigmoid(proj("gate_conv")).astype(BF16)


def _in_proj(x2, n1, sc1, sh1, w_in_bf, gq, gk, cos_t, sin_t, bsz, seq):
    n, d = x2.shape
    tm = min(seq, 512)
    tpb = seq // tm
    assert w_in_bf.shape[1] == len(IN_BLOCKS) * d
    row = lambda i: (i, 0)
    const = lambda i: (0, 0)
    tcol = lambda i: (i // tpb, 0, i % tpb)
    mod = lambda i: (i // tpb, 0, 0)
    nat = jax.ShapeDtypeStruct((n, d), BF16)
    tr = jax.ShapeDtypeStruct((bsz, d, seq), BF16)
    nrm = jax.ShapeDtypeStruct((bsz, N_HEADS, 2, seq), F32)
    nat_spec = pl.BlockSpec((tm, d), row)
    tr_spec = pl.BlockSpec((None, d, tm), tcol)
    nrm_spec = pl.BlockSpec((None, N_HEADS, 2, tm), lambda i: (i // tpb, 0, 0, i % tpb))
    return pl.pallas_call(
        _in_proj_kernel,
        out_shape=(tr, nat, tr, nat, nat, nat, nat, nrm, nrm),
        grid=(n // tm,),
        in_specs=[pl.BlockSpec((tm, d), row),
                  pl.BlockSpec((1, d), const),
                  pl.BlockSpec((None, 1, d), mod),
                  pl.BlockSpec((None, 1, d), mod),
                  pl.BlockSpec(w_in_bf.shape, const),
                  pl.BlockSpec((HEAD_DIM, 1), const),
                  pl.BlockSpec((HEAD_DIM, 1), const),
                  pl.BlockSpec((HEAD_DIM, tm), lambda i: (0, i % tpb)),
                  pl.BlockSpec((HEAD_DIM, tm), lambda i: (0, i % tpb))],
        out_specs=(tr_spec, nat_spec, tr_spec, nat_spec, nat_spec, nat_spec, nat_spec,
                   nrm_spec, nrm_spec),
        scratch_shapes=[pltpu.VMEM((tm, d), BF16)],
        compiler_params=pltpu.CompilerParams(
            dimension_semantics=("arbitrary",), vmem_limit_bytes=VMEM_LIMIT),
        name="in_proj",
    )(x2, n1, sc1, sh1, w_in_bf, gq, gk, cos_t, sin_t)


def _diff_attn_kernel(qt_ref, k_ref, vt_ref, qn_ref, kn_ref, lam_ref, sw_ref, o_ref, acc0, acc1,
                      *, tk, lambda_init):
    seq = k_ref.shape[0]
    tq = qt_ref.shape[1]
    qt = qt_ref[...]
    first = lax.broadcasted_iota(jnp.int32, qt.shape, 0) < HEAD_DIM
    zero = jnp.zeros_like(qt)
    qz = (jnp.where(first, qt, zero), jnp.where(first, zero, qt))
    accs = (acc0, acc1)
    acc0[...] = jnp.zeros_like(acc0)
    acc1[...] = jnp.zeros_like(acc1)
    n_chunks = seq // tk

    def load(j):
        off = pl.multiple_of(j * tk, tk)
        return k_ref[pl.ds(off, tk), :], vt_ref[:, pl.ds(off, tk)]

    def plain_body(j, carry):
        kk, vt = load(j)
        new = []
        for c in range(2):
            p = jnp.exp2(_dot(kk, qz[c]))
            new.append(carry[c] + jnp.sum(p, axis=0, keepdims=True))
            accs[c][...] += _dot(vt, p.astype(BF16))
        return tuple(new)

    def online_body(j, carry):
        kk, vt = load(j)
        new = []
        for c in range(2):
            m, l = carry[2 * c], carry[2 * c + 1]
            s = _dot(kk, qz[c])
            m_new = jnp.maximum(m, jnp.max(s, axis=0, keepdims=True))
            alpha = jnp.exp2(m - m_new)
            p = jnp.exp2(s - m_new)
            l = alpha * l + jnp.sum(p, axis=0, keepdims=True)
            accs[c][...] = alpha * accs[c][...] + _dot(vt, p.astype(BF16))
            new += [m_new, l]
        return tuple(new)

    m_init = jnp.full((1, tq), NEG_BIG, F32)
    l_init = jnp.zeros((1, tq), F32)

    def plain():
        return lax.fori_loop(0, n_chunks, plain_body, (l_init, l_init))

    def online():
        _, l0, _, l1 = lax.fori_loop(0, n_chunks, online_body, (m_init, l_init, m_init, l_init))
        return l0, l1

    bound2 = jnp.max(jnp.max(qn_ref[...], axis=-1, keepdims=True)
                     * jnp.max(kn_ref[...], axis=-1, keepdims=True))
    l0, l1 = lax.cond(bound2 <= SAFE_EXP2_BOUND * SAFE_EXP2_BOUND, plain, online)

    lq = lam_ref[...]
    lam = (jnp.exp(jnp.sum(lq[0:1] * lq[1:2], axis=-1, keepdims=True))
           - jnp.exp(jnp.sum(lq[2:3] * lq[3:4], axis=-1, keepdims=True)) + lambda_init)
    o = acc0[...] / l0 - lam * (acc1[...] / l1)
    o = o * lax.rsqrt(jnp.mean(o * o, axis=0, keepdims=True) + SUBLN_EPS)
    o = o * sw_ref[...] * (1.0 - lambda_init)
    o_ref[...] = o.T.astype(BF16)


def _diff_attn(qt, k3, vt, qn, kn, lam_vecs, subln_col, lambda_init):
    bsz, d, seq = qt.shape
    tq = min(seq, 1024)
    tk = min(seq, 4096)
    kern = functools.partial(_diff_attn_kernel, tk=tk, lambda_init=lambda_init)
    return pl.pallas_call(
        kern,
        out_shape=jax.ShapeDtypeStruct((bsz, seq, d), BF16),
        grid=(bsz, N_HEADS, seq // tq),
        in_specs=[pl.BlockSpec((None, V_DIM, tq), lambda b, h, i: (b, h, i)),
                  pl.BlockSpec((None, seq, V_DIM), lambda b, h, i: (b, 0, h)),
                  pl.BlockSpec((None, V_DIM, seq), lambda b, h, i: (b, h, 0)),
                  pl.BlockSpec((None, None, 2, tq), lambda b, h, i: (b, h, 0, i)),
                  pl.BlockSpec((None, None, 2, seq), lambda b, h, i: (b, h, 0, 0)),
                  pl.BlockSpec((4, HEAD_DIM), lambda b, h, i: (0, 0)),
                  pl.BlockSpec((V_DIM, 1), lambda b, h, i: (0, 0))],
        out_specs=pl.BlockSpec((None, tq, V_DIM), lambda b, h, i: (b, i, h)),
        scratch_shapes=[pltpu.VMEM((V_DIM, tq), F32), pltpu.VMEM((V_DIM, tq), F32)],
        compiler_params=pltpu.CompilerParams(
            dimension_semantics=("arbitrary", "arbitrary", "arbitrary"),
            vmem_limit_bytes=VMEM_LIMIT),
        name="diff_attn",
    )(qt, k3, vt, qn, kn, lam_vecs, subln_col)


def _post_mix_kernel(o_ref, cb_ref, z_ref, zp_ref, zn_ref, sga_ref, sgc_ref, x_ref,
                     g1_ref, sc2_ref, sh2_ref, n2_ref, cw_ref, wao_ref, wco_ref, wout_ref,
                     wrh_ref, wrl_ref, br_ref,
                     x1_ref, h2_ref, route_ref, cnt_ref, u_scr, *, tpb, n_sub):
    i = pl.program_id(0)
    tm = x_ref.shape[0]

    @pl.when(i == 0)
    def _():
        cnt_ref[...] = jnp.zeros_like(cnt_ref)

    z = z_ref[...].astype(F32)
    rows = lax.broadcasted_iota(jnp.int32, z.shape, 0)
    halo_rows = zp_ref.shape[0]
    prev_row = zp_ref[halo_rows - 1:halo_rows, :].astype(F32)
    next_row = zn_ref[0:1, :].astype(F32)
    prev_row = jnp.where(i % tpb == 0, jnp.zeros_like(prev_row), prev_row)
    next_row = jnp.where(i % tpb == tpb - 1, jnp.zeros_like(next_row), next_row)
    z_m1 = jnp.where(rows == 0, prev_row, pltpu.roll(z, 1, 0))
    z_p1 = jnp.where(rows == tm - 1, next_row, pltpu.roll(z, tm - 1, 0))
    cw = cw_ref[...]
    conv = z_m1 * cw[0:1] + z * cw[1:2] + z_p1 * cw[2:3]
    u_scr[...] = (cb_ref[...].astype(F32) * conv).astype(BF16)

    ts = tm // n_sub
    lane = lax.broadcasted_iota(jnp.int32, (ts, LANES), 1)
    r_i = lax.broadcasted_iota(jnp.int32, (ts, ts), 0)
    c_i = lax.broadcasted_iota(jnp.int32, (ts, ts), 1)
    lower = (r_i > c_i).astype(BF16)
    counts = cnt_ref[0:1, :]
    for s in range(n_sub):
        rs = slice(s * ts, (s + 1) * ts)
        y_attn = _dot(o_ref[rs, :], wao_ref[...])
        y_conv = _dot(u_scr[rs, :], wco_ref[...])
        m = sga_ref[rs, :].astype(F32) * y_attn + sgc_ref[rs, :].astype(F32) * y_conv
        x1 = x_ref[rs, :] + g1_ref[...] * _dot(m.astype(BF16), wout_ref[...])
        x1_ref[rs, :] = x1

        h2 = x1 * lax.rsqrt(jnp.mean(x1 * x1, axis=-1, keepdims=True) + RMS_EPS) * n2_ref[...]
        h2 = h2 * (1.0 + sc2_ref[...]) + sh2_ref[...]
        _store_tile_rows(h2_ref, s * ts, h2)

        h_hi = h2.astype(BF16)
        h_lo = (h2 - h_hi.astype(F32)).astype(BF16)
        logits = (_dot(h_hi, wrh_ref[...]) + _dot(h_lo, wrh_ref[...]) + _dot(h_hi, wrl_ref[...])
                  + br_ref[...])

        work = logits
        vals, idxs = [], []
        for _ in range(TOP_K):
            mx = jnp.max(work, axis=-1, keepdims=True)
            ix = jnp.min(jnp.where(work == mx, lane, LANES), axis=-1, keepdims=True)
            vals.append(mx)
            idxs.append(ix)
            work = jnp.where(lane == ix, 2.0 * NEG_BIG, work)
        exps = [jnp.exp(v - vals[0]) for v in vals]
        den = exps[0] + exps[1] + exps[2] + exps[3]

        sel = (work == 2.0 * NEG_BIG).astype(BF16)
        before = _dot(lower, sel) + counts
        counts = counts + jnp.sum(sel.astype(F32), axis=0, keepdims=True)

        route = jnp.zeros(logits.shape, F32)
        for k in range(TOP_K):
            rank = jnp.sum(jnp.where(lane == idxs[k], before, 0.0), axis=-1, keepdims=True)
            route = jnp.where(lane == k, exps[k] / den, route)
            route = jnp.where(lane == TOP_K + k, idxs[k].astype(F32), route)
            route = jnp.where(lane == 2 * TOP_K + k, rank, route)
        route_ref[rs, :] = route
    cnt_ref[...] = jnp.broadcast_to(counts, cnt_ref.shape)


def _post_mix(o2, cb, z, sga, sgc, x2, g1, sc2, sh2, n2, conv_w, wao, wco, wout,
              wr_hi, wr_lo, br_pad, seq):
    n, d = x2.shape
    tm = min(seq, 512)
    tpb = seq // tm
    halo = 16
    hb = tm // halo
    last_hb = n // halo - 1
    row = lambda i: (i, 0)
    const = lambda i: (0, 0)
    mod = lambda i: (i // tpb, 0, 0)
    wspec = pl.BlockSpec((d, d), const)
    kern = functools.partial(_post_mix_kernel, tpb=tpb, n_sub=1)
    return pl.pallas_call(
        kern,
        out_shape=(jax.ShapeDtypeStruct((n, d), F32),
                   jax.ShapeDtypeStruct((n * (d // LANES), LANES), F32),
                   jax.ShapeDtypeStruct((n, LANES), F32), jax.ShapeDtypeStruct((8, LANES), F32)),
        grid=(n // tm,),
        in_specs=[pl.BlockSpec((tm, d), row), pl.BlockSpec((tm, d), row), pl.BlockSpec((tm, d), row),
                  pl.BlockSpec((halo, d), lambda i: (jnp.maximum(i * hb - 1, 0), 0)),
                  pl.BlockSpec((halo, d), lambda i: (jnp.minimum((i + 1) * hb, last_hb), 0)),
                  pl.BlockSpec((tm, d), row), pl.BlockSpec((tm, d), row), pl.BlockSpec((tm, d), row),
                  pl.BlockSpec((None, 1, d), mod), pl.BlockSpec((None, 1, d), mod),
                  pl.BlockSpec((None, 1, d), mod),
                  pl.BlockSpec((1, d), const), pl.BlockSpec((3, d), const),
                  wspec, wspec, wspec,
                  pl.BlockSpec((d, LANES), const), pl.BlockSpec((d, LANES), const),
                  pl.BlockSpec((1, LANES), const)],
        out_specs=(pl.BlockSpec((tm, d), row), pl.BlockSpec((tm * (d // LANES), LANES), row),
                   pl.BlockSpec((tm, LANES), row), pl.BlockSpec((8, LANES), const)),
        scratch_shapes=[pltpu.VMEM((tm, d), BF16)],
        compiler_params=pltpu.CompilerParams(
            dimension_semantics=("arbitrary",), vmem_limit_bytes=VMEM_LIMIT),
        name="post_mix",
    )(o2, cb, z, z, z, sga, sgc, x2, g1, sc2, sh2, n2, conv_w, wao, wco, wout, wr_hi, wr_lo, br_pad)


def _tile_rows(ref, row, n_rows=1):
    start = pl.multiple_of(row * TILE_ROW, TILE_ROW)
    return ref.at[pl.ds(start, n_rows * TILE_ROW)]


def _row_copy(src, dst, s, t, sem):
    return pltpu.make_async_copy(_tile_rows(src, s), _tile_rows(dst, t), sem)


def _dispatch_kernel(dest_ref, ends_ref, h2_ref, xs_hbm, zbuf, sem, zsem, *, tc, nblk):
    base = pl.program_id(0) * tc

    @pl.when(pl.program_id(0) == 0)
    def _():
        zbuf[...] = jnp.zeros_like(zbuf)

        def zero_block(row):
            return pltpu.make_async_copy(zbuf, _tile_rows(xs_hbm, row, MOE_ROWS), zsem)

        def nonempty(e):
            return ends_ref[e] > (ends_ref[e - 1] if e else 0)

        total = ends_ref[N_EXPERTS - 1]
        n_tail = nblk - total // MOE_ROWS

        def tail_start(b, carry):
            zero_block(total + b * MOE_ROWS).start()
            return carry

        def tail_wait(b, carry):
            zero_block(0).wait()
            return carry

        for e in range(N_EXPERTS):
            @pl.when(nonempty(e))
            def _(e=e):
                zero_block(ends_ref[e] - MOE_ROWS).start()
        lax.fori_loop(0, n_tail, tail_start, 0)
        for e in range(N_EXPERTS):
            @pl.when(nonempty(e))
            def _():
                zero_block(0).wait()
        lax.fori_loop(0, n_tail, tail_wait, 0)

    def issue(t, carry):
        for k in range(TOP_K):
            _row_copy(h2_ref, xs_hbm, t, dest_ref[(base + t) * TOP_K + k], sem).start(priority=k % 2)
        return carry

    lax.fori_loop(0, tc, issue, 0, unroll=4)

    for _ in range(TOP_K):
        pltpu.make_async_copy(h2_ref, _tile_rows(xs_hbm, 0, tc), sem).wait()


def _moe_dispatch(dest_flat, ends, h2_tiles, n, p_rows):
    tc = min(n, 512)
    kern = functools.partial(_dispatch_kernel, tc=tc, nblk=p_rows // MOE_ROWS)
    return pl.pallas_call(
        kern,
        out_shape=jax.ShapeDtypeStruct((p_rows * TILE_ROW, LANES), F32),
        grid_spec=pltpu.PrefetchScalarGridSpec(
            num_scalar_prefetch=2, grid=(n // tc,),
            in_specs=[pl.BlockSpec((tc * TILE_ROW, LANES), lambda i, dr, en: (i, 0))],
            out_specs=pl.BlockSpec(memory_space=pl.ANY),
            scratch_shapes=[pltpu.VMEM((MOE_ROWS * TILE_ROW, LANES), F32),
                            pltpu.SemaphoreType.DMA, pltpu.SemaphoreType.DMA]),
        compiler_params=pltpu.CompilerParams(dimension_semantics=("arbitrary",)),
        name="moe_dispatch",
    )(dest_flat, ends, h2_tiles)


_NT = (((1,), (1,)), ((), ()))


def _ffn_kernel(first_ref, count_ref, slot_ref, next_ref, misc_ref,
                xs_hbm, wgu_hbm, wd_hbm, bg_ref, bu_ref, bd_ref, ys_hbm,
                wgu_buf, wd_buf, wt_scr, wg_scr, wu_scr, wd_scr, xbuf, ybuf, wsem, xsem, ysem,
                *, nblk):
    e = pl.program_id(0)
    first = first_ref[e]
    count = count_ref[e]
    slot = slot_ref[e]
    first_expert, total_blocks = misc_ref[0], misc_ref[1]

    def weight_copies(ex, s):
        return (pltpu.make_async_copy(wgu_hbm.at[ex], wgu_buf.at[s], wsem.at[0, s]),
                pltpu.make_async_copy(wd_hbm.at[ex], wd_buf.at[s], wsem.at[1, s]))

    def x_copy(b, s):
        return pltpu.make_async_copy(_tile_rows(xs_hbm, b * MOE_ROWS, MOE_ROWS), xbuf.at[s], xsem.at[s])

    def y_copy(b, s):
        return pltpu.make_async_copy(ybuf.at[s], _tile_rows(ys_hbm, b * MOE_ROWS, MOE_ROWS), ysem.at[s])

    @pl.when(e == first_expert)
    def _():
        for cp in weight_copies(e, 0):
            cp.start()

    @pl.when(count > 0)
    def _():
        x_copy(first, 0).start()
        for cp in weight_copies(e, slot):
            cp.wait()

        @pl.when(next_ref[e] >= 0)
        def _():
            for cp in weight_copies(next_ref[e], 1 - slot):
                cp.start()

        _, d, f2 = wgu_buf.shape
        for c in range(d // LANES):
            cols = slice(c * LANES, (c + 1) * LANES)
            wt_scr[c] = wgu_buf[slot, cols, :].T
            wg_scr[:, cols] = wt_scr[c, pl.ds(0, f2 // 2, stride=2), :].astype(BF16)
            wu_scr[:, cols] = wt_scr[c, pl.ds(1, f2 // 2, stride=2), :].astype(BF16)
        wd_scr[...] = wd_buf[slot].astype(BF16)
        bg, bu, bd = bg_ref[e], bu_ref[e], bd_ref[e]

        def block(j, carry):
            s = j % 2
            x_copy(first, s).wait()

            @pl.when(j + 1 < count)
            def _():
                x_copy(first + j + 1, 1 - s).start()

            @pl.when(j >= 2)
            def _():
                y_copy(first, s).wait()

            x = jnp.concatenate(_load_tile_rows(xbuf.at[s], TILE_ROW), axis=1).astype(BF16)
            gate = lax.dot_general(x, wg_scr[...], _NT, preferred_element_type=F32) + bg
            up = lax.dot_general(x, wu_scr[...], _NT, preferred_element_type=F32) + bu
            gate = jnp.minimum(gate, SWIGLU_LIMIT)
            up = jnp.clip(up, -SWIGLU_LIMIT, SWIGLU_LIMIT)
            glu = gate * jax.nn.sigmoid(gate * SWIGLU_ALPHA)
            mid = ((up + 1.0) * glu).astype(BF16)
            _store_tile_rows(ybuf.at[s], 0, _dot(mid, wd_scr[...]) + bd)
            y_copy(first + j, s).start()
            return carry

        lax.fori_loop(0, count, block, 0)

        @pl.when(count >= 2)
        def _():
            y_copy(first, count % 2).wait()
        y_copy(first, (count - 1) % 2).wait()

    @pl.when(e == pl.num_programs(0) - 1)
    def _():
        ybuf[0] = jnp.zeros(ybuf.shape[1:], ybuf.dtype)

        def tail_start(b, carry):
            y_copy(b, 0).start()
            return carry

        def tail_wait(b, carry):
            y_copy(b, 0).wait()
            return carry

        lax.fori_loop(total_blocks, nblk, tail_start, 0)
        lax.fori_loop(total_blocks, nblk, tail_wait, 0)


def _moe_ffn(first_blk, n_blk, slot, next_e, misc, xs, wgu, wd, bg, bu, bd):
    p_rows = xs.shape[0] // TILE_ROW
    n_exp, d, f2 = wgu.shape
    assert d == TILE_ROW * LANES
    f = f2 // 2
    whole = lambda shape: pl.BlockSpec(shape, lambda e, *_: (0,) * len(shape))
    tile = (MOE_ROWS * TILE_ROW, LANES)
    kern = functools.partial(_ffn_kernel, nblk=p_rows // MOE_ROWS)
    return pl.pallas_call(
        kern,
        out_shape=jax.ShapeDtypeStruct((p_rows * TILE_ROW, LANES), F32),
        grid_spec=pltpu.PrefetchScalarGridSpec(
            num_scalar_prefetch=5, grid=(n_exp,),
            in_specs=[pl.BlockSpec(memory_space=pl.ANY),
                      pl.BlockSpec(memory_space=pl.ANY), pl.BlockSpec(memory_space=pl.ANY),
                      whole(bg.shape), whole(bu.shape), whole(bd.shape)],
            out_specs=pl.BlockSpec(memory_space=pl.ANY),
            scratch_shapes=[pltpu.VMEM((2, d, f2), F32), pltpu.VMEM((2, f, d), F32),
                            pltpu.VMEM((d // LANES, f2, LANES), F32), pltpu.VMEM((f, d), BF16),
                            pltpu.VMEM((f, d), BF16), pltpu.VMEM((f, d), BF16),
                            pltpu.VMEM((2,) + tile, F32), pltpu.VMEM((2,) + tile, F32),
                            pltpu.SemaphoreType.DMA((2, 2)), pltpu.SemaphoreType.DMA((2,)),
                            pltpu.SemaphoreType.DMA((2,))]),
        compiler_params=pltpu.CompilerParams(
            dimension_semantics=("arbitrary",), vmem_limit_bytes=VMEM_LIMIT),
        name="moe_ffn",
    )(first_blk, n_blk, slot, next_e, misc, xs, wgu, wd, bg, bu, bd)


def _combine_kernel(dest_ref, ys_hbm, x1_ref, route_ref, g2_ref, o_ref, buf, sem, *, tc, n_steps):
    i = pl.program_id(0)
    slot = i % 2

    def gather(step, s):
        def issue(t, carry):
            for k in range(TOP_K):
                row = dest_ref[(step * tc + t) * TOP_K + k]
                _row_copy(ys_hbm, buf.at[s, k], row, t, sem.at[s]).start(priority=k % 2)
            return carry

        lax.fori_loop(0, tc, issue, 0, unroll=4)

    @pl.when(i == 0)
    def _():
        gather(0, 0)

    @pl.when(i + 1 < n_steps)
    def _():
        gather(i + 1, 1 - slot)

    for k in range(TOP_K):
        pltpu.make_async_copy(_tile_rows(ys_hbm, 0, tc), buf.at[slot, k], sem.at[slot]).wait()

    route = route_ref[...]
    for c in range(TILE_ROW):
        cols = slice(c * LANES, (c + 1) * LANES)
        y = buf[slot, 0, pl.ds(c, tc, stride=TILE_ROW), :] * route[:, 0:1]
        for k in range(1, TOP_K):
            y = y + buf[slot, k, pl.ds(c, tc, stride=TILE_ROW), :] * route[:, k:k + 1]
        o_ref[:, cols] = x1_ref[:, cols] + g2_ref[:, cols] * y


def _moe_combine(dest_flat, ys, x1, route, g2, seq):
    n, d = x1.shape
    assert d == TILE_ROW * LANES
    tc = min(seq, 256)
    tpb = seq // tc
    kern = functools.partial(_combine_kernel, tc=tc, n_steps=n // tc)
    return pl.pallas_call(
        kern,
        out_shape=jax.ShapeDtypeStruct((n, d), F32),
        grid_spec=pltpu.PrefetchScalarGridSpec(
            num_scalar_prefetch=1, grid=(n // tc,),
            in_specs=[pl.BlockSpec(memory_space=pl.ANY),
                      pl.BlockSpec((tc, d), lambda i, dr: (i, 0)),
                      pl.BlockSpec((tc, LANES), lambda i, dr: (i, 0)),
                      pl.BlockSpec((None, 1, d), lambda i, dr: (i // tpb, 0, 0))],
            out_specs=pl.BlockSpec((tc, d), lambda i, dr: (i, 0)),
            scratch_shapes=[pltpu.VMEM((2, TOP_K, tc * TILE_ROW, LANES), F32),
                            pltpu.SemaphoreType.DMA((2,))]),
        compiler_params=pltpu.CompilerParams(
            dimension_semantics=("arbitrary",), vmem_limit_bytes=VMEM_LIMIT),
        name="moe_combine",
    )(dest_flat, ys, x1, route, g2)


def _rope_tables(seq):
    inv_freq = (ROPE_THETA ** (-np.arange(0, HEAD_DIM, 2, dtype=np.float32) / HEAD_DIM)).astype(np.float32)
    ang = (inv_freq[:, None] * np.arange(seq, dtype=np.float32)[None, :]).astype(np.float32)
    cos, sin = np.cos(ang.astype(np.float64)), np.sin(ang.astype(np.float64))
    cos_t = np.concatenate([cos, cos], axis=0).astype(np.float32)
    sin_t = np.concatenate([-sin, sin], axis=0).astype(np.float32)
    return jnp.asarray(cos_t), jnp.asarray(sin_t)


def _layer(x, c, l, lambda_init, w_ada, b_ada, norm1_w, w_in, q_norm_w, k_norm_w, lambda_q1,
           lambda_k1, lambda_q2, lambda_k2, subln_w, w_attn_o, conv_w, w_conv_o, w_out, norm2_w,
           w_router, b_router, w_gate_up, b_gate_up, w_down, b_down):
    bsz, seq, d = x.shape
    n = bsz * seq
    x2 = x.reshape(n, d)

    mod = _ada_mod(c, w_ada[l], b_ada[l])
    sh1, sc1, g1, sh2, sc2, g2 = [m.reshape(bsz, 1, d) for m in jnp.split(mod, 6, axis=-1)]

    cos_t, sin_t = _rope_tables(seq)
    qt, k, vt, cb, z, sga, sgc, qn, kn = _in_proj(
        x2, norm1_w[l].reshape(1, d), sc1, sh1, w_in[l].astype(BF16),
        q_norm_w[l].reshape(HEAD_DIM, 1) * Q_SCALE, k_norm_w[l].reshape(HEAD_DIM, 1),
        cos_t, sin_t, bsz, seq)

    lam_vecs = jnp.stack([lambda_q1[l], lambda_k1[l], lambda_q2[l], lambda_k2[l]]).astype(F32)
    o = _diff_attn(qt, k.reshape(bsz, seq, d), vt, qn, kn, lam_vecs, subln_w[l].reshape(V_DIM, 1),
                   lambda_init)

    wr = jnp.zeros((d, LANES), F32).at[:, :N_EXPERTS].set(w_router[l].astype(F32))
    wr_hi = wr.astype(BF16)
    wr_lo = (wr - wr_hi.astype(F32)).astype(BF16)
    br_pad = jnp.full((1, LANES), NEG_BIG, F32).at[0, :N_EXPERTS].set(b_router[l].astype(F32))
    x1, h2, route, cnt = _post_mix(
        o.reshape(n, d), cb, z, sga, sgc, x2, g1, sc2, sh2, norm2_w[l].reshape(1, d), conv_w[l],
        w_attn_o[l].astype(BF16), w_conv_o[l].astype(BF16), w_out[l].astype(BF16),
        wr_hi, wr_lo, br_pad, seq)

    counts = cnt[0, :N_EXPERTS].astype(jnp.int32)
    padded = ((counts + MOE_ROWS - 1) // MOE_ROWS) * MOE_ROWS
    ends = jnp.cumsum(padded)
    start = ends - padded
    p_rows = n * TOP_K + N_EXPERTS * MOE_ROWS
    top_e = route[:, TOP_K:2 * TOP_K].astype(jnp.int32)
    rank = route[:, 2 * TOP_K:3 * TOP_K].astype(jnp.int32)
    experts = jnp.arange(N_EXPERTS, dtype=jnp.int32)
    seg_start = jnp.sum(jnp.where(top_e[..., None] == experts, start, 0), axis=-1)
    dest = (seg_start + rank).reshape(-1)
    nonempty = padded > 0
    later = jnp.logical_and(nonempty[None, :], experts[None, :] > experts[:, None])
    next_e = jnp.min(jnp.where(later, experts[None, :], N_EXPERTS), axis=1)
    next_e = jnp.where(next_e == N_EXPERTS, -1, next_e).astype(jnp.int32)
    slot = ((jnp.cumsum(nonempty.astype(jnp.int32)) - 1) % 2).astype(jnp.int32)
    first_blk = (start // MOE_ROWS).astype(jnp.int32)
    n_blk = (padded // MOE_ROWS).astype(jnp.int32)
    misc = jnp.stack([jnp.argmax(nonempty).astype(jnp.int32), (ends[-1] // MOE_ROWS).astype(jnp.int32)])

    xs = _moe_dispatch(dest, ends.astype(jnp.int32), h2, n, p_rows)
    ys = _moe_ffn(first_blk, n_blk, slot, next_e, misc, xs, w_gate_up[l], w_down[l],
                  b_gate_up[l][:, None, 0::2], b_gate_up[l][:, None, 1::2], b_down[l][:, None, :])
    out = _moe_combine(dest, ys, x1, route, g2, seq)
    return out.reshape(bsz, seq, d)


def kernel(x, c, w_ada, b_ada, norm1_w, w_in, q_norm_w, k_norm_w, lambda_q1, lambda_k1, lambda_q2,
           lambda_k2, subln_w, w_attn_o, conv_w, w_conv_o, w_out, norm2_w, w_router, b_router,
           w_gate_up, b_gate_up, w_down, b_down):
    depth = w_ada.shape[0]
    for l in range(depth):
        lambda_init = 0.8 - 0.6 * math.exp(-0.3 * l)
        x = _layer(x, c, l, lambda_init, w_ada, b_ada, norm1_w, w_in, q_norm_w, k_norm_w,
                   lambda_q1, lambda_k1, lambda_q2, lambda_k2, subln_w, w_attn_o, conv_w,
                   w_conv_o, w_out, norm2_w, w_router, b_router, w_gate_up, b_gate_up,
                   w_down, b_down)
    return x
```

```python
import functools
import math

import jax
import jax.numpy as jnp
import numpy as np
from jax import lax
from jax.experimental import pallas as pl
from jax.experimental.pallas import tpu as pltpu

N_HEADS = 8
HEAD_DIM = 64
V_DIM = 2 * HEAD_DIM
IN_BLOCKS = ("q", "k", "v", "conv_b", "conv_c", "conv_x", "gate_attn", "gate_conv")
N_EXPERTS = 32
TOP_K = 4
SWIGLU_LIMIT = 7.0
SWIGLU_ALPHA = 1.702
ROPE_THETA = 10000.0
RMS_EPS = 1e-6
SUBLN_EPS = 1e-5
LANES = 128
TILE_ROW = 8
MOE_ROWS = 512
NEG_BIG = -1e30
LOG2E = 1.4426950408889634
Q_SCALE = LOG2E / math.sqrt(HEAD_DIM)
SAFE_EXP2_BOUND = 80.0
VMEM_LIMIT = 56 * 1024 * 1024

F32 = jnp.float32
BF16 = jnp.bfloat16


def _dot(a, b):
    return jnp.dot(a, b, preferred_element_type=F32)


def _store_tile_rows(ref, row0, val):
    rows, d = val.shape
    chunks = d // LANES
    for c in range(chunks):
        ref[pl.ds(row0 * chunks + c, rows, stride=chunks), :] = val[:, c * LANES:(c + 1) * LANES]


def _load_tile_rows(ref, chunks):
    rows = ref.shape[0] // chunks
    return [ref[pl.ds(c, rows, stride=chunks), :] for c in range(chunks)]


def _ada_kernel(ct_ref, w_ref, b_ref, o_ref):
    ct = ct_ref[...]
    s = ct * jax.nn.sigmoid(ct)
    w = w_ref[...]
    for b in range(ct.shape[1]):
        o_ref[b:b + 1, :] = jnp.sum(w * s[:, b:b + 1], axis=0, keepdims=True) + b_ref[...]


def _ada_mod(c, w_ada, b_ada):
    bsz, d = c.shape
    n = w_ada.shape[1]
    tn = min(n, 1536)
    return pl.pallas_call(
        _ada_kernel,
        out_shape=jax.ShapeDtypeStruct((bsz, n), F32),
        grid=(n // tn,),
        in_specs=[pl.BlockSpec((d, bsz), lambda j: (0, 0)),
                  pl.BlockSpec((d, tn), lambda j: (0, j)),
                  pl.BlockSpec((1, tn), lambda j: (0, j))],
        out_specs=pl.BlockSpec((bsz, tn), lambda j: (0, j)),
        compiler_params=pltpu.CompilerParams(dimension_semantics=("arbitrary",)),
        name="ada_mod",
    )(c.T, w_ada, b_ada.reshape(1, n))


def _qk_norm_rope_t(y, g_col, cos_t, sin_t):
    tm, w = y.shape
    yt = y.T.reshape(w // HEAD_DIM, HEAD_DIM, tm)
    ms = jnp.mean(yt * yt, axis=1, keepdims=True)
    yn = yt * lax.rsqrt(ms + RMS_EPS) * g_col[None]
    half = HEAD_DIM // 2
    swapped = jnp.concatenate([yn[:, half:, :], yn[:, :half, :]], axis=1)
    out = yn * cos_t[None] + swapped * sin_t[None]
    norm2 = jnp.sum(out * out, axis=1).reshape(N_HEADS, 2, tm)
    return out.reshape(w, tm), norm2


def _in_proj_kernel(x_ref, n1_ref, sc_ref, sh_ref, w_ref, gq_ref, gk_ref, cos_ref, sin_ref,
                    qt_ref, k_ref, vt_ref, cb_ref, z_ref, sga_ref, sgc_ref, qn_ref, kn_ref, h_scr):
    x = x_ref[...]
    xn = x * lax.rsqrt(jnp.mean(x * x, axis=-1, keepdims=True) + RMS_EPS) * n1_ref[...]
    h_scr[...] = (xn * (1.0 + sc_ref[...]) + sh_ref[...]).astype(BF16)
    wcol = w_ref.shape[1] // len(IN_BLOCKS)

    def proj(name):
        c = IN_BLOCKS.index(name)
        return _dot(h_scr[...], w_ref[:, c * wcol:(c + 1) * wcol])

    qt, qn = _qk_norm_rope_t(proj("q"), gq_ref[...], cos_ref[...], sin_ref[...])
    qt_ref[...] = qt.astype(BF16)
    qn_ref[...] = qn
    kt, kn = _qk_norm_rope_t(proj("k"), gk_ref[...], cos_ref[...], sin_ref[...])
    k_ref[...] = kt.T.astype(BF16)
    kn_ref[...] = kn
    vt_ref[...] = proj("v").T.astype(BF16)
    cb_ref[...] = proj("conv_b").astype(BF16)
    z_ref[...] = (proj("conv_c") * proj("conv_x")).astype(BF16)
    sga_ref[...] = jax.nn.sigmoid(proj("gate_attn")).astype(BF16)
    sgc_ref[...] = jax.nn.sigmoid(proj("gate_conv")).astype(BF16)


def _in_proj(x2, n1, sc1, sh1, w_in_bf, gq, gk, cos_t, sin_t, bsz, seq):
    n, d = x2.shape
    tm = min(seq, 512)
    tpb = seq // tm
    assert w_in_bf.shape[1] == len(IN_BLOCKS) * d
    row = lambda i: (i, 0)
    const = lambda i: (0, 0)
    tcol = lambda i: (i // tpb, 0, i % tpb)
    mod = lambda i: (i // tpb, 0, 0)
    nat = jax.ShapeDtypeStruct((n, d), BF16)
    tr = jax.ShapeDtypeStruct((bsz, d, seq), BF16)
    nrm = jax.ShapeDtypeStruct((bsz, N_HEADS, 2, seq), F32)
    nat_spec = pl.BlockSpec((tm, d), row)
    tr_spec = pl.BlockSpec((None, d, tm), tcol)
    nrm_spec = pl.BlockSpec((None, N_HEADS, 2, tm), lambda i: (i // tpb, 0, 0, i % tpb))
    return pl.pallas_call(
        _in_proj_kernel,
        out_shape=(tr, nat, tr, nat, nat, nat, nat, nrm, nrm),
        grid=(n // tm,),
        in_specs=[pl.BlockSpec((tm, d), row),
                  pl.BlockSpec((1, d), const),
                  pl.BlockSpec((None, 1, d), mod),
                  pl.BlockSpec((None, 1, d), mod),
                  pl.BlockSpec(w_in_bf.shape, const),
                  pl.BlockSpec((HEAD_DIM, 1), const),
                  pl.BlockSpec((HEAD_DIM, 1), const),
                  pl.BlockSpec((HEAD_DIM, tm), lambda i: (0, i % tpb)),
                  pl.BlockSpec((HEAD_DIM, tm), lambda i: (0, i % tpb))],
        out_specs=(tr_spec, nat_spec, tr_spec, nat_spec, nat_spec, nat_spec, nat_spec,
                   nrm_spec, nrm_spec),
        scratch_shapes=[pltpu.VMEM((tm, d), BF16)],
        compiler_params=pltpu.CompilerParams(
            dimension_semantics=("arbitrary",), vmem_limit_bytes=VMEM_LIMIT),
        name="in_proj",
    )(x2, n1, sc1, sh1, w_in_bf, gq, gk, cos_t, sin_t)


def _diff_attn_kernel(qt_ref, k_ref, vt_ref, qn_ref, kn_ref, lam_ref, sw_ref, o_ref, acc0, acc1,
                      *, tk, lambda_init):
    seq = k_ref.shape[0]
    tq = qt_ref.shape[1]
    qt = qt_ref[...]
    first = lax.broadcasted_iota(jnp.int32, qt.shape, 0) < HEAD_DIM
    zero = jnp.zeros_like(qt)
    qz = (jnp.where(first, qt, zero), jnp.where(first, zero, qt))
    accs = (acc0, acc1)
    acc0[...] = jnp.zeros_like(acc0)
    acc1[...] = jnp.zeros_like(acc1)
    n_chunks = seq // tk

    def load(j):
        off = pl.multiple_of(j * tk, tk)
        return k_ref[pl.ds(off, tk), :], vt_ref[:, pl.ds(off, tk)]

    def plain_body(j, carry):
        kk, vt = load(j)
        new = []
        for c in range(2):
            p = jnp.exp2(_dot(kk, qz[c]))
            new.append(carry[c] + jnp.sum(p, axis=0, keepdims=True))
            accs[c][...] += _dot(vt, p.astype(BF16))
        return tuple(new)

    def online_body(j, carry):
        kk, vt = load(j)
        new = []
        for c in range(2):
            m, l = carry[2 * c], carry[2 * c + 1]
            s = _dot(kk, qz[c])
            m_new = jnp.maximum(m, jnp.max(s, axis=0, keepdims=True))
            alpha = jnp.exp2(m - m_new)
            p = jnp.exp2(s - m_new)
            l = alpha * l + jnp.sum(p, axis=0, keepdims=True)
            accs[c][...] = alpha * accs[c][...] + _dot(vt, p.astype(BF16))
            new += [m_new, l]
        return tuple(new)

    m_init = jnp.full((1, tq), NEG_BIG, F32)
    l_init = jnp.zeros((1, tq), F32)

    def plain():
        return lax.fori_loop(0, n_chunks, plain_body, (l_init, l_init))

    def online():
        _, l0, _, l1 = lax.fori_loop(0, n_chunks, online_body, (m_init, l_init, m_init, l_init))
        return l0, l1

    bound2 = jnp.max(jnp.max(qn_ref[...], axis=-1, keepdims=True)
                     * jnp.max(kn_ref[...], axis=-1, keepdims=True))
    l0, l1 = lax.cond(bound2 <= SAFE_EXP2_BOUND * SAFE_EXP2_BOUND, plain, online)

    lq = lam_ref[...]
    lam = (jnp.exp(jnp.sum(lq[0:1] * lq[1:2], axis=-1, keepdims=True))
           - jnp.exp(jnp.sum(lq[2:3] * lq[3:4], axis=-1, keepdims=True)) + lambda_init)
    o = acc0[...] / l0 - lam * (acc1[...] / l1)
    o = o * lax.rsqrt(jnp.mean(o * o, axis=0, keepdims=True) + SUBLN_EPS)
    o = o * sw_ref[...] * (1.0 - lambda_init)
    o_ref[...] = o.T.astype(BF16)


def _diff_attn(qt, k3, vt, qn, kn, lam_vecs, subln_col, lambda_init):
    bsz, d, seq = qt.shape
    tq = min(seq, 1024)
    tk = min(seq, 4096)
    kern = functools.partial(_diff_attn_kernel, tk=tk, lambda_init=lambda_init)
    return pl.pallas_call(
        kern,
        out_shape=jax.ShapeDtypeStruct((bsz, seq, d), BF16),
        grid=(bsz, N_HEADS, seq // tq),
        in_specs=[pl.BlockSpec((None, V_DIM, tq), lambda b, h, i: (b, h, i)),
                  pl.BlockSpec((None, seq, V_DIM), lambda b, h, i: (b, 0, h)),
                  pl.BlockSpec((None, V_DIM, seq), lambda b, h, i: (b, h, 0)),
                  pl.BlockSpec((None, None, 2, tq), lambda b, h, i: (b, h, 0, i)),
                  pl.BlockSpec((None, None, 2, seq), lambda b, h, i: (b, h, 0, 0)),
                  pl.BlockSpec((4, HEAD_DIM), lambda b, h, i: (0, 0)),
                  pl.BlockSpec((V_DIM, 1), lambda b, h, i: (0, 0))],
        out_specs=pl.BlockSpec((None, tq, V_DIM), lambda b, h, i: (b, i, h)),
        scratch_shapes=[pltpu.VMEM((V_DIM, tq), F32), pltpu.VMEM((V_DIM, tq), F32)],
        compiler_params=pltpu.CompilerParams(
            dimension_semantics=("arbitrary", "arbitrary", "arbitrary"),
            vmem_limit_bytes=VMEM_LIMIT),
        name="diff_attn",
    )(qt, k3, vt, qn, kn, lam_vecs, subln_col)


def _post_mix_kernel(o_ref, cb_ref, z_ref, zp_ref, zn_ref, sga_ref, sgc_ref, x_ref,
                     g1_ref, sc2_ref, sh2_ref, n2_ref, cw_ref, wao_ref, wco_ref, wout_ref,
                     wrh_ref, wrl_ref, br_ref,
                     x1_ref, h2_ref, route_ref, cnt_ref, u_scr, *, tpb, n_sub):
    i = pl.program_id(0)
    tm = x_ref.shape[0]

    @pl.when(i == 0)
    def _():
        cnt_ref[...] = jnp.zeros_like(cnt_ref)

    z = z_ref[...].astype(F32)
    rows = lax.broadcasted_iota(jnp.int32, z.shape, 0)
    halo_rows = zp_ref.shape[0]
    prev_row = zp_ref[halo_rows - 1:halo_rows, :].astype(F32)
    next_row = zn_ref[0:1, :].astype(F32)
    prev_row = jnp.where(i % tpb == 0, jnp.zeros_like(prev_row), prev_row)
    next_row = jnp.where(i % tpb == tpb - 1, jnp.zeros_like(next_row), next_row)
    z_m1 = jnp.where(rows == 0, prev_row, pltpu.roll(z, 1, 0))
    z_p1 = jnp.where(rows == tm - 1, next_row, pltpu.roll(z, tm - 1, 0))
    cw = cw_ref[...]
    conv = z_m1 * cw[0:1] + z * cw[1:2] + z_p1 * cw[2:3]
    u_scr[...] = (cb_ref[...].astype(F32) * conv).astype(BF16)

    ts = tm // n_sub
    lane = lax.broadcasted_iota(jnp.int32, (ts, LANES), 1)
    r_i = lax.broadcasted_iota(jnp.int32, (ts, ts), 0)
    c_i = lax.broadcasted_iota(jnp.int32, (ts, ts), 1)
    lower = (r_i > c_i).astype(BF16)
    counts = cnt_ref[0:1, :]
    for s in range(n_sub):
        rs = slice(s * ts, (s + 1) * ts)
        y_attn = _dot(o_ref[rs, :], wao_ref[...])
        y_conv = _dot(u_scr[rs, :], wco_ref[...])
        m = sga_ref[rs, :].astype(F32) * y_attn + sgc_ref[rs, :].astype(F32) * y_conv
        x1 = x_ref[rs, :] + g1_ref[...] * _dot(m.astype(BF16), wout_ref[...])
        x1_ref[rs, :] = x1

        h2 = x1 * lax.rsqrt(jnp.mean(x1 * x1, axis=-1, keepdims=True) + RMS_EPS) * n2_ref[...]
        h2 = h2 * (1.0 + sc2_ref[...]) + sh2_ref[...]
        _store_tile_rows(h2_ref, s * ts, h2)

        h_hi = h2.astype(BF16)
        h_lo = (h2 - h_hi.astype(F32)).astype(BF16)
        logits = (_dot(h_hi, wrh_ref[...]) + _dot(h_lo, wrh_ref[...]) + _dot(h_hi, wrl_ref[...])
                  + br_ref[...])

        work = logits
        vals, idxs = [], []
        for _ in range(TOP_K):
            mx = jnp.max(work, axis=-1, keepdims=True)
            ix = jnp.min(jnp.where(work == mx, lane, LANES), axis=-1, keepdims=True)
            vals.append(mx)
            idxs.append(ix)
            work = jnp.where(lane == ix, 2.0 * NEG_BIG, work)
        exps = [jnp.exp(v - vals[0]) for v in vals]
        den = exps[0] + exps[1] + exps[2] + exps[3]

        sel = (work == 2.0 * NEG_BIG).astype(BF16)
        before = _dot(lower, sel) + counts
        counts = counts + jnp.sum(sel.astype(F32), axis=0, keepdims=True)

        route = jnp.zeros(logits.shape, F32)
        for k in range(TOP_K):
            rank = jnp.sum(jnp.where(lane == idxs[k], before, 0.0), axis=-1, keepdims=True)
            route = jnp.where(lane == k, exps[k] / den, route)
            route = jnp.where(lane == TOP_K + k, idxs[k].astype(F32), route)
            route = jnp.where(lane == 2 * TOP_K + k, rank, route)
        route_ref[rs, :] = route
    cnt_ref[...] = jnp.broadcast_to(counts, cnt_ref.shape)


def _post_mix(o2, cb, z, sga, sgc, x2, g1, sc2, sh2, n2, conv_w, wao, wco, wout,
              wr_hi, wr_lo, br_pad, seq):
    n, d = x2.shape
    tm = min(seq, 512)
    tpb = seq // tm
    halo = 16
    hb = tm // halo
    last_hb = n // halo - 1
    row = lambda i: (i, 0)
    const = lambda i: (0, 0)
    mod = lambda i: (i // tpb, 0, 0)
    wspec = pl.BlockSpec((d, d), const)
    kern = functools.partial(_post_mix_kernel, tpb=tpb, n_sub=1)
    return pl.pallas_call(
        kern,
        out_shape=(jax.ShapeDtypeStruct((n, d), F32),
                   jax.ShapeDtypeStruct((n * (d // LANES), LANES), F32),
                   jax.ShapeDtypeStruct((n, LANES), F32), jax.ShapeDtypeStruct((8, LANES), F32)),
        grid=(n // tm,),
        in_specs=[pl.BlockSpec((tm, d), row), pl.BlockSpec((tm, d), row), pl.BlockSpec((tm, d), row),
                  pl.BlockSpec((halo, d), lambda i: (jnp.maximum(i * hb - 1, 0), 0)),
                  pl.BlockSpec((halo, d), lambda i: (jnp.minimum((i + 1) * hb, last_hb), 0)),
                  pl.BlockSpec((tm, d), row), pl.BlockSpec((tm, d), row), pl.BlockSpec((tm, d), row),
                  pl.BlockSpec((None, 1, d), mod), pl.BlockSpec((None, 1, d), mod),
                  pl.BlockSpec((None, 1, d), mod),
                  pl.BlockSpec((1, d), const), pl.BlockSpec((3, d), const),
                  wspec, wspec, wspec,
                  pl.BlockSpec((d, LANES), const), pl.BlockSpec((d, LANES), const),
                  pl.BlockSpec((1, LANES), const)],
        out_specs=(pl.BlockSpec((tm, d), row), pl.BlockSpec((tm * (d // LANES), LANES), row),
                   pl.BlockSpec((tm, LANES), row), pl.BlockSpec((8, LANES), const)),
        scratch_shapes=[pltpu.VMEM((tm, d), BF16)],
        compiler_params=pltpu.CompilerParams(
            dimension_semantics=("arbitrary",), vmem_limit_bytes=VMEM_LIMIT),
        name="post_mix",
    )(o2, cb, z, z, z, sga, sgc, x2, g1, sc2, sh2, n2, conv_w, wao, wco, wout, wr_hi, wr_lo, br_pad)


def _tile_rows(ref, row, n_rows=1):
    start = pl.multiple_of(row * TILE_ROW, TILE_ROW)
    return ref.at[pl.ds(start, n_rows * TILE_ROW)]


def _row_copy(src, dst, s, t, sem):
    return pltpu.make_async_copy(_tile_rows(src, s), _tile_rows(dst, t), sem)


def _dispatch_kernel(dest_ref, ends_ref, h2_ref, xs_hbm, zbuf, sem, zsem, *, tc, nblk):
    base = pl.program_id(0) * tc

    @pl.when(pl.program_id(0) == 0)
    def _():
        zbuf[...] = jnp.zeros_like(zbuf)

        def zero_block(row):
            return pltpu.make_async_copy(zbuf, _tile_rows(xs_hbm, row, MOE_ROWS), zsem)

        def nonempty(e):
            return ends_ref[e] > (ends_ref[e - 1] if e else 0)

        total = ends_ref[N_EXPERTS - 1]
        n_tail = nblk - total // MOE_ROWS

        def tail_start(b, carry):
            zero_block(total + b * MOE_ROWS).start()
            return carry

        def tail_wait(b, carry):
            zero_block(0).wait()
            return carry

        for e in range(N_EXPERTS):
            @pl.when(nonempty(e))
            def _(e=e):
                zero_block(ends_ref[e] - MOE_ROWS).start()
        lax.fori_loop(0, n_tail, tail_start, 0)
        for e in range(N_EXPERTS):
            @pl.when(nonempty(e))
            def _():
                zero_block(0).wait()
        lax.fori_loop(0, n_tail, tail_wait, 0)

    def issue(t, carry):
        for k in range(TOP_K):
            _row_copy(h2_ref, xs_hbm, t, dest_ref[(base + t) * TOP_K + k], sem).start(priority=k % 2)
        return carry

    lax.fori_loop(0, tc, issue, 0, unroll=4)

    for _ in range(TOP_K):
        pltpu.make_async_copy(h2_ref, _tile_rows(xs_hbm, 0, tc), sem).wait()


def _moe_dispatch(dest_flat, ends, h2_tiles, n, p_rows):
    tc = min(n, 512)
    kern = functools.partial(_dispatch_kernel, tc=tc, nblk=p_rows // MOE_ROWS)
    return pl.pallas_call(
        kern,
        out_shape=jax.ShapeDtypeStruct((p_rows * TILE_ROW, LANES), F32),
        grid_spec=pltpu.PrefetchScalarGridSpec(
            num_scalar_prefetch=2, grid=(n // tc,),
            in_specs=[pl.BlockSpec((tc * TILE_ROW, LANES), lambda i, dr, en: (i, 0))],
            out_specs=pl.BlockSpec(memory_space=pl.ANY),
            scratch_shapes=[pltpu.VMEM((MOE_ROWS * TILE_ROW, LANES), F32),
                            pltpu.SemaphoreType.DMA, pltpu.SemaphoreType.DMA]),
        compiler_params=pltpu.CompilerParams(dimension_semantics=("arbitrary",)),
        name="moe_dispatch",
    )(dest_flat, ends, h2_tiles)


_NT = (((1,), (1,)), ((), ()))


def _ffn_kernel(first_ref, count_ref, slot_ref, next_ref, misc_ref,
                xs_hbm, wgu_hbm, wd_hbm, bg_ref, bu_ref, bd_ref, ys_hbm,
                wgu_buf, wd_buf, wt_scr, wg_scr, wu_scr, wd_scr, xbuf, ybuf,
                wsem, xsem, ysem, *, nblk):
    e = pl.program_id(0)
    first = first_ref[e]
    count = count_ref[e]
    slot = slot_ref[e]
    first_expert, total_blocks = misc_ref[0], misc_ref[1]

    def weight_copies(ex, s):
        return (pltpu.make_async_copy(wgu_hbm.at[ex], wgu_buf.at[s], wsem.at[0, s]),
                pltpu.make_async_copy(wd_hbm.at[ex], wd_buf.at[s], wsem.at[1, s]))

    def x_copy(b, s):
        return pltpu.make_async_copy(_tile_rows(xs_hbm, b * MOE_ROWS, MOE_ROWS), xbuf.at[s], xsem.at[s])

    def y_copy(b, s):
        return pltpu.make_async_copy(ybuf.at[s], _tile_rows(ys_hbm, b * MOE_ROWS, MOE_ROWS), ysem.at[s])

    @pl.when(e == first_expert)
    def _():
        for cp in weight_copies(e, 0):
            cp.start()

    @pl.when(count > 0)
    def _():
        x_copy(first, 0).start()
        for cp in weight_copies(e, slot):
            cp.wait()

        @pl.when(next_ref[e] >= 0)
        def _():
            for cp in weight_copies(next_ref[e], 1 - slot):
                cp.start()

        _, d, f2 = wgu_buf.shape
        for c in range(d // LANES):
            cols = slice(c * LANES, (c + 1) * LANES)
            wt_scr[c] = wgu_buf[slot, cols, :].T
            wg_scr[:, cols] = wt_scr[c, pl.ds(0, f2 // 2, stride=2), :].astype(BF16)
            wu_scr[:, cols] = wt_scr[c, pl.ds(1, f2 // 2, stride=2), :].astype(BF16)
        wd_scr[...] = wd_buf[slot].astype(BF16)
        bg, bu, bd = bg_ref[e], bu_ref[e], bd_ref[e]

        def block(j, carry):
            s = j % 2
            x_copy(first, s).wait()

            @pl.when(j + 1 < count)
            def _():
                x_copy(first + j + 1, 1 - s).start()

            @pl.when(j >= 2)
            def _():
                y_copy(first, s).wait()

            x = jnp.concatenate(_load_tile_rows(xbuf.at[s], TILE_ROW), axis=1).astype(BF16)
            gate = lax.dot_general(x, wg_scr[...], _NT, preferred_element_type=F32) + bg
            up = lax.dot_general(x, wu_scr[...], _NT, preferred_element_type=F32) + bu
            gate = jnp.minimum(gate, SWIGLU_LIMIT)
            up = jnp.clip(up, -SWIGLU_LIMIT, SWIGLU_LIMIT)
            glu = gate * jax.nn.sigmoid(gate * SWIGLU_ALPHA)
            mid = ((up + 1.0) * glu).astype(BF16)
            _store_tile_rows(ybuf.at[s], 0, _dot(mid, wd_scr[...]) + bd)
            y_copy(first + j, s).start()
            return carry

        lax.fori_loop(0, count, block, 0)

        @pl.when(count >= 2)
        def _():
            y_copy(first, count % 2).wait()
        y_copy(first, (count - 1) % 2).wait()

    @pl.when(e == pl.num_programs(0) - 1)
    def _():
        ybuf[0] = jnp.zeros(ybuf.shape[1:], ybuf.dtype)

        def tail_start(b, carry):
            y_copy(b, 0).start()
            return carry

        def tail_wait(b, carry):
            y_copy(b, 0).wait()
            return carry

        lax.fori_loop(total_blocks, nblk, tail_start, 0)
        lax.fori_loop(total_blocks, nblk, tail_wait, 0)


def _moe_ffn(first_blk, n_blk, slot, next_e, misc, xs, wgu, wd, bg, bu, bd):
    p_rows = xs.shape[0] // TILE_ROW
    n_exp, d, f2 = wgu.shape
    f = f2 // 2
    assert d == TILE_ROW * LANES
    whole = lambda shape: pl.BlockSpec(shape, lambda e, *_: (0,) * len(shape))
    tile = (MOE_ROWS * TILE_ROW, LANES)
    kern = functools.partial(_ffn_kernel, nblk=p_rows // MOE_ROWS)
    return pl.pallas_call(
        kern,
        out_shape=jax.ShapeDtypeStruct((p_rows * TILE_ROW, LANES), F32),
        grid_spec=pltpu.PrefetchScalarGridSpec(
            num_scalar_prefetch=5, grid=(n_exp,),
            in_specs=[pl.BlockSpec(memory_space=pl.ANY),
                      pl.BlockSpec(memory_space=pl.ANY), pl.BlockSpec(memory_space=pl.ANY),
                      whole(bg.shape), whole(bu.shape), whole(bd.shape)],
            out_specs=pl.BlockSpec(memory_space=pl.ANY),
            scratch_shapes=[pltpu.VMEM((2, d, f2), F32), pltpu.VMEM((2, f, d), F32),
                            pltpu.VMEM((d // LANES, f2, LANES), F32), pltpu.VMEM((f, d), BF16),
                            pltpu.VMEM((f, d), BF16), pltpu.VMEM((f, d), BF16),
                            pltpu.VMEM((2,) + tile, F32), pltpu.VMEM((2,) + tile, F32),
                            pltpu.SemaphoreType.DMA((2, 2)), pltpu.SemaphoreType.DMA((2,)),
                            pltpu.SemaphoreType.DMA((2,))]),
        compiler_params=pltpu.CompilerParams(
            dimension_semantics=("arbitrary",), vmem_limit_bytes=VMEM_LIMIT),
        name="moe_ffn",
    )(first_blk, n_blk, slot, next_e, misc, xs, wgu, wd, bg, bu, bd)


def _combine_kernel(dest_ref, ys_hbm, x1_ref, route_ref, g2_ref, o_ref, buf, sem, *, tc, n_steps):
    i = pl.program_id(0)
    slot = i % 2

    def gather(step, s):
        def issue(t, carry):
            for k in range(TOP_K):
                row = dest_ref[(step * tc + t) * TOP_K + k]
                _row_copy(ys_hbm, buf.at[s, k], row, t, sem.at[s]).start(priority=k % 2)
            return carry

        lax.fori_loop(0, tc, issue, 0, unroll=4)

    @pl.when(i == 0)
    def _():
        gather(0, 0)

    @pl.when(i + 1 < n_steps)
    def _():
        gather(i + 1, 1 - slot)

    for k in range(TOP_K):
        pltpu.make_async_copy(_tile_rows(ys_hbm, 0, tc), buf.at[slot, k], sem.at[slot]).wait()

    route = route_ref[...]
    for c in range(TILE_ROW):
        cols = slice(c * LANES, (c + 1) * LANES)
        y = buf[slot, 0, pl.ds(c, tc, stride=TILE_ROW), :] * route[:, 0:1]
        for k in range(1, TOP_K):
            y = y + buf[slot, k, pl.ds(c, tc, stride=TILE_ROW), :] * route[:, k:k + 1]
        o_ref[:, cols] = x1_ref[:, cols] + g2_ref[:, cols] * y


def _moe_combine(dest_flat, ys, x1, route, g2, seq):
    n, d = x1.shape
    assert d == TILE_ROW * LANES
    tc = min(seq, 256)
    tpb = seq // tc
    kern = functools.partial(_combine_kernel, tc=tc, n_steps=n // tc)
    return pl.pallas_call(
        kern,
        out_shape=jax.ShapeDtypeStruct((n, d), F32),
        grid_spec=pltpu.PrefetchScalarGridSpec(
            num_scalar_prefetch=1, grid=(n // tc,),
            in_specs=[pl.BlockSpec(memory_space=pl.ANY),
                      pl.BlockSpec((tc, d), lambda i, dr: (i, 0)),
                      pl.BlockSpec((tc, LANES), lambda i, dr: (i, 0)),
                      pl.BlockSpec((None, 1, d), lambda i, dr: (i // tpb, 0, 0))],
            out_specs=pl.BlockSpec((tc, d), lambda i, dr: (i, 0)),
            scratch_shapes=[pltpu.VMEM((2, TOP_K, tc * TILE_ROW, LANES), F32),
                            pltpu.SemaphoreType.DMA((2,))]),
        compiler_params=pltpu.CompilerParams(
            dimension_semantics=("arbitrary",), vmem_limit_bytes=VMEM_LIMIT),
        name="moe_combine",
    )(dest_flat, ys, x1, route, g2)


def _rope_tables(seq):
    inv_freq = (ROPE_THETA ** (-np.arange(0, HEAD_DIM, 2, dtype=np.float32) / HEAD_DIM)).astype(np.float32)
    ang = (inv_freq[:, None] * np.arange(seq, dtype=np.float32)[None, :]).astype(np.float32)
    cos, sin = np.cos(ang.astype(np.float64)), np.sin(ang.astype(np.float64))
    cos_t = np.concatenate([cos, cos], axis=0).astype(np.float32)
    sin_t = np.concatenate([-sin, sin], axis=0).astype(np.float32)
    return jnp.asarray(cos_t), jnp.asarray(sin_t)


def _layer(x, c, l, lambda_init, w_ada, b_ada, norm1_w, w_in, q_norm_w, k_norm_w, lambda_q1,
           lambda_k1, lambda_q2, lambda_k2, subln_w, w_attn_o, conv_w, w_conv_o, w_out, norm2_w,
           w_router, b_router, w_gate_up, b_gate_up, w_down, b_down):
    bsz, seq, d = x.shape
    n = bsz * seq
    x2 = x.reshape(n, d)

    mod = _ada_mod(c, w_ada[l], b_ada[l])
    sh1, sc1, g1, sh2, sc2, g2 = [m.reshape(bsz, 1, d) for m in jnp.split(mod, 6, axis=-1)]

    cos_t, sin_t = _rope_tables(seq)
    qt, k, vt, cb, z, sga, sgc, qn, kn = _in_proj(
        x2, norm1_w[l].reshape(1, d), sc1, sh1, w_in[l].astype(BF16),
        q_norm_w[l].reshape(HEAD_DIM, 1) * Q_SCALE, k_norm_w[l].reshape(HEAD_DIM, 1),
        cos_t, sin_t, bsz, seq)

    lam_vecs = jnp.stack([lambda_q1[l], lambda_k1[l], lambda_q2[l], lambda_k2[l]]).astype(F32)
    o = _diff_attn(qt, k.reshape(bsz, seq, d), vt, qn, kn, lam_vecs, subln_w[l].reshape(V_DIM, 1),
                   lambda_init)

    wr = jnp.zeros((d, LANES), F32).at[:, :N_EXPERTS].set(w_router[l].astype(F32))
    wr_hi = wr.astype(BF16)
    wr_lo = (wr - wr_hi.astype(F32)).astype(BF16)
    br_pad = jnp.full((1, LANES), NEG_BIG, F32).at[0, :N_EXPERTS].set(b_router[l].astype(F32))
    x1, h2, route, cnt = _post_mix(
        o.reshape(n, d), cb, z, sga, sgc, x2, g1, sc2, sh2, norm2_w[l].reshape(1, d), conv_w[l],
        w_attn_o[l].astype(BF16), w_conv_o[l].astype(BF16), w_out[l].astype(BF16),
        wr_hi, wr_lo, br_pad, seq)

    counts = cnt[0, :N_EXPERTS].astype(jnp.int32)
    padded = ((counts + MOE_ROWS - 1) // MOE_ROWS) * MOE_ROWS
    ends = jnp.cumsum(padded)
    start = ends - padded
    p_rows = n * TOP_K + N_EXPERTS * MOE_ROWS
    top_e = route[:, TOP_K:2 * TOP_K].astype(jnp.int32)
    rank = route[:, 2 * TOP_K:3 * TOP_K].astype(jnp.int32)
    experts = jnp.arange(N_EXPERTS, dtype=jnp.int32)
    seg_start = jnp.sum(jnp.where(top_e[..., None] == experts, start, 0), axis=-1)
    dest = (seg_start + rank).reshape(-1)
    nonempty = padded > 0
    later = jnp.logical_and(nonempty[None, :], experts[None, :] > experts[:, None])
    next_e = jnp.min(jnp.where(later, experts[None, :], N_EXPERTS), axis=1)
    next_e = jnp.where(next_e == N_EXPERTS, -1, next_e).astype(jnp.int32)
    slot = ((jnp.cumsum(nonempty.astype(jnp.int32)) - 1) % 2).astype(jnp.int32)
    first_blk = (start // MOE_ROWS).astype(jnp.int32)
    n_blk = (padded // MOE_ROWS).astype(jnp.int32)
    misc = jnp.stack([jnp.argmax(nonempty).astype(jnp.int32), (ends[-1] // MOE_ROWS).astype(jnp.int32)])

    xs = _moe_dispatch(dest, ends.astype(jnp.int32), h2, n, p_rows)
    ys = _moe_ffn(first_blk, n_blk, slot, next_e, misc, xs, w_gate_up[l], w_down[l],
                  b_gate_up[l][:, None, 0::2], b_gate_up[l][:, None, 1::2], b_down[l][:, None, :])
    out = _moe_combine(dest, ys, x1, route, g2, seq)
    return out.reshape(bsz, seq, d)


def kernel(x, c, w_ada, b_ada, norm1_w, w_in, q_norm_w, k_norm_w, lambda_q1, lambda_k1, lambda_q2,
           lambda_k2, subln_w, w_attn_o, conv_w, w_conv_o, w_out, norm2_w, w_router, b_router,
           w_gate_up, b_gate_up, w_down, b_down):
    depth = w_ada.shape[0]
    for l in range(depth):
        lambda_init = 0.8 - 0.6 * math.exp(-0.3 * l)
        x = _layer(x, c, l, lambda_init, w_ada, b_ada, norm1_w, w_in, q_norm_w, k_norm_w,
                   lambda_q1, lambda_k1, lambda_q2, lambda_k2, subln_w, w_attn_o, conv_w,
                   w_conv_o, w_out, norm2_w, w_router, b_router, w_gate_up, b_gate_up,
                   w_down, b_down)
    return x
```

```python
import functools
import math

import jax
import jax.numpy as jnp
import numpy as np
from jax import lax
from jax.experimental import pallas as pl
from jax.experimental.pallas import tpu as pltpu

N_HEADS = 8
HEAD_DIM = 64
V_DIM = 2 * HEAD_DIM
IN_BLOCKS = ("q", "k", "v", "conv_b", "conv_c", "conv_x", "gate_attn", "gate_conv")
N_EXPERTS = 32
TOP_K = 4
SWIGLU_LIMIT = 7.0
SWIGLU_ALPHA = 1.702
ROPE_THETA = 10000.0
RMS_EPS = 1e-6
SUBLN_EPS = 1e-5
LANES = 128
TILE_ROW = 8
MOE_ROWS = 256
NEG_BIG = -1e30
LOG2E = 1.4426950408889634
Q_SCALE = LOG2E / math.sqrt(HEAD_DIM)
SAFE_EXP2_BOUND = 80.0
VMEM_LIMIT = 56 * 1024 * 1024

F32 = jnp.float32
BF16 = jnp.bfloat16


def _dot(a, b):
    return jnp.dot(a, b, preferred_element_type=F32)


def _store_tile_rows(ref, row0, val):
    rows, d = val.shape
    chunks = d // LANES
    for c in range(chunks):
        ref[pl.ds(row0 * chunks + c, rows, stride=chunks), :] = val[:, c * LANES:(c + 1) * LANES]


def _load_tile_rows(ref, chunks):
    rows = ref.shape[0] // chunks
    return [ref[pl.ds(c, rows, stride=chunks), :] for c in range(chunks)]


def _ada_kernel(ct_ref, w_ref, b_ref, o_ref):
    ct = ct_ref[...]
    s = ct * jax.nn.sigmoid(ct)
    w = w_ref[...]
    for b in range(ct.shape[1]):
        o_ref[b:b + 1, :] = jnp.sum(w * s[:, b:b + 1], axis=0, keepdims=True) + b_ref[...]


def _ada_mod(c, w_ada, b_ada):
    bsz, d = c.shape
    n = w_ada.shape[1]
    tn = min(n, 1536)
    return pl.pallas_call(
        _ada_kernel,
        out_shape=jax.ShapeDtypeStruct((bsz, n), F32),
        grid=(n // tn,),
        in_specs=[pl.BlockSpec((d, bsz), lambda j: (0, 0)),
                  pl.BlockSpec((d, tn), lambda j: (0, j)),
                  pl.BlockSpec((1, tn), lambda j: (0, j))],
        out_specs=pl.BlockSpec((bsz, tn), lambda j: (0, j)),
        compiler_params=pltpu.CompilerParams(dimension_semantics=("arbitrary",)),
        name="ada_mod",
    )(c.T, w_ada, b_ada.reshape(1, n))


def _qk_norm_rope_t(y, g_col, cos_t, sin_t):
    tm, w = y.shape
    yt = y.T.reshape(w // HEAD_DIM, HEAD_DIM, tm)
    ms = jnp.mean(yt * yt, axis=1, keepdims=True)
    yn = yt * lax.rsqrt(ms + RMS_EPS) * g_col[None]
    half = HEAD_DIM // 2
    swapped = jnp.concatenate([yn[:, half:, :], yn[:, :half, :]], axis=1)
    out = yn * cos_t[None] + swapped * sin_t[None]
    norm2 = jnp.sum(out * out, axis=1).reshape(N_HEADS, 2, tm)
    return out.reshape(w, tm), norm2


def _in_proj_kernel(x_ref, n1_ref, sc_ref, sh_ref, w_ref, gq_ref, gk_ref, cos_ref, sin_ref,
                    qt_ref, k_ref, vt_ref, cb_ref, z_ref, sga_ref, sgc_ref, qn_ref, kn_ref, h_scr):
    x = x_ref[...]
    xn = x * lax.rsqrt(jnp.mean(x * x, axis=-1, keepdims=True) + RMS_EPS) * n1_ref[...]
    h_scr[...] = (xn * (1.0 + sc_ref[...]) + sh_ref[...]).astype(BF16)
    wcol = w_ref.shape[1] // len(IN_BLOCKS)

    def proj(name):
        c = IN_BLOCKS.index(name)
        return _dot(h_scr[...], w_ref[:, c * wcol:(c + 1) * wcol])

    qt, qn = _qk_norm_rope_t(proj("q"), gq_ref[...], cos_ref[...], sin_ref[...])
    qt_ref[...] = qt.astype(BF16)
    qn_ref[...] = qn
    kt, kn = _qk_norm_rope_t(proj("k"), gk_ref[...], cos_ref[...], sin_ref[...])
    k_ref[...] = kt.T.astype(BF16)
    kn_ref[...] = kn
    vt_ref[...] = proj("v").T.astype(BF16)
    cb_ref[...] = proj("conv_b").astype(BF16)
    z_ref[...] = (proj("conv_c") * proj("conv_x")).astype(BF16)
    sga_ref[...] = jax.nn.sigmoid(proj("gate_attn")).astype(BF16)
    sgc_ref[...] = jax.nn.sigmoid(proj("gate_conv")).astype(BF16)


def _in_proj(x2, n1, sc1, sh1, w_in_bf, gq, gk, cos_t, sin_t, bsz, seq):
    n, d = x2.shape
    tm = min(seq, 512)
    tpb = seq // tm
    assert w_in_bf.shape[1] == len(IN_BLOCKS) * d
    row = lambda i: (i, 0)
    const = lambda i: (0, 0)
    tcol = lambda i: (i // tpb, 0, i % tpb)
    mod = lambda i: (i // tpb, 0, 0)
    nat = jax.ShapeDtypeStruct((n, d), BF16)
    tr = jax.ShapeDtypeStruct((bsz, d, seq), BF16)
    nrm = jax.ShapeDtypeStruct((bsz, N_HEADS, 2, seq), F32)
    nat_spec = pl.BlockSpec((tm, d), row)
    tr_spec = pl.BlockSpec((None, d, tm), tcol)
    nrm_spec = pl.BlockSpec((None, N_HEADS, 2, tm), lambda i: (i // tpb, 0, 0, i % tpb))
    return pl.pallas_call(
        _in_proj_kernel,
        out_shape=(tr, nat, tr, nat, nat, nat, nat, nrm, nrm),
        grid=(n // tm,),
        in_specs=[pl.BlockSpec((tm, d), row),
                  pl.BlockSpec((1, d), const),
                  pl.BlockSpec((None, 1, d), mod),
                  pl.BlockSpec((None, 1, d), mod),
                  pl.BlockSpec(w_in_bf.shape, const),
                  pl.BlockSpec((HEAD_DIM, 1), const),
                  pl.BlockSpec((HEAD_DIM, 1), const),
                  pl.BlockSpec((HEAD_DIM, tm), lambda i: (0, i % tpb)),
                  pl.BlockSpec((HEAD_DIM, tm), lambda i: (0, i % tpb))],
        out_specs=(tr_spec, nat_spec, tr_spec, nat_spec, nat_spec, nat_spec, nat_spec,
                   nrm_spec, nrm_spec),
        scratch_shapes=[pltpu.VMEM((tm, d), BF16)],
        compiler_params=pltpu.CompilerParams(
            dimension_semantics=("arbitrary",), vmem_limit_bytes=VMEM_LIMIT),
        name="in_proj",
    )(x2, n1, sc1, sh1, w_in_bf, gq, gk, cos_t, sin_t)


def _diff_attn_kernel(qt_ref, k_ref, vt_ref, qn_ref, kn_ref, lam_ref, sw_ref, o_ref, acc0, acc1,
                      *, tk, lambda_init):
    seq = k_ref.shape[0]
    tq = qt_ref.shape[1]
    qt = qt_ref[...]
    first = lax.broadcasted_iota(jnp.int32, qt.shape, 0) < HEAD_DIM
    zero = jnp.zeros_like(qt)
    qz = (jnp.where(first, qt, zero), jnp.where(first, zero, qt))
    accs = (acc0, acc1)
    acc0[...] = jnp.zeros_like(acc0)
    acc1[...] = jnp.zeros_like(acc1)
    n_chunks = seq // tk

    def load(j):
        off = pl.multiple_of(j * tk, tk)
        return k_ref[pl.ds(off, tk), :], vt_ref[:, pl.ds(off, tk)]

    def plain_body(j, carry):
        kk, vt = load(j)
        new = []
        for c in range(2):
            p = jnp.exp2(_dot(kk, qz[c]))
            new.append(carry[c] + jnp.sum(p, axis=0, keepdims=True))
            accs[c][...] += _dot(vt, p.astype(BF16))
        return tuple(new)

    def online_body(j, carry):
        kk, vt = load(j)
        new = []
        for c in range(2):
            m, l = carry[2 * c], carry[2 * c + 1]
            s = _dot(kk, qz[c])
            m_new = jnp.maximum(m, jnp.max(s, axis=0, keepdims=True))
            alpha = jnp.exp2(m - m_new)
            p = jnp.exp2(s - m_new)
            l = alpha * l + jnp.sum(p, axis=0, keepdims=True)
            accs[c][...] = alpha * accs[c][...] + _dot(vt, p.astype(BF16))
            new += [m_new, l]
        return tuple(new)

    m_init = jnp.full((1, tq), NEG_BIG, F32)
    l_init = jnp.zeros((1, tq), F32)

    def plain():
        return lax.fori_loop(0, n_chunks, plain_body, (l_init, l_init))

    def online():
        _, l0, _, l1 = lax.fori_loop(0, n_chunks, online_body, (m_init, l_init, m_init, l_init))
        return l0, l1

    bound2 = jnp.max(jnp.max(qn_ref[...], axis=-1, keepdims=True)
                     * jnp.max(kn_ref[...], axis=-1, keepdims=True))
    l0, l1 = lax.cond(bound2 <= SAFE_EXP2_BOUND * SAFE_EXP2_BOUND, plain, online)

    lq = lam_ref[...]
    lam = (jnp.exp(jnp.sum(lq[0:1] * lq[1:2], axis=-1, keepdims=True))
           - jnp.exp(jnp.sum(lq[2:3] * lq[3:4], axis=-1, keepdims=True)) + lambda_init)
    o = acc0[...] / l0 - lam * (acc1[...] / l1)
    o = o * lax.rsqrt(jnp.mean(o * o, axis=0, keepdims=True) + SUBLN_EPS)
    o = o * sw_ref[...] * (1.0 - lambda_init)
    o_ref[...] = o.T.astype(BF16)


def _diff_attn(qt, k3, vt, qn, kn, lam_vecs, subln_col, lambda_init):
    bsz, d, seq = qt.shape
    tq = min(seq, 1024)
    tk = min(seq, 4096)
    kern = functools.partial(_diff_attn_kernel, tk=tk, lambda_init=lambda_init)
    return pl.pallas_call(
        kern,
        out_shape=jax.ShapeDtypeStruct((bsz, seq, d), BF16),
        grid=(bsz, N_HEADS, seq // tq),
        in_specs=[pl.BlockSpec((None, V_DIM, tq), lambda b, h, i: (b, h, i)),
                  pl.BlockSpec((None, seq, V_DIM), lambda b, h, i: (b, 0, h)),
                  pl.BlockSpec((None, V_DIM, seq), lambda b, h, i: (b, h, 0)),
                  pl.BlockSpec((None, None, 2, tq), lambda b, h, i: (b, h, 0, i)),
                  pl.BlockSpec((None, None, 2, seq), lambda b, h, i: (b, h, 0, 0)),
                  pl.BlockSpec((4, HEAD_DIM), lambda b, h, i: (0, 0)),
                  pl.BlockSpec((V_DIM, 1), lambda b, h, i: (0, 0))],
        out_specs=pl.BlockSpec((None, tq, V_DIM), lambda b, h, i: (b, i, h)),
        scratch_shapes=[pltpu.VMEM((V_DIM, tq), F32), pltpu.VMEM((V_DIM, tq), F32)],
        compiler_params=pltpu.CompilerParams(
            dimension_semantics=("arbitrary", "arbitrary", "arbitrary"),
            vmem_limit_bytes=VMEM_LIMIT),
        name="diff_attn",
    )(qt, k3, vt, qn, kn, lam_vecs, subln_col)


def _post_mix_kernel(o_ref, cb_ref, z_ref, zp_ref, zn_ref, sga_ref, sgc_ref, x_ref,
                     g1_ref, sc2_ref, sh2_ref, n2_ref, cw_ref, wao_ref, wco_ref, wout_ref,
                     wrh_ref, wrl_ref, br_ref,
                     x1_ref, h2_ref, route_ref, cnt_ref, u_scr, *, tpb, n_sub):
    i = pl.program_id(0)
    tm = x_ref.shape[0]

    @pl.when(i == 0)
    def _():
        cnt_ref[...] = jnp.zeros_like(cnt_ref)

    z = z_ref[...].astype(F32)
    rows = lax.broadcasted_iota(jnp.int32, z.shape, 0)
    halo_rows = zp_ref.shape[0]
    prev_row = zp_ref[halo_rows - 1:halo_rows, :].astype(F32)
    next_row = zn_ref[0:1, :].astype(F32)
    prev_row = jnp.where(i % tpb == 0, jnp.zeros_like(prev_row), prev_row)
    next_row = jnp.where(i % tpb == tpb - 1, jnp.zeros_like(next_row), next_row)
    z_m1 = jnp.where(rows == 0, prev_row, pltpu.roll(z, 1, 0))
    z_p1 = jnp.where(rows == tm - 1, next_row, pltpu.roll(z, tm - 1, 0))
    cw = cw_ref[...]
    conv = z_m1 * cw[0:1] + z * cw[1:2] + z_p1 * cw[2:3]
    u_scr[...] = (cb_ref[...].astype(F32) * conv).astype(BF16)

    ts = tm // n_sub
    lane = lax.broadcasted_iota(jnp.int32, (ts, LANES), 1)
    r_i = lax.broadcasted_iota(jnp.int32, (ts, ts), 0)
    c_i = lax.broadcasted_iota(jnp.int32, (ts, ts), 1)
    lower = (r_i > c_i).astype(BF16)
    counts = cnt_ref[0:1, :]
    for s in range(n_sub):
        rs = slice(s * ts, (s + 1) * ts)
        y_attn = _dot(o_ref[rs, :], wao_ref[...])
        y_conv = _dot(u_scr[rs, :], wco_ref[...])
        m = sga_ref[rs, :].astype(F32) * y_attn + sgc_ref[rs, :].astype(F32) * y_conv
        x1 = x_ref[rs, :] + g1_ref[...] * _dot(m.astype(BF16), wout_ref[...])
        x1_ref[rs, :] = x1

        h2 = x1 * lax.rsqrt(jnp.mean(x1 * x1, axis=-1, keepdims=True) + RMS_EPS) * n2_ref[...]
        h2 = h2 * (1.0 + sc2_ref[...]) + sh2_ref[...]
        _store_tile_rows(h2_ref, s * ts, h2)

        h_hi = h2.astype(BF16)
        h_lo = (h2 - h_hi.astype(F32)).astype(BF16)
        logits = (_dot(h_hi, wrh_ref[...]) + _dot(h_lo, wrh_ref[...]) + _dot(h_hi, wrl_ref[...])
                  + br_ref[...])

        work = logits
        vals, idxs = [], []
        for _ in range(TOP_K):
            mx = jnp.max(work, axis=-1, keepdims=True)
            ix = jnp.min(jnp.where(work == mx, lane, LANES), axis=-1, keepdims=True)
            vals.append(mx)
            idxs.append(ix)
            work = jnp.where(lane == ix, 2.0 * NEG_BIG, work)
        exps = [jnp.exp(v - vals[0]) for v in vals]
        den = exps[0] + exps[1] + exps[2] + exps[3]

        sel = (work == 2.0 * NEG_BIG).astype(BF16)
        before = _dot(lower, sel) + counts
        counts = counts + jnp.sum(sel.astype(F32), axis=0, keepdims=True)

        route = jnp.zeros(logits.shape, F32)
        for k in range(TOP_K):
            rank = jnp.sum(jnp.where(lane == idxs[k], before, 0.0), axis=-1, keepdims=True)
            route = jnp.where(lane == k, exps[k] / den, route)
            route = jnp.where(lane == TOP_K + k, idxs[k].astype(F32), route)
            route = jnp.where(lane == 2 * TOP_K + k, rank, route)
        route_ref[rs, :] = route
    cnt_ref[...] = jnp.broadcast_to(counts, cnt_ref.shape)


def _post_mix(o2, cb, z, sga, sgc, x2, g1, sc2, sh2, n2, conv_w, wao, wco, wout,
              wr_hi, wr_lo, br_pad, seq):
    n, d = x2.shape
    tm = min(seq, 512)
    tpb = seq // tm
    halo = 16
    hb = tm // halo
    last_hb = n // halo - 1
    row = lambda i: (i, 0)
    const = lambda i: (0, 0)
    mod = lambda i: (i // tpb, 0, 0)
    wspec = pl.BlockSpec((d, d), const)
    kern = functools.partial(_post_mix_kernel, tpb=tpb, n_sub=1)
    return pl.pallas_call(
        kern,
        out_shape=(jax.ShapeDtypeStruct((n, d), F32),
                   jax.ShapeDtypeStruct((n * (d // LANES), LANES), F32),
                   jax.ShapeDtypeStruct((n, LANES), F32), jax.ShapeDtypeStruct((8, LANES), F32)),
        grid=(n // tm,),
        in_specs=[pl.BlockSpec((tm, d), row), pl.BlockSpec((tm, d), row), pl.BlockSpec((tm, d), row),
                  pl.BlockSpec((halo, d), lambda i: (jnp.maximum(i * hb - 1, 0), 0)),
                  pl.BlockSpec((halo, d), lambda i: (jnp.minimum((i + 1) * hb, last_hb), 0)),
                  pl.BlockSpec((tm, d), row), pl.BlockSpec((tm, d), row), pl.BlockSpec((tm, d), row),
                  pl.BlockSpec((None, 1, d), mod), pl.BlockSpec((None, 1, d), mod),
                  pl.BlockSpec((None, 1, d), mod),
                  pl.BlockSpec((1, d), const), pl.BlockSpec((3, d), const),
                  wspec, wspec, wspec,
                  pl.BlockSpec((d, LANES), const), pl.BlockSpec((d, LANES), const),
                  pl.BlockSpec((1, LANES), const)],
        out_specs=(pl.BlockSpec((tm, d), row), pl.BlockSpec((tm * (d // LANES), LANES), row),
                   pl.BlockSpec((tm, LANES), row), pl.BlockSpec((8, LANES), const)),
        scratch_shapes=[pltpu.VMEM((tm, d), BF16)],
        compiler_params=pltpu.CompilerParams(
            dimension_semantics=("arbitrary",), vmem_limit_bytes=VMEM_LIMIT),
        name="post_mix",
    )(o2, cb, z, z, z, sga, sgc, x2, g1, sc2, sh2, n2, conv_w, wao, wco, wout, wr_hi, wr_lo, br_pad)


def _tile_rows(ref, row, n_rows=1):
    start = pl.multiple_of(row * TILE_ROW, TILE_ROW)
    return ref.at[pl.ds(start, n_rows * TILE_ROW)]


def _row_copy(src, dst, s, t, sem):
    return pltpu.make_async_copy(_tile_rows(src, s), _tile_rows(dst, t), sem)


def _dispatch_kernel(dest_ref, ends_ref, h2_ref, xs_hbm, zbuf, sem, zsem, *, tc, nblk):
    base = pl.program_id(0) * tc

    @pl.when(pl.program_id(0) == 0)
    def _():
        zbuf[...] = jnp.zeros_like(zbuf)

        def zero_block(row):
            return pltpu.make_async_copy(zbuf, _tile_rows(xs_hbm, row, MOE_ROWS), zsem)

        def nonempty(e):
            return ends_ref[e] > (ends_ref[e - 1] if e else 0)

        total = ends_ref[N_EXPERTS - 1]
        n_tail = nblk - total // MOE_ROWS

        def tail_start(b, carry):
            zero_block(total + b * MOE_ROWS).start()
            return carry

        def tail_wait(b, carry):
            zero_block(0).wait()
            return carry

        for e in range(N_EXPERTS):
            @pl.when(nonempty(e))
            def _(e=e):
                zero_block(ends_ref[e] - MOE_ROWS).start()
        lax.fori_loop(0, n_tail, tail_start, 0)
        for e in range(N_EXPERTS):
            @pl.when(nonempty(e))
            def _():
                zero_block(0).wait()
        lax.fori_loop(0, n_tail, tail_wait, 0)

    def issue(t, carry):
        for k in range(TOP_K):
            _row_copy(h2_ref, xs_hbm, t, dest_ref[(base + t) * TOP_K + k], sem).start(priority=k % 2)
        return carry

    lax.fori_loop(0, tc, issue, 0, unroll=4)

    for _ in range(TOP_K):
        pltpu.make_async_copy(h2_ref, _tile_rows(xs_hbm, 0, tc), sem).wait()


def _moe_dispatch(dest_flat, ends, h2_tiles, n, p_rows):
    tc = min(n, 512)
    kern = functools.partial(_dispatch_kernel, tc=tc, nblk=p_rows // MOE_ROWS)
    return pl.pallas_call(
        kern,
        out_shape=jax.ShapeDtypeStruct((p_rows * TILE_ROW, LANES), F32),
        grid_spec=pltpu.PrefetchScalarGridSpec(
            num_scalar_prefetch=2, grid=(n // tc,),
            in_specs=[pl.BlockSpec((tc * TILE_ROW, LANES), lambda i, dr, en: (i, 0))],
            out_specs=pl.BlockSpec(memory_space=pl.ANY),
            scratch_shapes=[pltpu.VMEM((MOE_ROWS * TILE_ROW, LANES), F32),
                            pltpu.SemaphoreType.DMA, pltpu.SemaphoreType.DMA]),
        compiler_params=pltpu.CompilerParams(dimension_semantics=("arbitrary",)),
        name="moe_dispatch",
    )(dest_flat, ends, h2_tiles)


_NT = (((1,), (1,)), ((), ()))


def _ffn_kernel(first_ref, count_ref, slot_ref, next_ref, misc_ref,
                xs_hbm, wgu_hbm, wd_hbm, bg_ref, bu_ref, bd_ref, ys_hbm,
                wgu_buf, wd_buf, wg_scr, wu_scr, wd_scr, xbuf, ybuf,
                wsem, xsem, ysem, *, nblk):
    e = pl.program_id(0)
    first = first_ref[e]
    count = count_ref[e]
    slot = slot_ref[e]
    first_expert, total_blocks = misc_ref[0], misc_ref[1]

    def weight_copies(ex, s):
        return (pltpu.make_async_copy(wgu_hbm.at[ex], wgu_buf.at[s], wsem.at[0, s]),
                pltpu.make_async_copy(wd_hbm.at[ex], wd_buf.at[s], wsem.at[1, s]))

    def x_copy(b, s):
        return pltpu.make_async_copy(_tile_rows(xs_hbm, b * MOE_ROWS, MOE_ROWS), xbuf.at[s], xsem.at[s])

    def y_copy(b, s):
        return pltpu.make_async_copy(ybuf.at[s], _tile_rows(ys_hbm, b * MOE_ROWS, MOE_ROWS), ysem.at[s])

    @pl.when(e == first_expert)
    def _():
        for cp in weight_copies(e, 0):
            cp.start()

    @pl.when(count > 0)
    def _():
        x_copy(first, 0).start()
        for cp in weight_copies(e, slot):
            cp.wait()

        @pl.when(next_ref[e] >= 0)
        def _():
            for cp in weight_copies(next_ref[e], 1 - slot):
                cp.start()

        d = wgu_buf.shape[1]
        for c in range(d // LANES):
            cols = slice(c * LANES, (c + 1) * LANES)
            words = pltpu.bitcast(wgu_buf[slot, cols, :].astype(BF16).T, jnp.uint32)
            wg_scr[:, cols] = pltpu.bitcast(words << 16, F32).astype(BF16)
            wu_scr[:, cols] = pltpu.bitcast(words & jnp.uint32(0xFFFF0000), F32).astype(BF16)
        wd_scr[...] = wd_buf[slot].astype(BF16)
        bg, bu, bd = bg_ref[e], bu_ref[e], bd_ref[e]

        def block(j, carry):
            s = j % 2
            x_copy(first, s).wait()

            @pl.when(j + 1 < count)
            def _():
                x_copy(first + j + 1, 1 - s).start()

            @pl.when(j >= 2)
            def _():
                y_copy(first, s).wait()

            x = jnp.concatenate(_load_tile_rows(xbuf.at[s], TILE_ROW), axis=1).astype(BF16)
            gate = lax.dot_general(x, wg_scr[...], _NT, preferred_element_type=F32) + bg
            up = lax.dot_general(x, wu_scr[...], _NT, preferred_element_type=F32) + bu
            gate = jnp.minimum(gate, SWIGLU_LIMIT)
            up = jnp.clip(up, -SWIGLU_LIMIT, SWIGLU_LIMIT)
            glu = gate * jax.nn.sigmoid(gate * SWIGLU_ALPHA)
            mid = ((up + 1.0) * glu).astype(BF16)
            _store_tile_rows(ybuf.at[s], 0, _dot(mid, wd_scr[...]) + bd)
            y_copy(first + j, s).start()
            return carry

        lax.fori_loop(0, count, block, 0)

        @pl.when(count >= 2)
        def _():
            y_copy(first, count % 2).wait()
        y_copy(first, (count - 1) % 2).wait()

    @pl.when(e == pl.num_programs(0) - 1)
    def _():
        ybuf[0] = jnp.zeros(ybuf.shape[1:], ybuf.dtype)

        def tail_start(b, carry):
            y_copy(b, 0).start()
            return carry

        def tail_wait(b, carry):
            y_copy(b, 0).wait()
            return carry

        lax.fori_loop(total_blocks, nblk, tail_start, 0)
        lax.fori_loop(total_blocks, nblk, tail_wait, 0)


def _moe_ffn(first_blk, n_blk, slot, next_e, misc, xs, wgu, wd, bg, bu, bd):
    p_rows = xs.shape[0] // TILE_ROW
    n_exp, d, f2 = wgu.shape
    f = f2 // 2
    assert d == TILE_ROW * LANES
    whole = lambda shape: pl.BlockSpec(shape, lambda e, *_: (0,) * len(shape))
    tile = (MOE_ROWS * TILE_ROW, LANES)
    kern = functools.partial(_ffn_kernel, nblk=p_rows // MOE_ROWS)
    return pl.pallas_call(
        kern,
        out_shape=jax.ShapeDtypeStruct((p_rows * TILE_ROW, LANES), F32),
        grid_spec=pltpu.PrefetchScalarGridSpec(
            num_scalar_prefetch=5, grid=(n_exp,),
            in_specs=[pl.BlockSpec(memory_space=pl.ANY),
                      pl.BlockSpec(memory_space=pl.ANY), pl.BlockSpec(memory_space=pl.ANY),
                      whole(bg.shape), whole(bu.shape), whole(bd.shape)],
            out_specs=pl.BlockSpec(memory_space=pl.ANY),
            scratch_shapes=[pltpu.VMEM((2, d, f2), F32), pltpu.VMEM((2, f, d), F32),
                            pltpu.VMEM((f, d), BF16), pltpu.VMEM((f, d), BF16),
                            pltpu.VMEM((f, d), BF16),
                            pltpu.VMEM((2,) + tile, F32), pltpu.VMEM((2,) + tile, F32),
                            pltpu.SemaphoreType.DMA((2, 2)), pltpu.SemaphoreType.DMA((2,)),
                            pltpu.SemaphoreType.DMA((2,))]),
        compiler_params=pltpu.CompilerParams(
            dimension_semantics=("arbitrary",), vmem_limit_bytes=VMEM_LIMIT),
        name="moe_ffn",
    )(first_blk, n_blk, slot, next_e, misc, xs, wgu, wd, bg, bu, bd)


def _combine_kernel(dest_ref, ys_hbm, x1_ref, route_ref, g2_ref, o_ref, buf, sem, *, tc, n_steps):
    i = pl.program_id(0)
    slot = i % 2

    def gather(step, s):
        def issue(t, carry):
            for k in range(TOP_K):
                row = dest_ref[(step * tc + t) * TOP_K + k]
                _row_copy(ys_hbm, buf.at[s, k], row, t, sem.at[s]).start(priority=k % 2)
            return carry

        lax.fori_loop(0, tc, issue, 0, unroll=4)

    @pl.when(i == 0)
    def _():
        gather(0, 0)

    @pl.when(i + 1 < n_steps)
    def _():
        gather(i + 1, 1 - slot)

    for k in range(TOP_K):
        pltpu.make_async_copy(_tile_rows(ys_hbm, 0, tc), buf.at[slot, k], sem.at[slot]).wait()

    route = route_ref[...]
    for c in range(TILE_ROW):
        cols = slice(c * LANES, (c + 1) * LANES)
        y = buf[slot, 0, pl.ds(c, tc, stride=TILE_ROW), :] * route[:, 0:1]
        for k in range(1, TOP_K):
            y = y + buf[slot, k, pl.ds(c, tc, stride=TILE_ROW), :] * route[:, k:k + 1]
        o_ref[:, cols] = x1_ref[:, cols] + g2_ref[:, cols] * y


def _moe_combine(dest_flat, ys, x1, route, g2, seq):
    n, d = x1.shape
    assert d == TILE_ROW * LANES
    tc = min(seq, 256)
    tpb = seq // tc
    kern = functools.partial(_combine_kernel, tc=tc, n_steps=n // tc)
    return pl.pallas_call(
        kern,
        out_shape=jax.ShapeDtypeStruct((n, d), F32),
        grid_spec=pltpu.PrefetchScalarGridSpec(
            num_scalar_prefetch=1, grid=(n // tc,),
            in_specs=[pl.BlockSpec(memory_space=pl.ANY),
                      pl.BlockSpec((tc, d), lambda i, dr: (i, 0)),
                      pl.BlockSpec((tc, LANES), lambda i, dr: (i, 0)),
                      pl.BlockSpec((None, 1, d), lambda i, dr: (i // tpb, 0, 0))],
            out_specs=pl.BlockSpec((tc, d), lambda i, dr: (i, 0)),
            scratch_shapes=[pltpu.VMEM((2, TOP_K, tc * TILE_ROW, LANES), F32),
                            pltpu.SemaphoreType.DMA((2,))]),
        compiler_params=pltpu.CompilerParams(
            dimension_semantics=("arbitrary",), vmem_limit_bytes=VMEM_LIMIT),
        name="moe_combine",
    )(dest_flat, ys, x1, route, g2)


def _rope_tables(seq):
    inv_freq = (ROPE_THETA ** (-np.arange(0, HEAD_DIM, 2, dtype=np.float32) / HEAD_DIM)).astype(np.float32)
    ang = (inv_freq[:, None] * np.arange(seq, dtype=np.float32)[None, :]).astype(np.float32)
    cos, sin = np.cos(ang.astype(np.float64)), np.sin(ang.astype(np.float64))
    cos_t = np.concatenate([cos, cos], axis=0).astype(np.float32)
    sin_t = np.concatenate([-sin, sin], axis=0).astype(np.float32)
    return jnp.asarray(cos_t), jnp.asarray(sin_t)


def _layer(x, c, l, lambda_init, w_ada, b_ada, norm1_w, w_in, q_norm_w, k_norm_w, lambda_q1,
           lambda_k1, lambda_q2, lambda_k2, subln_w, w_attn_o, conv_w, w_conv_o, w_out, norm2_w,
           w_router, b_router, w_gate_up, b_gate_up, w_down, b_down):
    bsz, seq, d = x.shape
    n = bsz * seq
    x2 = x.reshape(n, d)

    mod = _ada_mod(c, w_ada[l], b_ada[l])
    sh1, sc1, g1, sh2, sc2, g2 = [m.reshape(bsz, 1, d) for m in jnp.split(mod, 6, axis=-1)]

    cos_t, sin_t = _rope_tables(seq)
    qt, k, vt, cb, z, sga, sgc, qn, kn = _in_proj(
        x2, norm1_w[l].reshape(1, d), sc1, sh1, w_in[l].astype(BF16),
        q_norm_w[l].reshape(HEAD_DIM, 1) * Q_SCALE, k_norm_w[l].reshape(HEAD_DIM, 1),
        cos_t, sin_t, bsz, seq)

    lam_vecs = jnp.stack([lambda_q1[l], lambda_k1[l], lambda_q2[l], lambda_k2[l]]).astype(F32)
    o = _diff_attn(qt, k.reshape(bsz, seq, d), vt, qn, kn, lam_vecs, subln_w[l].reshape(V_DIM, 1),
                   lambda_init)

    wr = jnp.zeros((d, LANES), F32).at[:, :N_EXPERTS].set(w_router[l].astype(F32))
    wr_hi = wr.astype(BF16)
    wr_lo = (wr - wr_hi.astype(F32)).astype(BF16)
    br_pad = jnp.full((1, LANES), NEG_BIG, F32).at[0, :N_EXPERTS].set(b_router[l].astype(F32))
    x1, h2, route, cnt = _post_mix(
        o.reshape(n, d), cb, z, sga, sgc, x2, g1, sc2, sh2, norm2_w[l].reshape(1, d), conv_w[l],
        w_attn_o[l].astype(BF16), w_conv_o[l].astype(BF16), w_out[l].astype(BF16),
        wr_hi, wr_lo, br_pad, seq)

    counts = cnt[0, :N_EXPERTS].astype(jnp.int32)
    padded = ((counts + MOE_ROWS - 1) // MOE_ROWS) * MOE_ROWS
    ends = jnp.cumsum(padded)
    start = ends - padded
    p_rows = n * TOP_K + N_EXPERTS * MOE_ROWS
    top_e = route[:, TOP_K:2 * TOP_K].astype(jnp.int32)
    rank = route[:, 2 * TOP_K:3 * TOP_K].astype(jnp.int32)
    experts = jnp.arange(N_EXPERTS, dtype=jnp.int32)
    seg_start = jnp.sum(jnp.where(top_e[..., None] == experts, start, 0), axis=-1)
    dest = (seg_start + rank).reshape(-1)
    nonempty = padded > 0
    later = jnp.logical_and(nonempty[None, :], experts[None, :] > experts[:, None])
    next_e = jnp.min(jnp.where(later, experts[None, :], N_EXPERTS), axis=1)
    next_e = jnp.where(next_e == N_EXPERTS, -1, next_e).astype(jnp.int32)
    slot = ((jnp.cumsum(nonempty.astype(jnp.int32)) - 1) % 2).astype(jnp.int32)
    first_blk = (start // MOE_ROWS).astype(jnp.int32)
    n_blk = (padded // MOE_ROWS).astype(jnp.int32)
    misc = jnp.stack([jnp.argmax(nonempty).astype(jnp.int32), (ends[-1] // MOE_ROWS).astype(jnp.int32)])

    xs = _moe_dispatch(dest, ends.astype(jnp.int32), h2, n, p_rows)
    ys = _moe_ffn(first_blk, n_blk, slot, next_e, misc, xs, w_gate_up[l], w_down[l],
                  b_gate_up[l][:, None, 0::2], b_gate_up[l][:, None, 1::2], b_down[l][:, None, :])
    out = _moe_combine(dest, ys, x1, route, g2, seq)
    return out.reshape(bsz, seq, d)


def kernel(x, c, w_ada, b_ada, norm1_w, w_in, q_norm_w, k_norm_w, lambda_q1, lambda_k1, lambda_q2,
           lambda_k2, subln_w, w_attn_o, conv_w, w_conv_o, w_out, norm2_w, w_router, b_router,
           w_gate_up, b_gate_up, w_down, b_down):
    depth = w_ada.shape[0]
    for l in range(depth):
        lambda_init = 0.8 - 0.6 * math.exp(-0.3 * l)
        x = _layer(x, c, l, lambda_init, w_ada, b_ada, norm1_w, w_in, q_norm_w, k_norm_w,
                   lambda_q1, lambda_k1, lambda_q2, lambda_k2, subln_w, w_attn_o, conv_w,
                   w_conv_o, w_out, norm2_w, w_router, b_router, w_gate_up, b_gate_up,
                   w_down, b_down)
    return x
```

```python
import functools
import math

import jax
import jax.numpy as jnp
import numpy as np
from jax import lax
from jax.experimental import pallas as pl
from jax.experimental.pallas import tpu as pltpu

N_HEADS = 8
HEAD_DIM = 64
V_DIM = 2 * HEAD_DIM
IN_BLOCKS = ("q", "k", "v", "conv_b", "conv_c", "conv_x", "gate_attn", "gate_conv")
N_EXPERTS = 32
TOP_K = 4
SWIGLU_LIMIT = 7.0
SWIGLU_ALPHA = 1.702
ROPE_THETA = 10000.0
RMS_EPS = 1e-6
SUBLN_EPS = 1e-5
LANES = 128
TILE_ROW = 8
MOE_ROWS = 256
NEG_BIG = -1e30
LOG2E = 1.4426950408889634
Q_SCALE = LOG2E / math.sqrt(HEAD_DIM)
SAFE_EXP2_BOUND = 80.0
VMEM_LIMIT = 56 * 1024 * 1024

F32 = jnp.float32
BF16 = jnp.bfloat16


def _dot(a, b):
    return jnp.dot(a, b, preferred_element_type=F32)


def _store_tile_rows(ref, row0, val):
    rows, d = val.shape
    chunks = d // LANES
    for c in range(chunks):
        ref[pl.ds(row0 * chunks + c, rows, stride=chunks), :] = val[:, c * LANES:(c + 1) * LANES]


def _load_tile_rows(ref, chunks):
    rows = ref.shape[0] // chunks
    return [ref[pl.ds(c, rows, stride=chunks), :] for c in range(chunks)]


def _ada_kernel(ct_ref, w_ref, b_ref, o_ref):
    ct = ct_ref[...]
    s = ct * jax.nn.sigmoid(ct)
    w = w_ref[...]
    for b in range(ct.shape[1]):
        o_ref[b:b + 1, :] = jnp.sum(w * s[:, b:b + 1], axis=0, keepdims=True) + b_ref[...]


def _ada_mod(c, w_ada, b_ada):
    bsz, d = c.shape
    n = w_ada.shape[1]
    tn = min(n, 1536)
    return pl.pallas_call(
        _ada_kernel,
        out_shape=jax.ShapeDtypeStruct((bsz, n), F32),
        grid=(n // tn,),
        in_specs=[pl.BlockSpec((d, bsz), lambda j: (0, 0)),
                  pl.BlockSpec((d, tn), lambda j: (0, j)),
                  pl.BlockSpec((1, tn), lambda j: (0, j))],
        out_specs=pl.BlockSpec((bsz, tn), lambda j: (0, j)),
        compiler_params=pltpu.CompilerParams(dimension_semantics=("arbitrary",)),
        name="ada_mod",
    )(c.T, w_ada, b_ada.reshape(1, n))


def _qk_norm_rope_t(y, g_col, cos_t, sin_t):
    tm, w = y.shape
    yt = y.T.reshape(w // HEAD_DIM, HEAD_DIM, tm)
    ms = jnp.mean(yt * yt, axis=1, keepdims=True)
    yn = yt * lax.rsqrt(ms + RMS_EPS) * g_col[None]
    half = HEAD_DIM // 2
    swapped = jnp.concatenate([yn[:, half:, :], yn[:, :half, :]], axis=1)
    out = yn * cos_t[None] + swapped * sin_t[None]
    norm2 = jnp.sum(out * out, axis=1).reshape(N_HEADS, 2, tm)
    return out.reshape(w, tm), norm2


def _in_proj_kernel(x_ref, n1_ref, sc_ref, sh_ref, w_ref, gq_ref, gk_ref, cos_ref, sin_ref,
                    qt_ref, k_ref, vt_ref, cb_ref, z_ref, sga_ref, sgc_ref, qn_ref, kn_ref, h_scr):
    x = x_ref[...]
    xn = x * lax.rsqrt(jnp.mean(x * x, axis=-1, keepdims=True) + RMS_EPS) * n1_ref[...]
    h_scr[...] = (xn * (1.0 + sc_ref[...]) + sh_ref[...]).astype(BF16)
    wcol = w_ref.shape[1] // len(IN_BLOCKS)

    def proj(name):
        c = IN_BLOCKS.index(name)
        return _dot(h_scr[...], w_ref[:, c * wcol:(c + 1) * wcol])

    qt, qn = _qk_norm_rope_t(proj("q"), gq_ref[...], cos_ref[...], sin_ref[...])
    qt_ref[...] = qt.astype(BF16)
    qn_ref[...] = qn
    kt, kn = _qk_norm_rope_t(proj("k"), gk_ref[...], cos_ref[...], sin_ref[...])
    k_ref[...] = kt.T.astype(BF16)
    kn_ref[...] = kn
    vt_ref[...] = proj("v").T.astype(BF16)
    cb_ref[...] = proj("conv_b").astype(BF16)
    z_ref[...] = (proj("conv_c") * proj("conv_x")).astype(BF16)
    sga_ref[...] = jax.nn.sigmoid(proj("gate_attn")).astype(BF16)
    sgc_ref[...] = jax.nn.sigmoid(proj("gate_conv")).astype(BF16)


def _in_proj(x2, n1, sc1, sh1, w_in_bf, gq, gk, cos_t, sin_t, bsz, seq):
    n, d = x2.shape
    tm = min(seq, 512)
    tpb = seq // tm
    assert w_in_bf.shape[1] == len(IN_BLOCKS) * d
    row = lambda i: (i, 0)
    const = lambda i: (0, 0)
    tcol = lambda i: (i // tpb, 0, i % tpb)
    mod = lambda i: (i // tpb, 0, 0)
    nat = jax.ShapeDtypeStruct((n, d), BF16)
    tr = jax.ShapeDtypeStruct((bsz, d, seq), BF16)
    nrm = jax.ShapeDtypeStruct((bsz, N_HEADS, 2, seq), F32)
    nat_spec = pl.BlockSpec((tm, d), row)
    tr_spec = pl.BlockSpec((None, d, tm), tcol)
    nrm_spec = pl.BlockSpec((None, N_HEADS, 2, tm), lambda i: (i // tpb, 0, 0, i % tpb))
    return pl.pallas_call(
        _in_proj_kernel,
        out_shape=(tr, nat, tr, nat, nat, nat, nat, nrm, nrm),
        grid=(n // tm,),
        in_specs=[pl.BlockSpec((tm, d), row),
                  pl.BlockSpec((1, d), const),
                  pl.BlockSpec((None, 1, d), mod),
                  pl.BlockSpec((None, 1, d), mod),
                  pl.BlockSpec(w_in_bf.shape, const),
                  pl.BlockSpec((HEAD_DIM, 1), const),
                  pl.BlockSpec((HEAD_DIM, 1), const),
                  pl.BlockSpec((HEAD_DIM, tm), lambda i: (0, i % tpb)),
                  pl.BlockSpec((HEAD_DIM, tm), lambda i: (0, i % tpb))],
        out_specs=(tr_spec, nat_spec, tr_spec, nat_spec, nat_spec, nat_spec, nat_spec,
                   nrm_spec, nrm_spec),
        scratch_shapes=[pltpu.VMEM((tm, d), BF16)],
        compiler_params=pltpu.CompilerParams(
            dimension_semantics=("arbitrary",), vmem_limit_bytes=VMEM_LIMIT),
        name="in_proj",
    )(x2, n1, sc1, sh1, w_in_bf, gq, gk, cos_t, sin_t)


def _diff_attn_kernel(qt_ref, k_ref, vt_ref, qn_ref, kn_ref, lam_ref, sw_ref, o_ref, acc0, acc1,
                      *, tk, lambda_init):
    seq = k_ref.shape[0]
    tq = qt_ref.shape[1]
    qt = qt_ref[...]
    first = lax.broadcasted_iota(jnp.int32, qt.shape, 0) < HEAD_DIM
    zero = jnp.zeros_like(qt)
    qz = (jnp.where(first, qt, zero), jnp.where(first, zero, qt))
    accs = (acc0, acc1)
    acc0[...] = jnp.zeros_like(acc0)
    acc1[...] = jnp.zeros_like(acc1)
    n_chunks = seq // tk

    def load(j):
        off = pl.multiple_of(j * tk, tk)
        return k_ref[pl.ds(off, tk), :], vt_ref[:, pl.ds(off, tk)]

    def plain_body(j, carry):
        kk, vt = load(j)
        new = []
        for c in range(2):
            p = jnp.exp2(_dot(kk, qz[c]))
            new.append(carry[c] + jnp.sum(p, axis=0, keepdims=True))
            accs[c][...] += _dot(vt, p.astype(BF16))
        return tuple(new)

    def online_body(j, carry):
        kk, vt = load(j)
        new = []
        for c in range(2):
            m, l = carry[2 * c], carry[2 * c + 1]
            s = _dot(kk, qz[c])
            m_new = jnp.maximum(m, jnp.max(s, axis=0, keepdims=True))
            alpha = jnp.exp2(m - m_new)
            p = jnp.exp2(s - m_new)
            l = alpha * l + jnp.sum(p, axis=0, keepdims=True)
            accs[c][...] = alpha * accs[c][...] + _dot(vt, p.astype(BF16))
            new += [m_new, l]
        return tuple(new)

    m_init = jnp.full((1, tq), NEG_BIG, F32)
    l_init = jnp.zeros((1, tq), F32)

    def plain():
        return lax.fori_loop(0, n_chunks, plain_body, (l_init, l_init))

    def online():
        _, l0, _, l1 = lax.fori_loop(0, n_chunks, online_body, (m_init, l_init, m_init, l_init))
        return l0, l1

    bound2 = jnp.max(jnp.max(qn_ref[...], axis=-1, keepdims=True)
                     * jnp.max(kn_ref[...], axis=-1, keepdims=True))
    l0, l1 = lax.cond(bound2 <= SAFE_EXP2_BOUND * SAFE_EXP2_BOUND, plain, online)

    lq = lam_ref[...]
    lam = (jnp.exp(jnp.sum(lq[0:1] * lq[1:2], axis=-1, keepdims=True))
           - jnp.exp(jnp.sum(lq[2:3] * lq[3:4], axis=-1, keepdims=True)) + lambda_init)
    o = acc0[...] / l0 - lam * (acc1[...] / l1)
    o = o * lax.rsqrt(jnp.mean(o * o, axis=0, keepdims=True) + SUBLN_EPS)
    o = o * sw_ref[...] * (1.0 - lambda_init)
    o_ref[...] = o.T.astype(BF16)


def _diff_attn(qt, k3, vt, qn, kn, lam_vecs, subln_col, lambda_init):
    bsz, d, seq = qt.shape
    tq = min(seq, 1024)
    tk = min(seq, 4096)
    kern = functools.partial(_diff_attn_kernel, tk=tk, lambda_init=lambda_init)
    return pl.pallas_call(
        kern,
        out_shape=jax.ShapeDtypeStruct((bsz, seq, d), BF16),
        grid=(bsz, N_HEADS, seq // tq),
        in_specs=[pl.BlockSpec((None, V_DIM, tq), lambda b, h, i: (b, h, i)),
                  pl.BlockSpec((None, seq, V_DIM), lambda b, h, i: (b, 0, h)),
                  pl.BlockSpec((None, V_DIM, seq), lambda b, h, i: (b, h, 0)),
                  pl.BlockSpec((None, None, 2, tq), lambda b, h, i: (b, h, 0, i)),
                  pl.BlockSpec((None, None, 2, seq), lambda b, h, i: (b, h, 0, 0)),
                  pl.BlockSpec((4, HEAD_DIM), lambda b, h, i: (0, 0)),
                  pl.BlockSpec((V_DIM, 1), lambda b, h, i: (0, 0))],
        out_specs=pl.BlockSpec((None, tq, V_DIM), lambda b, h, i: (b, i, h)),
        scratch_shapes=[pltpu.VMEM((V_DIM, tq), F32), pltpu.VMEM((V_DIM, tq), F32)],
        compiler_params=pltpu.CompilerParams(
            dimension_semantics=("arbitrary", "arbitrary", "arbitrary"),
            vmem_limit_bytes=VMEM_LIMIT),
        name="diff_attn",
    )(qt, k3, vt, qn, kn, lam_vecs, subln_col)


def _post_mix_kernel(o_ref, cb_ref, z_ref, zp_ref, zn_ref, sga_ref, sgc_ref, x_ref,
                     g1_ref, sc2_ref, sh2_ref, n2_ref, cw_ref, wao_ref, wco_ref, wout_ref,
                     wrh_ref, wrl_ref, br_ref,
                     x1_ref, h2_ref, route_ref, cnt_ref, u_scr, *, tpb, n_sub):
    i = pl.program_id(0)
    tm = x_ref.shape[0]

    @pl.when(i == 0)
    def _():
        cnt_ref[...] = jnp.zeros_like(cnt_ref)

    z = z_ref[...].astype(F32)
    rows = lax.broadcasted_iota(jnp.int32, z.shape, 0)
    halo_rows = zp_ref.shape[0]
    prev_row = zp_ref[halo_rows - 1:halo_rows, :].astype(F32)
    next_row = zn_ref[0:1, :].astype(F32)
    prev_row = jnp.where(i % tpb == 0, jnp.zeros_like(prev_row), prev_row)
    next_row = jnp.where(i % tpb == tpb - 1, jnp.zeros_like(next_row), next_row)
    z_m1 = jnp.where(rows == 0, prev_row, pltpu.roll(z, 1, 0))
    z_p1 = jnp.where(rows == tm - 1, next_row, pltpu.roll(z, tm - 1, 0))
    cw = cw_ref[...]
    conv = z_m1 * cw[0:1] + z * cw[1:2] + z_p1 * cw[2:3]
    u_scr[...] = (cb_ref[...].astype(F32) * conv).astype(BF16)

    ts = tm // n_sub
    lane = lax.broadcasted_iota(jnp.int32, (ts, LANES), 1)
    r_i = lax.broadcasted_iota(jnp.int32, (ts, ts), 0)
    c_i = lax.broadcasted_iota(jnp.int32, (ts, ts), 1)
    lower = (r_i > c_i).astype(BF16)
    counts = cnt_ref[0:1, :]
    for s in range(n_sub):
        rs = slice(s * ts, (s + 1) * ts)
        y_attn = _dot(o_ref[rs, :], wao_ref[...])
        y_conv = _dot(u_scr[rs, :], wco_ref[...])
        m = sga_ref[rs, :].astype(F32) * y_attn + sgc_ref[rs, :].astype(F32) * y_conv
        x1 = x_ref[rs, :] + g1_ref[...] * _dot(m.astype(BF16), wout_ref[...])
        x1_ref[rs, :] = x1

        h2 = x1 * lax.rsqrt(jnp.mean(x1 * x1, axis=-1, keepdims=True) + RMS_EPS) * n2_ref[...]
        h2 = h2 * (1.0 + sc2_ref[...]) + sh2_ref[...]
        _store_tile_rows(h2_ref, s * ts, h2)

        h_hi = h2.astype(BF16)
        h_lo = (h2 - h_hi.astype(F32)).astype(BF16)
        logits = (_dot(h_hi, wrh_ref[...]) + _dot(h_lo, wrh_ref[...]) + _dot(h_hi, wrl_ref[...])
                  + br_ref[...])

        work = logits
        vals, idxs = [], []
        for _ in range(TOP_K):
            mx = jnp.max(work, axis=-1, keepdims=True)
            ix = jnp.min(jnp.where(work == mx, lane, LANES), axis=-1, keepdims=True)
            vals.append(mx)
            idxs.append(ix)
            work = jnp.where(lane == ix, 2.0 * NEG_BIG, work)
        exps = [jnp.exp(v - vals[0]) for v in vals]
        den = exps[0] + exps[1] + exps[2] + exps[3]

        sel = (work == 2.0 * NEG_BIG).astype(BF16)
        before = _dot(lower, sel) + counts
        counts = counts + jnp.sum(sel.astype(F32), axis=0, keepdims=True)

        route = jnp.zeros(logits.shape, F32)
        for k in range(TOP_K):
            rank = jnp.sum(jnp.where(lane == idxs[k], before, 0.0), axis=-1, keepdims=True)
            route = jnp.where(lane == k, exps[k] / den, route)
            route = jnp.where(lane == TOP_K + k, idxs[k].astype(F32), route)
            route = jnp.where(lane == 2 * TOP_K + k, rank, route)
        route_ref[rs, :] = route
    cnt_ref[...] = jnp.broadcast_to(counts, cnt_ref.shape)


def _post_mix(o2, cb, z, sga, sgc, x2, g1, sc2, sh2, n2, conv_w, wao, wco, wout,
              wr_hi, wr_lo, br_pad, seq):
    n, d = x2.shape
    tm = min(seq, 512)
    tpb = seq // tm
    halo = 16
    hb = tm // halo
    last_hb = n // halo - 1
    row = lambda i: (i, 0)
    const = lambda i: (0, 0)
    mod = lambda i: (i // tpb, 0, 0)
    wspec = pl.BlockSpec((d, d), const)
    kern = functools.partial(_post_mix_kernel, tpb=tpb, n_sub=1)
    return pl.pallas_call(
        kern,
        out_shape=(jax.ShapeDtypeStruct((n, d), F32),
                   jax.ShapeDtypeStruct((n * (d // LANES), LANES), F32),
                   jax.ShapeDtypeStruct((n, LANES), F32), jax.ShapeDtypeStruct((8, LANES), F32)),
        grid=(n // tm,),
        in_specs=[pl.BlockSpec((tm, d), row), pl.BlockSpec((tm, d), row), pl.BlockSpec((tm, d), row),
                  pl.BlockSpec((halo, d), lambda i: (jnp.maximum(i * hb - 1, 0), 0)),
                  pl.BlockSpec((halo, d), lambda i: (jnp.minimum((i + 1) * hb, last_hb), 0)),
                  pl.BlockSpec((tm, d), row), pl.BlockSpec((tm, d), row), pl.BlockSpec((tm, d), row),
                  pl.BlockSpec((None, 1, d), mod), pl.BlockSpec((None, 1, d), mod),
                  pl.BlockSpec((None, 1, d), mod),
                  pl.BlockSpec((1, d), const), pl.BlockSpec((3, d), const),
                  wspec, wspec, wspec,
                  pl.BlockSpec((d, LANES), const), pl.BlockSpec((d, LANES), const),
                  pl.BlockSpec((1, LANES), const)],
        out_specs=(pl.BlockSpec((tm, d), row), pl.BlockSpec((tm * (d // LANES), LANES), row),
                   pl.BlockSpec((tm, LANES), row), pl.BlockSpec((8, LANES), const)),
        scratch_shapes=[pltpu.VMEM((tm, d), BF16)],
        compiler_params=pltpu.CompilerParams(
            dimension_semantics=("arbitrary",), vmem_limit_bytes=VMEM_LIMIT),
        name="post_mix",
    )(o2, cb, z, z, z, sga, sgc, x2, g1, sc2, sh2, n2, conv_w, wao, wco, wout, wr_hi, wr_lo, br_pad)


def _tile_rows(ref, row, n_rows=1):
    start = pl.multiple_of(row * TILE_ROW, TILE_ROW)
    return ref.at[pl.ds(start, n_rows * TILE_ROW)]


def _row_copy(src, dst, s, t, sem):
    return pltpu.make_async_copy(_tile_rows(src, s), _tile_rows(dst, t), sem)


def _dispatch_kernel(dest_ref, ends_ref, h2_ref, xs_hbm, zbuf, sem, zsem, *, tc, nblk):
    base = pl.program_id(0) * tc

    @pl.when(pl.program_id(0) == 0)
    def _():
        zbuf[...] = jnp.zeros_like(zbuf)

        def zero_block(row):
            return pltpu.make_async_copy(zbuf, _tile_rows(xs_hbm, row, MOE_ROWS), zsem)

        def nonempty(e):
            return ends_ref[e] > (ends_ref[e - 1] if e else 0)

        total = ends_ref[N_EXPERTS - 1]
        n_tail = nblk - total // MOE_ROWS

        def tail_start(b, carry):
            zero_block(total + b * MOE_ROWS).start()
            return carry

        def tail_wait(b, carry):
            zero_block(0).wait()
            return carry

        for e in range(N_EXPERTS):
            @pl.when(nonempty(e))
            def _(e=e):
                zero_block(ends_ref[e] - MOE_ROWS).start()
        lax.fori_loop(0, n_tail, tail_start, 0)
        for e in range(N_EXPERTS):
            @pl.when(nonempty(e))
            def _():
                zero_block(0).wait()
        lax.fori_loop(0, n_tail, tail_wait, 0)

    def issue(t, carry):
        for k in range(TOP_K):
            _row_copy(h2_ref, xs_hbm, t, dest_ref[(base + t) * TOP_K + k], sem).start(priority=k % 2)
        return carry

    lax.fori_loop(0, tc, issue, 0, unroll=4)

    for _ in range(TOP_K):
        pltpu.make_async_copy(h2_ref, _tile_rows(xs_hbm, 0, tc), sem).wait()


def _moe_dispatch(dest_flat, ends, h2_tiles, n, p_rows):
    tc = min(n, 512)
    kern = functools.partial(_dispatch_kernel, tc=tc, nblk=p_rows // MOE_ROWS)
    return pl.pallas_call(
        kern,
        out_shape=jax.ShapeDtypeStruct((p_rows * TILE_ROW, LANES), F32),
        grid_spec=pltpu.PrefetchScalarGridSpec(
            num_scalar_prefetch=2, grid=(n // tc,),
            in_specs=[pl.BlockSpec((tc * TILE_ROW, LANES), lambda i, dr, en: (i, 0))],
            out_specs=pl.BlockSpec(memory_space=pl.ANY),
            scratch_shapes=[pltpu.VMEM((MOE_ROWS * TILE_ROW, LANES), F32),
                            pltpu.SemaphoreType.DMA, pltpu.SemaphoreType.DMA]),
        compiler_params=pltpu.CompilerParams(dimension_semantics=("arbitrary",)),
        name="moe_dispatch",
    )(dest_flat, ends, h2_tiles)


_NT = (((1,), (1,)), ((), ()))


def _ffn_kernel(first_ref, count_ref, slot_ref, next_ref, misc_ref,
                xs_hbm, wgu_hbm, wd_hbm, bg_ref, bu_ref, bd_ref, ys_hbm,
                wgu_buf, wd_buf, wg_scr, wu_scr, wd_scr, xbuf, ybuf,
                wsem, xsem, ysem, *, nblk):
    e = pl.program_id(0)
    first = first_ref[e]
    count = count_ref[e]
    slot = slot_ref[e]
    first_expert, total_blocks = misc_ref[0], misc_ref[1]

    def weight_copies(ex, s):
        return (pltpu.make_async_copy(wgu_hbm.at[ex], wgu_buf.at[s], wsem.at[0, s]),
                pltpu.make_async_copy(wd_hbm.at[ex], wd_buf.at[s], wsem.at[1, s]))

    def x_copy(b, s):
        return pltpu.make_async_copy(_tile_rows(xs_hbm, b * MOE_ROWS, MOE_ROWS), xbuf.at[s], xsem.at[s])

    def y_copy(b, s):
        return pltpu.make_async_copy(ybuf.at[s], _tile_rows(ys_hbm, b * MOE_ROWS, MOE_ROWS), ysem.at[s])

    @pl.when(e == first_expert)
    def _():
        for cp in weight_copies(e, 0):
            cp.start()

    @pl.when(count > 0)
    def _():
        x_copy(first, 0).start()
        for cp in weight_copies(e, slot):
            cp.wait()

        @pl.when(next_ref[e] >= 0)
        def _():
            for cp in weight_copies(next_ref[e], 1 - slot):
                cp.start(priority=1)

        d = wgu_buf.shape[1]
        for c in range(d // LANES):
            cols = slice(c * LANES, (c + 1) * LANES)
            words = pltpu.bitcast(wgu_buf[slot, cols, :].astype(BF16).T, jnp.uint32)
            wg_scr[:, cols] = pltpu.bitcast(words << 16, F32).astype(BF16)
            wu_scr[:, cols] = pltpu.bitcast(words & jnp.uint32(0xFFFF0000), F32).astype(BF16)
        wd_scr[...] = wd_buf[slot].astype(BF16)
        bg, bu, bd = bg_ref[e], bu_ref[e], bd_ref[e]

        def block(j, carry):
            s = j % 2
            x_copy(first, s).wait()

            @pl.when(j + 1 < count)
            def _():
                x_copy(first + j + 1, 1 - s).start()

            @pl.when(j >= 2)
            def _():
                y_copy(first, s).wait()

            x = jnp.concatenate(_load_tile_rows(xbuf.at[s], TILE_ROW), axis=1).astype(BF16)
            gate = lax.dot_general(x, wg_scr[...], _NT, preferred_element_type=F32) + bg
            up = lax.dot_general(x, wu_scr[...], _NT, preferred_element_type=F32) + bu
            gate = jnp.minimum(gate, SWIGLU_LIMIT)
            up = jnp.clip(up, -SWIGLU_LIMIT, SWIGLU_LIMIT)
            glu = gate * jax.nn.sigmoid(gate * SWIGLU_ALPHA)
            mid = ((up + 1.0) * glu).astype(BF16)
            _store_tile_rows(ybuf.at[s], 0, _dot(mid, wd_scr[...]) + bd)
            y_copy(first + j, s).start()
            return carry

        lax.fori_loop(0, count, block, 0)

        @pl.when(count >= 2)
        def _():
            y_copy(first, count % 2).wait()
        y_copy(first, (count - 1) % 2).wait()

    @pl.when(e == pl.num_programs(0) - 1)
    def _():
        ybuf[0] = jnp.zeros(ybuf.shape[1:], ybuf.dtype)

        def tail_start(b, carry):
            y_copy(b, 0).start()
            return carry

        def tail_wait(b, carry):
            y_copy(b, 0).wait()
            return carry

        lax.fori_loop(total_blocks, nblk, tail_start, 0)
        lax.fori_loop(total_blocks, nblk, tail_wait, 0)


def _moe_ffn(first_blk, n_blk, slot, next_e, misc, xs, wgu, wd, bg, bu, bd):
    p_rows = xs.shape[0] // TILE_ROW
    n_exp, d, f2 = wgu.shape
    f = f2 // 2
    assert d == TILE_ROW * LANES
    whole = lambda shape: pl.BlockSpec(shape, lambda e, *_: (0,) * len(shape))
    tile = (MOE_ROWS * TILE_ROW, LANES)
    kern = functools.partial(_ffn_kernel, nblk=p_rows // MOE_ROWS)
    return pl.pallas_call(
        kern,
        out_shape=jax.ShapeDtypeStruct((p_rows * TILE_ROW, LANES), F32),
        grid_spec=pltpu.PrefetchScalarGridSpec(
            num_scalar_prefetch=5, grid=(n_exp,),
            in_specs=[pl.BlockSpec(memory_space=pl.ANY),
                      pl.BlockSpec(memory_space=pl.ANY), pl.BlockSpec(memory_space=pl.ANY),
                      whole(bg.shape), whole(bu.shape), whole(bd.shape)],
            out_specs=pl.BlockSpec(memory_space=pl.ANY),
            scratch_shapes=[pltpu.VMEM((2, d, f2), F32), pltpu.VMEM((2, f, d), F32),
                            pltpu.VMEM((f, d), BF16), pltpu.VMEM((f, d), BF16),
                            pltpu.VMEM((f, d), BF16),
                            pltpu.VMEM((2,) + tile, F32), pltpu.VMEM((2,) + tile, F32),
                            pltpu.SemaphoreType.DMA((2, 2)), pltpu.SemaphoreType.DMA((2,)),
                            pltpu.SemaphoreType.DMA((2,))]),
        compiler_params=pltpu.CompilerParams(
            dimension_semantics=("arbitrary",), vmem_limit_bytes=VMEM_LIMIT),
        name="moe_ffn",
    )(first_blk, n_blk, slot, next_e, misc, xs, wgu, wd, bg, bu, bd)


def _combine_kernel(dest_ref, ys_hbm, x1_ref, route_ref, g2_ref, o_ref, buf, sem, *, tc, n_steps):
    i = pl.program_id(0)
    slot = i % 2

    def gather(step, s):
        def issue(t, carry):
            for k in range(TOP_K):
                row = dest_ref[(step * tc + t) * TOP_K + k]
                _row_copy(ys_hbm, buf.at[s, k], row, t, sem.at[s]).start(priority=k % 2)
            return carry

        lax.fori_loop(0, tc, issue, 0, unroll=4)

    @pl.when(i == 0)
    def _():
        gather(0, 0)

    @pl.when(i + 1 < n_steps)
    def _():
        gather(i + 1, 1 - slot)

    for k in range(TOP_K):
        pltpu.make_async_copy(_tile_rows(ys_hbm, 0, tc), buf.at[slot, k], sem.at[slot]).wait()

    route = route_ref[...]
    for c in range(TILE_ROW):
        cols = slice(c * LANES, (c + 1) * LANES)
        y = buf[slot, 0, pl.ds(c, tc, stride=TILE_ROW), :] * route[:, 0:1]
        for k in range(1, TOP_K):
            y = y + buf[slot, k, pl.ds(c, tc, stride=TILE_ROW), :] * route[:, k:k + 1]
        o_ref[:, cols] = x1_ref[:, cols] + g2_ref[:, cols] * y


def _moe_combine(dest_flat, ys, x1, route, g2, seq):
    n, d = x1.shape
    assert d == TILE_ROW * LANES
    tc = min(seq, 256)
    tpb = seq // tc
    kern = functools.partial(_combine_kernel, tc=tc, n_steps=n // tc)
    return pl.pallas_call(
        kern,
        out_shape=jax.ShapeDtypeStruct((n, d), F32),
        grid_spec=pltpu.PrefetchScalarGridSpec(
            num_scalar_prefetch=1, grid=(n // tc,),
            in_specs=[pl.BlockSpec(memory_space=pl.ANY),
                      pl.BlockSpec((tc, d), lambda i, dr: (i, 0)),
                      pl.BlockSpec((tc, LANES), lambda i, dr: (i, 0)),
                      pl.BlockSpec((None, 1, d), lambda i, dr: (i // tpb, 0, 0))],
            out_specs=pl.BlockSpec((tc, d), lambda i, dr: (i, 0)),
            scratch_shapes=[pltpu.VMEM((2, TOP_K, tc * TILE_ROW, LANES), F32),
                            pltpu.SemaphoreType.DMA((2,))]),
        compiler_params=pltpu.CompilerParams(
            dimension_semantics=("arbitrary",), vmem_limit_bytes=VMEM_LIMIT),
        name="moe_combine",
    )(dest_flat, ys, x1, route, g2)


def _rope_tables(seq):
    inv_freq = (ROPE_THETA ** (-np.arange(0, HEAD_DIM, 2, dtype=np.float32) / HEAD_DIM)).astype(np.float32)
    ang = (inv_freq[:, None] * np.arange(seq, dtype=np.float32)[None, :]).astype(np.float32)
    cos, sin = np.cos(ang.astype(np.float64)), np.sin(ang.astype(np.float64))
    cos_t = np.concatenate([cos, cos], axis=0).astype(np.float32)
    sin_t = np.concatenate([-sin, sin], axis=0).astype(np.float32)
    return jnp.asarray(cos_t), jnp.asarray(sin_t)


def _layer(x, c, l, lambda_init, w_ada, b_ada, norm1_w, w_in, q_norm_w, k_norm_w, lambda_q1,
           lambda_k1, lambda_q2, lambda_k2, subln_w, w_attn_o, conv_w, w_conv_o, w_out, norm2_w,
           w_router, b_router, w_gate_up, b_gate_up, w_down, b_down):
    bsz, seq, d = x.shape
    n = bsz * seq
    x2 = x.reshape(n, d)

    mod = _ada_mod(c, w_ada[l], b_ada[l])
    sh1, sc1, g1, sh2, sc2, g2 = [m.reshape(bsz, 1, d) for m in jnp.split(mod, 6, axis=-1)]

    cos_t, sin_t = _rope_tables(seq)
    qt, k, vt, cb, z, sga, sgc, qn, kn = _in_proj(
        x2, norm1_w[l].reshape(1, d), sc1, sh1, w_in[l].astype(BF16),
        q_norm_w[l].reshape(HEAD_DIM, 1) * Q_SCALE, k_norm_w[l].reshape(HEAD_DIM, 1),
        cos_t, sin_t, bsz, seq)

    lam_vecs = jnp.stack([lambda_q1[l], lambda_k1[l], lambda_q2[l], lambda_k2[l]]).astype(F32)
    o = _diff_attn(qt, k.reshape(bsz, seq, d), vt, qn, kn, lam_vecs, subln_w[l].reshape(V_DIM, 1),
                   lambda_init)

    wr = jnp.zeros((d, LANES), F32).at[:, :N_EXPERTS].set(w_router[l].astype(F32))
    wr_hi = wr.astype(BF16)
    wr_lo = (wr - wr_hi.astype(F32)).astype(BF16)
    br_pad = jnp.full((1, LANES), NEG_BIG, F32).at[0, :N_EXPERTS].set(b_router[l].astype(F32))
    x1, h2, route, cnt = _post_mix(
        o.reshape(n, d), cb, z, sga, sgc, x2, g1, sc2, sh2, norm2_w[l].reshape(1, d), conv_w[l],
        w_attn_o[l].astype(BF16), w_conv_o[l].astype(BF16), w_out[l].astype(BF16),
        wr_hi, wr_lo, br_pad, seq)

    counts = cnt[0, :N_EXPERTS].astype(jnp.int32)
    padded = ((counts + MOE_ROWS - 1) // MOE_ROWS) * MOE_ROWS
    ends = jnp.cumsum(padded)
    start = ends - padded
    p_rows = n * TOP_K + N_EXPERTS * MOE_ROWS
    top_e = route[:, TOP_K:2 * TOP_K].astype(jnp.int32)
    rank = route[:, 2 * TOP_K:3 * TOP_K].astype(jnp.int32)
    experts = jnp.arange(N_EXPERTS, dtype=jnp.int32)
    seg_start = jnp.sum(jnp.where(top_e[..., None] == experts, start, 0), axis=-1)
    dest = (seg_start + rank).reshape(-1)
    nonempty = padded > 0
    later = jnp.logical_and(nonempty[None, :], experts[None, :] > experts[:, None])
    next_e = jnp.min(jnp.where(later, experts[None, :], N_EXPERTS), axis=1)
    next_e = jnp.where(next_e == N_EXPERTS, -1, next_e).astype(jnp.int32)
    slot = ((jnp.cumsum(nonempty.astype(jnp.int32)) - 1) % 2).astype(jnp.int32)
    first_blk = (start // MOE_ROWS).astype(jnp.int32)
    n_blk = (padded // MOE_ROWS).astype(jnp.int32)
    misc = jnp.stack([jnp.argmax(nonempty).astype(jnp.int32), (ends[-1] // MOE_ROWS).astype(jnp.int32)])

    xs = _moe_dispatch(dest, ends.astype(jnp.int32), h2, n, p_rows)
    ys = _moe_ffn(first_blk, n_blk, slot, next_e, misc, xs, w_gate_up[l], w_down[l],
                  b_gate_up[l][:, None, 0::2], b_gate_up[l][:, None, 1::2], b_down[l][:, None, :])
    out = _moe_combine(dest, ys, x1, route, g2, seq)
    return out.reshape(bsz, seq, d)


def kernel(x, c, w_ada, b_ada, norm1_w, w_in, q_norm_w, k_norm_w, lambda_q1, lambda_k1, lambda_q2,
           lambda_k2, subln_w, w_attn_o, conv_w, w_conv_o, w_out, norm2_w, w_router, b_router,
           w_gate_up, b_gate_up, w_down, b_down):
    depth = w_ada.shape[0]
    for l in range(depth):
        lambda_init = 0.8 - 0.6 * math.exp(-0.3 * l)
        x = _layer(x, c, l, lambda_init, w_ada, b_ada, norm1_w, w_in, q_norm_w, k_norm_w,
                   lambda_q1, lambda_k1, lambda_q2, lambda_k2, subln_w, w_attn_o, conv_w,
                   w_conv_o, w_out, norm2_w, w_router, b_router, w_gate_up, b_gate_up,
                   w_down, b_down)
    return x
```

```python
import functools
import math

import jax
import jax.numpy as jnp
import numpy as np
from jax import lax
from jax.experimental import pallas as pl
from jax.experimental.pallas import tpu as pltpu

N_HEADS = 8
HEAD_DIM = 64
V_DIM = 2 * HEAD_DIM
IN_BLOCKS = ("q", "k", "v", "conv_b", "conv_c", "conv_x", "gate_attn", "gate_conv")
N_EXPERTS = 32
TOP_K = 4
SWIGLU_LIMIT = 7.0
SWIGLU_ALPHA = 1.702
ROPE_THETA = 10000.0
RMS_EPS = 1e-6
SUBLN_EPS = 1e-5
LANES = 128
TILE_ROW = 4
MOE_ROWS = 256
NEG_BIG = -1e30
LOG2E = 1.4426950408889634
Q_SCALE = LOG2E / math.sqrt(HEAD_DIM)
SAFE_EXP2_BOUND = 80.0
VMEM_LIMIT = 56 * 1024 * 1024

F32 = jnp.float32
BF16 = jnp.bfloat16


def _dot(a, b):
    return jnp.dot(a, b, preferred_element_type=F32)


_HIGH_HALF = 0xFFFF0000


def _bf16_bits(x):
    return pltpu.bitcast(x.astype(BF16).astype(F32), jnp.uint32)


def _store_packed_rows(ref, row0, val):
    rows, d = val.shape
    half = d // 2
    for c in range(TILE_ROW):
        lo = _bf16_bits(val[:, c * LANES:(c + 1) * LANES]) >> 16
        hi = _bf16_bits(val[:, half + c * LANES:half + (c + 1) * LANES])
        ref[pl.ds(row0 * TILE_ROW + c, rows, stride=TILE_ROW), :] = lo | hi


def _load_packed_rows(ref, rows):
    words = [ref[pl.ds(c, rows, stride=TILE_ROW), :] for c in range(TILE_ROW)]
    lo = [pltpu.bitcast(w << 16, F32) for w in words]
    hi = [pltpu.bitcast(w & jnp.uint32(_HIGH_HALF), F32) for w in words]
    return lo + hi


def _ada_kernel(ct_ref, w_ref, b_ref, o_ref):
    ct = ct_ref[...]
    s = ct * jax.nn.sigmoid(ct)
    w = w_ref[...]
    for b in range(ct.shape[1]):
        o_ref[b:b + 1, :] = jnp.sum(w * s[:, b:b + 1], axis=0, keepdims=True) + b_ref[...]


def _ada_mod(c, w_ada, b_ada):
    bsz, d = c.shape
    n = w_ada.shape[1]
    tn = min(n, 1536)
    return pl.pallas_call(
        _ada_kernel,
        out_shape=jax.ShapeDtypeStruct((bsz, n), F32),
        grid=(n // tn,),
        in_specs=[pl.BlockSpec((d, bsz), lambda j: (0, 0)),
                  pl.BlockSpec((d, tn), lambda j: (0, j)),
                  pl.BlockSpec((1, tn), lambda j: (0, j))],
        out_specs=pl.BlockSpec((bsz, tn), lambda j: (0, j)),
        compiler_params=pltpu.CompilerParams(dimension_semantics=("arbitrary",)),
        name="ada_mod",
    )(c.T, w_ada, b_ada.reshape(1, n))


def _qk_norm_rope_t(y, g_col, cos_t, sin_t):
    tm, w = y.shape
    yt = y.T.reshape(w // HEAD_DIM, HEAD_DIM, tm)
    ms = jnp.mean(yt * yt, axis=1, keepdims=True)
    yn = yt * lax.rsqrt(ms + RMS_EPS) * g_col[None]
    half = HEAD_DIM // 2
    swapped = jnp.concatenate([yn[:, half:, :], yn[:, :half, :]], axis=1)
    out = yn * cos_t[None] + swapped * sin_t[None]
    norm2 = jnp.sum(out * out, axis=1).reshape(N_HEADS, 2, tm)
    return out.reshape(w, tm), norm2


def _in_proj_kernel(x_ref, n1_ref, sc_ref, sh_ref, w_ref, gq_ref, gk_ref, cos_ref, sin_ref,
                    qt_ref, k_ref, vt_ref, cb_ref, z_ref, sga_ref, sgc_ref, qn_ref, kn_ref, h_scr):
    x = x_ref[...]
    xn = x * lax.rsqrt(jnp.mean(x * x, axis=-1, keepdims=True) + RMS_EPS) * n1_ref[...]
    h_scr[...] = (xn * (1.0 + sc_ref[...]) + sh_ref[...]).astype(BF16)
    wcol = w_ref.shape[1] // len(IN_BLOCKS)

    def proj(name):
        c = IN_BLOCKS.index(name)
        return _dot(h_scr[...], w_ref[:, c * wcol:(c + 1) * wcol])

    qt, qn = _qk_norm_rope_t(proj("q"), gq_ref[...], cos_ref[...], sin_ref[...])
    qt_ref[...] = qt.astype(BF16)
    qn_ref[...] = qn
    kt, kn = _qk_norm_rope_t(proj("k"), gk_ref[...], cos_ref[...], sin_ref[...])
    k_ref[...] = kt.T.astype(BF16)
    kn_ref[...] = kn
    vt_ref[...] = proj("v").T.astype(BF16)
    cb_ref[...] = proj("conv_b").astype(BF16)
    z_ref[...] = (proj("conv_c") * proj("conv_x")).astype(BF16)
    sga_ref[...] = jax.nn.sigmoid(proj("gate_attn")).astype(BF16)
    sgc_ref[...] = jax.nn.sigmoid(proj("gate_conv")).astype(BF16)


def _in_proj(x2, n1, sc1, sh1, w_in_bf, gq, gk, cos_t, sin_t, bsz, seq):
    n, d = x2.shape
    tm = min(seq, 512)
    tpb = seq // tm
    assert w_in_bf.shape[1] == len(IN_BLOCKS) * d
    row = lambda i: (i, 0)
    const = lambda i: (0, 0)
    tcol = lambda i: (i // tpb, 0, i % tpb)
    mod = lambda i: (i // tpb, 0, 0)
    nat = jax.ShapeDtypeStruct((n, d), BF16)
    tr = jax.ShapeDtypeStruct((bsz, d, seq), BF16)
    nrm = jax.ShapeDtypeStruct((bsz, N_HEADS, 2, seq), F32)
    nat_spec = pl.BlockSpec((tm, d), row)
    tr_spec = pl.BlockSpec((None, d, tm), tcol)
    nrm_spec = pl.BlockSpec((None, N_HEADS, 2, tm), lambda i: (i // tpb, 0, 0, i % tpb))
    return pl.pallas_call(
        _in_proj_kernel,
        out_shape=(tr, nat, tr, nat, nat, nat, nat, nrm, nrm),
        grid=(n // tm,),
        in_specs=[pl.BlockSpec((tm, d), row),
                  pl.BlockSpec((1, d), const),
                  pl.BlockSpec((None, 1, d), mod),
                  pl.BlockSpec((None, 1, d), mod),
                  pl.BlockSpec(w_in_bf.shape, const),
                  pl.BlockSpec((HEAD_DIM, 1), const),
                  pl.BlockSpec((HEAD_DIM, 1), const),
                  pl.BlockSpec((HEAD_DIM, tm), lambda i: (0, i % tpb)),
                  pl.BlockSpec((HEAD_DIM, tm), lambda i: (0, i % tpb))],
        out_specs=(tr_spec, nat_spec, tr_spec, nat_spec, nat_spec, nat_spec, nat_spec,
                   nrm_spec, nrm_spec),
        scratch_shapes=[pltpu.VMEM((tm, d), BF16)],
        compiler_params=pltpu.CompilerParams(
            dimension_semantics=("arbitrary",), vmem_limit_bytes=VMEM_LIMIT),
        name="in_proj",
    )(x2, n1, sc1, sh1, w_in_bf, gq, gk, cos_t, sin_t)


def _diff_attn_kernel(qt_ref, k_ref, vt_ref, qn_ref, kn_ref, lam_ref, sw_ref, o_ref, acc0, acc1,
                      *, tk, lambda_init):
    seq = k_ref.shape[0]
    tq = qt_ref.shape[1]
    qt = qt_ref[...]
    first = lax.broadcasted_iota(jnp.int32, qt.shape, 0) < HEAD_DIM
    zero = jnp.zeros_like(qt)
    qz = (jnp.where(first, qt, zero), jnp.where(first, zero, qt))
    accs = (acc0, acc1)
    acc0[...] = jnp.zeros_like(acc0)
    acc1[...] = jnp.zeros_like(acc1)
    n_chunks = seq // tk

    def load(j):
        off = pl.multiple_of(j * tk, tk)
        return k_ref[pl.ds(off, tk), :], vt_ref[:, pl.ds(off, tk)]

    def plain_body(j, carry):
        kk, vt = load(j)
        new = []
        for c in range(2):
            p = jnp.exp2(_dot(kk, qz[c]))
            new.append(carry[c] + jnp.sum(p, axis=0, keepdims=True))
            accs[c][...] += _dot(vt, p.astype(BF16))
        return tuple(new)

    def online_body(j, carry):
        kk, vt = load(j)
        new = []
        for c in range(2):
            m, l = carry[2 * c], carry[2 * c + 1]
            s = _dot(kk, qz[c])
            m_new = jnp.maximum(m, jnp.max(s, axis=0, keepdims=True))
            alpha = jnp.exp2(m - m_new)
            p = jnp.exp2(s - m_new)
            l = alpha * l + jnp.sum(p, axis=0, keepdims=True)
            accs[c][...] = alpha * accs[c][...] + _dot(vt, p.astype(BF16))
            new += [m_new, l]
        return tuple(new)

    m_init = jnp.full((1, tq), NEG_BIG, F32)
    l_init = jnp.zeros((1, tq), F32)

    def plain():
        return lax.fori_loop(0, n_chunks, plain_body, (l_init, l_init))

    def online():
        _, l0, _, l1 = lax.fori_loop(0, n_chunks, online_body, (m_init, l_init, m_init, l_init))
        return l0, l1

    bound2 = jnp.max(jnp.max(qn_ref[...], axis=-1, keepdims=True)
                     * jnp.max(kn_ref[...], axis=-1, keepdims=True))
    l0, l1 = lax.cond(bound2 <= SAFE_EXP2_BOUND * SAFE_EXP2_BOUND, plain, online)

    lq = lam_ref[...]
    lam = (jnp.exp(jnp.sum(lq[0:1] * lq[1:2], axis=-1, keepdims=True))
           - jnp.exp(jnp.sum(lq[2:3] * lq[3:4], axis=-1, keepdims=True)) + lambda_init)
    o = acc0[...] / l0 - lam * (acc1[...] / l1)
    o = o * lax.rsqrt(jnp.mean(o * o, axis=0, keepdims=True) + SUBLN_EPS)
    o = o * sw_ref[...] * (1.0 - lambda_init)
    o_ref[...] = o.T.astype(BF16)


def _diff_attn(qt, k3, vt, qn, kn, lam_vecs, subln_col, lambda_init):
    bsz, d, seq = qt.shape
    tq = min(seq, 1024)
    tk = min(seq, 4096)
    kern = functools.partial(_diff_attn_kernel, tk=tk, lambda_init=lambda_init)
    return pl.pallas_call(
        kern,
        out_shape=jax.ShapeDtypeStruct((bsz, seq, d), BF16),
        grid=(bsz, N_HEADS, seq // tq),
        in_specs=[pl.BlockSpec((None, V_DIM, tq), lambda b, h, i: (b, h, i)),
                  pl.BlockSpec((None, seq, V_DIM), lambda b, h, i: (b, 0, h)),
                  pl.BlockSpec((None, V_DIM, seq), lambda b, h, i: (b, h, 0)),
                  pl.BlockSpec((None, None, 2, tq), lambda b, h, i: (b, h, 0, i)),
                  pl.BlockSpec((None, None, 2, seq), lambda b, h, i: (b, h, 0, 0)),
                  pl.BlockSpec((4, HEAD_DIM), lambda b, h, i: (0, 0)),
                  pl.BlockSpec((V_DIM, 1), lambda b, h, i: (0, 0))],
        out_specs=pl.BlockSpec((None, tq, V_DIM), lambda b, h, i: (b, i, h)),
        scratch_shapes=[pltpu.VMEM((V_DIM, tq), F32), pltpu.VMEM((V_DIM, tq), F32)],
        compiler_params=pltpu.CompilerParams(
            dimension_semantics=("arbitrary", "arbitrary", "arbitrary"),
            vmem_limit_bytes=VMEM_LIMIT),
        name="diff_attn",
    )(qt, k3, vt, qn, kn, lam_vecs, subln_col)


def _post_mix_kernel(o_ref, cb_ref, z_ref, zp_ref, zn_ref, sga_ref, sgc_ref, x_ref,
                     g1_ref, sc2_ref, sh2_ref, n2_ref, cw_ref, wao_ref, wco_ref, wout_ref,
                     wrh_ref, wrl_ref, br_ref,
                     x1_ref, h2_ref, route_ref, cnt_ref, u_scr, *, tpb, n_sub):
    i = pl.program_id(0)
    tm = x_ref.shape[0]

    @pl.when(i == 0)
    def _():
        cnt_ref[...] = jnp.zeros_like(cnt_ref)

    z = z_ref[...].astype(F32)
    rows = lax.broadcasted_iota(jnp.int32, z.shape, 0)
    halo_rows = zp_ref.shape[0]
    prev_row = zp_ref[halo_rows - 1:halo_rows, :].astype(F32)
    next_row = zn_ref[0:1, :].astype(F32)
    prev_row = jnp.where(i % tpb == 0, jnp.zeros_like(prev_row), prev_row)
    next_row = jnp.where(i % tpb == tpb - 1, jnp.zeros_like(next_row), next_row)
    z_m1 = jnp.where(rows == 0, prev_row, pltpu.roll(z, 1, 0))
    z_p1 = jnp.where(rows == tm - 1, next_row, pltpu.roll(z, tm - 1, 0))
    cw = cw_ref[...]
    conv = z_m1 * cw[0:1] + z * cw[1:2] + z_p1 * cw[2:3]
    u_scr[...] = (cb_ref[...].astype(F32) * conv).astype(BF16)

    ts = tm // n_sub
    lane = lax.broadcasted_iota(jnp.int32, (ts, LANES), 1)
    r_i = lax.broadcasted_iota(jnp.int32, (ts, ts), 0)
    c_i = lax.broadcasted_iota(jnp.int32, (ts, ts), 1)
    lower = (r_i > c_i).astype(BF16)
    counts = cnt_ref[0:1, :]
    for s in range(n_sub):
        rs = slice(s * ts, (s + 1) * ts)
        y_attn = _dot(o_ref[rs, :], wao_ref[...])
        y_conv = _dot(u_scr[rs, :], wco_ref[...])
        m = sga_ref[rs, :].astype(F32) * y_attn + sgc_ref[rs, :].astype(F32) * y_conv
        x1 = x_ref[rs, :] + g1_ref[...] * _dot(m.astype(BF16), wout_ref[...])
        x1_ref[rs, :] = x1

        h2 = x1 * lax.rsqrt(jnp.mean(x1 * x1, axis=-1, keepdims=True) + RMS_EPS) * n2_ref[...]
        h2 = h2 * (1.0 + sc2_ref[...]) + sh2_ref[...]
        _store_packed_rows(h2_ref, s * ts, h2)

        h_hi = h2.astype(BF16)
        h_lo = (h2 - h_hi.astype(F32)).astype(BF16)
        logits = (_dot(h_hi, wrh_ref[...]) + _dot(h_lo, wrh_ref[...]) + _dot(h_hi, wrl_ref[...])
                  + br_ref[...])

        work = logits
        vals, idxs = [], []
        for _ in range(TOP_K):
            mx = jnp.max(work, axis=-1, keepdims=True)
            ix = jnp.min(jnp.where(work == mx, lane, LANES), axis=-1, keepdims=True)
            vals.append(mx)
            idxs.append(ix)
            work = jnp.where(lane == ix, 2.0 * NEG_BIG, work)
        exps = [jnp.exp(v - vals[0]) for v in vals]
        den = exps[0] + exps[1] + exps[2] + exps[3]

        sel = (work == 2.0 * NEG_BIG).astype(BF16)
        before = _dot(lower, sel) + counts
        counts = counts + jnp.sum(sel.astype(F32), axis=0, keepdims=True)

        route = jnp.zeros(logits.shape, F32)
        for k in range(TOP_K):
            rank = jnp.sum(jnp.where(lane == idxs[k], before, 0.0), axis=-1, keepdims=True)
            route = jnp.where(lane == k, exps[k] / den, route)
            route = jnp.where(lane == TOP_K + k, idxs[k].astype(F32), route)
            route = jnp.where(lane == 2 * TOP_K + k, rank, route)
        route_ref[rs, :] = route
    cnt_ref[...] = jnp.broadcast_to(counts, cnt_ref.shape)


def _post_mix(o2, cb, z, sga, sgc, x2, g1, sc2, sh2, n2, conv_w, wao, wco, wout,
              wr_hi, wr_lo, br_pad, seq):
    n, d = x2.shape
    tm = min(seq, 512)
    tpb = seq // tm
    halo = 16
    hb = tm // halo
    last_hb = n // halo - 1
    row = lambda i: (i, 0)
    const = lambda i: (0, 0)
    mod = lambda i: (i // tpb, 0, 0)
    wspec = pl.BlockSpec((d, d), const)
    kern = functools.partial(_post_mix_kernel, tpb=tpb, n_sub=1)
    return pl.pallas_call(
        kern,
        out_shape=(jax.ShapeDtypeStruct((n, d), F32),
                   jax.ShapeDtypeStruct((n * TILE_ROW, LANES), jnp.uint32),
                   jax.ShapeDtypeStruct((n, LANES), F32), jax.ShapeDtypeStruct((8, LANES), F32)),
        grid=(n // tm,),
        in_specs=[pl.BlockSpec((tm, d), row), pl.BlockSpec((tm, d), row), pl.BlockSpec((tm, d), row),
                  pl.BlockSpec((halo, d), lambda i: (jnp.maximum(i * hb - 1, 0), 0)),
                  pl.BlockSpec((halo, d), lambda i: (jnp.minimum((i + 1) * hb, last_hb), 0)),
                  pl.BlockSpec((tm, d), row), pl.BlockSpec((tm, d), row), pl.BlockSpec((tm, d), row),
                  pl.BlockSpec((None, 1, d), mod), pl.BlockSpec((None, 1, d), mod),
                  pl.BlockSpec((None, 1, d), mod),
                  pl.BlockSpec((1, d), const), pl.BlockSpec((3, d), const),
                  wspec, wspec, wspec,
                  pl.BlockSpec((d, LANES), const), pl.BlockSpec((d, LANES), const),
                  pl.BlockSpec((1, LANES), const)],
        out_specs=(pl.BlockSpec((tm, d), row), pl.BlockSpec((tm * TILE_ROW, LANES), row),
                   pl.BlockSpec((tm, LANES), row), pl.BlockSpec((8, LANES), const)),
        scratch_shapes=[pltpu.VMEM((tm, d), BF16)],
        compiler_params=pltpu.CompilerParams(
            dimension_semantics=("arbitrary",), vmem_limit_bytes=VMEM_LIMIT),
        name="post_mix",
    )(o2, cb, z, z, z, sga, sgc, x2, g1, sc2, sh2, n2, conv_w, wao, wco, wout, wr_hi, wr_lo, br_pad)


def _tile_rows(ref, row, n_rows=1):
    start = pl.multiple_of(row * TILE_ROW, TILE_ROW)
    return ref.at[pl.ds(start, n_rows * TILE_ROW)]


def _row_copy(src, dst, s, t, sem):
    return pltpu.make_async_copy(_tile_rows(src, s), _tile_rows(dst, t), sem)


def _dispatch_kernel(dest_ref, ends_ref, h2_ref, xs_hbm, zbuf, sem, zsem, *, tc, nblk):
    base = pl.program_id(0) * tc

    @pl.when(pl.program_id(0) == 0)
    def _():
        zbuf[...] = jnp.zeros_like(zbuf)

        def zero_block(row):
            return pltpu.make_async_copy(zbuf, _tile_rows(xs_hbm, row, MOE_ROWS), zsem)

        def nonempty(e):
            return ends_ref[e] > (ends_ref[e - 1] if e else 0)

        total = ends_ref[N_EXPERTS - 1]
        n_tail = nblk - total // MOE_ROWS

        def tail_start(b, carry):
            zero_block(total + b * MOE_ROWS).start()
            return carry

        def tail_wait(b, carry):
            zero_block(0).wait()
            return carry

        for e in range(N_EXPERTS):
            @pl.when(nonempty(e))
            def _(e=e):
                zero_block(ends_ref[e] - MOE_ROWS).start()
        lax.fori_loop(0, n_tail, tail_start, 0)
        for e in range(N_EXPERTS):
            @pl.when(nonempty(e))
            def _():
                zero_block(0).wait()
        lax.fori_loop(0, n_tail, tail_wait, 0)

    def issue(t, carry):
        for k in range(TOP_K):
            _row_copy(h2_ref, xs_hbm, t, dest_ref[(base + t) * TOP_K + k], sem).start(priority=k % 2)
        return carry

    lax.fori_loop(0, tc, issue, 0, unroll=4)

    for _ in range(TOP_K):
        pltpu.make_async_copy(h2_ref, _tile_rows(xs_hbm, 0, tc), sem).wait()


def _moe_dispatch(dest_flat, ends, h2_tiles, n, p_rows):
    tc = min(n, 512)
    kern = functools.partial(_dispatch_kernel, tc=tc, nblk=p_rows // MOE_ROWS)
    return pl.pallas_call(
        kern,
        out_shape=jax.ShapeDtypeStruct((p_rows * TILE_ROW, LANES), jnp.uint32),
        grid_spec=pltpu.PrefetchScalarGridSpec(
            num_scalar_prefetch=2, grid=(n // tc,),
            in_specs=[pl.BlockSpec((tc * TILE_ROW, LANES), lambda i, dr, en: (i, 0))],
            out_specs=pl.BlockSpec(memory_space=pl.ANY),
            scratch_shapes=[pltpu.VMEM((MOE_ROWS * TILE_ROW, LANES), jnp.uint32),
                            pltpu.SemaphoreType.DMA, pltpu.SemaphoreType.DMA]),
        compiler_params=pltpu.CompilerParams(dimension_semantics=("arbitrary",)),
        name="moe_dispatch",
    )(dest_flat, ends, h2_tiles)


def _ffn_kernel(first_ref, count_ref, slot_ref, next_ref, misc_ref,
                xs_hbm, wgu_hbm, wd_hbm, bg_ref, bu_ref, bd_ref, ys_hbm,
                wgu_buf, wd_buf, wg_scr, wu_scr, wd_scr, xbuf, ybuf,
                wsem, xsem, ysem, *, nblk):
    e = pl.program_id(0)
    first = first_ref[e]
    count = count_ref[e]
    slot = slot_ref[e]
    first_expert, total_blocks = misc_ref[0], misc_ref[1]

    def weight_copies(ex, s):
        return (pltpu.make_async_copy(wgu_hbm.at[ex], wgu_buf.at[s], wsem.at[0, s]),
                pltpu.make_async_copy(wd_hbm.at[ex], wd_buf.at[s], wsem.at[1, s]))

    def x_copy(b, s):
        return pltpu.make_async_copy(_tile_rows(xs_hbm, b * MOE_ROWS, MOE_ROWS), xbuf.at[s], xsem.at[s])

    def y_copy(b, s):
        return pltpu.make_async_copy(ybuf.at[s], _tile_rows(ys_hbm, b * MOE_ROWS, MOE_ROWS), ysem.at[s])

    @pl.when(e == first_expert)
    def _():
        for cp in weight_copies(e, 0):
            cp.start()

    @pl.when(count > 0)
    def _():
        x_copy(first, 0).start()
        for cp in weight_copies(e, slot):
            cp.wait()

        @pl.when(next_ref[e] >= 0)
        def _():
            for cp in weight_copies(next_ref[e], 1 - slot):
                cp.start(priority=1)

        d = wgu_buf.shape[1]
        for c in range(d // LANES):
            cols = slice(c * LANES, (c + 1) * LANES)
            words = pltpu.bitcast(wgu_buf[slot, cols, :].astype(BF16).T, jnp.uint32)
            wg_scr[cols, :] = pltpu.bitcast(words << 16, F32).astype(BF16).T
            wu_scr[cols, :] = pltpu.bitcast(words & jnp.uint32(_HIGH_HALF), F32).astype(BF16).T
        wd_scr[...] = wd_buf[slot].astype(BF16)
        bg, bu, bd = bg_ref[e], bu_ref[e], bd_ref[e]

        def block(j, carry):
            s = j % 2
            x_copy(first, s).wait()

            @pl.when(j + 1 < count)
            def _():
                x_copy(first + j + 1, 1 - s).start()

            @pl.when(j >= 2)
            def _():
                y_copy(first, s).wait()

            x = jnp.concatenate(_load_packed_rows(xbuf.at[s], MOE_ROWS), axis=1).astype(BF16)
            gate = _dot(x, wg_scr[...]) + bg
            up = _dot(x, wu_scr[...]) + bu
            gate = jnp.minimum(gate, SWIGLU_LIMIT)
            up = jnp.clip(up, -SWIGLU_LIMIT, SWIGLU_LIMIT)
            glu = gate * jax.nn.sigmoid(gate * SWIGLU_ALPHA)
            mid = ((up + 1.0) * glu).astype(BF16)
            _store_packed_rows(ybuf.at[s], 0, _dot(mid, wd_scr[...]) + bd)
            y_copy(first + j, s).start()
            return carry

        lax.fori_loop(0, count, block, 0)

        @pl.when(count >= 2)
        def _():
            y_copy(first, count % 2).wait()
        y_copy(first, (count - 1) % 2).wait()

    @pl.when(e == pl.num_programs(0) - 1)
    def _():
        ybuf[0] = jnp.zeros(ybuf.shape[1:], ybuf.dtype)

        def tail_start(b, carry):
            y_copy(b, 0).start()
            return carry

        def tail_wait(b, carry):
            y_copy(b, 0).wait()
            return carry

        lax.fori_loop(total_blocks, nblk, tail_start, 0)
        lax.fori_loop(total_blocks, nblk, tail_wait, 0)


def _moe_ffn(first_blk, n_blk, slot, next_e, misc, xs, wgu, wd, bg, bu, bd):
    p_rows = xs.shape[0] // TILE_ROW
    n_exp, d, f2 = wgu.shape
    f = f2 // 2
    assert d == 2 * TILE_ROW * LANES
    whole = lambda shape: pl.BlockSpec(shape, lambda e, *_: (0,) * len(shape))
    tile = (MOE_ROWS * TILE_ROW, LANES)
    kern = functools.partial(_ffn_kernel, nblk=p_rows // MOE_ROWS)
    return pl.pallas_call(
        kern,
        out_shape=jax.ShapeDtypeStruct((p_rows * TILE_ROW, LANES), jnp.uint32),
        grid_spec=pltpu.PrefetchScalarGridSpec(
            num_scalar_prefetch=5, grid=(n_exp,),
            in_specs=[pl.BlockSpec(memory_space=pl.ANY),
                      pl.BlockSpec(memory_space=pl.ANY), pl.BlockSpec(memory_space=pl.ANY),
                      whole(bg.shape), whole(bu.shape), whole(bd.shape)],
            out_specs=pl.BlockSpec(memory_space=pl.ANY),
            scratch_shapes=[pltpu.VMEM((2, d, f2), F32), pltpu.VMEM((2, f, d), F32),
                            pltpu.VMEM((d, f), BF16), pltpu.VMEM((d, f), BF16),
                            pltpu.VMEM((f, d), BF16),
                            pltpu.VMEM((2,) + tile, jnp.uint32), pltpu.VMEM((2,) + tile, jnp.uint32),
                            pltpu.SemaphoreType.DMA((2, 2)), pltpu.SemaphoreType.DMA((2,)),
                            pltpu.SemaphoreType.DMA((2,))]),
        compiler_params=pltpu.CompilerParams(
            dimension_semantics=("arbitrary",), vmem_limit_bytes=VMEM_LIMIT),
        name="moe_ffn",
    )(first_blk, n_blk, slot, next_e, misc, xs, wgu, wd, bg, bu, bd)


def _combine_kernel(dest_ref, ys_hbm, x1_ref, route_ref, g2_ref, o_ref, buf, sem, *, tc, n_steps):
    i = pl.program_id(0)
    slot = i % 2

    def gather(step, s):
        def issue(t, carry):
            for k in range(TOP_K):
                row = dest_ref[(step * tc + t) * TOP_K + k]
                _row_copy(ys_hbm, buf.at[s, k], row, t, sem.at[s]).start(priority=k % 2)
            return carry

        lax.fori_loop(0, tc, issue, 0, unroll=4)

    @pl.when(i == 0)
    def _():
        gather(0, 0)

    @pl.when(i + 1 < n_steps)
    def _():
        gather(i + 1, 1 - slot)

    for k in range(TOP_K):
        pltpu.make_async_copy(_tile_rows(ys_hbm, 0, tc), buf.at[slot, k], sem.at[slot]).wait()

    route = route_ref[...]
    chunks = [_load_packed_rows(buf.at[slot, k], tc) for k in range(TOP_K)]
    for c in range(2 * TILE_ROW):
        cols = slice(c * LANES, (c + 1) * LANES)
        y = chunks[0][c] * route[:, 0:1]
        for k in range(1, TOP_K):
            y = y + chunks[k][c] * route[:, k:k + 1]
        o_ref[:, cols] = x1_ref[:, cols] + g2_ref[:, cols] * y


def _moe_combine(dest_flat, ys, x1, route, g2, seq):
    n, d = x1.shape
    assert d == 2 * TILE_ROW * LANES
    tc = min(seq, 256)
    tpb = seq // tc
    kern = functools.partial(_combine_kernel, tc=tc, n_steps=n // tc)
    return pl.pallas_call(
        kern,
        out_shape=jax.ShapeDtypeStruct((n, d), F32),
        grid_spec=pltpu.PrefetchScalarGridSpec(
            num_scalar_prefetch=1, grid=(n // tc,),
            in_specs=[pl.BlockSpec(memory_space=pl.ANY),
                      pl.BlockSpec((tc, d), lambda i, dr: (i, 0)),
                      pl.BlockSpec((tc, LANES), lambda i, dr: (i, 0)),
                      pl.BlockSpec((None, 1, d), lambda i, dr: (i // tpb, 0, 0))],
            out_specs=pl.BlockSpec((tc, d), lambda i, dr: (i, 0)),
            scratch_shapes=[pltpu.VMEM((2, TOP_K, tc * TILE_ROW, LANES), jnp.uint32),
                            pltpu.SemaphoreType.DMA((2,))]),
        compiler_params=pltpu.CompilerParams(
            dimension_semantics=("arbitrary",), vmem_limit_bytes=VMEM_LIMIT),
        name="moe_combine",
    )(dest_flat, ys, x1, route, g2)


def _rope_tables(seq):
    inv_freq = (ROPE_THETA ** (-np.arange(0, HEAD_DIM, 2, dtype=np.float32) / HEAD_DIM)).astype(np.float32)
    ang = (inv_freq[:, None] * np.arange(seq, dtype=np.float32)[None, :]).astype(np.float32)
    cos, sin = np.cos(ang.astype(np.float64)), np.sin(ang.astype(np.float64))
    cos_t = np.concatenate([cos, cos], axis=0).astype(np.float32)
    sin_t = np.concatenate([-sin, sin], axis=0).astype(np.float32)
    return jnp.asarray(cos_t), jnp.asarray(sin_t)


def _layer(x, c, l, lambda_init, w_ada, b_ada, norm1_w, w_in, q_norm_w, k_norm_w, lambda_q1,
           lambda_k1, lambda_q2, lambda_k2, subln_w, w_attn_o, conv_w, w_conv_o, w_out, norm2_w,
           w_router, b_router, w_gate_up, b_gate_up, w_down, b_down):
    bsz, seq, d = x.shape
    n = bsz * seq
    x2 = x.reshape(n, d)

    mod = _ada_mod(c, w_ada[l], b_ada[l])
    sh1, sc1, g1, sh2, sc2, g2 = [m.reshape(bsz, 1, d) for m in jnp.split(mod, 6, axis=-1)]

    cos_t, sin_t = _rope_tables(seq)
    qt, k, vt, cb, z, sga, sgc, qn, kn = _in_proj(
        x2, norm1_w[l].reshape(1, d), sc1, sh1, w_in[l].astype(BF16),
        q_norm_w[l].reshape(HEAD_DIM, 1) * Q_SCALE, k_norm_w[l].reshape(HEAD_DIM, 1),
        cos_t, sin_t, bsz, seq)

    lam_vecs = jnp.stack([lambda_q1[l], lambda_k1[l], lambda_q2[l], lambda_k2[l]]).astype(F32)
    o = _diff_attn(qt, k.reshape(bsz, seq, d), vt, qn, kn, lam_vecs, subln_w[l].reshape(V_DIM, 1),
                   lambda_init)

    wr = jnp.zeros((d, LANES), F32).at[:, :N_EXPERTS].set(w_router[l].astype(F32))
    wr_hi = wr.astype(BF16)
    wr_lo = (wr - wr_hi.astype(F32)).astype(BF16)
    br_pad = jnp.full((1, LANES), NEG_BIG, F32).at[0, :N_EXPERTS].set(b_router[l].astype(F32))
    x1, h2, route, cnt = _post_mix(
        o.reshape(n, d), cb, z, sga, sgc, x2, g1, sc2, sh2, norm2_w[l].reshape(1, d), conv_w[l],
        w_attn_o[l].astype(BF16), w_conv_o[l].astype(BF16), w_out[l].astype(BF16),
        wr_hi, wr_lo, br_pad, seq)

    counts = cnt[0, :N_EXPERTS].astype(jnp.int32)
    padded = ((counts + MOE_ROWS - 1) // MOE_ROWS) * MOE_ROWS
    ends = jnp.cumsum(padded)
    start = ends - padded
    p_rows = n * TOP_K + N_EXPERTS * MOE_ROWS
    top_e = route[:, TOP_K:2 * TOP_K].astype(jnp.int32)
    rank = route[:, 2 * TOP_K:3 * TOP_K].astype(jnp.int32)
    experts = jnp.arange(N_EXPERTS, dtype=jnp.int32)
    seg_start = jnp.sum(jnp.where(top_e[..., None] == experts, start, 0), axis=-1)
    dest = (seg_start + rank).reshape(-1)
    nonempty = padded > 0
    later = jnp.logical_and(nonempty[None, :], experts[None, :] > experts[:, None])
    next_e = jnp.min(jnp.where(later, experts[None, :], N_EXPERTS), axis=1)
    next_e = jnp.where(next_e == N_EXPERTS, -1, next_e).astype(jnp.int32)
    slot = ((jnp.cumsum(nonempty.astype(jnp.int32)) - 1) % 2).astype(jnp.int32)
    first_blk = (start // MOE_ROWS).astype(jnp.int32)
    n_blk = (padded // MOE_ROWS).astype(jnp.int32)
    misc = jnp.stack([jnp.argmax(nonempty).astype(jnp.int32), (ends[-1] // MOE_ROWS).astype(jnp.int32)])

    xs = _moe_dispatch(dest, ends.astype(jnp.int32), h2, n, p_rows)
    ys = _moe_ffn(first_blk, n_blk, slot, next_e, misc, xs, w_gate_up[l], w_down[l],
                  b_gate_up[l][:, None, 0::2], b_gate_up[l][:, None, 1::2], b_down[l][:, None, :])
    out = _moe_combine(dest, ys, x1, route, g2, seq)
    return out.reshape(bsz, seq, d)


def kernel(x, c, w_ada, b_ada, norm1_w, w_in, q_norm_w, k_norm_w, lambda_q1, lambda_k1, lambda_q2,
           lambda_k2, subln_w, w_attn_o, conv_w, w_conv_o, w_out, norm2_w, w_router, b_router,
           w_gate_up, b_gate_up, w_down, b_down):
    depth = w_ada.shape[0]
    for l in range(depth):
        lambda_init = 0.8 - 0.6 * math.exp(-0.3 * l)
        x = _layer(x, c, l, lambda_init, w_ada, b_ada, norm1_w, w_in, q_norm_w, k_norm_w,
                   lambda_q1, lambda_k1, lambda_q2, lambda_k2, subln_w, w_attn_o, conv_w,
                   w_conv_o, w_out, norm2_w, w_router, b_router, w_gate_up, b_gate_up,
                   w_down, b_down)
    return x
```

```python
import functools
import math

import jax
import jax.numpy as jnp
import numpy as np
from jax import lax
from jax.experimental import pallas as pl
from jax.experimental.pallas import tpu as pltpu

N_HEADS = 8
HEAD_DIM = 64
V_DIM = 2 * HEAD_DIM
IN_BLOCKS = ("q", "k", "v", "conv_b", "conv_c", "conv_x", "gate_attn", "gate_conv")
N_EXPERTS = 32
TOP_K = 4
SWIGLU_LIMIT = 7.0
SWIGLU_ALPHA = 1.702
ROPE_THETA = 10000.0
RMS_EPS = 1e-6
SUBLN_EPS = 1e-5
LANES = 128
TILE_ROW = 4
MOE_ROWS = 256
WEIGHT_CHUNKS = 8
NEG_BIG = -1e30
LOG2E = 1.4426950408889634
Q_SCALE = LOG2E / math.sqrt(HEAD_DIM)
SAFE_EXP2_BOUND = 80.0
VMEM_LIMIT = 56 * 1024 * 1024

F32 = jnp.float32
BF16 = jnp.bfloat16


def _dot(a, b):
    return jnp.dot(a, b, preferred_element_type=F32)


_HIGH_HALF = 0xFFFF0000


def _bf16_bits(x):
    return pltpu.bitcast(x.astype(BF16).astype(F32), jnp.uint32)


def _store_packed_rows(ref, row0, val):
    rows, d = val.shape
    half = d // 2
    for c in range(TILE_ROW):
        lo = _bf16_bits(val[:, c * LANES:(c + 1) * LANES]) >> 16
        hi = _bf16_bits(val[:, half + c * LANES:half + (c + 1) * LANES])
        ref[pl.ds(row0 * TILE_ROW + c, rows, stride=TILE_ROW), :] = lo | hi


def _load_packed_rows(ref, rows):
    words = [ref[pl.ds(c, rows, stride=TILE_ROW), :] for c in range(TILE_ROW)]
    lo = [pltpu.bitcast(w << 16, F32) for w in words]
    hi = [pltpu.bitcast(w & jnp.uint32(_HIGH_HALF), F32) for w in words]
    return lo + hi


def _ada_kernel(ct_ref, w_ref, b_ref, o_ref):
    ct = ct_ref[...]
    s = ct * jax.nn.sigmoid(ct)
    w = w_ref[...]
    for b in range(ct.shape[1]):
        o_ref[b:b + 1, :] = jnp.sum(w * s[:, b:b + 1], axis=0, keepdims=True) + b_ref[...]


def _ada_mod(c, w_ada, b_ada):
    bsz, d = c.shape
    n = w_ada.shape[1]
    tn = min(n, 1536)
    return pl.pallas_call(
        _ada_kernel,
        out_shape=jax.ShapeDtypeStruct((bsz, n), F32),
        grid=(n // tn,),
        in_specs=[pl.BlockSpec((d, bsz), lambda j: (0, 0)),
                  pl.BlockSpec((d, tn), lambda j: (0, j)),
                  pl.BlockSpec((1, tn), lambda j: (0, j))],
        out_specs=pl.BlockSpec((bsz, tn), lambda j: (0, j)),
        compiler_params=pltpu.CompilerParams(dimension_semantics=("arbitrary",)),
        name="ada_mod",
    )(c.T, w_ada, b_ada.reshape(1, n))


def _qk_norm_rope_t(y, g_col, cos_t, sin_t):
    tm, w = y.shape
    yt = y.T.reshape(w // HEAD_DIM, HEAD_DIM, tm)
    ms = jnp.mean(yt * yt, axis=1, keepdims=True)
    yn = yt * lax.rsqrt(ms + RMS_EPS) * g_col[None]
    half = HEAD_DIM // 2
    swapped = jnp.concatenate([yn[:, half:, :], yn[:, :half, :]], axis=1)
    out = yn * cos_t[None] + swapped * sin_t[None]
    norm2 = jnp.sum(out * out, axis=1).reshape(N_HEADS, 2, tm)
    return out.reshape(w, tm), norm2


def _in_proj_kernel(x_ref, n1_ref, sc_ref, sh_ref, w_ref, gq_ref, gk_ref, cos_ref, sin_ref,
                    qt_ref, k_ref, vt_ref, cb_ref, z_ref, sga_ref, sgc_ref, qn_ref, kn_ref, h_scr):
    x = x_ref[...]
    xn = x * lax.rsqrt(jnp.mean(x * x, axis=-1, keepdims=True) + RMS_EPS) * n1_ref[...]
    h_scr[...] = (xn * (1.0 + sc_ref[...]) + sh_ref[...]).astype(BF16)
    wcol = w_ref.shape[1] // len(IN_BLOCKS)

    def proj(name):
        c = IN_BLOCKS.index(name)
        return _dot(h_scr[...], w_ref[:, c * wcol:(c + 1) * wcol])

    qt, qn = _qk_norm_rope_t(proj("q"), gq_ref[...], cos_ref[...], sin_ref[...])
    qt_ref[...] = qt.astype(BF16)
    qn_ref[...] = qn
    kt, kn = _qk_norm_rope_t(proj("k"), gk_ref[...], cos_ref[...], sin_ref[...])
    k_ref[...] = kt.T.astype(BF16)
    kn_ref[...] = kn
    vt_ref[...] = proj("v").T.astype(BF16)
    cb_ref[...] = proj("conv_b").astype(BF16)
    z_ref[...] = (proj("conv_c") * proj("conv_x")).astype(BF16)
    sga_ref[...] = jax.nn.sigmoid(proj("gate_attn")).astype(BF16)
    sgc_ref[...] = jax.nn.sigmoid(proj("gate_conv")).astype(BF16)


def _in_proj(x2, n1, sc1, sh1, w_in_bf, gq, gk, cos_t, sin_t, bsz, seq):
    n, d = x2.shape
    tm = min(seq, 512)
    tpb = seq // tm
    assert w_in_bf.shape[1] == len(IN_BLOCKS) * d
    row = lambda i: (i, 0)
    const = lambda i: (0, 0)
    tcol = lambda i: (i // tpb, 0, i % tpb)
    mod = lambda i: (i // tpb, 0, 0)
    nat = jax.ShapeDtypeStruct((n, d), BF16)
    tr = jax.ShapeDtypeStruct((bsz, d, seq), BF16)
    nrm = jax.ShapeDtypeStruct((bsz, N_HEADS, 2, seq), F32)
    nat_spec = pl.BlockSpec((tm, d), row)
    tr_spec = pl.BlockSpec((None, d, tm), tcol)
    nrm_spec = pl.BlockSpec((None, N_HEADS, 2, tm), lambda i: (i // tpb, 0, 0, i % tpb))
    return pl.pallas_call(
        _in_proj_kernel,
        out_shape=(tr, nat, tr, nat, nat, nat, nat, nrm, nrm),
        grid=(n // tm,),
        in_specs=[pl.BlockSpec((tm, d), row),
                  pl.BlockSpec((1, d), const),
                  pl.BlockSpec((None, 1, d), mod),
                  pl.BlockSpec((None, 1, d), mod),
                  pl.BlockSpec(w_in_bf.shape, const),
                  pl.BlockSpec((HEAD_DIM, 1), const),
                  pl.BlockSpec((HEAD_DIM, 1), const),
                  pl.BlockSpec((HEAD_DIM, tm), lambda i: (0, i % tpb)),
                  pl.BlockSpec((HEAD_DIM, tm), lambda i: (0, i % tpb))],
        out_specs=(tr_spec, nat_spec, tr_spec, nat_spec, nat_spec, nat_spec, nat_spec,
                   nrm_spec, nrm_spec),
        scratch_shapes=[pltpu.VMEM((tm, d), BF16)],
        compiler_params=pltpu.CompilerParams(
            dimension_semantics=("arbitrary",), vmem_limit_bytes=VMEM_LIMIT),
        name="in_proj",
    )(x2, n1, sc1, sh1, w_in_bf, gq, gk, cos_t, sin_t)


def _diff_attn_kernel(qt_ref, k_ref, vt_ref, qn_ref, kn_ref, lam_ref, sw_ref, o_ref, acc0, acc1,
                      *, tk, lambda_init):
    seq = k_ref.shape[0]
    tq = qt_ref.shape[1]
    qt = qt_ref[...]
    first = lax.broadcasted_iota(jnp.int32, qt.shape, 0) < HEAD_DIM
    zero = jnp.zeros_like(qt)
    qz = (jnp.where(first, qt, zero), jnp.where(first, zero, qt))
    accs = (acc0, acc1)
    acc0[...] = jnp.zeros_like(acc0)
    acc1[...] = jnp.zeros_like(acc1)
    n_chunks = seq // tk

    def load(j):
        off = pl.multiple_of(j * tk, tk)
        return k_ref[pl.ds(off, tk), :], vt_ref[:, pl.ds(off, tk)]

    def plain_body(j, carry):
        kk, vt = load(j)
        new = []
        for c in range(2):
            p = jnp.exp2(_dot(kk, qz[c]))
            new.append(carry[c] + jnp.sum(p, axis=0, keepdims=True))
            accs[c][...] += _dot(vt, p.astype(BF16))
        return tuple(new)

    def online_body(j, carry):
        kk, vt = load(j)
        new = []
        for c in range(2):
            m, l = carry[2 * c], carry[2 * c + 1]
            s = _dot(kk, qz[c])
            m_new = jnp.maximum(m, jnp.max(s, axis=0, keepdims=True))
            alpha = jnp.exp2(m - m_new)
            p = jnp.exp2(s - m_new)
            l = alpha * l + jnp.sum(p, axis=0, keepdims=True)
            accs[c][...] = alpha * accs[c][...] + _dot(vt, p.astype(BF16))
            new += [m_new, l]
        return tuple(new)

    m_init = jnp.full((1, tq), NEG_BIG, F32)
    l_init = jnp.zeros((1, tq), F32)

    def plain():
        return lax.fori_loop(0, n_chunks, plain_body, (l_init, l_init))

    def online():
        _, l0, _, l1 = lax.fori_loop(0, n_chunks, online_body, (m_init, l_init, m_init, l_init))
        return l0, l1

    bound2 = jnp.max(jnp.max(qn_ref[...], axis=-1, keepdims=True)
                     * jnp.max(kn_ref[...], axis=-1, keepdims=True))
    l0, l1 = lax.cond(bound2 <= SAFE_EXP2_BOUND * SAFE_EXP2_BOUND, plain, online)

    lq = lam_ref[...]
    lam = (jnp.exp(jnp.sum(lq[0:1] * lq[1:2], axis=-1, keepdims=True))
           - jnp.exp(jnp.sum(lq[2:3] * lq[3:4], axis=-1, keepdims=True)) + lambda_init)
    o = acc0[...] / l0 - lam * (acc1[...] / l1)
    o = o * lax.rsqrt(jnp.mean(o * o, axis=0, keepdims=True) + SUBLN_EPS)
    o = o * sw_ref[...] * (1.0 - lambda_init)
    o_ref[...] = o.T.astype(BF16)


def _diff_attn(qt, k3, vt, qn, kn, lam_vecs, subln_col, lambda_init):
    bsz, d, seq = qt.shape
    tq = min(seq, 1024)
    tk = min(seq, 4096)
    kern = functools.partial(_diff_attn_kernel, tk=tk, lambda_init=lambda_init)
    return pl.pallas_call(
        kern,
        out_shape=jax.ShapeDtypeStruct((bsz, seq, d), BF16),
        grid=(bsz, N_HEADS, seq // tq),
        in_specs=[pl.BlockSpec((None, V_DIM, tq), lambda b, h, i: (b, h, i)),
                  pl.BlockSpec((None, seq, V_DIM), lambda b, h, i: (b, 0, h)),
                  pl.BlockSpec((None, V_DIM, seq), lambda b, h, i: (b, h, 0)),
                  pl.BlockSpec((None, None, 2, tq), lambda b, h, i: (b, h, 0, i)),
                  pl.BlockSpec((None, None, 2, seq), lambda b, h, i: (b, h, 0, 0)),
                  pl.BlockSpec((4, HEAD_DIM), lambda b, h, i: (0, 0)),
                  pl.BlockSpec((V_DIM, 1), lambda b, h, i: (0, 0))],
        out_specs=pl.BlockSpec((None, tq, V_DIM), lambda b, h, i: (b, i, h)),
        scratch_shapes=[pltpu.VMEM((V_DIM, tq), F32), pltpu.VMEM((V_DIM, tq), F32)],
        compiler_params=pltpu.CompilerParams(
            dimension_semantics=("arbitrary", "arbitrary", "arbitrary"),
            vmem_limit_bytes=VMEM_LIMIT),
        name="diff_attn",
    )(qt, k3, vt, qn, kn, lam_vecs, subln_col)


def _post_mix_kernel(o_ref, cb_ref, z_ref, zp_ref, zn_ref, sga_ref, sgc_ref, x_ref,
                     g1_ref, sc2_ref, sh2_ref, n2_ref, cw_ref, wao_ref, wco_ref, wout_ref,
                     wrh_ref, wrl_ref, br_ref,
                     x1_ref, h2_ref, route_ref, cnt_ref, u_scr, *, tpb, n_sub):
    i = pl.program_id(0)
    tm = x_ref.shape[0]

    @pl.when(i == 0)
    def _():
        cnt_ref[...] = jnp.zeros_like(cnt_ref)

    z = z_ref[...].astype(F32)
    rows = lax.broadcasted_iota(jnp.int32, z.shape, 0)
    halo_rows = zp_ref.shape[0]
    prev_row = zp_ref[halo_rows - 1:halo_rows, :].astype(F32)
    next_row = zn_ref[0:1, :].astype(F32)
    prev_row = jnp.where(i % tpb == 0, jnp.zeros_like(prev_row), prev_row)
    next_row = jnp.where(i % tpb == tpb - 1, jnp.zeros_like(next_row), next_row)
    z_m1 = jnp.where(rows == 0, prev_row, pltpu.roll(z, 1, 0))
    z_p1 = jnp.where(rows == tm - 1, next_row, pltpu.roll(z, tm - 1, 0))
    cw = cw_ref[...]
    conv = z_m1 * cw[0:1] + z * cw[1:2] + z_p1 * cw[2:3]
    u_scr[...] = (cb_ref[...].astype(F32) * conv).astype(BF16)

    ts = tm // n_sub
    lane = lax.broadcasted_iota(jnp.int32, (ts, LANES), 1)
    r_i = lax.broadcasted_iota(jnp.int32, (ts, ts), 0)
    c_i = lax.broadcasted_iota(jnp.int32, (ts, ts), 1)
    lower = (r_i > c_i).astype(BF16)
    counts = cnt_ref[0:1, :]
    for s in range(n_sub):
        rs = slice(s * ts, (s + 1) * ts)
        y_attn = _dot(o_ref[rs, :], wao_ref[...])
        y_conv = _dot(u_scr[rs, :], wco_ref[...])
        m = sga_ref[rs, :].astype(F32) * y_attn + sgc_ref[rs, :].astype(F32) * y_conv
        x1 = x_ref[rs, :] + g1_ref[...] * _dot(m.astype(BF16), wout_ref[...])
        x1_ref[rs, :] = x1

        h2 = x1 * lax.rsqrt(jnp.mean(x1 * x1, axis=-1, keepdims=True) + RMS_EPS) * n2_ref[...]
        h2 = h2 * (1.0 + sc2_ref[...]) + sh2_ref[...]
        _store_packed_rows(h2_ref, s * ts, h2)

        h_hi = h2.astype(BF16)
        h_lo = (h2 - h_hi.astype(F32)).astype(BF16)
        logits = (_dot(h_hi, wrh_ref[...]) + _dot(h_lo, wrh_ref[...]) + _dot(h_hi, wrl_ref[...])
                  + br_ref[...])

        work = logits
        vals, idxs = [], []
        for _ in range(TOP_K):
            mx = jnp.max(work, axis=-1, keepdims=True)
            ix = jnp.min(jnp.where(work == mx, lane, LANES), axis=-1, keepdims=True)
            vals.append(mx)
            idxs.append(ix)
            work = jnp.where(lane == ix, 2.0 * NEG_BIG, work)
        exps = [jnp.exp(v - vals[0]) for v in vals]
        den = exps[0] + exps[1] + exps[2] + exps[3]

        sel = (work == 2.0 * NEG_BIG).astype(BF16)
        before = _dot(lower, sel) + counts
        counts = counts + jnp.sum(sel.astype(F32), axis=0, keepdims=True)

        route = jnp.zeros(logits.shape, F32)
        for k in range(TOP_K):
            rank = jnp.sum(jnp.where(lane == idxs[k], before, 0.0), axis=-1, keepdims=True)
            route = jnp.where(lane == k, exps[k] / den, route)
            route = jnp.where(lane == TOP_K + k, idxs[k].astype(F32), route)
            route = jnp.where(lane == 2 * TOP_K + k, rank, route)
        route_ref[rs, :] = route
    cnt_ref[...] = jnp.broadcast_to(counts, cnt_ref.shape)


def _post_mix(o2, cb, z, sga, sgc, x2, g1, sc2, sh2, n2, conv_w, wao, wco, wout,
              wr_hi, wr_lo, br_pad, seq):
    n, d = x2.shape
    tm = min(seq, 512)
    tpb = seq // tm
    halo = 16
    hb = tm // halo
    last_hb = n // halo - 1
    row = lambda i: (i, 0)
    const = lambda i: (0, 0)
    mod = lambda i: (i // tpb, 0, 0)
    wspec = pl.BlockSpec((d, d), const)
    kern = functools.partial(_post_mix_kernel, tpb=tpb, n_sub=1)
    return pl.pallas_call(
        kern,
        out_shape=(jax.ShapeDtypeStruct((n, d), F32),
                   jax.ShapeDtypeStruct((n * TILE_ROW, LANES), jnp.uint32),
                   jax.ShapeDtypeStruct((n, LANES), F32), jax.ShapeDtypeStruct((8, LANES), F32)),
        grid=(n // tm,),
        in_specs=[pl.BlockSpec((tm, d), row), pl.BlockSpec((tm, d), row), pl.BlockSpec((tm, d), row),
                  pl.BlockSpec((halo, d), lambda i: (jnp.maximum(i * hb - 1, 0), 0)),
                  pl.BlockSpec((halo, d), lambda i: (jnp.minimum((i + 1) * hb, last_hb), 0)),
                  pl.BlockSpec((tm, d), row), pl.BlockSpec((tm, d), row), pl.BlockSpec((tm, d), row),
                  pl.BlockSpec((None, 1, d), mod), pl.BlockSpec((None, 1, d), mod),
                  pl.BlockSpec((None, 1, d), mod),
                  pl.BlockSpec((1, d), const), pl.BlockSpec((3, d), const),
                  wspec, wspec, wspec,
                  pl.BlockSpec((d, LANES), const), pl.BlockSpec((d, LANES), const),
                  pl.BlockSpec((1, LANES), const)],
        out_specs=(pl.BlockSpec((tm, d), row), pl.BlockSpec((tm * TILE_ROW, LANES), row),
                   pl.BlockSpec((tm, LANES), row), pl.BlockSpec((8, LANES), const)),
        scratch_shapes=[pltpu.VMEM((tm, d), BF16)],
        compiler_params=pltpu.CompilerParams(
            dimension_semantics=("arbitrary",), vmem_limit_bytes=VMEM_LIMIT),
        name="post_mix",
    )(o2, cb, z, z, z, sga, sgc, x2, g1, sc2, sh2, n2, conv_w, wao, wco, wout, wr_hi, wr_lo, br_pad)


def _tile_rows(ref, row, n_rows=1):
    start = pl.multiple_of(row * TILE_ROW, TILE_ROW)
    return ref.at[pl.ds(start, n_rows * TILE_ROW)]


def _row_copy(src, dst, s, t, sem):
    return pltpu.make_async_copy(_tile_rows(src, s), _tile_rows(dst, t), sem)


def _dispatch_kernel(dest_ref, ends_ref, h2_ref, xs_hbm, zbuf, sem, zsem, *, tc, nblk):
    base = pl.program_id(0) * tc

    @pl.when(pl.program_id(0) == 0)
    def _():
        zbuf[...] = jnp.zeros_like(zbuf)

        def zero_block(row):
            return pltpu.make_async_copy(zbuf, _tile_rows(xs_hbm, row, MOE_ROWS), zsem)

        def nonempty(e):
            return ends_ref[e] > (ends_ref[e - 1] if e else 0)

        total = ends_ref[N_EXPERTS - 1]
        n_tail = nblk - total // MOE_ROWS

        def tail_start(b, carry):
            zero_block(total + b * MOE_ROWS).start()
            return carry

        def tail_wait(b, carry):
            zero_block(0).wait()
            return carry

        for e in range(N_EXPERTS):
            @pl.when(nonempty(e))
            def _(e=e):
                zero_block(ends_ref[e] - MOE_ROWS).start()
        lax.fori_loop(0, n_tail, tail_start, 0)
        for e in range(N_EXPERTS):
            @pl.when(nonempty(e))
            def _():
                zero_block(0).wait()
        lax.fori_loop(0, n_tail, tail_wait, 0)

    def issue(t, carry):
        for k in range(TOP_K):
            _row_copy(h2_ref, xs_hbm, t, dest_ref[(base + t) * TOP_K + k], sem).start(priority=k % 2)
        return carry

    lax.fori_loop(0, tc, issue, 0, unroll=4)

    for _ in range(TOP_K):
        pltpu.make_async_copy(h2_ref, _tile_rows(xs_hbm, 0, tc), sem).wait()


def _moe_dispatch(dest_flat, ends, h2_tiles, n, p_rows):
    tc = min(n, 512)
    kern = functools.partial(_dispatch_kernel, tc=tc, nblk=p_rows // MOE_ROWS)
    return pl.pallas_call(
        kern,
        out_shape=jax.ShapeDtypeStruct((p_rows * TILE_ROW, LANES), jnp.uint32),
        grid_spec=pltpu.PrefetchScalarGridSpec(
            num_scalar_prefetch=2, grid=(n // tc,),
            in_specs=[pl.BlockSpec((tc * TILE_ROW, LANES), lambda i, dr, en: (i, 0))],
            out_specs=pl.BlockSpec(memory_space=pl.ANY),
            scratch_shapes=[pltpu.VMEM((MOE_ROWS * TILE_ROW, LANES), jnp.uint32),
                            pltpu.SemaphoreType.DMA, pltpu.SemaphoreType.DMA]),
        compiler_params=pltpu.CompilerParams(dimension_semantics=("arbitrary",)),
        name="moe_dispatch",
    )(dest_flat, ends, h2_tiles)


def _ffn_kernel(first_ref, count_ref, slot_ref, next_ref, misc_ref,
                xs_hbm, wgu_hbm, wd_hbm, bg_ref, bu_ref, bd_ref, ys_hbm,
                wgu_buf, wd_buf, wg_scr, wu_scr, wd_scr, xbuf, ybuf,
                wsem, xsem, ysem, *, nblk):
    e = pl.program_id(0)
    first = first_ref[e]
    count = count_ref[e]
    slot = slot_ref[e]
    first_expert, total_blocks = misc_ref[0], misc_ref[1]

    def weight_copies(ex, s):
        copies = []
        for hbm, buf, sem in ((wgu_hbm, wgu_buf, wsem.at[0, s]), (wd_hbm, wd_buf, wsem.at[1, s])):
            rows = hbm.shape[1] // WEIGHT_CHUNKS
            for c in range(WEIGHT_CHUNKS):
                part = pl.ds(c * rows, rows)
                copies.append(pltpu.make_async_copy(hbm.at[ex, part], buf.at[s, part], sem))
        return copies

    def x_copy(b, s):
        return pltpu.make_async_copy(_tile_rows(xs_hbm, b * MOE_ROWS, MOE_ROWS), xbuf.at[s], xsem.at[s])

    def y_copy(b, s):
        return pltpu.make_async_copy(ybuf.at[s], _tile_rows(ys_hbm, b * MOE_ROWS, MOE_ROWS), ysem.at[s])

    @pl.when(e == first_expert)
    def _():
        for cp in weight_copies(e, 0):
            cp.start()

    @pl.when(count > 0)
    def _():
        x_copy(first, 0).start()
        for cp in weight_copies(e, slot):
            cp.wait()

        @pl.when(next_ref[e] >= 0)
        def _():
            for cp in weight_copies(next_ref[e], 1 - slot):
                cp.start(priority=1)

        d = wgu_buf.shape[1]
        for c in range(d // LANES):
            cols = slice(c * LANES, (c + 1) * LANES)
            words = pltpu.bitcast(wgu_buf[slot, cols, :].astype(BF16).T, jnp.uint32)
            wg_scr[cols, :] = pltpu.bitcast(words << 16, F32).astype(BF16).T
            wu_scr[cols, :] = pltpu.bitcast(words & jnp.uint32(_HIGH_HALF), F32).astype(BF16).T
        wd_scr[...] = wd_buf[slot].astype(BF16)
        bg, bu, bd = bg_ref[e], bu_ref[e], bd_ref[e]

        def block(j, carry):
            s = j % 2
            x_copy(first, s).wait()

            @pl.when(j + 1 < count)
            def _():
                x_copy(first + j + 1, 1 - s).start()

            @pl.when(j >= 2)
            def _():
                y_copy(first, s).wait()

            x = jnp.concatenate(_load_packed_rows(xbuf.at[s], MOE_ROWS), axis=1).astype(BF16)
            gate = _dot(x, wg_scr[...]) + bg
            up = _dot(x, wu_scr[...]) + bu
            gate = jnp.minimum(gate, SWIGLU_LIMIT)
            up = jnp.clip(up, -SWIGLU_LIMIT, SWIGLU_LIMIT)
            glu = gate * jax.nn.sigmoid(gate * SWIGLU_ALPHA)
            mid = ((up + 1.0) * glu).astype(BF16)
            _store_packed_rows(ybuf.at[s], 0, _dot(mid, wd_scr[...]) + bd)
            y_copy(first + j, s).start()
            return carry

        lax.fori_loop(0, count, block, 0)

        @pl.when(count >= 2)
        def _():
            y_copy(first, count % 2).wait()
        y_copy(first, (count - 1) % 2).wait()

    @pl.when(e == pl.num_programs(0) - 1)
    def _():
        ybuf[0] = jnp.zeros(ybuf.shape[1:], ybuf.dtype)

        def tail_start(b, carry):
            y_copy(b, 0).start()
            return carry

        def tail_wait(b, carry):
            y_copy(b, 0).wait()
            return carry

        lax.fori_loop(total_blocks, nblk, tail_start, 0)
        lax.fori_loop(total_blocks, nblk, tail_wait, 0)


def _moe_ffn(first_blk, n_blk, slot, next_e, misc, xs, wgu, wd, bg, bu, bd):
    p_rows = xs.shape[0] // TILE_ROW
    n_exp, d, f2 = wgu.shape
    f = f2 // 2
    assert d == 2 * TILE_ROW * LANES
    whole = lambda shape: pl.BlockSpec(shape, lambda e, *_: (0,) * len(shape))
    tile = (MOE_ROWS * TILE_ROW, LANES)
    kern = functools.partial(_ffn_kernel, nblk=p_rows // MOE_ROWS)
    return pl.pallas_call(
        kern,
        out_shape=jax.ShapeDtypeStruct((p_rows * TILE_ROW, LANES), jnp.uint32),
        grid_spec=pltpu.PrefetchScalarGridSpec(
            num_scalar_prefetch=5, grid=(n_exp,),
            in_specs=[pl.BlockSpec(memory_space=pl.ANY),
                      pl.BlockSpec(memory_space=pl.ANY), pl.BlockSpec(memory_space=pl.ANY),
                      whole(bg.shape), whole(bu.shape), whole(bd.shape)],
            out_specs=pl.BlockSpec(memory_space=pl.ANY),
            scratch_shapes=[pltpu.VMEM((2, d, f2), F32), pltpu.VMEM((2, f, d), F32),
                            pltpu.VMEM((d, f), BF16), pltpu.VMEM((d, f), BF16),
                            pltpu.VMEM((f, d), BF16),
                            pltpu.VMEM((2,) + tile, jnp.uint32), pltpu.VMEM((2,) + tile, jnp.uint32),
                            pltpu.SemaphoreType.DMA((2, 2)), pltpu.SemaphoreType.DMA((2,)),
                            pltpu.SemaphoreType.DMA((2,))]),
        compiler_params=pltpu.CompilerParams(
            dimension_semantics=("arbitrary",), vmem_limit_bytes=VMEM_LIMIT),
        name="moe_ffn",
    )(first_blk, n_blk, slot, next_e, misc, xs, wgu, wd, bg, bu, bd)


def _combine_kernel(dest_ref, ys_hbm, x1_ref, route_ref, g2_ref, o_ref, buf, sem, *, tc, n_steps):
    i = pl.program_id(0)
    slot = i % 2

    def gather(step, s):
        def issue(t, carry):
            for k in range(TOP_K):
                row = dest_ref[(step * tc + t) * TOP_K + k]
                _row_copy(ys_hbm, buf.at[s, k], row, t, sem.at[s]).start(priority=k % 2)
            return carry

        lax.fori_loop(0, tc, issue, 0, unroll=4)

    @pl.when(i == 0)
    def _():
        gather(0, 0)

    @pl.when(i + 1 < n_steps)
    def _():
        gather(i + 1, 1 - slot)

    for k in range(TOP_K):
        pltpu.make_async_copy(_tile_rows(ys_hbm, 0, tc), buf.at[slot, k], sem.at[slot]).wait()

    route = route_ref[...]
    chunks = [_load_packed_rows(buf.at[slot, k], tc) for k in range(TOP_K)]
    for c in range(2 * TILE_ROW):
        cols = slice(c * LANES, (c + 1) * LANES)
        y = chunks[0][c] * route[:, 0:1]
        for k in range(1, TOP_K):
            y = y + chunks[k][c] * route[:, k:k + 1]
        o_ref[:, cols] = x1_ref[:, cols] + g2_ref[:, cols] * y


def _moe_combine(dest_flat, ys, x1, route, g2, seq):
    n, d = x1.shape
    assert d == 2 * TILE_ROW * LANES
    tc = min(seq, 256)
    tpb = seq // tc
    kern = functools.partial(_combine_kernel, tc=tc, n_steps=n // tc)
    return pl.pallas_call(
        kern,
        out_shape=jax.ShapeDtypeStruct((n, d), F32),
        grid_spec=pltpu.PrefetchScalarGridSpec(
            num_scalar_prefetch=1, grid=(n // tc,),
            in_specs=[pl.BlockSpec(memory_space=pl.ANY),
                      pl.BlockSpec((tc, d), lambda i, dr: (i, 0)),
                      pl.BlockSpec((tc, LANES), lambda i, dr: (i, 0)),
                      pl.BlockSpec((None, 1, d), lambda i, dr: (i // tpb, 0, 0))],
            out_specs=pl.BlockSpec((tc, d), lambda i, dr: (i, 0)),
            scratch_shapes=[pltpu.VMEM((2, TOP_K, tc * TILE_ROW, LANES), jnp.uint32),
                            pltpu.SemaphoreType.DMA((2,))]),
        compiler_params=pltpu.CompilerParams(
            dimension_semantics=("arbitrary",), vmem_limit_bytes=VMEM_LIMIT),
        name="moe_combine",
    )(dest_flat, ys, x1, route, g2)


def _rope_tables(seq):
    inv_freq = (ROPE_THETA ** (-np.arange(0, HEAD_DIM, 2, dtype=np.float32) / HEAD_DIM)).astype(np.float32)
    ang = (inv_freq[:, None] * np.arange(seq, dtype=np.float32)[None, :]).astype(np.float32)
    cos, sin = np.cos(ang.astype(np.float64)), np.sin(ang.astype(np.float64))
    cos_t = np.concatenate([cos, cos], axis=0).astype(np.float32)
    sin_t = np.concatenate([-sin, sin], axis=0).astype(np.float32)
    return jnp.asarray(cos_t), jnp.asarray(sin_t)


def _layer(x, c, l, lambda_init, w_ada, b_ada, norm1_w, w_in, q_norm_w, k_norm_w, lambda_q1,
           lambda_k1, lambda_q2, lambda_k2, subln_w, w_attn_o, conv_w, w_conv_o, w_out, norm2_w,
           w_router, b_router, w_gate_up, b_gate_up, w_down, b_down):
    bsz, seq, d = x.shape
    n = bsz * seq
    x2 = x.reshape(n, d)

    mod = _ada_mod(c, w_ada[l], b_ada[l])
    sh1, sc1, g1, sh2, sc2, g2 = [m.reshape(bsz, 1, d) for m in jnp.split(mod, 6, axis=-1)]

    cos_t, sin_t = _rope_tables(seq)
    qt, k, vt, cb, z, sga, sgc, qn, kn = _in_proj(
        x2, norm1_w[l].reshape(1, d), sc1, sh1, w_in[l].astype(BF16),
        q_norm_w[l].reshape(HEAD_DIM, 1) * Q_SCALE, k_norm_w[l].reshape(HEAD_DIM, 1),
        cos_t, sin_t, bsz, seq)

    lam_vecs = jnp.stack([lambda_q1[l], lambda_k1[l], lambda_q2[l], lambda_k2[l]]).astype(F32)
    o = _diff_attn(qt, k.reshape(bsz, seq, d), vt, qn, kn, lam_vecs, subln_w[l].reshape(V_DIM, 1),
                   lambda_init)

    wr = jnp.zeros((d, LANES), F32).at[:, :N_EXPERTS].set(w_router[l].astype(F32))
    wr_hi = wr.astype(BF16)
    wr_lo = (wr - wr_hi.astype(F32)).astype(BF16)
    br_pad = jnp.full((1, LANES), NEG_BIG, F32).at[0, :N_EXPERTS].set(b_router[l].astype(F32))
    x1, h2, route, cnt = _post_mix(
        o.reshape(n, d), cb, z, sga, sgc, x2, g1, sc2, sh2, norm2_w[l].reshape(1, d), conv_w[l],
        w_attn_o[l].astype(BF16), w_conv_o[l].astype(BF16), w_out[l].astype(BF16),
        wr_hi, wr_lo, br_pad, seq)

    counts = cnt[0, :N_EXPERTS].astype(jnp.int32)
    padded = ((counts + MOE_ROWS - 1) // MOE_ROWS) * MOE_ROWS
    ends = jnp.cumsum(padded)
    start = ends - padded
    p_rows = n * TOP_K + N_EXPERTS * MOE_ROWS
    top_e = route[:, TOP_K:2 * TOP_K].astype(jnp.int32)
    rank = route[:, 2 * TOP_K:3 * TOP_K].astype(jnp.int32)
    experts = jnp.arange(N_EXPERTS, dtype=jnp.int32)
    seg_start = jnp.sum(jnp.where(top_e[..., None] == experts, start, 0), axis=-1)
    dest = (seg_start + rank).reshape(-1)
    nonempty = padded > 0
    later = jnp.logical_and(nonempty[None, :], experts[None, :] > experts[:, None])
    next_e = jnp.min(jnp.where(later, experts[None, :], N_EXPERTS), axis=1)
    next_e = jnp.where(next_e == N_EXPERTS, -1, next_e).astype(jnp.int32)
    slot = ((jnp.cumsum(nonempty.astype(jnp.int32)) - 1) % 2).astype(jnp.int32)
    first_blk = (start // MOE_ROWS).astype(jnp.int32)
    n_blk = (padded // MOE_ROWS).astype(jnp.int32)
    misc = jnp.stack([jnp.argmax(nonempty).astype(jnp.int32), (ends[-1] // MOE_ROWS).astype(jnp.int32)])

    xs = _moe_dispatch(dest, ends.astype(jnp.int32), h2, n, p_rows)
    ys = _moe_ffn(first_blk, n_blk, slot, next_e, misc, xs, w_gate_up[l], w_down[l],
                  b_gate_up[l][:, None, 0::2], b_gate_up[l][:, None, 1::2], b_down[l][:, None, :])
    out = _moe_combine(dest, ys, x1, route, g2, seq)
    return out.reshape(bsz, seq, d)


def kernel(x, c, w_ada, b_ada, norm1_w, w_in, q_norm_w, k_norm_w, lambda_q1, lambda_k1, lambda_q2,
           lambda_k2, subln_w, w_attn_o, conv_w, w_conv_o, w_out, norm2_w, w_router, b_router,
           w_gate_up, b_gate_up, w_down, b_down):
    depth = w_ada.shape[0]
    for l in range(depth):
        lambda_init = 0.8 - 0.6 * math.exp(-0.3 * l)
        x = _layer(x, c, l, lambda_init, w_ada, b_ada, norm1_w, w_in, q_norm_w, k_norm_w,
                   lambda_q1, lambda_k1, lambda_q2, lambda_k2, subln_w, w_attn_o, conv_w,
                   w_conv_o, w_out, norm2_w, w_router, b_router, w_gate_up, b_gate_up,
                   w_down, b_down)
    return x
```

```python
import functools
import math

import jax
import jax.numpy as jnp
import numpy as np
from jax import lax
from jax.experimental import pallas as pl
from jax.experimental.pallas import tpu as pltpu

N_HEADS = 8
HEAD_DIM = 64
V_DIM = 2 * HEAD_DIM
IN_BLOCKS = ("q", "k", "v", "conv_b", "conv_c", "conv_x", "gate_attn", "gate_conv")
N_EXPERTS = 32
TOP_K = 4
SWIGLU_LIMIT = 7.0
SWIGLU_ALPHA = 1.702
ROPE_THETA = 10000.0
RMS_EPS = 1e-6
SUBLN_EPS = 1e-5
LANES = 128
TILE_ROW = 4
MOE_ROWS = 256
WEIGHT_CHUNKS = 8
GATE_UP_CHUNKS_ON_QUEUE_1 = 6
NEG_BIG = -1e30
LOG2E = 1.4426950408889634
Q_SCALE = LOG2E / math.sqrt(HEAD_DIM)
SAFE_EXP2_BOUND = 80.0
VMEM_LIMIT = 56 * 1024 * 1024

F32 = jnp.float32
BF16 = jnp.bfloat16


def _dot(a, b):
    return jnp.dot(a, b, preferred_element_type=F32)


_HIGH_HALF = 0xFFFF0000


def _bf16_bits(x):
    return pltpu.bitcast(x.astype(BF16).astype(F32), jnp.uint32)


def _store_packed_rows(ref, row0, val):
    rows, d = val.shape
    half = d // 2
    for c in range(TILE_ROW):
        lo = _bf16_bits(val[:, c * LANES:(c + 1) * LANES]) >> 16
        hi = _bf16_bits(val[:, half + c * LANES:half + (c + 1) * LANES])
        ref[pl.ds(row0 * TILE_ROW + c, rows, stride=TILE_ROW), :] = lo | hi


def _load_packed_rows(ref, rows):
    words = [ref[pl.ds(c, rows, stride=TILE_ROW), :] for c in range(TILE_ROW)]
    lo = [pltpu.bitcast(w << 16, F32) for w in words]
    hi = [pltpu.bitcast(w & jnp.uint32(_HIGH_HALF), F32) for w in words]
    return lo + hi


def _ada_kernel(ct_ref, w_ref, b_ref, o_ref):
    ct = ct_ref[...]
    s = ct * jax.nn.sigmoid(ct)
    w = w_ref[...]
    for b in range(ct.shape[1]):
        o_ref[b:b + 1, :] = jnp.sum(w * s[:, b:b + 1], axis=0, keepdims=True) + b_ref[...]


def _ada_mod(c, w_ada, b_ada):
    bsz, d = c.shape
    n = w_ada.shape[1]
    tn = min(n, 1536)
    return pl.pallas_call(
        _ada_kernel,
        out_shape=jax.ShapeDtypeStruct((bsz, n), F32),
        grid=(n // tn,),
        in_specs=[pl.BlockSpec((d, bsz), lambda j: (0, 0)),
                  pl.BlockSpec((d, tn), lambda j: (0, j)),
                  pl.BlockSpec((1, tn), lambda j: (0, j))],
        out_specs=pl.BlockSpec((bsz, tn), lambda j: (0, j)),
        compiler_params=pltpu.CompilerParams(dimension_semantics=("arbitrary",)),
        name="ada_mod",
    )(c.T, w_ada, b_ada.reshape(1, n))


def _qk_norm_rope_t(y, g_col, cos_t, sin_t):
    tm, w = y.shape
    yt = y.T.reshape(w // HEAD_DIM, HEAD_DIM, tm)
    ms = jnp.mean(yt * yt, axis=1, keepdims=True)
    yn = yt * lax.rsqrt(ms + RMS_EPS) * g_col[None]
    half = HEAD_DIM // 2
    swapped = jnp.concatenate([yn[:, half:, :], yn[:, :half, :]], axis=1)
    out = yn * cos_t[None] + swapped * sin_t[None]
    norm2 = jnp.sum(out * out, axis=1).reshape(N_HEADS, 2, tm)
    return out.reshape(w, tm), norm2


def _in_proj_kernel(x_ref, n1_ref, sc_ref, sh_ref, w_ref, gq_ref, gk_ref, cos_ref, sin_ref,
                    qt_ref, k_ref, vt_ref, cb_ref, z_ref, sga_ref, sgc_ref, qn_ref, kn_ref, h_scr):
    x = x_ref[...]
    xn = x * lax.rsqrt(jnp.mean(x * x, axis=-1, keepdims=True) + RMS_EPS) * n1_ref[...]
    h_scr[...] = (xn * (1.0 + sc_ref[...]) + sh_ref[...]).astype(BF16)
    wcol = w_ref.shape[1] // len(IN_BLOCKS)

    def proj(name):
        c = IN_BLOCKS.index(name)
        return _dot(h_scr[...], w_ref[:, c * wcol:(c + 1) * wcol])

    qt, qn = _qk_norm_rope_t(proj("q"), gq_ref[...], cos_ref[...], sin_ref[...])
    qt_ref[...] = qt.astype(BF16)
    qn_ref[...] = qn
    kt, kn = _qk_norm_rope_t(proj("k"), gk_ref[...], cos_ref[...], sin_ref[...])
    k_ref[...] = kt.T.astype(BF16)
    kn_ref[...] = kn
    vt_ref[...] = proj("v").T.astype(BF16)
    cb_ref[...] = proj("conv_b").astype(BF16)
    z_ref[...] = (proj("conv_c") * proj("conv_x")).astype(BF16)
    sga_ref[...] = jax.nn.sigmoid(proj("gate_attn")).astype(BF16)
    sgc_ref[...] = jax.nn.sigmoid(proj("gate_conv")).astype(BF16)


def _in_proj(x2, n1, sc1, sh1, w_in_bf, gq, gk, cos_t, sin_t, bsz, seq):
    n, d = x2.shape
    tm = min(seq, 512)
    tpb = seq // tm
    assert w_in_bf.shape[1] == len(IN_BLOCKS) * d
    row = lambda i: (i, 0)
    const = lambda i: (0, 0)
    tcol = lambda i: (i // tpb, 0, i % tpb)
    mod = lambda i: (i // tpb, 0, 0)
    nat = jax.ShapeDtypeStruct((n, d), BF16)
    tr = jax.ShapeDtypeStruct((bsz, d, seq), BF16)
    nrm = jax.ShapeDtypeStruct((bsz, N_HEADS, 2, seq), F32)
    nat_spec = pl.BlockSpec((tm, d), row)
    tr_spec = pl.BlockSpec((None, d, tm), tcol)
    nrm_spec = pl.BlockSpec((None, N_HEADS, 2, tm), lambda i: (i // tpb, 0, 0, i % tpb))
    return pl.pallas_call(
        _in_proj_kernel,
        out_shape=(tr, nat, tr, nat, nat, nat, nat, nrm, nrm),
        grid=(n // tm,),
        in_specs=[pl.BlockSpec((tm, d), row),
                  pl.BlockSpec((1, d), const),
                  pl.BlockSpec((None, 1, d), mod),
                  pl.BlockSpec((None, 1, d), mod),
                  pl.BlockSpec(w_in_bf.shape, const),
                  pl.BlockSpec((HEAD_DIM, 1), const),
                  pl.BlockSpec((HEAD_DIM, 1), const),
                  pl.BlockSpec((HEAD_DIM, tm), lambda i: (0, i % tpb)),
                  pl.BlockSpec((HEAD_DIM, tm), lambda i: (0, i % tpb))],
        out_specs=(tr_spec, nat_spec, tr_spec, nat_spec, nat_spec, nat_spec, nat_spec,
                   nrm_spec, nrm_spec),
        scratch_shapes=[pltpu.VMEM((tm, d), BF16)],
        compiler_params=pltpu.CompilerParams(
            dimension_semantics=("arbitrary",), vmem_limit_bytes=VMEM_LIMIT),
        name="in_proj",
    )(x2, n1, sc1, sh1, w_in_bf, gq, gk, cos_t, sin_t)


def _diff_attn_kernel(qt_ref, k_ref, vt_ref, qn_ref, kn_ref, lam_ref, sw_ref, o_ref, acc0, acc1,
                      *, tk, lambda_init):
    seq = k_ref.shape[0]
    tq = qt_ref.shape[1]
    qt = qt_ref[...]
    first = lax.broadcasted_iota(jnp.int32, qt.shape, 0) < HEAD_DIM
    zero = jnp.zeros_like(qt)
    qz = (jnp.where(first, qt, zero), jnp.where(first, zero, qt))
    accs = (acc0, acc1)
    acc0[...] = jnp.zeros_like(acc0)
    acc1[...] = jnp.zeros_like(acc1)
    n_chunks = seq // tk

    def load(j):
        off = pl.multiple_of(j * tk, tk)
        return k_ref[pl.ds(off, tk), :], vt_ref[:, pl.ds(off, tk)]

    def plain_body(j, carry):
        kk, vt = load(j)
        new = []
        for c in range(2):
            p = jnp.exp2(_dot(kk, qz[c]))
            new.append(carry[c] + jnp.sum(p, axis=0, keepdims=True))
            accs[c][...] += _dot(vt, p.astype(BF16))
        return tuple(new)

    def online_body(j, carry):
        kk, vt = load(j)
        new = []
        for c in range(2):
            m, l = carry[2 * c], carry[2 * c + 1]
            s = _dot(kk, qz[c])
            m_new = jnp.maximum(m, jnp.max(s, axis=0, keepdims=True))
            alpha = jnp.exp2(m - m_new)
            p = jnp.exp2(s - m_new)
            l = alpha * l + jnp.sum(p, axis=0, keepdims=True)
            accs[c][...] = alpha * accs[c][...] + _dot(vt, p.astype(BF16))
            new += [m_new, l]
        return tuple(new)

    m_init = jnp.full((1, tq), NEG_BIG, F32)
    l_init = jnp.zeros((1, tq), F32)

    def plain():
        return lax.fori_loop(0, n_chunks, plain_body, (l_init, l_init))

    def online():
        _, l0, _, l1 = lax.fori_loop(0, n_chunks, online_body, (m_init, l_init, m_init, l_init))
        return l0, l1

    bound2 = jnp.max(jnp.max(qn_ref[...], axis=-1, keepdims=True)
                     * jnp.max(kn_ref[...], axis=-1, keepdims=True))
    l0, l1 = lax.cond(bound2 <= SAFE_EXP2_BOUND * SAFE_EXP2_BOUND, plain, online)

    lq = lam_ref[...]
    lam = (jnp.exp(jnp.sum(lq[0:1] * lq[1:2], axis=-1, keepdims=True))
           - jnp.exp(jnp.sum(lq[2:3] * lq[3:4], axis=-1, keepdims=True)) + lambda_init)
    o = acc0[...] / l0 - lam * (acc1[...] / l1)
    o = o * lax.rsqrt(jnp.mean(o * o, axis=0, keepdims=True) + SUBLN_EPS)
    o = o * sw_ref[...] * (1.0 - lambda_init)
    o_ref[...] = o.T.astype(BF16)


def _diff_attn(qt, k3, vt, qn, kn, lam_vecs, subln_col, lambda_init):
    bsz, d, seq = qt.shape
    tq = min(seq, 1024)
    tk = min(seq, 4096)
    kern = functools.partial(_diff_attn_kernel, tk=tk, lambda_init=lambda_init)
    return pl.pallas_call(
        kern,
        out_shape=jax.ShapeDtypeStruct((bsz, seq, d), BF16),
        grid=(bsz, N_HEADS, seq // tq),
        in_specs=[pl.BlockSpec((None, V_DIM, tq), lambda b, h, i: (b, h, i)),
                  pl.BlockSpec((None, seq, V_DIM), lambda b, h, i: (b, 0, h)),
                  pl.BlockSpec((None, V_DIM, seq), lambda b, h, i: (b, h, 0)),
                  pl.BlockSpec((None, None, 2, tq), lambda b, h, i: (b, h, 0, i)),
                  pl.BlockSpec((None, None, 2, seq), lambda b, h, i: (b, h, 0, 0)),
                  pl.BlockSpec((4, HEAD_DIM), lambda b, h, i: (0, 0)),
                  pl.BlockSpec((V_DIM, 1), lambda b, h, i: (0, 0))],
        out_specs=pl.BlockSpec((None, tq, V_DIM), lambda b, h, i: (b, i, h)),
        scratch_shapes=[pltpu.VMEM((V_DIM, tq), F32), pltpu.VMEM((V_DIM, tq), F32)],
        compiler_params=pltpu.CompilerParams(
            dimension_semantics=("arbitrary", "arbitrary", "arbitrary"),
            vmem_limit_bytes=VMEM_LIMIT),
        name="diff_attn",
    )(qt, k3, vt, qn, kn, lam_vecs, subln_col)


def _post_mix_kernel(o_ref, cb_ref, z_ref, zp_ref, zn_ref, sga_ref, sgc_ref, x_ref,
                     g1_ref, sc2_ref, sh2_ref, n2_ref, cw_ref, wao_ref, wco_ref, wout_ref,
                     wr_ref, br_ref,
                     x1_ref, h2_ref, route_ref, cnt_ref, *, tpb):
    i = pl.program_id(0)
    tm = x_ref.shape[0]

    @pl.when(i == 0)
    def _():
        cnt_ref[...] = jnp.zeros_like(cnt_ref)

    z = z_ref[...].astype(F32)
    rows = lax.broadcasted_iota(jnp.int32, z.shape, 0)
    halo_rows = zp_ref.shape[0]
    prev_row = zp_ref[halo_rows - 1:halo_rows, :].astype(F32)
    next_row = zn_ref[0:1, :].astype(F32)
    prev_row = jnp.where(i % tpb == 0, jnp.zeros_like(prev_row), prev_row)
    next_row = jnp.where(i % tpb == tpb - 1, jnp.zeros_like(next_row), next_row)
    z_m1 = jnp.where(rows == 0, prev_row, pltpu.roll(z, 1, 0))
    z_p1 = jnp.where(rows == tm - 1, next_row, pltpu.roll(z, tm - 1, 0))
    cw = cw_ref[...]
    conv = z_m1 * cw[0:1] + z * cw[1:2] + z_p1 * cw[2:3]
    u = (cb_ref[...].astype(F32) * conv).astype(BF16)

    y_attn = _dot(o_ref[...], wao_ref[...])
    y_conv = _dot(u, wco_ref[...])
    m = sga_ref[...].astype(F32) * y_attn + sgc_ref[...].astype(F32) * y_conv
    x1 = x_ref[...] + g1_ref[...] * _dot(m.astype(BF16), wout_ref[...])
    x1_ref[...] = x1

    h2 = x1 * lax.rsqrt(jnp.mean(x1 * x1, axis=-1, keepdims=True) + RMS_EPS) * n2_ref[...]
    h2 = h2 * (1.0 + sc2_ref[...]) + sh2_ref[...]
    _store_packed_rows(h2_ref, 0, h2)

    route, counts = _route_tile(h2, wr_ref, br_ref, cnt_ref[0:1, :])
    route_ref[...] = route
    cnt_ref[...] = jnp.broadcast_to(counts, cnt_ref.shape)


def _route_tile(h2, wr_ref, br_ref, counts):
    tm = h2.shape[0]
    h_hi = h2.astype(BF16)
    h_lo = (h2 - h_hi.astype(F32)).astype(BF16)
    hi_both = _dot(h_hi, wr_ref[...])
    logits = (hi_both[:, :LANES] + hi_both[:, LANES:] + _dot(h_lo, wr_ref[:, :LANES])
              + br_ref[...])

    lane = lax.broadcasted_iota(jnp.int32, logits.shape, 1)
    work = logits
    vals, idxs = [], []
    for _ in range(TOP_K):
        mx = jnp.max(work, axis=-1, keepdims=True)
        ix = jnp.min(jnp.where(work == mx, lane, LANES), axis=-1, keepdims=True)
        vals.append(mx)
        idxs.append(ix)
        work = jnp.where(lane == ix, 2.0 * NEG_BIG, work)
    exps = [jnp.exp(v - vals[0]) for v in vals]
    den = exps[0] + exps[1] + exps[2] + exps[3]

    sel = (work == 2.0 * NEG_BIG).astype(BF16)
    r_i = lax.broadcasted_iota(jnp.int32, (tm, tm), 0)
    c_i = lax.broadcasted_iota(jnp.int32, (tm, tm), 1)
    lower = (r_i > c_i).astype(BF16)
    before = _dot(lower, sel) + counts
    counts = counts + jnp.sum(sel.astype(F32), axis=0, keepdims=True)

    route = jnp.zeros(logits.shape, F32)
    for k in range(TOP_K):
        rank = jnp.sum(jnp.where(lane == idxs[k], before, 0.0), axis=-1, keepdims=True)
        route = jnp.where(lane == k, exps[k] / den, route)
        route = jnp.where(lane == TOP_K + k, idxs[k].astype(F32), route)
        route = jnp.where(lane == 2 * TOP_K + k, rank, route)
    return route, counts


def _post_mix(o2, cb, z, sga, sgc, x2, g1, sc2, sh2, n2, conv_w, wao, wco, wout,
              wr_split, br_pad, seq):
    n, d = x2.shape
    tm = min(seq, 512)
    tpb = seq // tm
    halo = 16
    hb = tm // halo
    last_hb = n // halo - 1
    row = lambda i: (i, 0)
    const = lambda i: (0, 0)
    mod = lambda i: (i // tpb, 0, 0)
    wspec = pl.BlockSpec((d, d), const)
    kern = functools.partial(_post_mix_kernel, tpb=tpb)
    return pl.pallas_call(
        kern,
        out_shape=(jax.ShapeDtypeStruct((n, d), F32),
                   jax.ShapeDtypeStruct((n * TILE_ROW, LANES), jnp.uint32),
                   jax.ShapeDtypeStruct((n, LANES), F32), jax.ShapeDtypeStruct((8, LANES), F32)),
        grid=(n // tm,),
        in_specs=[pl.BlockSpec((tm, d), row), pl.BlockSpec((tm, d), row), pl.BlockSpec((tm, d), row),
                  pl.BlockSpec((halo, d), lambda i: (jnp.maximum(i * hb - 1, 0), 0)),
                  pl.BlockSpec((halo, d), lambda i: (jnp.minimum((i + 1) * hb, last_hb), 0)),
                  pl.BlockSpec((tm, d), row), pl.BlockSpec((tm, d), row), pl.BlockSpec((tm, d), row),
                  pl.BlockSpec((None, 1, d), mod), pl.BlockSpec((None, 1, d), mod),
                  pl.BlockSpec((None, 1, d), mod),
                  pl.BlockSpec((1, d), const), pl.BlockSpec((3, d), const),
                  wspec, wspec, wspec,
                  pl.BlockSpec((d, 2 * LANES), const), pl.BlockSpec((1, LANES), const)],
        out_specs=(pl.BlockSpec((tm, d), row), pl.BlockSpec((tm * TILE_ROW, LANES), row),
                   pl.BlockSpec((tm, LANES), row), pl.BlockSpec((8, LANES), const)),
        compiler_params=pltpu.CompilerParams(
            dimension_semantics=("arbitrary",), vmem_limit_bytes=VMEM_LIMIT),
        name="post_mix",
    )(o2, cb, z, z, z, sga, sgc, x2, g1, sc2, sh2, n2, conv_w, wao, wco, wout, wr_split, br_pad)


def _tile_rows(ref, row, n_rows=1):
    start = pl.multiple_of(row * TILE_ROW, TILE_ROW)
    return ref.at[pl.ds(start, n_rows * TILE_ROW)]


def _row_copy(src, dst, s, t, sem):
    return pltpu.make_async_copy(_tile_rows(src, s), _tile_rows(dst, t), sem)


def _dispatch_kernel(dest_ref, ends_ref, h2_ref, xs_hbm, zbuf, sem, zsem, *, tc, nblk):
    base = pl.program_id(0) * tc

    @pl.when(pl.program_id(0) == 0)
    def _():
        zbuf[...] = jnp.zeros_like(zbuf)

        def zero_block(row):
            return pltpu.make_async_copy(zbuf, _tile_rows(xs_hbm, row, MOE_ROWS), zsem)

        def nonempty(e):
            return ends_ref[e] > (ends_ref[e - 1] if e else 0)

        total = ends_ref[N_EXPERTS - 1]
        n_tail = nblk - total // MOE_ROWS

        def tail_start(b, carry):
            zero_block(total + b * MOE_ROWS).start()
            return carry

        def tail_wait(b, carry):
            zero_block(0).wait()
            return carry

        for e in range(N_EXPERTS):
            @pl.when(nonempty(e))
            def _(e=e):
                zero_block(ends_ref[e] - MOE_ROWS).start()
        lax.fori_loop(0, n_tail, tail_start, 0)
        for e in range(N_EXPERTS):
            @pl.when(nonempty(e))
            def _():
                zero_block(0).wait()
        lax.fori_loop(0, n_tail, tail_wait, 0)

    def issue(t, carry):
        for k in range(TOP_K):
            _row_copy(h2_ref, xs_hbm, t, dest_ref[(base + t) * TOP_K + k], sem).start(priority=k % 2)
        return carry

    lax.fori_loop(0, tc, issue, 0, unroll=4)

    for _ in range(TOP_K):
        pltpu.make_async_copy(h2_ref, _tile_rows(xs_hbm, 0, tc), sem).wait()


def _moe_dispatch(dest_flat, ends, h2_tiles, n, p_rows):
    tc = min(n, 512)
    kern = functools.partial(_dispatch_kernel, tc=tc, nblk=p_rows // MOE_ROWS)
    return pl.pallas_call(
        kern,
        out_shape=jax.ShapeDtypeStruct((p_rows * TILE_ROW, LANES), jnp.uint32),
        grid_spec=pltpu.PrefetchScalarGridSpec(
            num_scalar_prefetch=2, grid=(n // tc,),
            in_specs=[pl.BlockSpec((tc * TILE_ROW, LANES), lambda i, dr, en: (i, 0))],
            out_specs=pl.BlockSpec(memory_space=pl.ANY),
            scratch_shapes=[pltpu.VMEM((MOE_ROWS * TILE_ROW, LANES), jnp.uint32),
                            pltpu.SemaphoreType.DMA, pltpu.SemaphoreType.DMA]),
        compiler_params=pltpu.CompilerParams(dimension_semantics=("arbitrary",)),
        name="moe_dispatch",
    )(dest_flat, ends, h2_tiles)


def _ffn_kernel(first_ref, count_ref, slot_ref, next_ref, misc_ref,
                xs_hbm, wgu_hbm, wd_hbm, bg_ref, bu_ref, bd_ref, ys_hbm,
                wgu_buf, wd_buf, wg_scr, wu_scr, wd_scr, xbuf, ybuf,
                wsem, xsem, ysem, *, nblk):
    e = pl.program_id(0)
    first = first_ref[e]
    count = count_ref[e]
    slot = slot_ref[e]
    first_expert, total_blocks = misc_ref[0], misc_ref[1]

    def weight_chunk(ex, s, c):
        copies = []
        for hbm, buf, sem in ((wgu_hbm, wgu_buf, wsem.at[0, s]), (wd_hbm, wd_buf, wsem.at[1, s])):
            rows = hbm.shape[1] // WEIGHT_CHUNKS
            part = pl.ds(pl.multiple_of(c * rows, rows), rows)
            copies.append(pltpu.make_async_copy(hbm.at[ex, part], buf.at[s, part], sem))
        return copies

    def start_chunk(ex, s, c):
        gate_up, down = weight_chunk(ex, s, c)
        on_queue_1 = jnp.asarray(c, jnp.int32) < GATE_UP_CHUNKS_ON_QUEUE_1

        @pl.when(on_queue_1)
        def _():
            gate_up.start(priority=1)

        @pl.when(jnp.logical_not(on_queue_1))
        def _():
            gate_up.start(priority=0)

        down.start(priority=0)

    def x_copy(b, s):
        return pltpu.make_async_copy(_tile_rows(xs_hbm, b * MOE_ROWS, MOE_ROWS), xbuf.at[s], xsem.at[s])

    def y_copy(b, s):
        return pltpu.make_async_copy(ybuf.at[s], _tile_rows(ys_hbm, b * MOE_ROWS, MOE_ROWS), ysem.at[s])

    @pl.when(e == first_expert)
    def _():
        for c in range(WEIGHT_CHUNKS):
            start_chunk(e, 0, c)

    @pl.when(count > 0)
    def _():
        x_copy(first, 0).start()
        for c in range(WEIGHT_CHUNKS):
            for cp in weight_chunk(e, slot, c):
                cp.wait()
        next_expert = next_ref[e]
        has_next = next_expert >= 0

        d = wgu_buf.shape[1]
        for c in range(d // LANES):
            cols = slice(c * LANES, (c + 1) * LANES)
            words = pltpu.bitcast(wgu_buf[slot, cols, :].astype(BF16).T, jnp.uint32)
            wg_scr[cols, :] = pltpu.bitcast(words << 16, F32).astype(BF16).T
            wu_scr[cols, :] = pltpu.bitcast(words & jnp.uint32(_HIGH_HALF), F32).astype(BF16).T
        wd_scr[...] = wd_buf[slot].astype(BF16)
        bg, bu, bd = bg_ref[e], bu_ref[e], bd_ref[e]

        def block(j, carry):
            s = j % 2
            x_copy(first, s).wait()

            @pl.when(j + 1 < count)
            def _():
                x_copy(first + j + 1, 1 - s).start()

            @pl.when(j >= 2)
            def _():
                y_copy(first, s).wait()

            @pl.when(jnp.logical_and(has_next, j < WEIGHT_CHUNKS))
            def _():
                start_chunk(next_expert, 1 - slot, j)

            x = jnp.concatenate(_load_packed_rows(xbuf.at[s], MOE_ROWS), axis=1).astype(BF16)
            gate = _dot(x, wg_scr[...]) + bg
            up = _dot(x, wu_scr[...]) + bu
            gate = jnp.minimum(gate, SWIGLU_LIMIT)
            up = jnp.clip(up, -SWIGLU_LIMIT, SWIGLU_LIMIT)
            glu = gate * jax.nn.sigmoid(gate * SWIGLU_ALPHA)
            mid = ((up + 1.0) * glu).astype(BF16)
            _store_packed_rows(ybuf.at[s], 0, _dot(mid, wd_scr[...]) + bd)
            y_copy(first + j, s).start(priority=1)
            return carry

        lax.fori_loop(0, count, block, 0)

        @pl.when(has_next)
        def _():
            def rest(c, carry):
                start_chunk(next_expert, 1 - slot, c)
                return carry

            lax.fori_loop(jnp.minimum(count, WEIGHT_CHUNKS), WEIGHT_CHUNKS, rest, 0)

        @pl.when(count >= 2)
        def _():
            y_copy(first, count % 2).wait()
        y_copy(first, (count - 1) % 2).wait()

    @pl.when(e == pl.num_programs(0) - 1)
    def _():
        ybuf[0] = jnp.zeros(ybuf.shape[1:], ybuf.dtype)

        def tail_start(b, carry):
            y_copy(b, 0).start()
            return carry

        def tail_wait(b, carry):
            y_copy(b, 0).wait()
            return carry

        lax.fori_loop(total_blocks, nblk, tail_start, 0)
        lax.fori_loop(total_blocks, nblk, tail_wait, 0)


def _moe_ffn(first_blk, n_blk, slot, next_e, misc, xs, wgu, wd, bg, bu, bd):
    p_rows = xs.shape[0] // TILE_ROW
    n_exp, d, f2 = wgu.shape
    f = f2 // 2
    assert d == 2 * TILE_ROW * LANES
    whole = lambda shape: pl.BlockSpec(shape, lambda e, *_: (0,) * len(shape))
    tile = (MOE_ROWS * TILE_ROW, LANES)
    kern = functools.partial(_ffn_kernel, nblk=p_rows // MOE_ROWS)
    return pl.pallas_call(
        kern,
        out_shape=jax.ShapeDtypeStruct((p_rows * TILE_ROW, LANES), jnp.uint32),
        grid_spec=pltpu.PrefetchScalarGridSpec(
            num_scalar_prefetch=5, grid=(n_exp,),
            in_specs=[pl.BlockSpec(memory_space=pl.ANY),
                      pl.BlockSpec(memory_space=pl.ANY), pl.BlockSpec(memory_space=pl.ANY),
                      whole(bg.shape), whole(bu.shape), whole(bd.shape)],
            out_specs=pl.BlockSpec(memory_space=pl.ANY),
            scratch_shapes=[pltpu.VMEM((2, d, f2), F32), pltpu.VMEM((2, f, d), F32),
                            pltpu.VMEM((d, f), BF16), pltpu.VMEM((d, f), BF16),
                            pltpu.VMEM((f, d), BF16),
                            pltpu.VMEM((2,) + tile, jnp.uint32), pltpu.VMEM((2,) + tile, jnp.uint32),
                            pltpu.SemaphoreType.DMA((2, 2)), pltpu.SemaphoreType.DMA((2,)),
                            pltpu.SemaphoreType.DMA((2,))]),
        compiler_params=pltpu.CompilerParams(
            dimension_semantics=("arbitrary",), vmem_limit_bytes=VMEM_LIMIT),
        name="moe_ffn",
    )(first_blk, n_blk, slot, next_e, misc, xs, wgu, wd, bg, bu, bd)


def _combine_kernel(dest_ref, ys_hbm, x1_ref, route_ref, g2_ref, o_ref, buf, sem, *, tc, n_steps):
    i = pl.program_id(0)
    slot = i % 2

    def gather(step, s):
        def issue(t, carry):
            for k in range(TOP_K):
                row = dest_ref[(step * tc + t) * TOP_K + k]
                _row_copy(ys_hbm, buf.at[s, k], row, t, sem.at[s]).start(priority=k % 2)
            return carry

        lax.fori_loop(0, tc, issue, 0, unroll=4)

    @pl.when(i == 0)
    def _():
        gather(0, 0)

    @pl.when(i + 1 < n_steps)
    def _():
        gather(i + 1, 1 - slot)

    for k in range(TOP_K):
        pltpu.make_async_copy(_tile_rows(ys_hbm, 0, tc), buf.at[slot, k], sem.at[slot]).wait()

    route = route_ref[...]
    chunks = [_load_packed_rows(buf.at[slot, k], tc) for k in range(TOP_K)]
    for c in range(2 * TILE_ROW):
        cols = slice(c * LANES, (c + 1) * LANES)
        y = chunks[0][c] * route[:, 0:1]
        for k in range(1, TOP_K):
            y = y + chunks[k][c] * route[:, k:k + 1]
        o_ref[:, cols] = x1_ref[:, cols] + g2_ref[:, cols] * y


def _moe_combine(dest_flat, ys, x1, route, g2, seq):
    n, d = x1.shape
    assert d == 2 * TILE_ROW * LANES
    tc = min(seq, 256)
    tpb = seq // tc
    kern = functools.partial(_combine_kernel, tc=tc, n_steps=n // tc)
    return pl.pallas_call(
        kern,
        out_shape=jax.ShapeDtypeStruct((n, d), F32),
        grid_spec=pltpu.PrefetchScalarGridSpec(
            num_scalar_prefetch=1, grid=(n // tc,),
            in_specs=[pl.BlockSpec(memory_space=pl.ANY),
                      pl.BlockSpec((tc, d), lambda i, dr: (i, 0)),
                      pl.BlockSpec((tc, LANES), lambda i, dr: (i, 0)),
                      pl.BlockSpec((None, 1, d), lambda i, dr: (i // tpb, 0, 0))],
            out_specs=pl.BlockSpec((tc, d), lambda i, dr: (i, 0)),
            scratch_shapes=[pltpu.VMEM((2, TOP_K, tc * TILE_ROW, LANES), jnp.uint32),
                            pltpu.SemaphoreType.DMA((2,))]),
        compiler_params=pltpu.CompilerParams(
            dimension_semantics=("arbitrary",), vmem_limit_bytes=VMEM_LIMIT),
        name="moe_combine",
    )(dest_flat, ys, x1, route, g2)


def _rope_tables(seq):
    inv_freq = (ROPE_THETA ** (-np.arange(0, HEAD_DIM, 2, dtype=np.float32) / HEAD_DIM)).astype(np.float32)
    ang = (inv_freq[:, None] * np.arange(seq, dtype=np.float32)[None, :]).astype(np.float32)
    cos, sin = np.cos(ang.astype(np.float64)), np.sin(ang.astype(np.float64))
    cos_t = np.concatenate([cos, cos], axis=0).astype(np.float32)
    sin_t = np.concatenate([-sin, sin], axis=0).astype(np.float32)
    return jnp.asarray(cos_t), jnp.asarray(sin_t)


def _layer(x, c, l, lambda_init, w_ada, b_ada, norm1_w, w_in, q_norm_w, k_norm_w, lambda_q1,
           lambda_k1, lambda_q2, lambda_k2, subln_w, w_attn_o, conv_w, w_conv_o, w_out, norm2_w,
           w_router, b_router, w_gate_up, b_gate_up, w_down, b_down):
    bsz, seq, d = x.shape
    n = bsz * seq
    x2 = x.reshape(n, d)

    mod = _ada_mod(c, w_ada[l], b_ada[l])
    sh1, sc1, g1, sh2, sc2, g2 = [m.reshape(bsz, 1, d) for m in jnp.split(mod, 6, axis=-1)]

    cos_t, sin_t = _rope_tables(seq)
    qt, k, vt, cb, z, sga, sgc, qn, kn = _in_proj(
        x2, norm1_w[l].reshape(1, d), sc1, sh1, w_in[l].astype(BF16),
        q_norm_w[l].reshape(HEAD_DIM, 1) * Q_SCALE, k_norm_w[l].reshape(HEAD_DIM, 1),
        cos_t, sin_t, bsz, seq)

    lam_vecs = jnp.stack([lambda_q1[l], lambda_k1[l], lambda_q2[l], lambda_k2[l]]).astype(F32)
    o = _diff_attn(qt, k.reshape(bsz, seq, d), vt, qn, kn, lam_vecs, subln_w[l].reshape(V_DIM, 1),
                   lambda_init)

    pad = LANES - N_EXPERTS
    wr = jnp.pad(w_router[l].astype(F32), ((0, 0), (0, pad)))
    wr_hi = wr.astype(BF16)
    wr_split = jnp.concatenate([wr_hi, (wr - wr_hi.astype(F32)).astype(BF16)], axis=1)
    br_pad = jnp.pad(b_router[l].astype(F32), (0, pad), constant_values=NEG_BIG).reshape(1, LANES)
    x1, h2, route, cnt = _post_mix(
        o.reshape(n, d), cb, z, sga, sgc, x2, g1, sc2, sh2, norm2_w[l].reshape(1, d), conv_w[l],
        w_attn_o[l].astype(BF16), w_conv_o[l].astype(BF16), w_out[l].astype(BF16),
        wr_split, br_pad, seq)

    counts = cnt[0, :N_EXPERTS].astype(jnp.int32)
    padded = ((counts + MOE_ROWS - 1) // MOE_ROWS) * MOE_ROWS
    ends = jnp.cumsum(padded)
    start = ends - padded
    p_rows = n * TOP_K + N_EXPERTS * MOE_ROWS
    top_e = route[:, TOP_K:2 * TOP_K].astype(jnp.int32)
    rank = route[:, 2 * TOP_K:3 * TOP_K].astype(jnp.int32)
    experts = jnp.arange(N_EXPERTS, dtype=jnp.int32)
    seg_start = jnp.sum(jnp.where(top_e[..., None] == experts, start, 0), axis=-1)
    dest = (seg_start + rank).reshape(-1)
    nonempty = padded > 0
    later = jnp.logical_and(nonempty[None, :], experts[None, :] > experts[:, None])
    next_e = jnp.min(jnp.where(later, experts[None, :], N_EXPERTS), axis=1)
    next_e = jnp.where(next_e == N_EXPERTS, -1, next_e).astype(jnp.int32)
    slot = ((jnp.cumsum(nonempty.astype(jnp.int32)) - 1) % 2).astype(jnp.int32)
    first_blk = (start // MOE_ROWS).astype(jnp.int32)
    n_blk = (padded // MOE_ROWS).astype(jnp.int32)
    misc = jnp.stack([jnp.argmax(nonempty).astype(jnp.int32), (ends[-1] // MOE_ROWS).astype(jnp.int32)])

    xs = _moe_dispatch(dest, ends.astype(jnp.int32), h2, n, p_rows)
    ys = _moe_ffn(first_blk, n_blk, slot, next_e, misc, xs, w_gate_up[l], w_down[l],
                  b_gate_up[l][:, None, 0::2], b_gate_up[l][:, None, 1::2], b_down[l][:, None, :])
    out = _moe_combine(dest, ys, x1, route, g2, seq)
    return out.reshape(bsz, seq, d)


def kernel(x, c, w_ada, b_ada, norm1_w, w_in, q_norm_w, k_norm_w, lambda_q1, lambda_k1, lambda_q2,
           lambda_k2, subln_w, w_attn_o, conv_w, w_conv_o, w_out, norm2_w, w_router, b_router,
           w_gate_up, b_gate_up, w_down, b_down):
    depth = w_ada.shape[0]
    for l in range(depth):
        lambda_init = 0.8 - 0.6 * math.exp(-0.3 * l)
        x = _layer(x, c, l, lambda_init, w_ada, b_ada, norm1_w, w_in, q_norm_w, k_norm_w,
                   lambda_q1, lambda_k1, lambda_q2, lambda_k2, subln_w, w_attn_o, conv_w,
                   w_conv_o, w_out, norm2_w, w_router, b_router, w_gate_up, b_gate_up,
                   w_down, b_down)
    return x
```

```python
import functools
import math

import jax
import jax.numpy as jnp
import numpy as np
from jax import lax
from jax.experimental import pallas as pl
from jax.experimental.pallas import tpu as pltpu

N_HEADS = 8
HEAD_DIM = 64
V_DIM = 2 * HEAD_DIM
IN_BLOCKS = ("q", "k", "v", "conv_b", "conv_c", "conv_x", "gate_attn", "gate_conv")
N_EXPERTS = 32
TOP_K = 4
SWIGLU_LIMIT = 7.0
SWIGLU_ALPHA = 1.702
ROPE_THETA = 10000.0
RMS_EPS = 1e-6
SUBLN_EPS = 1e-5
LANES = 128
TILE_ROW = 4
MOE_ROWS = 256
NEG_BIG = -1e30
LOG2E = 1.4426950408889634
Q_SCALE = LOG2E / math.sqrt(HEAD_DIM)
SAFE_EXP2_BOUND = 80.0
VMEM_LIMIT = 56 * 1024 * 1024

F32 = jnp.float32
BF16 = jnp.bfloat16


def _dot(a, b):
    return jnp.dot(a, b, preferred_element_type=F32)


_HIGH_HALF = 0xFFFF0000


def _bf16_bits(x):
    return pltpu.bitcast(x.astype(BF16).astype(F32), jnp.uint32)


def _store_packed_rows(ref, row0, val):
    rows, d = val.shape
    half = d // 2
    for c in range(TILE_ROW):
        lo = _bf16_bits(val[:, c * LANES:(c + 1) * LANES]) >> 16
        hi = _bf16_bits(val[:, half + c * LANES:half + (c + 1) * LANES])
        ref[pl.ds(row0 * TILE_ROW + c, rows, stride=TILE_ROW), :] = lo | hi


def _load_packed_rows(ref, rows):
    words = [ref[pl.ds(c, rows, stride=TILE_ROW), :] for c in range(TILE_ROW)]
    lo = [pltpu.bitcast(w << 16, F32) for w in words]
    hi = [pltpu.bitcast(w & jnp.uint32(_HIGH_HALF), F32) for w in words]
    return lo + hi


def _ada_kernel(ct_ref, w_ref, b_ref, o_ref):
    ct = ct_ref[...]
    s = ct * jax.nn.sigmoid(ct)
    w = w_ref[...]
    for b in range(ct.shape[1]):
        o_ref[b:b + 1, :] = jnp.sum(w * s[:, b:b + 1], axis=0, keepdims=True) + b_ref[...]


def _ada_mod(c, w_ada, b_ada):
    bsz, d = c.shape
    n = w_ada.shape[1]
    tn = min(n, 1536)
    return pl.pallas_call(
        _ada_kernel,
        out_shape=jax.ShapeDtypeStruct((bsz, n), F32),
        grid=(n // tn,),
        in_specs=[pl.BlockSpec((d, bsz), lambda j: (0, 0)),
                  pl.BlockSpec((d, tn), lambda j: (0, j)),
                  pl.BlockSpec((1, tn), lambda j: (0, j))],
        out_specs=pl.BlockSpec((bsz, tn), lambda j: (0, j)),
        compiler_params=pltpu.CompilerParams(dimension_semantics=("arbitrary",)),
        name="ada_mod",
    )(c.T, w_ada, b_ada.reshape(1, n))


def _qk_norm_rope_t(y, g_col, cos_t, sin_t):
    tm, w = y.shape
    yt = y.T.reshape(w // HEAD_DIM, HEAD_DIM, tm)
    ms = jnp.mean(yt * yt, axis=1, keepdims=True)
    yn = yt * lax.rsqrt(ms + RMS_EPS) * g_col[None]
    half = HEAD_DIM // 2
    swapped = jnp.concatenate([yn[:, half:, :], yn[:, :half, :]], axis=1)
    out = yn * cos_t[None] + swapped * sin_t[None]
    norm2 = jnp.sum(out * out, axis=1).reshape(N_HEADS, 2, tm)
    return out.reshape(w, tm), norm2


def _in_proj_kernel(x_ref, n1_ref, sc_ref, sh_ref, w_ref, gq_ref, gk_ref, cos_ref, sin_ref,
                    qt_ref, k_ref, vt_ref, cb_ref, z_ref, sga_ref, sgc_ref, qn_ref, kn_ref, h_scr):
    x = x_ref[...]
    xn = x * lax.rsqrt(jnp.mean(x * x, axis=-1, keepdims=True) + RMS_EPS) * n1_ref[...]
    h_scr[...] = (xn * (1.0 + sc_ref[...]) + sh_ref[...]).astype(BF16)
    wcol = w_ref.shape[1] // len(IN_BLOCKS)

    def proj(name):
        c = IN_BLOCKS.index(name)
        return _dot(h_scr[...], w_ref[:, c * wcol:(c + 1) * wcol])

    qt, qn = _qk_norm_rope_t(proj("q"), gq_ref[...], cos_ref[...], sin_ref[...])
    qt_ref[...] = qt.astype(BF16)
    qn_ref[...] = qn
    kt, kn = _qk_norm_rope_t(proj("k"), gk_ref[...], cos_ref[...], sin_ref[...])
    k_ref[...] = kt.T.astype(BF16)
    kn_ref[...] = kn
    vt_ref[...] = proj("v").T.astype(BF16)
    cb_ref[...] = proj("conv_b").astype(BF16)
    z_ref[...] = (proj("conv_c") * proj("conv_x")).astype(BF16)
    sga_ref[...] = jax.nn.sigmoid(proj("gate_attn")).astype(BF16)
    sgc_ref[...] = jax.nn.sigmoid(proj("gate_conv")).astype(BF16)


def _in_proj(x2, n1, sc1, sh1, w_in_bf, gq, gk, cos_t, sin_t, bsz, seq):
    n, d = x2.shape
    tm = min(seq, 512)
    tpb = seq // tm
    assert w_in_bf.shape[1] == len(IN_BLOCKS) * d
    row = lambda i: (i, 0)
    const = lambda i: (0, 0)
    tcol = lambda i: (i // tpb, 0, i % tpb)
    mod = lambda i: (i // tpb, 0, 0)
    nat = jax.ShapeDtypeStruct((n, d), BF16)
    tr = jax.ShapeDtypeStruct((bsz, d, seq), BF16)
    nrm = jax.ShapeDtypeStruct((bsz, N_HEADS, 2, seq), F32)
    nat_spec = pl.BlockSpec((tm, d), row)
    tr_spec = pl.BlockSpec((None, d, tm), tcol)
    nrm_spec = pl.BlockSpec((None, N_HEADS, 2, tm), lambda i: (i // tpb, 0, 0, i % tpb))
    return pl.pallas_call(
        _in_proj_kernel,
        out_shape=(tr, nat, tr, nat, nat, nat, nat, nrm, nrm),
        grid=(n // tm,),
        in_specs=[pl.BlockSpec((tm, d), row),
                  pl.BlockSpec((1, d), const),
                  pl.BlockSpec((None, 1, d), mod),
                  pl.BlockSpec((None, 1, d), mod),
                  pl.BlockSpec(w_in_bf.shape, const),
                  pl.BlockSpec((HEAD_DIM, 1), const),
                  pl.BlockSpec((HEAD_DIM, 1), const),
                  pl.BlockSpec((HEAD_DIM, tm), lambda i: (0, i % tpb)),
                  pl.BlockSpec((HEAD_DIM, tm), lambda i: (0, i % tpb))],
        out_specs=(tr_spec, nat_spec, tr_spec, nat_spec, nat_spec, nat_spec, nat_spec,
                   nrm_spec, nrm_spec),
        scratch_shapes=[pltpu.VMEM((tm, d), BF16)],
        compiler_params=pltpu.CompilerParams(
            dimension_semantics=("arbitrary",), vmem_limit_bytes=VMEM_LIMIT),
        name="in_proj",
    )(x2, n1, sc1, sh1, w_in_bf, gq, gk, cos_t, sin_t)


def _diff_attn_kernel(qt_ref, k_ref, vt_ref, qn_ref, kn_ref, lam_ref, sw_ref, o_ref, acc0, acc1,
                      *, tk, lambda_init):
    seq = k_ref.shape[0]
    tq = qt_ref.shape[1]
    qt = qt_ref[...]
    first = lax.broadcasted_iota(jnp.int32, qt.shape, 0) < HEAD_DIM
    zero = jnp.zeros_like(qt)
    qz = (jnp.where(first, qt, zero), jnp.where(first, zero, qt))
    accs = (acc0, acc1)
    acc0[...] = jnp.zeros_like(acc0)
    acc1[...] = jnp.zeros_like(acc1)
    n_chunks = seq // tk

    def load(j):
        off = pl.multiple_of(j * tk, tk)
        return k_ref[pl.ds(off, tk), :], vt_ref[:, pl.ds(off, tk)]

    def plain_body(j, carry):
        kk, vt = load(j)
        new = []
        for c in range(2):
            p = jnp.exp2(_dot(kk, qz[c]))
            new.append(carry[c] + jnp.sum(p, axis=0, keepdims=True))
            accs[c][...] += _dot(vt, p.astype(BF16))
        return tuple(new)

    def online_body(j, carry):
        kk, vt = load(j)
        new = []
        for c in range(2):
            m, l = carry[2 * c], carry[2 * c + 1]
            s = _dot(kk, qz[c])
            m_new = jnp.maximum(m, jnp.max(s, axis=0, keepdims=True))
            alpha = jnp.exp2(m - m_new)
            p = jnp.exp2(s - m_new)
            l = alpha * l + jnp.sum(p, axis=0, keepdims=True)
            accs[c][...] = alpha * accs[c][...] + _dot(vt, p.astype(BF16))
            new += [m_new, l]
        return tuple(new)

    m_init = jnp.full((1, tq), NEG_BIG, F32)
    l_init = jnp.zeros((1, tq), F32)

    def plain():
        return lax.fori_loop(0, n_chunks, plain_body, (l_init, l_init))

    def online():
        _, l0, _, l1 = lax.fori_loop(0, n_chunks, online_body, (m_init, l_init, m_init, l_init))
        return l0, l1

    bound2 = jnp.max(jnp.max(qn_ref[...], axis=-1, keepdims=True)
                     * jnp.max(kn_ref[...], axis=-1, keepdims=True))
    l0, l1 = lax.cond(bound2 <= SAFE_EXP2_BOUND * SAFE_EXP2_BOUND, plain, online)

    lq = lam_ref[...]
    lam = (jnp.exp(jnp.sum(lq[0:1] * lq[1:2], axis=-1, keepdims=True))
           - jnp.exp(jnp.sum(lq[2:3] * lq[3:4], axis=-1, keepdims=True)) + lambda_init)
    o = acc0[...] / l0 - lam * (acc1[...] / l1)
    o = o * lax.rsqrt(jnp.mean(o * o, axis=0, keepdims=True) + SUBLN_EPS)
    o = o * sw_ref[...] * (1.0 - lambda_init)
    o_ref[...] = o.T.astype(BF16)


def _diff_attn(qt, k3, vt, qn, kn, lam_vecs, subln_col, lambda_init):
    bsz, d, seq = qt.shape
    tq = min(seq, 1024)
    tk = min(seq, 4096)
    kern = functools.partial(_diff_attn_kernel, tk=tk, lambda_init=lambda_init)
    return pl.pallas_call(
        kern,
        out_shape=jax.ShapeDtypeStruct((bsz, seq, d), BF16),
        grid=(bsz, N_HEADS, seq // tq),
        in_specs=[pl.BlockSpec((None, V_DIM, tq), lambda b, h, i: (b, h, i)),
                  pl.BlockSpec((None, seq, V_DIM), lambda b, h, i: (b, 0, h)),
                  pl.BlockSpec((None, V_DIM, seq), lambda b, h, i: (b, h, 0)),
                  pl.BlockSpec((None, None, 2, tq), lambda b, h, i: (b, h, 0, i)),
                  pl.BlockSpec((None, None, 2, seq), lambda b, h, i: (b, h, 0, 0)),
                  pl.BlockSpec((4, HEAD_DIM), lambda b, h, i: (0, 0)),
                  pl.BlockSpec((V_DIM, 1), lambda b, h, i: (0, 0))],
        out_specs=pl.BlockSpec((None, tq, V_DIM), lambda b, h, i: (b, i, h)),
        scratch_shapes=[pltpu.VMEM((V_DIM, tq), F32), pltpu.VMEM((V_DIM, tq), F32)],
        compiler_params=pltpu.CompilerParams(
            dimension_semantics=("arbitrary", "arbitrary", "arbitrary"),
            vmem_limit_bytes=VMEM_LIMIT),
        name="diff_attn",
    )(qt, k3, vt, qn, kn, lam_vecs, subln_col)


def _post_mix_kernel(o_ref, cb_ref, z_ref, zp_ref, zn_ref, sga_ref, sgc_ref, x_ref,
                     g1_ref, sc2_ref, sh2_ref, n2_ref, cw_ref, wao_ref, wco_ref, wout_ref,
                     wr_ref, br_ref,
                     x1_ref, h2_ref, route_ref, cnt_ref, *, tpb):
    i = pl.program_id(0)
    tm = x_ref.shape[0]

    @pl.when(i == 0)
    def _():
        cnt_ref[...] = jnp.zeros_like(cnt_ref)

    z = z_ref[...].astype(F32)
    rows = lax.broadcasted_iota(jnp.int32, z.shape, 0)
    halo_rows = zp_ref.shape[0]
    prev_row = zp_ref[halo_rows - 1:halo_rows, :].astype(F32)
    next_row = zn_ref[0:1, :].astype(F32)
    prev_row = jnp.where(i % tpb == 0, jnp.zeros_like(prev_row), prev_row)
    next_row = jnp.where(i % tpb == tpb - 1, jnp.zeros_like(next_row), next_row)
    z_m1 = jnp.where(rows == 0, prev_row, pltpu.roll(z, 1, 0))
    z_p1 = jnp.where(rows == tm - 1, next_row, pltpu.roll(z, tm - 1, 0))
    cw = cw_ref[...]
    conv = z_m1 * cw[0:1] + z * cw[1:2] + z_p1 * cw[2:3]
    u = (cb_ref[...].astype(F32) * conv).astype(BF16)

    y_attn = _dot(o_ref[...], wao_ref[...])
    y_conv = _dot(u, wco_ref[...])
    m = sga_ref[...].astype(F32) * y_attn + sgc_ref[...].astype(F32) * y_conv
    x1 = x_ref[...] + g1_ref[...] * _dot(m.astype(BF16), wout_ref[...])
    x1_ref[...] = x1

    h2 = x1 * lax.rsqrt(jnp.mean(x1 * x1, axis=-1, keepdims=True) + RMS_EPS) * n2_ref[...]
    h2 = h2 * (1.0 + sc2_ref[...]) + sh2_ref[...]
    _store_packed_rows(h2_ref, 0, h2)

    route, counts = _route_tile(h2, wr_ref, br_ref, cnt_ref[0:1, :])
    route_ref[...] = route
    cnt_ref[...] = jnp.broadcast_to(counts, cnt_ref.shape)


def _route_tile(h2, wr_ref, br_ref, counts):
    tm = h2.shape[0]
    h_hi = h2.astype(BF16)
    h_lo = (h2 - h_hi.astype(F32)).astype(BF16)
    hi_both = _dot(h_hi, wr_ref[...])
    logits = (hi_both[:, :LANES] + hi_both[:, LANES:] + _dot(h_lo, wr_ref[:, :LANES])
              + br_ref[...])

    lane = lax.broadcasted_iota(jnp.int32, logits.shape, 1)
    work = logits
    vals, idxs = [], []
    for _ in range(TOP_K):
        mx = jnp.max(work, axis=-1, keepdims=True)
        ix = jnp.min(jnp.where(work == mx, lane, LANES), axis=-1, keepdims=True)
        vals.append(mx)
        idxs.append(ix)
        work = jnp.where(lane == ix, 2.0 * NEG_BIG, work)
    exps = [jnp.exp(v - vals[0]) for v in vals]
    den = exps[0] + exps[1] + exps[2] + exps[3]

    sel = (work == 2.0 * NEG_BIG).astype(BF16)
    r_i = lax.broadcasted_iota(jnp.int32, (tm, tm), 0)
    c_i = lax.broadcasted_iota(jnp.int32, (tm, tm), 1)
    lower = (r_i > c_i).astype(BF16)
    before = _dot(lower, sel) + counts
    counts = counts + jnp.sum(sel.astype(F32), axis=0, keepdims=True)

    route = jnp.zeros(logits.shape, F32)
    for k in range(TOP_K):
        rank = jnp.sum(jnp.where(lane == idxs[k], before, 0.0), axis=-1, keepdims=True)
        route = jnp.where(lane == k, exps[k] / den, route)
        route = jnp.where(lane == TOP_K + k, idxs[k].astype(F32), route)
        route = jnp.where(lane == 2 * TOP_K + k, rank, route)
    return route, counts


def _post_mix(o2, cb, z, sga, sgc, x2, g1, sc2, sh2, n2, conv_w, wao, wco, wout,
              wr_split, br_pad, seq):
    n, d = x2.shape
    tm = min(seq, 512)
    tpb = seq // tm
    halo = 16
    hb = tm // halo
    last_hb = n // halo - 1
    row = lambda i: (i, 0)
    const = lambda i: (0, 0)
    mod = lambda i: (i // tpb, 0, 0)
    wspec = pl.BlockSpec((d, d), const)
    kern = functools.partial(_post_mix_kernel, tpb=tpb)
    return pl.pallas_call(
        kern,
        out_shape=(jax.ShapeDtypeStruct((n, d), F32),
                   jax.ShapeDtypeStruct((n * TILE_ROW, LANES), jnp.uint32),
                   jax.ShapeDtypeStruct((n, LANES), F32), jax.ShapeDtypeStruct((8, LANES), F32)),
        grid=(n // tm,),
        in_specs=[pl.BlockSpec((tm, d), row), pl.BlockSpec((tm, d), row), pl.BlockSpec((tm, d), row),
                  pl.BlockSpec((halo, d), lambda i: (jnp.maximum(i * hb - 1, 0), 0)),
                  pl.BlockSpec((halo, d), lambda i: (jnp.minimum((i + 1) * hb, last_hb), 0)),
                  pl.BlockSpec((tm, d), row), pl.BlockSpec((tm, d), row), pl.BlockSpec((tm, d), row),
                  pl.BlockSpec((None, 1, d), mod), pl.BlockSpec((None, 1, d), mod),
                  pl.BlockSpec((None, 1, d), mod),
                  pl.BlockSpec((1, d), const), pl.BlockSpec((3, d), const),
                  wspec, wspec, wspec,
                  pl.BlockSpec((d, 2 * LANES), const), pl.BlockSpec((1, LANES), const)],
        out_specs=(pl.BlockSpec((tm, d), row), pl.BlockSpec((tm * TILE_ROW, LANES), row),
                   pl.BlockSpec((tm, LANES), row), pl.BlockSpec((8, LANES), const)),
        compiler_params=pltpu.CompilerParams(
            dimension_semantics=("arbitrary",), vmem_limit_bytes=VMEM_LIMIT),
        name="post_mix",
    )(o2, cb, z, z, z, sga, sgc, x2, g1, sc2, sh2, n2, conv_w, wao, wco, wout, wr_split, br_pad)


def _tile_rows(ref, row, n_rows=1):
    start = pl.multiple_of(row * TILE_ROW, TILE_ROW)
    return ref.at[pl.ds(start, n_rows * TILE_ROW)]


def _row_copy(src, dst, s, t, sem):
    return pltpu.make_async_copy(_tile_rows(src, s), _tile_rows(dst, t), sem)


def _dispatch_kernel(dest_ref, ends_ref, h2_ref, xs_hbm, zbuf, sem, zsem, *, tc, nblk):
    base = pl.program_id(0) * tc

    @pl.when(pl.program_id(0) == 0)
    def _():
        zbuf[...] = jnp.zeros_like(zbuf)

        def zero_block(row):
            return pltpu.make_async_copy(zbuf, _tile_rows(xs_hbm, row, MOE_ROWS), zsem)

        def nonempty(e):
            return ends_ref[e] > (ends_ref[e - 1] if e else 0)

        total = ends_ref[N_EXPERTS - 1]
        n_tail = nblk - total // MOE_ROWS

        def tail_start(b, carry):
            zero_block(total + b * MOE_ROWS).start()
            return carry

        def tail_wait(b, carry):
            zero_block(0).wait()
            return carry

        for e in range(N_EXPERTS):
            @pl.when(nonempty(e))
            def _(e=e):
                zero_block(ends_ref[e] - MOE_ROWS).start()
        lax.fori_loop(0, n_tail, tail_start, 0)
        for e in range(N_EXPERTS):
            @pl.when(nonempty(e))
            def _():
                zero_block(0).wait()
        lax.fori_loop(0, n_tail, tail_wait, 0)

    def issue(t, carry):
        for k in range(TOP_K):
            _row_copy(h2_ref, xs_hbm, t, dest_ref[(base + t) * TOP_K + k], sem).start(priority=k % 2)
        return carry

    lax.fori_loop(0, tc, issue, 0, unroll=4)

    for _ in range(TOP_K):
        pltpu.make_async_copy(h2_ref, _tile_rows(xs_hbm, 0, tc), sem).wait()


def _moe_dispatch(dest_flat, ends, h2_tiles, n, p_rows):
    tc = min(n, 512)
    kern = functools.partial(_dispatch_kernel, tc=tc, nblk=p_rows // MOE_ROWS)
    return pl.pallas_call(
        kern,
        out_shape=jax.ShapeDtypeStruct((p_rows * TILE_ROW, LANES), jnp.uint32),
        grid_spec=pltpu.PrefetchScalarGridSpec(
            num_scalar_prefetch=2, grid=(n // tc,),
            in_specs=[pl.BlockSpec((tc * TILE_ROW, LANES), lambda i, dr, en: (i, 0))],
            out_specs=pl.BlockSpec(memory_space=pl.ANY),
            scratch_shapes=[pltpu.VMEM((MOE_ROWS * TILE_ROW, LANES), jnp.uint32),
                            pltpu.SemaphoreType.DMA, pltpu.SemaphoreType.DMA]),
        compiler_params=pltpu.CompilerParams(dimension_semantics=("arbitrary",)),
        name="moe_dispatch",
    )(dest_flat, ends, h2_tiles)


def _ffn_kernel(first_ref, count_ref, slot_ref, next_ref, misc_ref,
                xs_hbm, wgu_hbm, wd_hbm, bg_ref, bu_ref, bd_ref, ys_hbm,
                wgu_buf, wd_buf, wg_scr, wu_scr, wd_scr, xbuf, ybuf,
                wsem, xsem, ysem, *, nblk):
    e = pl.program_id(0)
    first = first_ref[e]
    count = count_ref[e]
    slot = slot_ref[e]
    first_expert, total_blocks = misc_ref[0], misc_ref[1]

    def weight_copies(ex, s):
        return (pltpu.make_async_copy(wgu_hbm.at[ex], wgu_buf.at[s], wsem.at[0, s]),
                pltpu.make_async_copy(wd_hbm.at[ex], wd_buf.at[s], wsem.at[1, s]))

    def x_copy(b, s):
        return pltpu.make_async_copy(_tile_rows(xs_hbm, b * MOE_ROWS, MOE_ROWS), xbuf.at[s], xsem.at[s])

    def y_copy(b):
        return pltpu.make_async_copy(ybuf.at[b % 2], _tile_rows(ys_hbm, b * MOE_ROWS, MOE_ROWS),
                                     ysem.at[b % 2])

    @pl.when(e == first_expert)
    def _():
        for cp in weight_copies(e, 0):
            cp.start()

    @pl.when(count > 0)
    def _():
        x_copy(first, 0).start()
        for cp in weight_copies(e, slot):
            cp.wait()

        @pl.when(next_ref[e] >= 0)
        def _():
            for cp in weight_copies(next_ref[e], 1 - slot):
                cp.start(priority=1)

        d = wgu_buf.shape[1]
        for c in range(d // LANES):
            cols = slice(c * LANES, (c + 1) * LANES)
            words = pltpu.bitcast(wgu_buf[slot, cols, :].astype(BF16).T, jnp.uint32)
            wg_scr[cols, :] = pltpu.bitcast(words << 16, F32).astype(BF16).T
            wu_scr[cols, :] = pltpu.bitcast(words & jnp.uint32(_HIGH_HALF), F32).astype(BF16).T
        wd_scr[...] = wd_buf[slot].astype(BF16)
        bg, bu, bd = bg_ref[e], bu_ref[e], bd_ref[e]

        def block(j, carry):
            s = j % 2
            b = first + j
            x_copy(b, s).wait()

            @pl.when(j + 1 < count)
            def _():
                x_copy(b + 1, 1 - s).start()

            @pl.when(b >= 2)
            def _():
                y_copy(b - 2).wait()

            x = jnp.concatenate(_load_packed_rows(xbuf.at[s], MOE_ROWS), axis=1).astype(BF16)
            gate = _dot(x, wg_scr[...]) + bg
            up = _dot(x, wu_scr[...]) + bu
            gate = jnp.minimum(gate, SWIGLU_LIMIT)
            up = jnp.clip(up, -SWIGLU_LIMIT, SWIGLU_LIMIT)
            glu = gate * jax.nn.sigmoid(gate * SWIGLU_ALPHA)
            mid = ((up + 1.0) * glu).astype(BF16)
            _store_packed_rows(ybuf.at[b % 2], 0, _dot(mid, wd_scr[...]) + bd)
            y_copy(b).start()
            return carry

        lax.fori_loop(0, count, block, 0)

    @pl.when(e == pl.num_programs(0) - 1)
    def _():
        @pl.when(total_blocks >= 2)
        def _():
            y_copy(total_blocks - 2).wait()
        y_copy(total_blocks - 1).wait()

        ybuf[...] = jnp.zeros_like(ybuf)

        def tail_start(b, carry):
            y_copy(b).start()
            return carry

        def tail_wait(b, carry):
            y_copy(b).wait()
            return carry

        lax.fori_loop(total_blocks, nblk, tail_start, 0)
        lax.fori_loop(total_blocks, nblk, tail_wait, 0)


def _moe_ffn(first_blk, n_blk, slot, next_e, misc, xs, wgu, wd, bg, bu, bd):
    p_rows = xs.shape[0] // TILE_ROW
    n_exp, d, f2 = wgu.shape
    f = f2 // 2
    assert d == 2 * TILE_ROW * LANES
    whole = lambda shape: pl.BlockSpec(shape, lambda e, *_: (0,) * len(shape))
    tile = (MOE_ROWS * TILE_ROW, LANES)
    kern = functools.partial(_ffn_kernel, nblk=p_rows // MOE_ROWS)
    return pl.pallas_call(
        kern,
        out_shape=jax.ShapeDtypeStruct((p_rows * TILE_ROW, LANES), jnp.uint32),
        grid_spec=pltpu.PrefetchScalarGridSpec(
            num_scalar_prefetch=5, grid=(n_exp,),
            in_specs=[pl.BlockSpec(memory_space=pl.ANY),
                      pl.BlockSpec(memory_space=pl.ANY), pl.BlockSpec(memory_space=pl.ANY),
                      whole(bg.shape), whole(bu.shape), whole(bd.shape)],
            out_specs=pl.BlockSpec(memory_space=pl.ANY),
            scratch_shapes=[pltpu.VMEM((2, d, f2), F32), pltpu.VMEM((2, f, d), F32),
                            pltpu.VMEM((d, f), BF16), pltpu.VMEM((d, f), BF16),
                            pltpu.VMEM((f, d), BF16),
                            pltpu.VMEM((2,) + tile, jnp.uint32), pltpu.VMEM((2,) + tile, jnp.uint32),
                            pltpu.SemaphoreType.DMA((2, 2)), pltpu.SemaphoreType.DMA((2,)),
                            pltpu.SemaphoreType.DMA((2,))]),
        compiler_params=pltpu.CompilerParams(
            dimension_semantics=("arbitrary",), vmem_limit_bytes=VMEM_LIMIT),
        name="moe_ffn",
    )(first_blk, n_blk, slot, next_e, misc, xs, wgu, wd, bg, bu, bd)


def _combine_kernel(dest_ref, ys_hbm, x1_ref, route_ref, g2_ref, o_ref, buf, sem, *, tc, n_steps):
    i = pl.program_id(0)
    slot = i % 2

    def gather(step, s):
        def issue(t, carry):
            for k in range(TOP_K):
                row = dest_ref[(step * tc + t) * TOP_K + k]
                _row_copy(ys_hbm, buf.at[s, k], row, t, sem.at[s]).start(priority=k % 2)
            return carry

        lax.fori_loop(0, tc, issue, 0, unroll=4)

    @pl.when(i == 0)
    def _():
        gather(0, 0)

    @pl.when(i + 1 < n_steps)
    def _():
        gather(i + 1, 1 - slot)

    for k in range(TOP_K):
        pltpu.make_async_copy(_tile_rows(ys_hbm, 0, tc), buf.at[slot, k], sem.at[slot]).wait()

    route = route_ref[...]
    chunks = [_load_packed_rows(buf.at[slot, k], tc) for k in range(TOP_K)]
    for c in range(2 * TILE_ROW):
        cols = slice(c * LANES, (c + 1) * LANES)
        y = chunks[0][c] * route[:, 0:1]
        for k in range(1, TOP_K):
            y = y + chunks[k][c] * route[:, k:k + 1]
        o_ref[:, cols] = x1_ref[:, cols] + g2_ref[:, cols] * y


def _moe_combine(dest_flat, ys, x1, route, g2, seq):
    n, d = x1.shape
    assert d == 2 * TILE_ROW * LANES
    tc = min(seq, 256)
    tpb = seq // tc
    kern = functools.partial(_combine_kernel, tc=tc, n_steps=n // tc)
    return pl.pallas_call(
        kern,
        out_shape=jax.ShapeDtypeStruct((n, d), F32),
        grid_spec=pltpu.PrefetchScalarGridSpec(
            num_scalar_prefetch=1, grid=(n // tc,),
            in_specs=[pl.BlockSpec(memory_space=pl.ANY),
                      pl.BlockSpec((tc, d), lambda i, dr: (i, 0)),
                      pl.BlockSpec((tc, LANES), lambda i, dr: (i, 0)),
                      pl.BlockSpec((None, 1, d), lambda i, dr: (i // tpb, 0, 0))],
            out_specs=pl.BlockSpec((tc, d), lambda i, dr: (i, 0)),
            scratch_shapes=[pltpu.VMEM((2, TOP_K, tc * TILE_ROW, LANES), jnp.uint32),
                            pltpu.SemaphoreType.DMA((2,))]),
        compiler_params=pltpu.CompilerParams(
            dimension_semantics=("arbitrary",), vmem_limit_bytes=VMEM_LIMIT),
        name="moe_combine",
    )(dest_flat, ys, x1, route, g2)


def _rope_tables(seq):
    inv_freq = (ROPE_THETA ** (-np.arange(0, HEAD_DIM, 2, dtype=np.float32) / HEAD_DIM)).astype(np.float32)
    ang = (inv_freq[:, None] * np.arange(seq, dtype=np.float32)[None, :]).astype(np.float32)
    cos, sin = np.cos(ang.astype(np.float64)), np.sin(ang.astype(np.float64))
    cos_t = np.concatenate([cos, cos], axis=0).astype(np.float32)
    sin_t = np.concatenate([-sin, sin], axis=0).astype(np.float32)
    return jnp.asarray(cos_t), jnp.asarray(sin_t)


def _layer(x, c, l, lambda_init, w_ada, b_ada, norm1_w, w_in, q_norm_w, k_norm_w, lambda_q1,
           lambda_k1, lambda_q2, lambda_k2, subln_w, w_attn_o, conv_w, w_conv_o, w_out, norm2_w,
           w_router, b_router, w_gate_up, b_gate_up, w_down, b_down):
    bsz, seq, d = x.shape
    n = bsz * seq
    x2 = x.reshape(n, d)

    mod = _ada_mod(c, w_ada[l], b_ada[l])
    sh1, sc1, g1, sh2, sc2, g2 = [m.reshape(bsz, 1, d) for m in jnp.split(mod, 6, axis=-1)]

    cos_t, sin_t = _rope_tables(seq)
    qt, k, vt, cb, z, sga, sgc, qn, kn = _in_proj(
        x2, norm1_w[l].reshape(1, d), sc1, sh1, w_in[l].astype(BF16),
        q_norm_w[l].reshape(HEAD_DIM, 1) * Q_SCALE, k_norm_w[l].reshape(HEAD_DIM, 1),
        cos_t, sin_t, bsz, seq)

    lam_vecs = jnp.stack([lambda_q1[l], lambda_k1[l], lambda_q2[l], lambda_k2[l]]).astype(F32)
    o = _diff_attn(qt, k.reshape(bsz, seq, d), vt, qn, kn, lam_vecs, subln_w[l].reshape(V_DIM, 1),
                   lambda_init)

    pad = LANES - N_EXPERTS
    wr = jnp.pad(w_router[l].astype(F32), ((0, 0), (0, pad)))
    wr_hi = wr.astype(BF16)
    wr_split = jnp.concatenate([wr_hi, (wr - wr_hi.astype(F32)).astype(BF16)], axis=1)
    br_pad = jnp.pad(b_router[l].astype(F32), (0, pad), constant_values=NEG_BIG).reshape(1, LANES)
    x1, h2, route, cnt = _post_mix(
        o.reshape(n, d), cb, z, sga, sgc, x2, g1, sc2, sh2, norm2_w[l].reshape(1, d), conv_w[l],
        w_attn_o[l].astype(BF16), w_conv_o[l].astype(BF16), w_out[l].astype(BF16),
        wr_split, br_pad, seq)

    counts = cnt[0, :N_EXPERTS].astype(jnp.int32)
    padded = ((counts + MOE_ROWS - 1) // MOE_ROWS) * MOE_ROWS
    ends = jnp.cumsum(padded)
    start = ends - padded
    p_rows = n * TOP_K + N_EXPERTS * MOE_ROWS
    top_e = route[:, TOP_K:2 * TOP_K].astype(jnp.int32)
    rank = route[:, 2 * TOP_K:3 * TOP_K].astype(jnp.int32)
    experts = jnp.arange(N_EXPERTS, dtype=jnp.int32)
    seg_start = jnp.sum(jnp.where(top_e[..., None] == experts, start, 0), axis=-1)
    dest = (seg_start + rank).reshape(-1)
    nonempty = padded > 0
    later = jnp.logical_and(nonempty[None, :], experts[None, :] > experts[:, None])
    next_e = jnp.min(jnp.where(later, experts[None, :], N_EXPERTS), axis=1)
    next_e = jnp.where(next_e == N_EXPERTS, -1, next_e).astype(jnp.int32)
    slot = ((jnp.cumsum(nonempty.astype(jnp.int32)) - 1) % 2).astype(jnp.int32)
    first_blk = (start // MOE_ROWS).astype(jnp.int32)
    n_blk = (padded // MOE_ROWS).astype(jnp.int32)
    misc = jnp.stack([jnp.argmax(nonempty).astype(jnp.int32), (ends[-1] // MOE_ROWS).astype(jnp.int32)])

    xs = _moe_dispatch(dest, ends.astype(jnp.int32), h2, n, p_rows)
    ys = _moe_ffn(first_blk, n_blk, slot, next_e, misc, xs, w_gate_up[l], w_down[l],
                  b_gate_up[l][:, None, 0::2], b_gate_up[l][:, None, 1::2], b_down[l][:, None, :])
    out = _moe_combine(dest, ys, x1, route, g2, seq)
    return out.reshape(bsz, seq, d)


def kernel(x, c, w_ada, b_ada, norm1_w, w_in, q_norm_w, k_norm_w, lambda_q1, lambda_k1, lambda_q2,
           lambda_k2, subln_w, w_attn_o, conv_w, w_conv_o, w_out, norm2_w, w_router, b_router,
           w_gate_up, b_gate_up, w_down, b_down):
    depth = w_ada.shape[0]
    for l in range(depth):
        lambda_init = 0.8 - 0.6 * math.exp(-0.3 * l)
        x = _layer(x, c, l, lambda_init, w_ada, b_ada, norm1_w, w_in, q_norm_w, k_norm_w,
                   lambda_q1, lambda_k1, lambda_q2, lambda_k2, subln_w, w_attn_o, conv_w,
                   w_conv_o, w_out, norm2_w, w_router, b_router, w_gate_up, b_gate_up,
                   w_down, b_down)
    return x
```

```python
import functools
import math

import jax
import jax.numpy as jnp
import numpy as np
from jax import lax
from jax.experimental import pallas as pl
from jax.experimental.pallas import tpu as pltpu

N_HEADS = 8
HEAD_DIM = 64
V_DIM = 2 * HEAD_DIM
IN_BLOCKS = ("q", "k", "v", "conv_b", "conv_c", "conv_x", "gate_attn", "gate_conv")
N_EXPERTS = 32
TOP_K = 4
SWIGLU_LIMIT = 7.0
SWIGLU_ALPHA = 1.702
ROPE_THETA = 10000.0
RMS_EPS = 1e-6
SUBLN_EPS = 1e-5
LANES = 128
TILE_ROW = 4
MOE_ROWS = 256
NEG_BIG = -1e30
LOG2E = 1.4426950408889634
Q_SCALE = LOG2E / math.sqrt(HEAD_DIM)
SAFE_EXP2_BOUND = 80.0
VMEM_LIMIT = 56 * 1024 * 1024

F32 = jnp.float32
BF16 = jnp.bfloat16


def _dot(a, b):
    return jnp.dot(a, b, preferred_element_type=F32)


_HIGH_HALF = 0xFFFF0000


def _bf16_bits(x):
    return pltpu.bitcast(x.astype(BF16).astype(F32), jnp.uint32)


def _store_packed_rows(ref, row0, val):
    rows, d = val.shape
    half = d // 2
    for c in range(TILE_ROW):
        lo = _bf16_bits(val[:, c * LANES:(c + 1) * LANES]) >> 16
        hi = _bf16_bits(val[:, half + c * LANES:half + (c + 1) * LANES])
        ref[pl.ds(row0 * TILE_ROW + c, rows, stride=TILE_ROW), :] = lo | hi


def _load_packed_rows(ref, rows):
    words = [ref[pl.ds(c, rows, stride=TILE_ROW), :] for c in range(TILE_ROW)]
    lo = [pltpu.bitcast(w << 16, F32) for w in words]
    hi = [pltpu.bitcast(w & jnp.uint32(_HIGH_HALF), F32) for w in words]
    return lo + hi


def _ada_kernel(ct_ref, w_ref, b_ref, o_ref):
    ct = ct_ref[...]
    s = ct * jax.nn.sigmoid(ct)
    w = w_ref[...]
    for b in range(ct.shape[1]):
        o_ref[b:b + 1, :] = jnp.sum(w * s[:, b:b + 1], axis=0, keepdims=True) + b_ref[...]


def _ada_mod(c, w_ada, b_ada):
    bsz, d = c.shape
    n = w_ada.shape[1]
    tn = min(n, 1536)
    return pl.pallas_call(
        _ada_kernel,
        out_shape=jax.ShapeDtypeStruct((bsz, n), F32),
        grid=(n // tn,),
        in_specs=[pl.BlockSpec((d, bsz), lambda j: (0, 0)),
                  pl.BlockSpec((d, tn), lambda j: (0, j)),
                  pl.BlockSpec((1, tn), lambda j: (0, j))],
        out_specs=pl.BlockSpec((bsz, tn), lambda j: (0, j)),
        compiler_params=pltpu.CompilerParams(dimension_semantics=("arbitrary",)),
        name="ada_mod",
    )(c.T, w_ada, b_ada.reshape(1, n))


def _qk_norm_rope_t(y, g_col, cos_t, sin_t):
    tm, w = y.shape
    yt = y.T.reshape(w // HEAD_DIM, HEAD_DIM, tm)
    ms = jnp.mean(yt * yt, axis=1, keepdims=True)
    yn = yt * lax.rsqrt(ms + RMS_EPS) * g_col[None]
    half = HEAD_DIM // 2
    swapped = jnp.concatenate([yn[:, half:, :], yn[:, :half, :]], axis=1)
    out = yn * cos_t[None] + swapped * sin_t[None]
    norm2 = jnp.sum(out * out, axis=1).reshape(N_HEADS, 2, tm)
    return out.reshape(w, tm), norm2


def _in_proj_kernel(x_ref, n1_ref, sc_ref, sh_ref, w_ref, gq_ref, gk_ref, cos_ref, sin_ref,
                    qt_ref, k_ref, vt_ref, cb_ref, z_ref, sga_ref, sgc_ref, qn_ref, kn_ref, h_scr):
    x = x_ref[...]
    xn = x * lax.rsqrt(jnp.mean(x * x, axis=-1, keepdims=True) + RMS_EPS) * n1_ref[...]
    h_scr[...] = (xn * (1.0 + sc_ref[...]) + sh_ref[...]).astype(BF16)
    wcol = w_ref.shape[1] // len(IN_BLOCKS)

    def proj(name):
        c = IN_BLOCKS.index(name)
        return _dot(h_scr[...], w_ref[:, c * wcol:(c + 1) * wcol])

    qt, qn = _qk_norm_rope_t(proj("q"), gq_ref[...], cos_ref[...], sin_ref[...])
    qt_ref[...] = qt.astype(BF16)
    qn_ref[...] = qn
    kt, kn = _qk_norm_rope_t(proj("k"), gk_ref[...], cos_ref[...], sin_ref[...])
    k_ref[...] = kt.T.astype(BF16)
    kn_ref[...] = kn
    vt_ref[...] = proj("v").T.astype(BF16)
    cb_ref[...] = proj("conv_b").astype(BF16)
    z_ref[...] = (proj("conv_c") * proj("conv_x")).astype(BF16)
    sga_ref[...] = jax.nn.sigmoid(proj("gate_attn")).astype(BF16)
    sgc_ref[...] = jax.nn.sigmoid(proj("gate_conv")).astype(BF16)


def _in_proj(x2, n1, sc1, sh1, w_in_bf, gq, gk, cos_t, sin_t, bsz, seq):
    n, d = x2.shape
    tm = min(seq, 512)
    tpb = seq // tm
    assert w_in_bf.shape[1] == len(IN_BLOCKS) * d
    row = lambda i: (i, 0)
    const = lambda i: (0, 0)
    tcol = lambda i: (i // tpb, 0, i % tpb)
    mod = lambda i: (i // tpb, 0, 0)
    nat = jax.ShapeDtypeStruct((n, d), BF16)
    tr = jax.ShapeDtypeStruct((bsz, d, seq), BF16)
    nrm = jax.ShapeDtypeStruct((bsz, N_HEADS, 2, seq), F32)
    nat_spec = pl.BlockSpec((tm, d), row)
    tr_spec = pl.BlockSpec((None, d, tm), tcol)
    nrm_spec = pl.BlockSpec((None, N_HEADS, 2, tm), lambda i: (i // tpb, 0, 0, i % tpb))
    return pl.pallas_call(
        _in_proj_kernel,
        out_shape=(tr, nat, tr, nat, nat, nat, nat, nrm, nrm),
        grid=(n // tm,),
        in_specs=[pl.BlockSpec((tm, d), row),
                  pl.BlockSpec((1, d), const),
                  pl.BlockSpec((None, 1, d), mod),
                  pl.BlockSpec((None, 1, d), mod),
                  pl.BlockSpec(w_in_bf.shape, const),
                  pl.BlockSpec((HEAD_DIM, 1), const),
                  pl.BlockSpec((HEAD_DIM, 1), const),
                  pl.BlockSpec((HEAD_DIM, tm), lambda i: (0, i % tpb)),
                  pl.BlockSpec((HEAD_DIM, tm), lambda i: (0, i % tpb))],
        out_specs=(tr_spec, nat_spec, tr_spec, nat_spec, nat_spec, nat_spec, nat_spec,
                   nrm_spec, nrm_spec),
        scratch_shapes=[pltpu.VMEM((tm, d), BF16)],
        compiler_params=pltpu.CompilerParams(
            dimension_semantics=("arbitrary",), vmem_limit_bytes=VMEM_LIMIT),
        name="in_proj",
    )(x2, n1, sc1, sh1, w_in_bf, gq, gk, cos_t, sin_t)


def _diff_attn_kernel(qt_ref, k_ref, vt_ref, qn_ref, kn_ref, lam_ref, sw_ref, o_ref, acc0, acc1,
                      *, tk, lambda_init):
    seq = k_ref.shape[0]
    tq = qt_ref.shape[1]
    qt = qt_ref[...]
    first = lax.broadcasted_iota(jnp.int32, qt.shape, 0) < HEAD_DIM
    zero = jnp.zeros_like(qt)
    qz = (jnp.where(first, qt, zero), jnp.where(first, zero, qt))
    accs = (acc0, acc1)
    acc0[...] = jnp.zeros_like(acc0)
    acc1[...] = jnp.zeros_like(acc1)
    n_chunks = seq // tk

    def load(j):
        off = pl.multiple_of(j * tk, tk)
        return k_ref[pl.ds(off, tk), :], vt_ref[:, pl.ds(off, tk)]

    def plain_body(j, carry):
        kk, vt = load(j)
        new = []
        for c in range(2):
            p = jnp.exp2(_dot(kk, qz[c]))
            new.append(carry[c] + jnp.sum(p, axis=0, keepdims=True))
            accs[c][...] += _dot(vt, p.astype(BF16))
        return tuple(new)

    def online_body(j, carry):
        kk, vt = load(j)
        new = []
        for c in range(2):
            m, l = carry[2 * c], carry[2 * c + 1]
            s = _dot(kk, qz[c])
            m_new = jnp.maximum(m, jnp.max(s, axis=0, keepdims=True))
            alpha = jnp.exp2(m - m_new)
            p = jnp.exp2(s - m_new)
            l = alpha * l + jnp.sum(p, axis=0, keepdims=True)
            accs[c][...] = alpha * accs[c][...] + _dot(vt, p.astype(BF16))
            new += [m_new, l]
        return tuple(new)

    m_init = jnp.full((1, tq), NEG_BIG, F32)
    l_init = jnp.zeros((1, tq), F32)

    def plain():
        return lax.fori_loop(0, n_chunks, plain_body, (l_init, l_init))

    def online():
        _, l0, _, l1 = lax.fori_loop(0, n_chunks, online_body, (m_init, l_init, m_init, l_init))
        return l0, l1

    bound2 = jnp.max(jnp.max(qn_ref[...], axis=-1, keepdims=True)
                     * jnp.max(kn_ref[...], axis=-1, keepdims=True))
    l0, l1 = lax.cond(bound2 <= SAFE_EXP2_BOUND * SAFE_EXP2_BOUND, plain, online)

    lq = lam_ref[...]
    lam = (jnp.exp(jnp.sum(lq[0:1] * lq[1:2], axis=-1, keepdims=True))
           - jnp.exp(jnp.sum(lq[2:3] * lq[3:4], axis=-1, keepdims=True)) + lambda_init)
    o = acc0[...] / l0 - lam * (acc1[...] / l1)
    o = o * lax.rsqrt(jnp.mean(o * o, axis=0, keepdims=True) + SUBLN_EPS)
    o = o * sw_ref[...] * (1.0 - lambda_init)
    o_ref[...] = o.T.astype(BF16)


def _diff_attn(qt, k3, vt, qn, kn, lam_vecs, subln_col, lambda_init):
    bsz, d, seq = qt.shape
    tq = min(seq, 1024)
    tk = min(seq, 4096)
    kern = functools.partial(_diff_attn_kernel, tk=tk, lambda_init=lambda_init)
    return pl.pallas_call(
        kern,
        out_shape=jax.ShapeDtypeStruct((bsz, seq, d), BF16),
        grid=(bsz, N_HEADS, seq // tq),
        in_specs=[pl.BlockSpec((None, V_DIM, tq), lambda b, h, i: (b, h, i)),
                  pl.BlockSpec((None, seq, V_DIM), lambda b, h, i: (b, 0, h)),
                  pl.BlockSpec((None, V_DIM, seq), lambda b, h, i: (b, h, 0)),
                  pl.BlockSpec((None, None, 2, tq), lambda b, h, i: (b, h, 0, i)),
                  pl.BlockSpec((None, None, 2, seq), lambda b, h, i: (b, h, 0, 0)),
                  pl.BlockSpec((4, HEAD_DIM), lambda b, h, i: (0, 0)),
                  pl.BlockSpec((V_DIM, 1), lambda b, h, i: (0, 0))],
        out_specs=pl.BlockSpec((None, tq, V_DIM), lambda b, h, i: (b, i, h)),
        scratch_shapes=[pltpu.VMEM((V_DIM, tq), F32), pltpu.VMEM((V_DIM, tq), F32)],
        compiler_params=pltpu.CompilerParams(
            dimension_semantics=("arbitrary", "arbitrary", "arbitrary"),
            vmem_limit_bytes=VMEM_LIMIT),
        name="diff_attn",
    )(qt, k3, vt, qn, kn, lam_vecs, subln_col)


def _post_mix_kernel(o_ref, cb_ref, z_ref, zp_ref, zn_ref, sga_ref, sgc_ref, x_ref,
                     g1_ref, sc2_ref, sh2_ref, n2_ref, cw_ref, wao_ref, wco_ref, wout_ref,
                     wr_ref, br_ref,
                     x1_ref, h2_ref, route_ref, cnt_ref, *, tpb):
    i = pl.program_id(0)
    tm = x_ref.shape[0]

    @pl.when(i == 0)
    def _():
        cnt_ref[...] = jnp.zeros_like(cnt_ref)

    z = z_ref[...].astype(F32)
    rows = lax.broadcasted_iota(jnp.int32, z.shape, 0)
    halo_rows = zp_ref.shape[0]
    prev_row = zp_ref[halo_rows - 1:halo_rows, :].astype(F32)
    next_row = zn_ref[0:1, :].astype(F32)
    prev_row = jnp.where(i % tpb == 0, jnp.zeros_like(prev_row), prev_row)
    next_row = jnp.where(i % tpb == tpb - 1, jnp.zeros_like(next_row), next_row)
    z_m1 = jnp.where(rows == 0, prev_row, pltpu.roll(z, 1, 0))
    z_p1 = jnp.where(rows == tm - 1, next_row, pltpu.roll(z, tm - 1, 0))
    cw = cw_ref[...]
    conv = z_m1 * cw[0:1] + z * cw[1:2] + z_p1 * cw[2:3]
    u = (cb_ref[...].astype(F32) * conv).astype(BF16)

    y_attn = _dot(o_ref[...], wao_ref[...])
    y_conv = _dot(u, wco_ref[...])
    m = sga_ref[...].astype(F32) * y_attn + sgc_ref[...].astype(F32) * y_conv
    x1 = x_ref[...] + g1_ref[...] * _dot(m.astype(BF16), wout_ref[...])
    x1_ref[...] = x1

    h2 = x1 * lax.rsqrt(jnp.mean(x1 * x1, axis=-1, keepdims=True) + RMS_EPS) * n2_ref[...]
    h2 = h2 * (1.0 + sc2_ref[...]) + sh2_ref[...]
    _store_packed_rows(h2_ref, 0, h2)

    route, counts = _route_tile(h2, wr_ref, br_ref, cnt_ref[0:1, :])
    route_ref[...] = route
    cnt_ref[...] = jnp.broadcast_to(counts, cnt_ref.shape)


def _route_tile(h2, wr_ref, br_ref, counts):
    tm = h2.shape[0]
    h_hi = h2.astype(BF16)
    h_lo = (h2 - h_hi.astype(F32)).astype(BF16)
    hi_both = _dot(h_hi, wr_ref[...])
    logits = (hi_both[:, :LANES] + hi_both[:, LANES:] + _dot(h_lo, wr_ref[:, :LANES])
              + br_ref[...])

    lane = lax.broadcasted_iota(jnp.int32, logits.shape, 1)
    work = logits
    vals, idxs = [], []
    for _ in range(TOP_K):
        mx = jnp.max(work, axis=-1, keepdims=True)
        ix = jnp.min(jnp.where(work == mx, lane, LANES), axis=-1, keepdims=True)
        vals.append(mx)
        idxs.append(ix)
        work = jnp.where(lane == ix, 2.0 * NEG_BIG, work)
    exps = [jnp.exp(v - vals[0]) for v in vals]
    den = exps[0] + exps[1] + exps[2] + exps[3]

    sel = (work == 2.0 * NEG_BIG).astype(BF16)
    r_i = lax.broadcasted_iota(jnp.int32, (tm, tm), 0)
    c_i = lax.broadcasted_iota(jnp.int32, (tm, tm), 1)
    lower = (r_i > c_i).astype(BF16)
    before = _dot(lower, sel) + counts
    counts = counts + jnp.sum(sel.astype(F32), axis=0, keepdims=True)

    route = jnp.zeros(logits.shape, F32)
    for k in range(TOP_K):
        rank = jnp.sum(jnp.where(lane == idxs[k], before, 0.0), axis=-1, keepdims=True)
        route = jnp.where(lane == k, exps[k] / den, route)
        route = jnp.where(lane == TOP_K + k, idxs[k].astype(F32), route)
        route = jnp.where(lane == 2 * TOP_K + k, rank, route)
    return route, counts


def _post_mix(o2, cb, z, sga, sgc, x2, g1, sc2, sh2, n2, conv_w, wao, wco, wout,
              wr_split, br_pad, seq):
    n, d = x2.shape
    tm = min(seq, 512)
    tpb = seq // tm
    halo = 16
    hb = tm // halo
    last_hb = n // halo - 1
    row = lambda i: (i, 0)
    const = lambda i: (0, 0)
    mod = lambda i: (i // tpb, 0, 0)
    wspec = pl.BlockSpec((d, d), const)
    kern = functools.partial(_post_mix_kernel, tpb=tpb)
    return pl.pallas_call(
        kern,
        out_shape=(jax.ShapeDtypeStruct((n, d), F32),
                   jax.ShapeDtypeStruct((n * TILE_ROW, LANES), jnp.uint32),
                   jax.ShapeDtypeStruct((n, LANES), F32), jax.ShapeDtypeStruct((8, LANES), F32)),
        grid=(n // tm,),
        in_specs=[pl.BlockSpec((tm, d), row), pl.BlockSpec((tm, d), row), pl.BlockSpec((tm, d), row),
                  pl.BlockSpec((halo, d), lambda i: (jnp.maximum(i * hb - 1, 0), 0)),
                  pl.BlockSpec((halo, d), lambda i: (jnp.minimum((i + 1) * hb, last_hb), 0)),
                  pl.BlockSpec((tm, d), row), pl.BlockSpec((tm, d), row), pl.BlockSpec((tm, d), row),
                  pl.BlockSpec((None, 1, d), mod), pl.BlockSpec((None, 1, d), mod),
                  pl.BlockSpec((None, 1, d), mod),
                  pl.BlockSpec((1, d), const), pl.BlockSpec((3, d), const),
                  wspec, wspec, wspec,
                  pl.BlockSpec((d, 2 * LANES), const), pl.BlockSpec((1, LANES), const)],
        out_specs=(pl.BlockSpec((tm, d), row), pl.BlockSpec((tm * TILE_ROW, LANES), row),
                   pl.BlockSpec((tm, LANES), row), pl.BlockSpec((8, LANES), const)),
        compiler_params=pltpu.CompilerParams(
            dimension_semantics=("arbitrary",), vmem_limit_bytes=VMEM_LIMIT),
        name="post_mix",
    )(o2, cb, z, z, z, sga, sgc, x2, g1, sc2, sh2, n2, conv_w, wao, wco, wout, wr_split, br_pad)


def _tile_rows(ref, row, n_rows=1):
    start = pl.multiple_of(row * TILE_ROW, TILE_ROW)
    return ref.at[pl.ds(start, n_rows * TILE_ROW)]


def _row_copy(src, dst, s, t, sem):
    return pltpu.make_async_copy(_tile_rows(src, s), _tile_rows(dst, t), sem)


def _dispatch_kernel(dest_ref, ends_ref, h2_ref, xs_hbm, zbuf, sem, zsem, *, tc, nblk):
    base = pl.program_id(0) * tc

    @pl.when(pl.program_id(0) == 0)
    def _():
        zbuf[...] = jnp.zeros_like(zbuf)

        def zero_block(row):
            return pltpu.make_async_copy(zbuf, _tile_rows(xs_hbm, row, MOE_ROWS), zsem)

        def nonempty(e):
            return ends_ref[e] > (ends_ref[e - 1] if e else 0)

        total = ends_ref[N_EXPERTS - 1]
        n_tail = nblk - total // MOE_ROWS

        def tail_start(b, carry):
            zero_block(total + b * MOE_ROWS).start()
            return carry

        def tail_wait(b, carry):
            zero_block(0).wait()
            return carry

        for e in range(N_EXPERTS):
            @pl.when(nonempty(e))
            def _(e=e):
                zero_block(ends_ref[e] - MOE_ROWS).start()
        lax.fori_loop(0, n_tail, tail_start, 0)
        for e in range(N_EXPERTS):
            @pl.when(nonempty(e))
            def _():
                zero_block(0).wait()
        lax.fori_loop(0, n_tail, tail_wait, 0)

    def issue(t, carry):
        for k in range(TOP_K):
            _row_copy(h2_ref, xs_hbm, t, dest_ref[(base + t) * TOP_K + k], sem).start(priority=k % 2)
        return carry

    lax.fori_loop(0, tc, issue, 0, unroll=4)

    for _ in range(TOP_K):
        pltpu.make_async_copy(h2_ref, _tile_rows(xs_hbm, 0, tc), sem).wait()


def _moe_dispatch(dest_flat, ends, h2_tiles, n, p_rows):
    tc = min(n, 512)
    kern = functools.partial(_dispatch_kernel, tc=tc, nblk=p_rows // MOE_ROWS)
    return pl.pallas_call(
        kern,
        out_shape=jax.ShapeDtypeStruct((p_rows * TILE_ROW, LANES), jnp.uint32),
        grid_spec=pltpu.PrefetchScalarGridSpec(
            num_scalar_prefetch=2, grid=(n // tc,),
            in_specs=[pl.BlockSpec((tc * TILE_ROW, LANES), lambda i, dr, en: (i, 0))],
            out_specs=pl.BlockSpec(memory_space=pl.ANY),
            scratch_shapes=[pltpu.VMEM((MOE_ROWS * TILE_ROW, LANES), jnp.uint32),
                            pltpu.SemaphoreType.DMA, pltpu.SemaphoreType.DMA]),
        compiler_params=pltpu.CompilerParams(dimension_semantics=("arbitrary",)),
        name="moe_dispatch",
    )(dest_flat, ends, h2_tiles)


_NT = (((1,), (1,)), ((), ()))


def _ffn_kernel(first_ref, count_ref, slot_ref, next_ref, misc_ref,
                xs_hbm, wgu_hbm, wd_hbm, bg_ref, bu_ref, bd_ref, ys_hbm,
                wgu_buf, wd_buf, wg_scr, wu_scr, wd_scr, xbuf, ybuf,
                wsem, xsem, ysem, *, nblk):
    e = pl.program_id(0)
    first = first_ref[e]
    count = count_ref[e]
    slot = slot_ref[e]
    first_expert, total_blocks = misc_ref[0], misc_ref[1]

    def weight_copies(ex, s):
        return (pltpu.make_async_copy(wgu_hbm.at[ex], wgu_buf.at[s], wsem.at[0, s]),
                pltpu.make_async_copy(wd_hbm.at[ex], wd_buf.at[s], wsem.at[1, s]))

    def x_copy(b, s):
        return pltpu.make_async_copy(_tile_rows(xs_hbm, b * MOE_ROWS, MOE_ROWS), xbuf.at[s], xsem.at[s])

    def y_copy(b):
        return pltpu.make_async_copy(ybuf.at[b % 2], _tile_rows(ys_hbm, b * MOE_ROWS, MOE_ROWS),
                                     ysem.at[b % 2])

    @pl.when(e == first_expert)
    def _():
        for cp in weight_copies(e, 0):
            cp.start()

    @pl.when(count > 0)
    def _():
        x_copy(first, 0).start()
        for cp in weight_copies(e, slot):
            cp.wait()

        @pl.when(next_ref[e] >= 0)
        def _():
            for cp in weight_copies(next_ref[e], 1 - slot):
                cp.start(priority=1)

        d = wgu_buf.shape[1]
        for c in range(d // LANES):
            cols = slice(c * LANES, (c + 1) * LANES)
            words = pltpu.bitcast(wgu_buf[slot, cols, :].astype(BF16).T, jnp.uint32)
            wg_scr[:, cols] = pltpu.bitcast(words << 16, F32).astype(BF16)
            wu_scr[:, cols] = pltpu.bitcast(words & jnp.uint32(_HIGH_HALF), F32).astype(BF16)
        wd_scr[...] = wd_buf[slot].astype(BF16)
        bg, bu, bd = bg_ref[e], bu_ref[e], bd_ref[e]

        def block(j, carry):
            s = j % 2
            b = first + j
            x_copy(b, s).wait()

            @pl.when(j + 1 < count)
            def _():
                x_copy(b + 1, 1 - s).start()

            @pl.when(b >= 2)
            def _():
                y_copy(b - 2).wait()

            x = jnp.concatenate(_load_packed_rows(xbuf.at[s], MOE_ROWS), axis=1).astype(BF16)
            gate = lax.dot_general(x, wg_scr[...], _NT, preferred_element_type=F32) + bg
            up = lax.dot_general(x, wu_scr[...], _NT, preferred_element_type=F32) + bu
            gate = jnp.minimum(gate, SWIGLU_LIMIT)
            up = jnp.clip(up, -SWIGLU_LIMIT, SWIGLU_LIMIT)
            glu = gate * jax.nn.sigmoid(gate * SWIGLU_ALPHA)
            mid = ((up + 1.0) * glu).astype(BF16)
            _store_packed_rows(ybuf.at[b % 2], 0, _dot(mid, wd_scr[...]) + bd)
            y_copy(b).start()
            return carry

        lax.fori_loop(0, count, block, 0)

    @pl.when(e == pl.num_programs(0) - 1)
    def _():
        @pl.when(total_blocks >= 2)
        def _():
            y_copy(total_blocks - 2).wait()
        y_copy(total_blocks - 1).wait()

        ybuf[...] = jnp.zeros_like(ybuf)

        def tail_start(b, carry):
            y_copy(b).start()
            return carry

        def tail_wait(b, carry):
            y_copy(b).wait()
            return carry

        lax.fori_loop(total_blocks, nblk, tail_start, 0)
        lax.fori_loop(total_blocks, nblk, tail_wait, 0)


def _moe_ffn(first_blk, n_blk, slot, next_e, misc, xs, wgu, wd, bg, bu, bd):
    p_rows = xs.shape[0] // TILE_ROW
    n_exp, d, f2 = wgu.shape
    f = f2 // 2
    assert d == 2 * TILE_ROW * LANES
    whole = lambda shape: pl.BlockSpec(shape, lambda e, *_: (0,) * len(shape))
    tile = (MOE_ROWS * TILE_ROW, LANES)
    kern = functools.partial(_ffn_kernel, nblk=p_rows // MOE_ROWS)
    return pl.pallas_call(
        kern,
        out_shape=jax.ShapeDtypeStruct((p_rows * TILE_ROW, LANES), jnp.uint32),
        grid_spec=pltpu.PrefetchScalarGridSpec(
            num_scalar_prefetch=5, grid=(n_exp,),
            in_specs=[pl.BlockSpec(memory_space=pl.ANY),
                      pl.BlockSpec(memory_space=pl.ANY), pl.BlockSpec(memory_space=pl.ANY),
                      whole(bg.shape), whole(bu.shape), whole(bd.shape)],
            out_specs=pl.BlockSpec(memory_space=pl.ANY),
            scratch_shapes=[pltpu.VMEM((2, d, f2), F32), pltpu.VMEM((2, f, d), F32),
                            pltpu.VMEM((f, d), BF16), pltpu.VMEM((f, d), BF16),
                            pltpu.VMEM((f, d), BF16),
                            pltpu.VMEM((2,) + tile, jnp.uint32), pltpu.VMEM((2,) + tile, jnp.uint32),
                            pltpu.SemaphoreType.DMA((2, 2)), pltpu.SemaphoreType.DMA((2,)),
                            pltpu.SemaphoreType.DMA((2,))]),
        compiler_params=pltpu.CompilerParams(
            dimension_semantics=("arbitrary",), vmem_limit_bytes=VMEM_LIMIT),
        name="moe_ffn",
    )(first_blk, n_blk, slot, next_e, misc, xs, wgu, wd, bg, bu, bd)


def _combine_kernel(dest_ref, ys_hbm, x1_ref, route_ref, g2_ref, o_ref, buf, sem, *, tc, n_steps):
    i = pl.program_id(0)
    slot = i % 2

    def gather(step, s):
        def issue(t, carry):
            for k in range(TOP_K):
                row = dest_ref[(step * tc + t) * TOP_K + k]
                _row_copy(ys_hbm, buf.at[s, k], row, t, sem.at[s]).start(priority=k % 2)
            return carry

        lax.fori_loop(0, tc, issue, 0, unroll=4)

    @pl.when(i == 0)
    def _():
        gather(0, 0)

    @pl.when(i + 1 < n_steps)
    def _():
        gather(i + 1, 1 - slot)

    for k in range(TOP_K):
        pltpu.make_async_copy(_tile_rows(ys_hbm, 0, tc), buf.at[slot, k], sem.at[slot]).wait()

    route = route_ref[...]
    chunks = [_load_packed_rows(buf.at[slot, k], tc) for k in range(TOP_K)]
    for c in range(2 * TILE_ROW):
        cols = slice(c * LANES, (c + 1) * LANES)
        y = chunks[0][c] * route[:, 0:1]
        for k in range(1, TOP_K):
            y = y + chunks[k][c] * route[:, k:k + 1]
        o_ref[:, cols] = x1_ref[:, cols] + g2_ref[:, cols] * y


def _moe_combine(dest_flat, ys, x1, route, g2, seq):
    n, d = x1.shape
    assert d == 2 * TILE_ROW * LANES
    tc = min(seq, 256)
    tpb = seq // tc
    kern = functools.partial(_combine_kernel, tc=tc, n_steps=n // tc)
    return pl.pallas_call(
        kern,
        out_shape=jax.ShapeDtypeStruct((n, d), F32),
        grid_spec=pltpu.PrefetchScalarGridSpec(
            num_scalar_prefetch=1, grid=(n // tc,),
            in_specs=[pl.BlockSpec(memory_space=pl.ANY),
                      pl.BlockSpec((tc, d), lambda i, dr: (i, 0)),
                      pl.BlockSpec((tc, LANES), lambda i, dr: (i, 0)),
                      pl.BlockSpec((None, 1, d), lambda i, dr: (i // tpb, 0, 0))],
            out_specs=pl.BlockSpec((tc, d), lambda i, dr: (i, 0)),
            scratch_shapes=[pltpu.VMEM((2, TOP_K, tc * TILE_ROW, LANES), jnp.uint32),
                            pltpu.SemaphoreType.DMA((2,))]),
        compiler_params=pltpu.CompilerParams(
            dimension_semantics=("arbitrary",), vmem_limit_bytes=VMEM_LIMIT),
        name="moe_combine",
    )(dest_flat, ys, x1, route, g2)


def _rope_tables(seq):
    inv_freq = (ROPE_THETA ** (-np.arange(0, HEAD_DIM, 2, dtype=np.float32) / HEAD_DIM)).astype(np.float32)
    ang = (inv_freq[:, None] * np.arange(seq, dtype=np.float32)[None, :]).astype(np.float32)
    cos, sin = np.cos(ang.astype(np.float64)), np.sin(ang.astype(np.float64))
    cos_t = np.concatenate([cos, cos], axis=0).astype(np.float32)
    sin_t = np.concatenate([-sin, sin], axis=0).astype(np.float32)
    return jnp.asarray(cos_t), jnp.asarray(sin_t)


def _layer(x, c, l, lambda_init, w_ada, b_ada, norm1_w, w_in, q_norm_w, k_norm_w, lambda_q1,
           lambda_k1, lambda_q2, lambda_k2, subln_w, w_attn_o, conv_w, w_conv_o, w_out, norm2_w,
           w_router, b_router, w_gate_up, b_gate_up, w_down, b_down):
    bsz, seq, d = x.shape
    n = bsz * seq
    x2 = x.reshape(n, d)

    mod = _ada_mod(c, w_ada[l], b_ada[l])
    sh1, sc1, g1, sh2, sc2, g2 = [m.reshape(bsz, 1, d) for m in jnp.split(mod, 6, axis=-1)]

    cos_t, sin_t = _rope_tables(seq)
    qt, k, vt, cb, z, sga, sgc, qn, kn = _in_proj(
        x2, norm1_w[l].reshape(1, d), sc1, sh1, w_in[l].astype(BF16),
        q_norm_w[l].reshape(HEAD_DIM, 1) * Q_SCALE, k_norm_w[l].reshape(HEAD_DIM, 1),
        cos_t, sin_t, bsz, seq)

    lam_vecs = jnp.stack([lambda_q1[l], lambda_k1[l], lambda_q2[l], lambda_k2[l]]).astype(F32)
    o = _diff_attn(qt, k.reshape(bsz, seq, d), vt, qn, kn, lam_vecs, subln_w[l].reshape(V_DIM, 1),
                   lambda_init)

    pad = LANES - N_EXPERTS
    wr = jnp.pad(w_router[l].astype(F32), ((0, 0), (0, pad)))
    wr_hi = wr.astype(BF16)
    wr_split = jnp.concatenate([wr_hi, (wr - wr_hi.astype(F32)).astype(BF16)], axis=1)
    br_pad = jnp.pad(b_router[l].astype(F32), (0, pad), constant_values=NEG_BIG).reshape(1, LANES)
    x1, h2, route, cnt = _post_mix(
        o.reshape(n, d), cb, z, sga, sgc, x2, g1, sc2, sh2, norm2_w[l].reshape(1, d), conv_w[l],
        w_attn_o[l].astype(BF16), w_conv_o[l].astype(BF16), w_out[l].astype(BF16),
        wr_split, br_pad, seq)

    counts = cnt[0, :N_EXPERTS].astype(jnp.int32)
    padded = ((counts + MOE_ROWS - 1) // MOE_ROWS) * MOE_ROWS
    ends = jnp.cumsum(padded)
    start = ends - padded
    p_rows = n * TOP_K + N_EXPERTS * MOE_ROWS
    top_e = route[:, TOP_K:2 * TOP_K].astype(jnp.int32)
    rank = route[:, 2 * TOP_K:3 * TOP_K].astype(jnp.int32)
    experts = jnp.arange(N_EXPERTS, dtype=jnp.int32)
    seg_start = jnp.sum(jnp.where(top_e[..., None] == experts, start, 0), axis=-1)
    dest = (seg_start + rank).reshape(-1)
    nonempty = padded > 0
    later = jnp.logical_and(nonempty[None, :], experts[None, :] > experts[:, None])
    next_e = jnp.min(jnp.where(later, experts[None, :], N_EXPERTS), axis=1)
    next_e = jnp.where(next_e == N_EXPERTS, -1, next_e).astype(jnp.int32)
    slot = ((jnp.cumsum(nonempty.astype(jnp.int32)) - 1) % 2).astype(jnp.int32)
    first_blk = (start // MOE_ROWS).astype(jnp.int32)
    n_blk = (padded // MOE_ROWS).astype(jnp.int32)
    misc = jnp.stack([jnp.argmax(nonempty).astype(jnp.int32), (ends[-1] // MOE_ROWS).astype(jnp.int32)])

    xs = _moe_dispatch(dest, ends.astype(jnp.int32), h2, n, p_rows)
    ys = _moe_ffn(first_blk, n_blk, slot, next_e, misc, xs, w_gate_up[l], w_down[l],
                  b_gate_up[l][:, None, 0::2], b_gate_up[l][:, None, 1::2], b_down[l][:, None, :])
    out = _moe_combine(dest, ys, x1, route, g2, seq)
    return out.reshape(bsz, seq, d)


def kernel(x, c, w_ada, b_ada, norm1_w, w_in, q_norm_w, k_norm_w, lambda_q1, lambda_k1, lambda_q2,
           lambda_k2, subln_w, w_attn_o, conv_w, w_conv_o, w_out, norm2_w, w_router, b_router,
           w_gate_up, b_gate_up, w_down, b_down):
    depth = w_ada.shape[0]
    for l in range(depth):
        lambda_init = 0.8 - 0.6 * math.exp(-0.3 * l)
        x = _layer(x, c, l, lambda_init, w_ada, b_ada, norm1_w, w_in, q_norm_w, k_norm_w,
                   lambda_q1, lambda_k1, lambda_q2, lambda_k2, subln_w, w_attn_o, conv_w,
                   w_conv_o, w_out, norm2_w, w_router, b_router, w_gate_up, b_gate_up,
                   w_down, b_down)
    return x
```

```python
import functools
import math

import jax
import jax.numpy as jnp
import numpy as np
from jax import lax
from jax.experimental import pallas as pl
from jax.experimental.pallas import tpu as pltpu

N_HEADS = 8
HEAD_DIM = 64
V_DIM = 2 * HEAD_DIM
IN_BLOCKS = ("q", "k", "v", "conv_b", "conv_c", "conv_x", "gate_attn", "gate_conv")
N_EXPERTS = 32
TOP_K = 4
SWIGLU_LIMIT = 7.0
SWIGLU_ALPHA = 1.702
ROPE_THETA = 10000.0
RMS_EPS = 1e-6
SUBLN_EPS = 1e-5
LANES = 128
TILE_ROW = 4
MOE_ROWS = 256
NEG_BIG = -1e30
LOG2E = 1.4426950408889634
Q_SCALE = LOG2E / math.sqrt(HEAD_DIM)
SAFE_EXP2_BOUND = 80.0
VMEM_LIMIT = 56 * 1024 * 1024

F32 = jnp.float32
BF16 = jnp.bfloat16


def _dot(a, b):
    return jnp.dot(a, b, preferred_element_type=F32)


_HIGH_HALF = 0xFFFF0000


def _bf16_bits(x):
    return pltpu.bitcast(x.astype(BF16).astype(F32), jnp.uint32)


def _store_packed_rows(ref, row0, val):
    rows, d = val.shape
    half = d // 2
    for c in range(TILE_ROW):
        lo = _bf16_bits(val[:, c * LANES:(c + 1) * LANES]) >> 16
        hi = _bf16_bits(val[:, half + c * LANES:half + (c + 1) * LANES])
        ref[pl.ds(row0 * TILE_ROW + c, rows, stride=TILE_ROW), :] = lo | hi


def _load_packed_rows(ref, rows):
    words = [ref[pl.ds(c, rows, stride=TILE_ROW), :] for c in range(TILE_ROW)]
    lo = [pltpu.bitcast(w << 16, F32) for w in words]
    hi = [pltpu.bitcast(w & jnp.uint32(_HIGH_HALF), F32) for w in words]
    return lo + hi


def _ada_kernel(ct_ref, w_ref, b_ref, o_ref):
    ct = ct_ref[...]
    s = ct * jax.nn.sigmoid(ct)
    w = w_ref[...]
    for b in range(ct.shape[1]):
        o_ref[b:b + 1, :] = jnp.sum(w * s[:, b:b + 1], axis=0, keepdims=True) + b_ref[...]


def _ada_mod(c, w_ada, b_ada):
    bsz, d = c.shape
    n = w_ada.shape[1]
    tn = min(n, 1536)
    return pl.pallas_call(
        _ada_kernel,
        out_shape=jax.ShapeDtypeStruct((bsz, n), F32),
        grid=(n // tn,),
        in_specs=[pl.BlockSpec((d, bsz), lambda j: (0, 0)),
                  pl.BlockSpec((d, tn), lambda j: (0, j)),
                  pl.BlockSpec((1, tn), lambda j: (0, j))],
        out_specs=pl.BlockSpec((bsz, tn), lambda j: (0, j)),
        compiler_params=pltpu.CompilerParams(dimension_semantics=("arbitrary",)),
        name="ada_mod",
    )(c.T, w_ada, b_ada.reshape(1, n))


def _qk_norm_rope_t(y, g_col, cos_t, sin_t):
    tm, w = y.shape
    yt = y.T.reshape(w // HEAD_DIM, HEAD_DIM, tm)
    ms = jnp.mean(yt * yt, axis=1, keepdims=True)
    yn = yt * lax.rsqrt(ms + RMS_EPS) * g_col[None]
    half = HEAD_DIM // 2
    swapped = jnp.concatenate([yn[:, half:, :], yn[:, :half, :]], axis=1)
    out = yn * cos_t[None] + swapped * sin_t[None]
    norm2 = jnp.sum(out * out, axis=1).reshape(N_HEADS, 2, tm)
    return out.reshape(w, tm), norm2


def _in_proj_kernel(x_ref, n1_ref, sc_ref, sh_ref, w_ref, gq_ref, gk_ref, cos_ref, sin_ref,
                    qt_ref, k_ref, vt_ref, cb_ref, z_ref, sga_ref, sgc_ref, qn_ref, kn_ref, h_scr):
    x = x_ref[...]
    xn = x * lax.rsqrt(jnp.mean(x * x, axis=-1, keepdims=True) + RMS_EPS) * n1_ref[...]
    h_scr[...] = (xn * (1.0 + sc_ref[...]) + sh_ref[...]).astype(BF16)
    wcol = w_ref.shape[1] // len(IN_BLOCKS)

    def proj(name):
        c = IN_BLOCKS.index(name)
        return _dot(h_scr[...], w_ref[:, c * wcol:(c + 1) * wcol])

    qt, qn = _qk_norm_rope_t(proj("q"), gq_ref[...], cos_ref[...], sin_ref[...])
    qt_ref[...] = qt.astype(BF16)
    qn_ref[...] = qn
    kt, kn = _qk_norm_rope_t(proj("k"), gk_ref[...], cos_ref[...], sin_ref[...])
    k_ref[...] = kt.T.astype(BF16)
    kn_ref[...] = kn
    vt_ref[...] = proj("v").T.astype(BF16)
    cb_ref[...] = proj("conv_b").astype(BF16)
    z_ref[...] = (proj("conv_c") * proj("conv_x")).astype(BF16)
    sga_ref[...] = jax.nn.sigmoid(proj("gate_attn")).astype(BF16)
    sgc_ref[...] = jax.nn.sigmoid(proj("gate_conv")).astype(BF16)


def _in_proj(x2, n1, sc1, sh1, w_in_bf, gq, gk, cos_t, sin_t, bsz, seq):
    n, d = x2.shape
    tm = min(seq, 512)
    tpb = seq // tm
    assert w_in_bf.shape[1] == len(IN_BLOCKS) * d
    row = lambda i: (i, 0)
    const = lambda i: (0, 0)
    tcol = lambda i: (i // tpb, 0, i % tpb)
    mod = lambda i: (i // tpb, 0, 0)
    nat = jax.ShapeDtypeStruct((n, d), BF16)
    tr = jax.ShapeDtypeStruct((bsz, d, seq), BF16)
    nrm = jax.ShapeDtypeStruct((bsz, N_HEADS, 2, seq), F32)
    nat_spec = pl.BlockSpec((tm, d), row)
    tr_spec = pl.BlockSpec((None, d, tm), tcol)
    nrm_spec = pl.BlockSpec((None, N_HEADS, 2, tm), lambda i: (i // tpb, 0, 0, i % tpb))
    return pl.pallas_call(
        _in_proj_kernel,
        out_shape=(tr, nat, tr, nat, nat, nat, nat, nrm, nrm),
        grid=(n // tm,),
        in_specs=[pl.BlockSpec((tm, d), row),
                  pl.BlockSpec((1, d), const),
                  pl.BlockSpec((None, 1, d), mod),
                  pl.BlockSpec((None, 1, d), mod),
                  pl.BlockSpec(w_in_bf.shape, const),
                  pl.BlockSpec((HEAD_DIM, 1), const),
                  pl.BlockSpec((HEAD_DIM, 1), const),
                  pl.BlockSpec((HEAD_DIM, tm), lambda i: (0, i % tpb)),
                  pl.BlockSpec((HEAD_DIM, tm), lambda i: (0, i % tpb))],
        out_specs=(tr_spec, nat_spec, tr_spec, nat_spec, nat_spec, nat_spec, nat_spec,
                   nrm_spec, nrm_spec),
        scratch_shapes=[pltpu.VMEM((tm, d), BF16)],
        compiler_params=pltpu.CompilerParams(
            dimension_semantics=("arbitrary",), vmem_limit_bytes=VMEM_LIMIT),
        name="in_proj",
    )(x2, n1, sc1, sh1, w_in_bf, gq, gk, cos_t, sin_t)


def _diff_attn_kernel(qt_ref, k_ref, vt_ref, qn_ref, kn_ref, lam_ref, sw_ref, o_ref, acc0, acc1,
                      *, tk, lambda_init):
    seq = k_ref.shape[0]
    tq = qt_ref.shape[1]
    qt = qt_ref[...]
    first = lax.broadcasted_iota(jnp.int32, qt.shape, 0) < HEAD_DIM
    zero = jnp.zeros_like(qt)
    qz = (jnp.where(first, qt, zero), jnp.where(first, zero, qt))
    accs = (acc0, acc1)
    acc0[...] = jnp.zeros_like(acc0)
    acc1[...] = jnp.zeros_like(acc1)
    n_chunks = seq // tk

    def load(j):
        off = pl.multiple_of(j * tk, tk)
        return k_ref[pl.ds(off, tk), :], vt_ref[:, pl.ds(off, tk)]

    def plain_body(j, carry):
        kk, vt = load(j)
        new = []
        for c in range(2):
            p = jnp.exp2(_dot(kk, qz[c]))
            new.append(carry[c] + jnp.sum(p, axis=0, keepdims=True))
            accs[c][...] += _dot(vt, p.astype(BF16))
        return tuple(new)

    def online_body(j, carry):
        kk, vt = load(j)
        new = []
        for c in range(2):
            m, l = carry[2 * c], carry[2 * c + 1]
            s = _dot(kk, qz[c])
            m_new = jnp.maximum(m, jnp.max(s, axis=0, keepdims=True))
            alpha = jnp.exp2(m - m_new)
            p = jnp.exp2(s - m_new)
            l = alpha * l + jnp.sum(p, axis=0, keepdims=True)
            accs[c][...] = alpha * accs[c][...] + _dot(vt, p.astype(BF16))
            new += [m_new, l]
        return tuple(new)

    m_init = jnp.full((1, tq), NEG_BIG, F32)
    l_init = jnp.zeros((1, tq), F32)

    def plain():
        return lax.fori_loop(0, n_chunks, plain_body, (l_init, l_init))

    def online():
        _, l0, _, l1 = lax.fori_loop(0, n_chunks, online_body, (m_init, l_init, m_init, l_init))
        return l0, l1

    bound2 = jnp.max(jnp.max(qn_ref[...], axis=-1, keepdims=True)
                     * jnp.max(kn_ref[...], axis=-1, keepdims=True))
    l0, l1 = lax.cond(bound2 <= SAFE_EXP2_BOUND * SAFE_EXP2_BOUND, plain, online)

    lq = lam_ref[...]
    lam = (jnp.exp(jnp.sum(lq[0:1] * lq[1:2], axis=-1, keepdims=True))
           - jnp.exp(jnp.sum(lq[2:3] * lq[3:4], axis=-1, keepdims=True)) + lambda_init)
    o = acc0[...] / l0 - lam * (acc1[...] / l1)
    o = o * lax.rsqrt(jnp.mean(o * o, axis=0, keepdims=True) + SUBLN_EPS)
    o = o * sw_ref[...] * (1.0 - lambda_init)
    o_ref[...] = o.T.astype(BF16)


def _diff_attn(qt, k3, vt, qn, kn, lam_vecs, subln_col, lambda_init):
    bsz, d, seq = qt.shape
    tq = min(seq, 1024)
    tk = min(seq, 4096)
    kern = functools.partial(_diff_attn_kernel, tk=tk, lambda_init=lambda_init)
    return pl.pallas_call(
        kern,
        out_shape=jax.ShapeDtypeStruct((bsz, seq, d), BF16),
        grid=(bsz, N_HEADS, seq // tq),
        in_specs=[pl.BlockSpec((None, V_DIM, tq), lambda b, h, i: (b, h, i)),
                  pl.BlockSpec((None, seq, V_DIM), lambda b, h, i: (b, 0, h)),
                  pl.BlockSpec((None, V_DIM, seq), lambda b, h, i: (b, h, 0)),
                  pl.BlockSpec((None, None, 2, tq), lambda b, h, i: (b, h, 0, i)),
                  pl.BlockSpec((None, None, 2, seq), lambda b, h, i: (b, h, 0, 0)),
                  pl.BlockSpec((4, HEAD_DIM), lambda b, h, i: (0, 0)),
                  pl.BlockSpec((V_DIM, 1), lambda b, h, i: (0, 0))],
        out_specs=pl.BlockSpec((None, tq, V_DIM), lambda b, h, i: (b, i, h)),
        scratch_shapes=[pltpu.VMEM((V_DIM, tq), F32), pltpu.VMEM((V_DIM, tq), F32)],
        compiler_params=pltpu.CompilerParams(
            dimension_semantics=("arbitrary", "arbitrary", "arbitrary"),
            vmem_limit_bytes=VMEM_LIMIT),
        name="diff_attn",
    )(qt, k3, vt, qn, kn, lam_vecs, subln_col)


def _post_mix_kernel(o_ref, cb_ref, z_ref, zp_ref, zn_ref, sga_ref, sgc_ref, x_ref,
                     g1_ref, sc2_ref, sh2_ref, n2_ref, cw_ref, wao_ref, wco_ref, wout_ref,
                     wr_ref, br_ref,
                     x1_ref, h2_ref, route_ref, cnt_ref, *, tpb):
    i = pl.program_id(0)
    tm = x_ref.shape[0]

    @pl.when(i == 0)
    def _():
        cnt_ref[...] = jnp.zeros_like(cnt_ref)

    z = z_ref[...].astype(F32)
    rows = lax.broadcasted_iota(jnp.int32, z.shape, 0)
    halo_rows = zp_ref.shape[0]
    prev_row = zp_ref[halo_rows - 1:halo_rows, :].astype(F32)
    next_row = zn_ref[0:1, :].astype(F32)
    prev_row = jnp.where(i % tpb == 0, jnp.zeros_like(prev_row), prev_row)
    next_row = jnp.where(i % tpb == tpb - 1, jnp.zeros_like(next_row), next_row)
    z_m1 = jnp.where(rows == 0, prev_row, pltpu.roll(z, 1, 0))
    z_p1 = jnp.where(rows == tm - 1, next_row, pltpu.roll(z, tm - 1, 0))
    cw = cw_ref[...]
    conv = z_m1 * cw[0:1] + z * cw[1:2] + z_p1 * cw[2:3]
    u = (cb_ref[...].astype(F32) * conv).astype(BF16)

    y_attn = _dot(o_ref[...], wao_ref[...])
    y_conv = _dot(u, wco_ref[...])
    m = sga_ref[...].astype(F32) * y_attn + sgc_ref[...].astype(F32) * y_conv
    x1 = x_ref[...] + g1_ref[...] * _dot(m.astype(BF16), wout_ref[...])
    x1_ref[...] = x1

    h2 = x1 * lax.rsqrt(jnp.mean(x1 * x1, axis=-1, keepdims=True) + RMS_EPS) * n2_ref[...]
    h2 = h2 * (1.0 + sc2_ref[...]) + sh2_ref[...]
    _store_packed_rows(h2_ref, 0, h2)

    route, counts = _route_tile(h2, wr_ref, br_ref, cnt_ref[0:1, :])
    route_ref[...] = route
    cnt_ref[...] = jnp.broadcast_to(counts, cnt_ref.shape)


def _route_tile(h2, wr_ref, br_ref, counts):
    tm = h2.shape[0]
    h_hi = h2.astype(BF16)
    h_lo = (h2 - h_hi.astype(F32)).astype(BF16)
    hi_both = _dot(h_hi, wr_ref[...])
    logits = (hi_both[:, :LANES] + hi_both[:, LANES:] + _dot(h_lo, wr_ref[:, :LANES])
              + br_ref[...])

    lane = lax.broadcasted_iota(jnp.int32, logits.shape, 1)
    work = logits
    vals, idxs = [], []
    for _ in range(TOP_K):
        mx = jnp.max(work, axis=-1, keepdims=True)
        ix = jnp.min(jnp.where(work == mx, lane, LANES), axis=-1, keepdims=True)
        vals.append(mx)
        idxs.append(ix)
        work = jnp.where(lane == ix, 2.0 * NEG_BIG, work)
    exps = [jnp.exp(v - vals[0]) for v in vals]
    den = exps[0] + exps[1] + exps[2] + exps[3]

    sel = (work == 2.0 * NEG_BIG).astype(BF16)
    r_i = lax.broadcasted_iota(jnp.int32, (tm, tm), 0)
    c_i = lax.broadcasted_iota(jnp.int32, (tm, tm), 1)
    lower = (r_i > c_i).astype(BF16)
    before = _dot(lower, sel) + counts
    counts = counts + jnp.sum(sel.astype(F32), axis=0, keepdims=True)

    route = jnp.zeros(logits.shape, F32)
    for k in range(TOP_K):
        rank = jnp.sum(jnp.where(lane == idxs[k], before, 0.0), axis=-1, keepdims=True)
        route = jnp.where(lane == k, exps[k] / den, route)
        route = jnp.where(lane == TOP_K + k, idxs[k].astype(F32), route)
        route = jnp.where(lane == 2 * TOP_K + k, rank, route)
    return route, counts


def _post_mix(o2, cb, z, sga, sgc, x2, g1, sc2, sh2, n2, conv_w, wao, wco, wout,
              wr_split, br_pad, seq):
    n, d = x2.shape
    tm = min(seq, 512)
    tpb = seq // tm
    halo = 16
    hb = tm // halo
    last_hb = n // halo - 1
    row = lambda i: (i, 0)
    const = lambda i: (0, 0)
    mod = lambda i: (i // tpb, 0, 0)
    wspec = pl.BlockSpec((d, d), const)
    kern = functools.partial(_post_mix_kernel, tpb=tpb)
    return pl.pallas_call(
        kern,
        out_shape=(jax.ShapeDtypeStruct((n, d), F32),
                   jax.ShapeDtypeStruct((n * TILE_ROW, LANES), jnp.uint32),
                   jax.ShapeDtypeStruct((n, LANES), F32), jax.ShapeDtypeStruct((8, LANES), F32)),
        grid=(n // tm,),
        in_specs=[pl.BlockSpec((tm, d), row), pl.BlockSpec((tm, d), row), pl.BlockSpec((tm, d), row),
                  pl.BlockSpec((halo, d), lambda i: (jnp.maximum(i * hb - 1, 0), 0)),
                  pl.BlockSpec((halo, d), lambda i: (jnp.minimum((i + 1) * hb, last_hb), 0)),
                  pl.BlockSpec((tm, d), row), pl.BlockSpec((tm, d), row), pl.BlockSpec((tm, d), row),
                  pl.BlockSpec((None, 1, d), mod), pl.BlockSpec((None, 1, d), mod),
                  pl.BlockSpec((None, 1, d), mod),
                  pl.BlockSpec((1, d), const), pl.BlockSpec((3, d), const),
                  wspec, wspec, wspec,
                  pl.BlockSpec((d, 2 * LANES), const), pl.BlockSpec((1, LANES), const)],
        out_specs=(pl.BlockSpec((tm, d), row), pl.BlockSpec((tm * TILE_ROW, LANES), row),
                   pl.BlockSpec((tm, LANES), row), pl.BlockSpec((8, LANES), const)),
        compiler_params=pltpu.CompilerParams(
            dimension_semantics=("arbitrary",), vmem_limit_bytes=VMEM_LIMIT),
        name="post_mix",
    )(o2, cb, z, z, z, sga, sgc, x2, g1, sc2, sh2, n2, conv_w, wao, wco, wout, wr_split, br_pad)


def _tile_rows(ref, row, n_rows=1):
    start = pl.multiple_of(row * TILE_ROW, TILE_ROW)
    return ref.at[pl.ds(start, n_rows * TILE_ROW)]


def _row_copy(src, dst, s, t, sem):
    return pltpu.make_async_copy(_tile_rows(src, s), _tile_rows(dst, t), sem)


def _dispatch_kernel(dest_ref, ends_ref, h2_ref, xs_hbm, zbuf, sem, zsem, *, tc, nblk):
    base = pl.program_id(0) * tc

    @pl.when(pl.program_id(0) == 0)
    def _():
        zbuf[...] = jnp.zeros_like(zbuf)

        def zero_block(row):
            return pltpu.make_async_copy(zbuf, _tile_rows(xs_hbm, row, MOE_ROWS), zsem)

        def nonempty(e):
            return ends_ref[e] > (ends_ref[e - 1] if e else 0)

        total = ends_ref[N_EXPERTS - 1]
        n_tail = nblk - total // MOE_ROWS

        def tail_start(b, carry):
            zero_block(total + b * MOE_ROWS).start()
            return carry

        def tail_wait(b, carry):
            zero_block(0).wait()
            return carry

        for e in range(N_EXPERTS):
            @pl.when(nonempty(e))
            def _(e=e):
                zero_block(ends_ref[e] - MOE_ROWS).start()
        lax.fori_loop(0, n_tail, tail_start, 0)
        for e in range(N_EXPERTS):
            @pl.when(nonempty(e))
            def _():
                zero_block(0).wait()
        lax.fori_loop(0, n_tail, tail_wait, 0)

    def issue(t, carry):
        for k in range(TOP_K):
            _row_copy(h2_ref, xs_hbm, t, dest_ref[(base + t) * TOP_K + k], sem).start(priority=k % 2)
        return carry

    lax.fori_loop(0, tc, issue, 0, unroll=4)

    for _ in range(TOP_K):
        pltpu.make_async_copy(h2_ref, _tile_rows(xs_hbm, 0, tc), sem).wait()


def _moe_dispatch(dest_flat, ends, h2_tiles, n, p_rows):
    tc = min(n, 1024)
    kern = functools.partial(_dispatch_kernel, tc=tc, nblk=p_rows // MOE_ROWS)
    return pl.pallas_call(
        kern,
        out_shape=jax.ShapeDtypeStruct((p_rows * TILE_ROW, LANES), jnp.uint32),
        grid_spec=pltpu.PrefetchScalarGridSpec(
            num_scalar_prefetch=2, grid=(n // tc,),
            in_specs=[pl.BlockSpec((tc * TILE_ROW, LANES), lambda i, dr, en: (i, 0))],
            out_specs=pl.BlockSpec(memory_space=pl.ANY),
            scratch_shapes=[pltpu.VMEM((MOE_ROWS * TILE_ROW, LANES), jnp.uint32),
                            pltpu.SemaphoreType.DMA, pltpu.SemaphoreType.DMA]),
        compiler_params=pltpu.CompilerParams(dimension_semantics=("arbitrary",)),
        name="moe_dispatch",
    )(dest_flat, ends, h2_tiles)


_NT = (((1,), (1,)), ((), ()))


def _ffn_kernel(first_ref, count_ref, slot_ref, next_ref, misc_ref,
                xs_hbm, wgu_hbm, wd_hbm, bg_ref, bu_ref, bd_ref, ys_hbm,
                wgu_buf, wd_buf, wg_scr, wu_scr, wd_scr, xbuf, ybuf,
                wsem, xsem, ysem, *, nblk):
    e = pl.program_id(0)
    first = first_ref[e]
    count = count_ref[e]
    slot = slot_ref[e]
    first_expert, total_blocks = misc_ref[0], misc_ref[1]

    def weight_copies(ex, s):
        return (pltpu.make_async_copy(wgu_hbm.at[ex], wgu_buf.at[s], wsem.at[0, s]),
                pltpu.make_async_copy(wd_hbm.at[ex], wd_buf.at[s], wsem.at[1, s]))

    def x_copy(b, s):
        return pltpu.make_async_copy(_tile_rows(xs_hbm, b * MOE_ROWS, MOE_ROWS), xbuf.at[s], xsem.at[s])

    def y_copy(b):
        return pltpu.make_async_copy(ybuf.at[b % 2], _tile_rows(ys_hbm, b * MOE_ROWS, MOE_ROWS),
                                     ysem.at[b % 2])

    @pl.when(e == first_expert)
    def _():
        for cp in weight_copies(e, 0):
            cp.start()

    @pl.when(count > 0)
    def _():
        x_copy(first, 0).start()
        for cp in weight_copies(e, slot):
            cp.wait()

        @pl.when(next_ref[e] >= 0)
        def _():
            for cp in weight_copies(next_ref[e], 1 - slot):
                cp.start(priority=1)

        d = wgu_buf.shape[1]
        for c in range(d // LANES):
            cols = slice(c * LANES, (c + 1) * LANES)
            words = pltpu.bitcast(wgu_buf[slot, cols, :].astype(BF16).T, jnp.uint32)
            wg_scr[:, cols] = pltpu.bitcast(words << 16, F32).astype(BF16)
            wu_scr[:, cols] = pltpu.bitcast(words & jnp.uint32(_HIGH_HALF), F32).astype(BF16)
        wd_scr[...] = wd_buf[slot].astype(BF16)
        bg, bu, bd = bg_ref[e], bu_ref[e], bd_ref[e]

        def block(j, carry):
            s = j % 2
            b = first + j
            x_copy(b, s).wait()

            @pl.when(j + 1 < count)
            def _():
                x_copy(b + 1, 1 - s).start()

            @pl.when(b >= 2)
            def _():
                y_copy(b - 2).wait()

            x = jnp.concatenate(_load_packed_rows(xbuf.at[s], MOE_ROWS), axis=1).astype(BF16)
            gate = lax.dot_general(x, wg_scr[...], _NT, preferred_element_type=F32) + bg
            up = lax.dot_general(x, wu_scr[...], _NT, preferred_element_type=F32) + bu
            gate = jnp.minimum(gate, SWIGLU_LIMIT)
            up = jnp.clip(up, -SWIGLU_LIMIT, SWIGLU_LIMIT)
            glu = gate * jax.nn.sigmoid(gate * SWIGLU_ALPHA)
            mid = ((up + 1.0) * glu).astype(BF16)
            _store_packed_rows(ybuf.at[b % 2], 0, _dot(mid, wd_scr[...]) + bd)
            y_copy(b).start()
            return carry

        lax.fori_loop(0, count, block, 0)

    @pl.when(e == pl.num_programs(0) - 1)
    def _():
        @pl.when(total_blocks >= 2)
        def _():
            y_copy(total_blocks - 2).wait()
        y_copy(total_blocks - 1).wait()

        ybuf[...] = jnp.zeros_like(ybuf)

        def tail_start(b, carry):
            y_copy(b).start()
            return carry

        def tail_wait(b, carry):
            y_copy(b).wait()
            return carry

        lax.fori_loop(total_blocks, nblk, tail_start, 0)
        lax.fori_loop(total_blocks, nblk, tail_wait, 0)


def _moe_ffn(first_blk, n_blk, slot, next_e, misc, xs, wgu, wd, bg, bu, bd):
    p_rows = xs.shape[0] // TILE_ROW
    n_exp, d, f2 = wgu.shape
    f = f2 // 2
    assert d == 2 * TILE_ROW * LANES
    whole = lambda shape: pl.BlockSpec(shape, lambda e, *_: (0,) * len(shape))
    tile = (MOE_ROWS * TILE_ROW, LANES)
    kern = functools.partial(_ffn_kernel, nblk=p_rows // MOE_ROWS)
    return pl.pallas_call(
        kern,
        out_shape=jax.ShapeDtypeStruct((p_rows * TILE_ROW, LANES), jnp.uint32),
        grid_spec=pltpu.PrefetchScalarGridSpec(
            num_scalar_prefetch=5, grid=(n_exp,),
            in_specs=[pl.BlockSpec(memory_space=pl.ANY),
                      pl.BlockSpec(memory_space=pl.ANY), pl.BlockSpec(memory_space=pl.ANY),
                      whole(bg.shape), whole(bu.shape), whole(bd.shape)],
            out_specs=pl.BlockSpec(memory_space=pl.ANY),
            scratch_shapes=[pltpu.VMEM((2, d, f2), F32), pltpu.VMEM((2, f, d), F32),
                            pltpu.VMEM((f, d), BF16), pltpu.VMEM((f, d), BF16),
                            pltpu.VMEM((f, d), BF16),
                            pltpu.VMEM((2,) + tile, jnp.uint32), pltpu.VMEM((2,) + tile, jnp.uint32),
                            pltpu.SemaphoreType.DMA((2, 2)), pltpu.SemaphoreType.DMA((2,)),
                            pltpu.SemaphoreType.DMA((2,))]),
        compiler_params=pltpu.CompilerParams(
            dimension_semantics=("arbitrary",), vmem_limit_bytes=VMEM_LIMIT),
        name="moe_ffn",
    )(first_blk, n_blk, slot, next_e, misc, xs, wgu, wd, bg, bu, bd)


def _combine_kernel(dest_ref, ys_hbm, x1_ref, route_ref, g2_ref, o_ref, buf, sem, *, tc, n_steps):
    i = pl.program_id(0)
    slot = i % 2

    def gather(step, s):
        def issue(t, carry):
            for k in range(TOP_K):
                row = dest_ref[(step * tc + t) * TOP_K + k]
                _row_copy(ys_hbm, buf.at[s, k], row, t, sem.at[s]).start(priority=k % 2)
            return carry

        lax.fori_loop(0, tc, issue, 0, unroll=4)

    @pl.when(i == 0)
    def _():
        gather(0, 0)

    @pl.when(i + 1 < n_steps)
    def _():
        gather(i + 1, 1 - slot)

    for k in range(TOP_K):
        pltpu.make_async_copy(_tile_rows(ys_hbm, 0, tc), buf.at[slot, k], sem.at[slot]).wait()

    route = route_ref[...]
    chunks = [_load_packed_rows(buf.at[slot, k], tc) for k in range(TOP_K)]
    for c in range(2 * TILE_ROW):
        cols = slice(c * LANES, (c + 1) * LANES)
        y = chunks[0][c] * route[:, 0:1]
        for k in range(1, TOP_K):
            y = y + chunks[k][c] * route[:, k:k + 1]
        o_ref[:, cols] = x1_ref[:, cols] + g2_ref[:, cols] * y


def _moe_combine(dest_flat, ys, x1, route, g2, seq):
    n, d = x1.shape
    assert d == 2 * TILE_ROW * LANES
    tc = min(seq, 512)
    tpb = seq // tc
    kern = functools.partial(_combine_kernel, tc=tc, n_steps=n // tc)
    return pl.pallas_call(
        kern,
        out_shape=jax.ShapeDtypeStruct((n, d), F32),
        grid_spec=pltpu.PrefetchScalarGridSpec(
            num_scalar_prefetch=1, grid=(n // tc,),
            in_specs=[pl.BlockSpec(memory_space=pl.ANY),
                      pl.BlockSpec((tc, d), lambda i, dr: (i, 0)),
                      pl.BlockSpec((tc, LANES), lambda i, dr: (i, 0)),
                      pl.BlockSpec((None, 1, d), lambda i, dr: (i // tpb, 0, 0))],
            out_specs=pl.BlockSpec((tc, d), lambda i, dr: (i, 0)),
            scratch_shapes=[pltpu.VMEM((2, TOP_K, tc * TILE_ROW, LANES), jnp.uint32),
                            pltpu.SemaphoreType.DMA((2,))]),
        compiler_params=pltpu.CompilerParams(
            dimension_semantics=("arbitrary",), vmem_limit_bytes=VMEM_LIMIT),
        name="moe_combine",
    )(dest_flat, ys, x1, route, g2)


def _rope_tables(seq):
    inv_freq = (ROPE_THETA ** (-np.arange(0, HEAD_DIM, 2, dtype=np.float32) / HEAD_DIM)).astype(np.float32)
    ang = (inv_freq[:, None] * np.arange(seq, dtype=np.float32)[None, :]).astype(np.float32)
    cos, sin = np.cos(ang.astype(np.float64)), np.sin(ang.astype(np.float64))
    cos_t = np.concatenate([cos, cos], axis=0).astype(np.float32)
    sin_t = np.concatenate([-sin, sin], axis=0).astype(np.float32)
    return jnp.asarray(cos_t), jnp.asarray(sin_t)


def _layer(x, c, l, lambda_init, w_ada, b_ada, norm1_w, w_in, q_norm_w, k_norm_w, lambda_q1,
           lambda_k1, lambda_q2, lambda_k2, subln_w, w_attn_o, conv_w, w_conv_o, w_out, norm2_w,
           w_router, b_router, w_gate_up, b_gate_up, w_down, b_down):
    bsz, seq, d = x.shape
    n = bsz * seq
    x2 = x.reshape(n, d)

    mod = _ada_mod(c, w_ada[l], b_ada[l])
    sh1, sc1, g1, sh2, sc2, g2 = [m.reshape(bsz, 1, d) for m in jnp.split(mod, 6, axis=-1)]

    cos_t, sin_t = _rope_tables(seq)
    qt, k, vt, cb, z, sga, sgc, qn, kn = _in_proj(
        x2, norm1_w[l].reshape(1, d), sc1, sh1, w_in[l].astype(BF16),
        q_norm_w[l].reshape(HEAD_DIM, 1) * Q_SCALE, k_norm_w[l].reshape(HEAD_DIM, 1),
        cos_t, sin_t, bsz, seq)

    lam_vecs = jnp.stack([lambda_q1[l], lambda_k1[l], lambda_q2[l], lambda_k2[l]]).astype(F32)
    o = _diff_attn(qt, k.reshape(bsz, seq, d), vt, qn, kn, lam_vecs, subln_w[l].reshape(V_DIM, 1),
                   lambda_init)

    pad = LANES - N_EXPERTS
    wr = jnp.pad(w_router[l].astype(F32), ((0, 0), (0, pad)))
    wr_hi = wr.astype(BF16)
    wr_split = jnp.concatenate([wr_hi, (wr - wr_hi.astype(F32)).astype(BF16)], axis=1)
    br_pad = jnp.pad(b_router[l].astype(F32), (0, pad), constant_values=NEG_BIG).reshape(1, LANES)
    x1, h2, route, cnt = _post_mix(
        o.reshape(n, d), cb, z, sga, sgc, x2, g1, sc2, sh2, norm2_w[l].reshape(1, d), conv_w[l],
        w_attn_o[l].astype(BF16), w_conv_o[l].astype(BF16), w_out[l].astype(BF16),
        wr_split, br_pad, seq)

    counts = cnt[0, :N_EXPERTS].astype(jnp.int32)
    padded = ((counts + MOE_ROWS - 1) // MOE_ROWS) * MOE_ROWS
    ends = jnp.cumsum(padded)
    start = ends - padded
    p_rows = n * TOP_K + N_EXPERTS * MOE_ROWS
    top_e = route[:, TOP_K:2 * TOP_K].astype(jnp.int32)
    rank = route[:, 2 * TOP_K:3 * TOP_K].astype(jnp.int32)
    experts = jnp.arange(N_EXPERTS, dtype=jnp.int32)
    seg_start = jnp.sum(jnp.where(top_e[..., None] == experts, start, 0), axis=-1)
    dest = (seg_start + rank).reshape(-1)
    nonempty = padded > 0
    later = jnp.logical_and(nonempty[None, :], experts[None, :] > experts[:, None])
    next_e = jnp.min(jnp.where(later, experts[None, :], N_EXPERTS), axis=1)
    next_e = jnp.where(next_e == N_EXPERTS, -1, next_e).astype(jnp.int32)
    slot = ((jnp.cumsum(nonempty.astype(jnp.int32)) - 1) % 2).astype(jnp.int32)
    first_blk = (start // MOE_ROWS).astype(jnp.int32)
    n_blk = (padded // MOE_ROWS).astype(jnp.int32)
    misc = jnp.stack([jnp.argmax(nonempty).astype(jnp.int32), (ends[-1] // MOE_ROWS).astype(jnp.int32)])

    xs = _moe_dispatch(dest, ends.astype(jnp.int32), h2, n, p_rows)
    ys = _moe_ffn(first_blk, n_blk, slot, next_e, misc, xs, w_gate_up[l], w_down[l],
                  b_gate_up[l][:, None, 0::2], b_gate_up[l][:, None, 1::2], b_down[l][:, None, :])
    out = _moe_combine(dest, ys, x1, route, g2, seq)
    return out.reshape(bsz, seq, d)


def kernel(x, c, w_ada, b_ada, norm1_w, w_in, q_norm_w, k_norm_w, lambda_q1, lambda_k1, lambda_q2,
           lambda_k2, subln_w, w_attn_o, conv_w, w_conv_o, w_out, norm2_w, w_router, b_router,
           w_gate_up, b_gate_up, w_down, b_down):
    depth = w_ada.shape[0]
    for l in range(depth):
        lambda_init = 0.8 - 0.6 * math.exp(-0.3 * l)
        x = _layer(x, c, l, lambda_init, w_ada, b_ada, norm1_w, w_in, q_norm_w, k_norm_w,
                   lambda_q1, lambda_k1, lambda_q2, lambda_k2, subln_w, w_attn_o, conv_w,
                   w_conv_o, w_out, norm2_w, w_router, b_router, w_gate_up, b_gate_up,
                   w_down, b_down)
    return x
```

```python
import functools
import math

import jax
import jax.numpy as jnp
import numpy as np
from jax import lax
from jax.experimental import pallas as pl
from jax.experimental.pallas import tpu as pltpu

N_HEADS = 8
HEAD_DIM = 64
V_DIM = 2 * HEAD_DIM
IN_BLOCKS = ("q", "k", "v", "conv_b", "conv_c", "conv_x", "gate_attn", "gate_conv")
N_EXPERTS = 32
TOP_K = 4
SWIGLU_LIMIT = 7.0
SWIGLU_ALPHA = 1.702
ROPE_THETA = 10000.0
RMS_EPS = 1e-6
SUBLN_EPS = 1e-5
LANES = 128
TILE_ROW = 4
ADA_COLS = 1536
ROW_TILE = 512
ATTN_Q_TILE = 1024
ATTN_K_TILE = 4096
MOE_ROWS = 256
DISPATCH_TOKENS = 2048
COMBINE_TOKENS = 512
VMEM_LIMIT = 56 * 1024 * 1024
NEG_BIG = -1e30
LOG2E = 1.4426950408889634
Q_SCALE = LOG2E / math.sqrt(HEAD_DIM)
SAFE_EXP2_BOUND = 80.0

F32 = jnp.float32
BF16 = jnp.bfloat16


def _dot(a, b):
    return jnp.dot(a, b, preferred_element_type=F32)


_HIGH_HALF = 0xFFFF0000


def _bf16_bits(x):
    return pltpu.bitcast(x.astype(BF16).astype(F32), jnp.uint32)


def _store_packed_rows(ref, row0, val):
    rows, d = val.shape
    half = d // 2
    for c in range(TILE_ROW):
        lo = _bf16_bits(val[:, c * LANES:(c + 1) * LANES]) >> 16
        hi = _bf16_bits(val[:, half + c * LANES:half + (c + 1) * LANES])
        ref[pl.ds(row0 * TILE_ROW + c, rows, stride=TILE_ROW), :] = lo | hi


def _load_packed_rows(ref, rows):
    words = [ref[pl.ds(c, rows, stride=TILE_ROW), :] for c in range(TILE_ROW)]
    lo = [pltpu.bitcast(w << 16, F32) for w in words]
    hi = [pltpu.bitcast(w & jnp.uint32(_HIGH_HALF), F32) for w in words]
    return lo + hi


def _ada_kernel(ct_ref, w_ref, b_ref, o_ref):
    ct = ct_ref[...]
    s = ct * jax.nn.sigmoid(ct)
    w = w_ref[...]
    for b in range(ct.shape[1]):
        o_ref[b:b + 1, :] = jnp.sum(w * s[:, b:b + 1], axis=0, keepdims=True) + b_ref[...]


def _ada_mod(c, w_ada, b_ada):
    bsz, d = c.shape
    n = w_ada.shape[1]
    tn = min(n, ADA_COLS)
    return pl.pallas_call(
        _ada_kernel,
        out_shape=jax.ShapeDtypeStruct((bsz, n), F32),
        grid=(n // tn,),
        in_specs=[pl.BlockSpec((d, bsz), lambda j: (0, 0)),
                  pl.BlockSpec((d, tn), lambda j: (0, j)),
                  pl.BlockSpec((1, tn), lambda j: (0, j))],
        out_specs=pl.BlockSpec((bsz, tn), lambda j: (0, j)),
        compiler_params=pltpu.CompilerParams(dimension_semantics=("arbitrary",)),
        name="ada_mod",
    )(c.T, w_ada, b_ada.reshape(1, n))


def _qk_norm_rope_t(y, g_col, cos_t, sin_t):
    tm, w = y.shape
    yt = y.T.reshape(w // HEAD_DIM, HEAD_DIM, tm)
    ms = jnp.mean(yt * yt, axis=1, keepdims=True)
    yn = yt * lax.rsqrt(ms + RMS_EPS) * g_col[None]
    half = HEAD_DIM // 2
    swapped = jnp.concatenate([yn[:, half:, :], yn[:, :half, :]], axis=1)
    out = yn * cos_t[None] + swapped * sin_t[None]
    norm2 = jnp.sum(out * out, axis=1).reshape(N_HEADS, 2, tm)
    return out.reshape(w, tm), norm2


def _in_proj_kernel(x_ref, n1_ref, sc_ref, sh_ref, w_ref, gq_ref, gk_ref, cos_ref, sin_ref,
                    qt_ref, k_ref, vt_ref, cb_ref, z_ref, sga_ref, sgc_ref, qn_ref, kn_ref, h_scr):
    x = x_ref[...]
    xn = x * lax.rsqrt(jnp.mean(x * x, axis=-1, keepdims=True) + RMS_EPS) * n1_ref[...]
    h_scr[...] = (xn * (1.0 + sc_ref[...]) + sh_ref[...]).astype(BF16)
    wcol = w_ref.shape[1] // len(IN_BLOCKS)

    def proj(name):
        c = IN_BLOCKS.index(name)
        return _dot(h_scr[...], w_ref[:, c * wcol:(c + 1) * wcol])

    qt, qn = _qk_norm_rope_t(proj("q"), gq_ref[...], cos_ref[...], sin_ref[...])
    qt_ref[...] = qt.astype(BF16)
    qn_ref[...] = qn
    kt, kn = _qk_norm_rope_t(proj("k"), gk_ref[...], cos_ref[...], sin_ref[...])
    k_ref[...] = kt.T.astype(BF16)
    kn_ref[...] = kn
    vt_ref[...] = proj("v").T.astype(BF16)
    cb_ref[...] = proj("conv_b").astype(BF16)
    z_ref[...] = (proj("conv_c") * proj("conv_x")).astype(BF16)
    sga_ref[...] = jax.nn.sigmoid(proj("gate_attn")).astype(BF16)
    sgc_ref[...] = jax.nn.sigmoid(proj("gate_conv")).astype(BF16)


def _in_proj(x2, n1, sc1, sh1, w_in_bf, gq, gk, cos_t, sin_t, bsz, seq):
    n, d = x2.shape
    tm = min(seq, ROW_TILE)
    tpb = seq // tm
    assert w_in_bf.shape[1] == len(IN_BLOCKS) * d
    row = lambda i: (i, 0)
    const = lambda i: (0, 0)
    tcol = lambda i: (i // tpb, 0, i % tpb)
    mod = lambda i: (i // tpb, 0, 0)
    nat = jax.ShapeDtypeStruct((n, d), BF16)
    tr = jax.ShapeDtypeStruct((bsz, d, seq), BF16)
    nrm = jax.ShapeDtypeStruct((bsz, N_HEADS, 2, seq), F32)
    nat_spec = pl.BlockSpec((tm, d), row)
    tr_spec = pl.BlockSpec((None, d, tm), tcol)
    nrm_spec = pl.BlockSpec((None, N_HEADS, 2, tm), lambda i: (i // tpb, 0, 0, i % tpb))
    return pl.pallas_call(
        _in_proj_kernel,
        out_shape=(tr, nat, tr, nat, nat, nat, nat, nrm, nrm),
        grid=(n // tm,),
        in_specs=[pl.BlockSpec((tm, d), row),
                  pl.BlockSpec((1, d), const),
                  pl.BlockSpec((None, 1, d), mod),
                  pl.BlockSpec((None, 1, d), mod),
                  pl.BlockSpec(w_in_bf.shape, const),
                  pl.BlockSpec((HEAD_DIM, 1), const),
                  pl.BlockSpec((HEAD_DIM, 1), const),
                  pl.BlockSpec((HEAD_DIM, tm), lambda i: (0, i % tpb)),
                  pl.BlockSpec((HEAD_DIM, tm), lambda i: (0, i % tpb))],
        out_specs=(tr_spec, nat_spec, tr_spec, nat_spec, nat_spec, nat_spec, nat_spec,
                   nrm_spec, nrm_spec),
        scratch_shapes=[pltpu.VMEM((tm, d), BF16)],
        compiler_params=pltpu.CompilerParams(
            dimension_semantics=("arbitrary",), vmem_limit_bytes=VMEM_LIMIT),
        name="in_proj",
    )(x2, n1, sc1, sh1, w_in_bf, gq, gk, cos_t, sin_t)


def _diff_attn_kernel(qt_ref, k_ref, vt_ref, qn_ref, kn_ref, lam_ref, sw_ref, o_ref, acc0, acc1,
                      *, tk, lambda_init):
    seq = k_ref.shape[0]
    tq = qt_ref.shape[1]
    qt = qt_ref[...]
    first = lax.broadcasted_iota(jnp.int32, qt.shape, 0) < HEAD_DIM
    zero = jnp.zeros_like(qt)
    qz = (jnp.where(first, qt, zero), jnp.where(first, zero, qt))
    accs = (acc0, acc1)
    acc0[...] = jnp.zeros_like(acc0)
    acc1[...] = jnp.zeros_like(acc1)
    n_chunks = seq // tk

    def load(j):
        off = pl.multiple_of(j * tk, tk)
        return k_ref[pl.ds(off, tk), :], vt_ref[:, pl.ds(off, tk)]

    def plain_body(j, carry):
        kk, vt = load(j)
        new = []
        for c in range(2):
            p = jnp.exp2(_dot(kk, qz[c]))
            new.append(carry[c] + jnp.sum(p, axis=0, keepdims=True))
            accs[c][...] += _dot(vt, p.astype(BF16))
        return tuple(new)

    def online_body(j, carry):
        kk, vt = load(j)
        new = []
        for c in range(2):
            m, l = carry[2 * c], carry[2 * c + 1]
            s = _dot(kk, qz[c])
            m_new = jnp.maximum(m, jnp.max(s, axis=0, keepdims=True))
            alpha = jnp.exp2(m - m_new)
            p = jnp.exp2(s - m_new)
            l = alpha * l + jnp.sum(p, axis=0, keepdims=True)
            accs[c][...] = alpha * accs[c][...] + _dot(vt, p.astype(BF16))
            new += [m_new, l]
        return tuple(new)

    m_init = jnp.full((1, tq), NEG_BIG, F32)
    l_init = jnp.zeros((1, tq), F32)

    def plain():
        return lax.fori_loop(0, n_chunks, plain_body, (l_init, l_init))

    def online():
        _, l0, _, l1 = lax.fori_loop(0, n_chunks, online_body, (m_init, l_init, m_init, l_init))
        return l0, l1

    bound2 = jnp.max(jnp.max(qn_ref[...], axis=-1, keepdims=True)
                     * jnp.max(kn_ref[...], axis=-1, keepdims=True))
    l0, l1 = lax.cond(bound2 <= SAFE_EXP2_BOUND * SAFE_EXP2_BOUND, plain, online)

    lq = lam_ref[...]
    lam = (jnp.exp(jnp.sum(lq[0:1] * lq[1:2], axis=-1, keepdims=True))
           - jnp.exp(jnp.sum(lq[2:3] * lq[3:4], axis=-1, keepdims=True)) + lambda_init)
    o = acc0[...] / l0 - lam * (acc1[...] / l1)
    o = o * lax.rsqrt(jnp.mean(o * o, axis=0, keepdims=True) + SUBLN_EPS)
    o = o * sw_ref[...] * (1.0 - lambda_init)
    o_ref[...] = o.T.astype(BF16)


def _diff_attn(qt, k3, vt, qn, kn, lam_vecs, subln_col, lambda_init):
    bsz, d, seq = qt.shape
    tq = min(seq, ATTN_Q_TILE)
    tk = min(seq, ATTN_K_TILE)
    kern = functools.partial(_diff_attn_kernel, tk=tk, lambda_init=lambda_init)
    return pl.pallas_call(
        kern,
        out_shape=jax.ShapeDtypeStruct((bsz, seq, d), BF16),
        grid=(bsz, N_HEADS, seq // tq),
        in_specs=[pl.BlockSpec((None, V_DIM, tq), lambda b, h, i: (b, h, i)),
                  pl.BlockSpec((None, seq, V_DIM), lambda b, h, i: (b, 0, h)),
                  pl.BlockSpec((None, V_DIM, seq), lambda b, h, i: (b, h, 0)),
                  pl.BlockSpec((None, None, 2, tq), lambda b, h, i: (b, h, 0, i)),
                  pl.BlockSpec((None, None, 2, seq), lambda b, h, i: (b, h, 0, 0)),
                  pl.BlockSpec((4, HEAD_DIM), lambda b, h, i: (0, 0)),
                  pl.BlockSpec((V_DIM, 1), lambda b, h, i: (0, 0))],
        out_specs=pl.BlockSpec((None, tq, V_DIM), lambda b, h, i: (b, i, h)),
        scratch_shapes=[pltpu.VMEM((V_DIM, tq), F32), pltpu.VMEM((V_DIM, tq), F32)],
        compiler_params=pltpu.CompilerParams(
            dimension_semantics=("arbitrary", "arbitrary", "arbitrary"),
            vmem_limit_bytes=VMEM_LIMIT),
        name="diff_attn",
    )(qt, k3, vt, qn, kn, lam_vecs, subln_col)


def _post_mix_kernel(o_ref, cb_ref, z_ref, zp_ref, zn_ref, sga_ref, sgc_ref, x_ref,
                     g1_ref, sc2_ref, sh2_ref, n2_ref, cw_ref, wao_ref, wco_ref, wout_ref,
                     wr_ref, br_ref,
                     x1_ref, h2_ref, route_ref, cnt_ref, *, tpb):
    i = pl.program_id(0)
    tm = x_ref.shape[0]

    @pl.when(i == 0)
    def _():
        cnt_ref[...] = jnp.zeros_like(cnt_ref)

    z = z_ref[...].astype(F32)
    rows = lax.broadcasted_iota(jnp.int32, z.shape, 0)
    halo_rows = zp_ref.shape[0]
    prev_row = zp_ref[halo_rows - 1:halo_rows, :].astype(F32)
    next_row = zn_ref[0:1, :].astype(F32)
    prev_row = jnp.where(i % tpb == 0, jnp.zeros_like(prev_row), prev_row)
    next_row = jnp.where(i % tpb == tpb - 1, jnp.zeros_like(next_row), next_row)
    z_m1 = jnp.where(rows == 0, prev_row, pltpu.roll(z, 1, 0))
    z_p1 = jnp.where(rows == tm - 1, next_row, pltpu.roll(z, tm - 1, 0))
    cw = cw_ref[...]
    conv = z_m1 * cw[0:1] + z * cw[1:2] + z_p1 * cw[2:3]
    u = (cb_ref[...].astype(F32) * conv).astype(BF16)

    y_attn = _dot(o_ref[...], wao_ref[...])
    y_conv = _dot(u, wco_ref[...])
    m = sga_ref[...].astype(F32) * y_attn + sgc_ref[...].astype(F32) * y_conv
    x1 = x_ref[...] + g1_ref[...] * _dot(m.astype(BF16), wout_ref[...])
    x1_ref[...] = x1

    h2 = x1 * lax.rsqrt(jnp.mean(x1 * x1, axis=-1, keepdims=True) + RMS_EPS) * n2_ref[...]
    h2 = h2 * (1.0 + sc2_ref[...]) + sh2_ref[...]
    _store_packed_rows(h2_ref, 0, h2)

    route, counts = _route_tile(h2, wr_ref, br_ref, cnt_ref[0:1, :])
    route_ref[...] = route
    cnt_ref[...] = jnp.broadcast_to(counts, cnt_ref.shape)


def _route_tile(h2, wr_ref, br_ref, counts):
    tm = h2.shape[0]
    h_hi = h2.astype(BF16)
    h_lo = (h2 - h_hi.astype(F32)).astype(BF16)
    hi_both = _dot(h_hi, wr_ref[...])
    logits = (hi_both[:, :LANES] + hi_both[:, LANES:] + _dot(h_lo, wr_ref[:, :LANES])
              + br_ref[...])

    lane = lax.broadcasted_iota(jnp.int32, logits.shape, 1)
    work = logits
    vals, idxs = [], []
    for _ in range(TOP_K):
        mx = jnp.max(work, axis=-1, keepdims=True)
        ix = jnp.min(jnp.where(work == mx, lane, LANES), axis=-1, keepdims=True)
        vals.append(mx)
        idxs.append(ix)
        work = jnp.where(lane == ix, 2.0 * NEG_BIG, work)
    exps = [jnp.exp(v - vals[0]) for v in vals]
    den = exps[0] + exps[1] + exps[2] + exps[3]

    sel = (work == 2.0 * NEG_BIG).astype(BF16)
    r_i = lax.broadcasted_iota(jnp.int32, (tm, tm), 0)
    c_i = lax.broadcasted_iota(jnp.int32, (tm, tm), 1)
    lower = (r_i > c_i).astype(BF16)
    before = _dot(lower, sel) + counts
    counts = counts + jnp.sum(sel.astype(F32), axis=0, keepdims=True)

    route = jnp.zeros(logits.shape, F32)
    for k in range(TOP_K):
        rank = jnp.sum(jnp.where(lane == idxs[k], before, 0.0), axis=-1, keepdims=True)
        route = jnp.where(lane == k, exps[k] / den, route)
        route = jnp.where(lane == TOP_K + k, idxs[k].astype(F32), route)
        route = jnp.where(lane == 2 * TOP_K + k, rank, route)
    return route, counts


def _post_mix(o2, cb, z, sga, sgc, x2, g1, sc2, sh2, n2, conv_w, wao, wco, wout,
              wr_split, br_pad, seq):
    n, d = x2.shape
    tm = min(seq, ROW_TILE)
    tpb = seq // tm
    halo = 16
    hb = tm // halo
    last_hb = n // halo - 1
    row = lambda i: (i, 0)
    const = lambda i: (0, 0)
    mod = lambda i: (i // tpb, 0, 0)
    wspec = pl.BlockSpec((d, d), const)
    kern = functools.partial(_post_mix_kernel, tpb=tpb)
    return pl.pallas_call(
        kern,
        out_shape=(jax.ShapeDtypeStruct((n, d), F32),
                   jax.ShapeDtypeStruct((n * TILE_ROW, LANES), jnp.uint32),
                   jax.ShapeDtypeStruct((n, LANES), F32), jax.ShapeDtypeStruct((8, LANES), F32)),
        grid=(n // tm,),
        in_specs=[pl.BlockSpec((tm, d), row), pl.BlockSpec((tm, d), row), pl.BlockSpec((tm, d), row),
                  pl.BlockSpec((halo, d), lambda i: (jnp.maximum(i * hb - 1, 0), 0)),
                  pl.BlockSpec((halo, d), lambda i: (jnp.minimum((i + 1) * hb, last_hb), 0)),
                  pl.BlockSpec((tm, d), row), pl.BlockSpec((tm, d), row), pl.BlockSpec((tm, d), row),
                  pl.BlockSpec((None, 1, d), mod), pl.BlockSpec((None, 1, d), mod),
                  pl.BlockSpec((None, 1, d), mod),
                  pl.BlockSpec((1, d), const), pl.BlockSpec((3, d), const),
                  wspec, wspec, wspec,
                  pl.BlockSpec((d, 2 * LANES), const), pl.BlockSpec((1, LANES), const)],
        out_specs=(pl.BlockSpec((tm, d), row), pl.BlockSpec((tm * TILE_ROW, LANES), row),
                   pl.BlockSpec((tm, LANES), row), pl.BlockSpec((8, LANES), const)),
        compiler_params=pltpu.CompilerParams(
            dimension_semantics=("arbitrary",), vmem_limit_bytes=VMEM_LIMIT),
        name="post_mix",
    )(o2, cb, z, z, z, sga, sgc, x2, g1, sc2, sh2, n2, conv_w, wao, wco, wout, wr_split, br_pad)


def _tile_rows(ref, row, n_rows=1):
    start = pl.multiple_of(row * TILE_ROW, TILE_ROW)
    return ref.at[pl.ds(start, n_rows * TILE_ROW)]


def _row_copy(src, dst, s, t, sem):
    return pltpu.make_async_copy(_tile_rows(src, s), _tile_rows(dst, t), sem)


def _dispatch_kernel(dest_ref, ends_ref, h2_ref, xs_hbm, zbuf, sem, zsem, *, tc, nblk):
    base = pl.program_id(0) * tc

    @pl.when(pl.program_id(0) == 0)
    def _():
        zbuf[...] = jnp.zeros_like(zbuf)

        def zero_block(row):
            return pltpu.make_async_copy(zbuf, _tile_rows(xs_hbm, row, MOE_ROWS), zsem)

        def nonempty(e):
            return ends_ref[e] > (ends_ref[e - 1] if e else 0)

        total = ends_ref[N_EXPERTS - 1]
        n_tail = nblk - total // MOE_ROWS

        def tail_start(b, carry):
            zero_block(total + b * MOE_ROWS).start()
            return carry

        def tail_wait(b, carry):
            zero_block(0).wait()
            return carry

        for e in range(N_EXPERTS):
            @pl.when(nonempty(e))
            def _(e=e):
                zero_block(ends_ref[e] - MOE_ROWS).start()
        lax.fori_loop(0, n_tail, tail_start, 0)
        for e in range(N_EXPERTS):
            @pl.when(nonempty(e))
            def _():
                zero_block(0).wait()
        lax.fori_loop(0, n_tail, tail_wait, 0)

    def issue(t, carry):
        for k in range(TOP_K):
            _row_copy(h2_ref, xs_hbm, t, dest_ref[(base + t) * TOP_K + k], sem).start(priority=k % 2)
        return carry

    lax.fori_loop(0, tc, issue, 0, unroll=4)

    for _ in range(TOP_K):
        pltpu.make_async_copy(h2_ref, _tile_rows(xs_hbm, 0, tc), sem).wait()


def _moe_dispatch(dest_flat, ends, h2_tiles, n, p_rows):
    tc = min(n, DISPATCH_TOKENS)
    kern = functools.partial(_dispatch_kernel, tc=tc, nblk=p_rows // MOE_ROWS)
    return pl.pallas_call(
        kern,
        out_shape=jax.ShapeDtypeStruct((p_rows * TILE_ROW, LANES), jnp.uint32),
        grid_spec=pltpu.PrefetchScalarGridSpec(
            num_scalar_prefetch=2, grid=(n // tc,),
            in_specs=[pl.BlockSpec((tc * TILE_ROW, LANES), lambda i, dr, en: (i, 0))],
            out_specs=pl.BlockSpec(memory_space=pl.ANY),
            scratch_shapes=[pltpu.VMEM((MOE_ROWS * TILE_ROW, LANES), jnp.uint32),
                            pltpu.SemaphoreType.DMA, pltpu.SemaphoreType.DMA]),
        compiler_params=pltpu.CompilerParams(dimension_semantics=("arbitrary",)),
        name="moe_dispatch",
    )(dest_flat, ends, h2_tiles)


_NT = (((1,), (1,)), ((), ()))


def _ffn_kernel(first_ref, count_ref, slot_ref, next_ref, misc_ref,
                xs_hbm, wgu_hbm, wd_hbm, bg_ref, bu_ref, bd_ref, ys_hbm,
                wgu_buf, wd_buf, wg_scr, wu_scr, wd_scr, xbuf, ybuf,
                wsem, xsem, ysem, *, nblk):
    e = pl.program_id(0)
    first = first_ref[e]
    count = count_ref[e]
    slot = slot_ref[e]
    first_expert, total_blocks = misc_ref[0], misc_ref[1]

    def weight_copies(ex, s):
        return (pltpu.make_async_copy(wgu_hbm.at[ex], wgu_buf.at[s], wsem.at[0, s]),
                pltpu.make_async_copy(wd_hbm.at[ex], wd_buf.at[s], wsem.at[1, s]))

    def x_copy(b, s):
        return pltpu.make_async_copy(_tile_rows(xs_hbm, b * MOE_ROWS, MOE_ROWS), xbuf.at[s], xsem.at[s])

    def y_copy(b):
        return pltpu.make_async_copy(ybuf.at[b % 2], _tile_rows(ys_hbm, b * MOE_ROWS, MOE_ROWS),
                                     ysem.at[b % 2])

    @pl.when(e == first_expert)
    def _():
        for cp in weight_copies(e, 0):
            cp.start()

    @pl.when(count > 0)
    def _():
        x_copy(first, 0).start()
        for cp in weight_copies(e, slot):
            cp.wait()

        @pl.when(next_ref[e] >= 0)
        def _():
            for cp in weight_copies(next_ref[e], 1 - slot):
                cp.start(priority=1)

        d = wgu_buf.shape[1]
        for c in range(d // LANES):
            cols = slice(c * LANES, (c + 1) * LANES)
            words = pltpu.bitcast(wgu_buf[slot, cols, :].astype(BF16).T, jnp.uint32)
            wg_scr[:, cols] = pltpu.bitcast(words << 16, F32).astype(BF16)
            wu_scr[:, cols] = pltpu.bitcast(words & jnp.uint32(_HIGH_HALF), F32).astype(BF16)
        wd_scr[...] = wd_buf[slot].astype(BF16)
        bg, bu, bd = bg_ref[e], bu_ref[e], bd_ref[e]

        def block(j, carry):
            s = j % 2
            b = first + j
            x_copy(b, s).wait()

            @pl.when(j + 1 < count)
            def _():
                x_copy(b + 1, 1 - s).start()

            @pl.when(b >= 2)
            def _():
                y_copy(b - 2).wait()

            x = jnp.concatenate(_load_packed_rows(xbuf.at[s], MOE_ROWS), axis=1).astype(BF16)
            gate = lax.dot_general(x, wg_scr[...], _NT, preferred_element_type=F32) + bg
            up = lax.dot_general(x, wu_scr[...], _NT, preferred_element_type=F32) + bu
            gate = jnp.minimum(gate, SWIGLU_LIMIT)
            up = jnp.clip(up, -SWIGLU_LIMIT, SWIGLU_LIMIT)
            glu = gate * jax.nn.sigmoid(gate * SWIGLU_ALPHA)
            mid = ((up + 1.0) * glu).astype(BF16)
            _store_packed_rows(ybuf.at[b % 2], 0, _dot(mid, wd_scr[...]) + bd)
            y_copy(b).start()
            return carry

        lax.fori_loop(0, count, block, 0)

    @pl.when(e == pl.num_programs(0) - 1)
    def _():
        @pl.when(total_blocks >= 2)
        def _():
            y_copy(total_blocks - 2).wait()
        y_copy(total_blocks - 1).wait()

        ybuf[...] = jnp.zeros_like(ybuf)

        def tail_start(b, carry):
            y_copy(b).start()
            return carry

        def tail_wait(b, carry):
            y_copy(b).wait()
            return carry

        lax.fori_loop(total_blocks, nblk, tail_start, 0)
        lax.fori_loop(total_blocks, nblk, tail_wait, 0)


def _moe_ffn(first_blk, n_blk, slot, next_e, misc, xs, wgu, wd, bg, bu, bd):
    p_rows = xs.shape[0] // TILE_ROW
    n_exp, d, f2 = wgu.shape
    f = f2 // 2
    assert d == 2 * TILE_ROW * LANES
    whole = lambda shape: pl.BlockSpec(shape, lambda e, *_: (0,) * len(shape))
    tile = (MOE_ROWS * TILE_ROW, LANES)
    kern = functools.partial(_ffn_kernel, nblk=p_rows // MOE_ROWS)
    return pl.pallas_call(
        kern,
        out_shape=jax.ShapeDtypeStruct((p_rows * TILE_ROW, LANES), jnp.uint32),
        grid_spec=pltpu.PrefetchScalarGridSpec(
            num_scalar_prefetch=5, grid=(n_exp,),
            in_specs=[pl.BlockSpec(memory_space=pl.ANY),
                      pl.BlockSpec(memory_space=pl.ANY), pl.BlockSpec(memory_space=pl.ANY),
                      whole(bg.shape), whole(bu.shape), whole(bd.shape)],
            out_specs=pl.BlockSpec(memory_space=pl.ANY),
            scratch_shapes=[pltpu.VMEM((2, d, f2), F32), pltpu.VMEM((2, f, d), F32),
                            pltpu.VMEM((f, d), BF16), pltpu.VMEM((f, d), BF16),
                            pltpu.VMEM((f, d), BF16),
                            pltpu.VMEM((2,) + tile, jnp.uint32), pltpu.VMEM((2,) + tile, jnp.uint32),
                            pltpu.SemaphoreType.DMA((2, 2)), pltpu.SemaphoreType.DMA((2,)),
                            pltpu.SemaphoreType.DMA((2,))]),
        compiler_params=pltpu.CompilerParams(
            dimension_semantics=("arbitrary",), vmem_limit_bytes=VMEM_LIMIT),
        name="moe_ffn",
    )(first_blk, n_blk, slot, next_e, misc, xs, wgu, wd, bg, bu, bd)


def _combine_kernel(dest_ref, ys_hbm, x1_ref, route_ref, g2_ref, o_ref, buf, sem, *, tc, n_steps):
    i = pl.program_id(0)
    slot = i % 2

    def gather(step, s):
        def issue(t, carry):
            for k in range(TOP_K):
                row = dest_ref[(step * tc + t) * TOP_K + k]
                _row_copy(ys_hbm, buf.at[s, k], row, t, sem.at[s]).start(priority=k % 2)
            return carry

        lax.fori_loop(0, tc, issue, 0, unroll=4)

    @pl.when(i == 0)
    def _():
        gather(0, 0)

    @pl.when(i + 1 < n_steps)
    def _():
        gather(i + 1, 1 - slot)

    for k in range(TOP_K):
        pltpu.make_async_copy(_tile_rows(ys_hbm, 0, tc), buf.at[slot, k], sem.at[slot]).wait()

    route = route_ref[...]
    chunks = [_load_packed_rows(buf.at[slot, k], tc) for k in range(TOP_K)]
    for c in range(2 * TILE_ROW):
        cols = slice(c * LANES, (c + 1) * LANES)
        y = chunks[0][c] * route[:, 0:1]
        for k in range(1, TOP_K):
            y = y + chunks[k][c] * route[:, k:k + 1]
        o_ref[:, cols] = x1_ref[:, cols] + g2_ref[:, cols] * y


def _moe_combine(dest_flat, ys, x1, route, g2, seq):
    n, d = x1.shape
    assert d == 2 * TILE_ROW * LANES
    tc = min(seq, COMBINE_TOKENS)
    tpb = seq // tc
    kern = functools.partial(_combine_kernel, tc=tc, n_steps=n // tc)
    return pl.pallas_call(
        kern,
        out_shape=jax.ShapeDtypeStruct((n, d), F32),
        grid_spec=pltpu.PrefetchScalarGridSpec(
            num_scalar_prefetch=1, grid=(n // tc,),
            in_specs=[pl.BlockSpec(memory_space=pl.ANY),
                      pl.BlockSpec((tc, d), lambda i, dr: (i, 0)),
                      pl.BlockSpec((tc, LANES), lambda i, dr: (i, 0)),
                      pl.BlockSpec((None, 1, d), lambda i, dr: (i // tpb, 0, 0))],
            out_specs=pl.BlockSpec((tc, d), lambda i, dr: (i, 0)),
            scratch_shapes=[pltpu.VMEM((2, TOP_K, tc * TILE_ROW, LANES), jnp.uint32),
                            pltpu.SemaphoreType.DMA((2,))]),
        compiler_params=pltpu.CompilerParams(
            dimension_semantics=("arbitrary",), vmem_limit_bytes=VMEM_LIMIT),
        name="moe_combine",
    )(dest_flat, ys, x1, route, g2)


def _rope_tables(seq):
    inv_freq = (ROPE_THETA ** (-np.arange(0, HEAD_DIM, 2, dtype=np.float32) / HEAD_DIM)).astype(np.float32)
    ang = (inv_freq[:, None] * np.arange(seq, dtype=np.float32)[None, :]).astype(np.float32)
    cos, sin = np.cos(ang.astype(np.float64)), np.sin(ang.astype(np.float64))
    cos_t = np.concatenate([cos, cos], axis=0).astype(np.float32)
    sin_t = np.concatenate([-sin, sin], axis=0).astype(np.float32)
    return jnp.asarray(cos_t), jnp.asarray(sin_t)


def _layer(x, c, l, lambda_init, w_ada, b_ada, norm1_w, w_in, q_norm_w, k_norm_w, lambda_q1,
           lambda_k1, lambda_q2, lambda_k2, subln_w, w_attn_o, conv_w, w_conv_o, w_out, norm2_w,
           w_router, b_router, w_gate_up, b_gate_up, w_down, b_down):
    bsz, seq, d = x.shape
    n = bsz * seq
    x2 = x.reshape(n, d)

    mod = _ada_mod(c, w_ada[l], b_ada[l])
    sh1, sc1, g1, sh2, sc2, g2 = [m.reshape(bsz, 1, d) for m in jnp.split(mod, 6, axis=-1)]

    cos_t, sin_t = _rope_tables(seq)
    qt, k, vt, cb, z, sga, sgc, qn, kn = _in_proj(
        x2, norm1_w[l].reshape(1, d), sc1, sh1, w_in[l].astype(BF16),
        q_norm_w[l].reshape(HEAD_DIM, 1) * Q_SCALE, k_norm_w[l].reshape(HEAD_DIM, 1),
        cos_t, sin_t, bsz, seq)

    lam_vecs = jnp.stack([lambda_q1[l], lambda_k1[l], lambda_q2[l], lambda_k2[l]]).astype(F32)
    o = _diff_attn(qt, k.reshape(bsz, seq, d), vt, qn, kn, lam_vecs, subln_w[l].reshape(V_DIM, 1),
                   lambda_init)

    pad = LANES - N_EXPERTS
    wr = jnp.pad(w_router[l].astype(F32), ((0, 0), (0, pad)))
    wr_hi = wr.astype(BF16)
    wr_split = jnp.concatenate([wr_hi, (wr - wr_hi.astype(F32)).astype(BF16)], axis=1)
    br_pad = jnp.pad(b_router[l].astype(F32), (0, pad), constant_values=NEG_BIG).reshape(1, LANES)
    x1, h2, route, cnt = _post_mix(
        o.reshape(n, d), cb, z, sga, sgc, x2, g1, sc2, sh2, norm2_w[l].reshape(1, d), conv_w[l],
        w_attn_o[l].astype(BF16), w_conv_o[l].astype(BF16), w_out[l].astype(BF16),
        wr_split, br_pad, seq)

    counts = cnt[0, :N_EXPERTS].astype(jnp.int32)
    padded = ((counts + MOE_ROWS - 1) // MOE_ROWS) * MOE_ROWS
    ends = jnp.cumsum(padded)
    start = ends - padded
    p_rows = n * TOP_K + N_EXPERTS * MOE_ROWS
    top_e = route[:, TOP_K:2 * TOP_K].astype(jnp.int32)
    rank = route[:, 2 * TOP_K:3 * TOP_K].astype(jnp.int32)
    experts = jnp.arange(N_EXPERTS, dtype=jnp.int32)
    seg_start = jnp.sum(jnp.where(top_e[..., None] == experts, start, 0), axis=-1)
    dest = (seg_start + rank).reshape(-1)
    nonempty = padded > 0
    later = jnp.logical_and(nonempty[None, :], experts[None, :] > experts[:, None])
    next_e = jnp.min(jnp.where(later, experts[None, :], N_EXPERTS), axis=1)
    next_e = jnp.where(next_e == N_EXPERTS, -1, next_e).astype(jnp.int32)
    slot = ((jnp.cumsum(nonempty.astype(jnp.int32)) - 1) % 2).astype(jnp.int32)
    first_blk = (start // MOE_ROWS).astype(jnp.int32)
    n_blk = (padded // MOE_ROWS).astype(jnp.int32)
    misc = jnp.stack([jnp.argmax(nonempty).astype(jnp.int32), (ends[-1] // MOE_ROWS).astype(jnp.int32)])

    xs = _moe_dispatch(dest, ends.astype(jnp.int32), h2, n, p_rows)
    ys = _moe_ffn(first_blk, n_blk, slot, next_e, misc, xs, w_gate_up[l], w_down[l],
                  b_gate_up[l][:, None, 0::2], b_gate_up[l][:, None, 1::2], b_down[l][:, None, :])
    out = _moe_combine(dest, ys, x1, route, g2, seq)
    return out.reshape(bsz, seq, d)


def kernel(x, c, w_ada, b_ada, norm1_w, w_in, q_norm_w, k_norm_w, lambda_q1, lambda_k1, lambda_q2,
           lambda_k2, subln_w, w_attn_o, conv_w, w_conv_o, w_out, norm2_w, w_router, b_router,
           w_gate_up, b_gate_up, w_down, b_down):
    depth = w_ada.shape[0]
    for l in range(depth):
        lambda_init = 0.8 - 0.6 * math.exp(-0.3 * l)
        x = _layer(x, c, l, lambda_init, w_ada, b_ada, norm1_w, w_in, q_norm_w, k_norm_w,
                   lambda_q1, lambda_k1, lambda_q2, lambda_k2, subln_w, w_attn_o, conv_w,
                   w_conv_o, w_out, norm2_w, w_router, b_router, w_gate_up, b_gate_up,
                   w_down, b_down)
    return x
```

```python
import functools
import math

import jax
import jax.numpy as jnp
import numpy as np
from jax import lax
from jax.experimental import pallas as pl
from jax.experimental.pallas import tpu as pltpu

N_HEADS = 8
HEAD_DIM = 64
V_DIM = 2 * HEAD_DIM
IN_BLOCKS = ("q", "k", "v", "conv_b", "conv_c", "conv_x", "gate_attn", "gate_conv")
N_EXPERTS = 32
TOP_K = 4
SWIGLU_LIMIT = 7.0
SWIGLU_ALPHA = 1.702
ROPE_THETA = 10000.0
RMS_EPS = 1e-6
SUBLN_EPS = 1e-5
LANES = 128
TILE_ROW = 4
ADA_COLS = 1536
ROW_TILE = 512
ATTN_Q_TILE = 1024
ATTN_K_TILE = 4096
MOE_ROWS = 256
DISPATCH_TOKENS = 2048
COMBINE_TOKENS = 512
VMEM_LIMIT = 56 * 1024 * 1024
NEG_BIG = -1e30
LOG2E = 1.4426950408889634
Q_SCALE = LOG2E / math.sqrt(HEAD_DIM)
SAFE_EXP2_BOUND = 80.0

F32 = jnp.float32
BF16 = jnp.bfloat16


def _dot(a, b):
    return jnp.dot(a, b, preferred_element_type=F32)


_HIGH_HALF = 0xFFFF0000


def _bf16_bits(x):
    return pltpu.bitcast(x.astype(BF16).astype(F32), jnp.uint32)


def _store_packed_rows(ref, row0, val):
    rows, d = val.shape
    half = d // 2
    for c in range(TILE_ROW):
        lo = _bf16_bits(val[:, c * LANES:(c + 1) * LANES]) >> 16
        hi = _bf16_bits(val[:, half + c * LANES:half + (c + 1) * LANES])
        ref[pl.ds(row0 * TILE_ROW + c, rows, stride=TILE_ROW), :] = lo | hi


def _load_packed_rows(ref, rows):
    words = [ref[pl.ds(c, rows, stride=TILE_ROW), :] for c in range(TILE_ROW)]
    lo = [pltpu.bitcast(w << 16, F32) for w in words]
    hi = [pltpu.bitcast(w & jnp.uint32(_HIGH_HALF), F32) for w in words]
    return lo + hi


def _ada_kernel(ct_ref, w_ref, b_ref, o_ref):
    ct = ct_ref[...]
    s = ct * jax.nn.sigmoid(ct)
    w = w_ref[...]
    for b in range(ct.shape[1]):
        o_ref[b:b + 1, :] = jnp.sum(w * s[:, b:b + 1], axis=0, keepdims=True) + b_ref[...]


def _ada_mod(c, w_ada, b_ada):
    bsz, d = c.shape
    n = w_ada.shape[1]
    tn = min(n, ADA_COLS)
    return pl.pallas_call(
        _ada_kernel,
        out_shape=jax.ShapeDtypeStruct((bsz, n), F32),
        grid=(n // tn,),
        in_specs=[pl.BlockSpec((d, bsz), lambda j: (0, 0)),
                  pl.BlockSpec((d, tn), lambda j: (0, j)),
                  pl.BlockSpec((1, tn), lambda j: (0, j))],
        out_specs=pl.BlockSpec((bsz, tn), lambda j: (0, j)),
        compiler_params=pltpu.CompilerParams(dimension_semantics=("arbitrary",)),
        name="ada_mod",
    )(c.T, w_ada, b_ada.reshape(1, n))


def _qk_norm_rope_t(y, g_col, cos_t, sin_t):
    tm, w = y.shape
    yt = y.T.reshape(w // HEAD_DIM, HEAD_DIM, tm)
    ms = jnp.mean(yt * yt, axis=1, keepdims=True)
    yn = yt * lax.rsqrt(ms + RMS_EPS) * g_col[None]
    half = HEAD_DIM // 2
    swapped = jnp.concatenate([yn[:, half:, :], yn[:, :half, :]], axis=1)
    out = yn * cos_t[None] + swapped * sin_t[None]
    norm2 = jnp.sum(out * out, axis=1).reshape(N_HEADS, 2, tm)
    return out.reshape(w, tm), norm2


def _in_proj_kernel(x_ref, n1_ref, sc_ref, sh_ref, w_ref, gq_ref, gk_ref, cos_ref, sin_ref,
                    qt_ref, k_ref, vt_ref, cb_ref, z_ref, sga_ref, sgc_ref, qn_ref, kn_ref, h_scr):
    x = x_ref[...]
    xn = x * lax.rsqrt(jnp.mean(x * x, axis=-1, keepdims=True) + RMS_EPS) * n1_ref[...]
    h_scr[...] = (xn * (1.0 + sc_ref[...]) + sh_ref[...]).astype(BF16)
    wcol = w_ref.shape[1] // len(IN_BLOCKS)

    def proj(name):
        c = IN_BLOCKS.index(name)
        return _dot(h_scr[...], w_ref[:, c * wcol:(c + 1) * wcol])

    qt, qn = _qk_norm_rope_t(proj("q"), gq_ref[...], cos_ref[...], sin_ref[...])
    qt_ref[...] = qt.astype(BF16)
    qn_ref[...] = qn
    kt, kn = _qk_norm_rope_t(proj("k"), gk_ref[...], cos_ref[...], sin_ref[...])
    k_ref[...] = kt.T.astype(BF16)
    kn_ref[...] = kn
    vt_ref[...] = proj("v").T.astype(BF16)
    cb_ref[...] = proj("conv_b").astype(BF16)
    z_ref[...] = (proj("conv_c") * proj("conv_x")).astype(BF16)
    sga_ref[...] = jax.nn.sigmoid(proj("gate_attn")).astype(BF16)
    sgc_ref[...] = jax.nn.sigmoid(proj("gate_conv")).astype(BF16)


def _in_proj(x2, n1, sc1, sh1, w_in_bf, gq, gk, cos_t, sin_t, bsz, seq):
    n, d = x2.shape
    tm = min(seq, ROW_TILE)
    tpb = seq // tm
    assert w_in_bf.shape[1] == len(IN_BLOCKS) * d
    row = lambda i: (i, 0)
    const = lambda i: (0, 0)
    tcol = lambda i: (i // tpb, 0, i % tpb)
    mod = lambda i: (i // tpb, 0, 0)
    nat = jax.ShapeDtypeStruct((n, d), BF16)
    tr = jax.ShapeDtypeStruct((bsz, d, seq), BF16)
    nrm = jax.ShapeDtypeStruct((bsz, N_HEADS, 2, seq), F32)
    nat_spec = pl.BlockSpec((tm, d), row)
    tr_spec = pl.BlockSpec((None, d, tm), tcol)
    nrm_spec = pl.BlockSpec((None, N_HEADS, 2, tm), lambda i: (i // tpb, 0, 0, i % tpb))
    return pl.pallas_call(
        _in_proj_kernel,
        out_shape=(tr, nat, tr, nat, nat, nat, nat, nrm, nrm),
        grid=(n // tm,),
        in_specs=[pl.BlockSpec((tm, d), row),
                  pl.BlockSpec((1, d), const),
                  pl.BlockSpec((None, 1, d), mod),
                  pl.BlockSpec((None, 1, d), mod),
                  pl.BlockSpec(w_in_bf.shape, const),
                  pl.BlockSpec((HEAD_DIM, 1), const),
                  pl.BlockSpec((HEAD_DIM, 1), const),
                  pl.BlockSpec((HEAD_DIM, tm), lambda i: (0, i % tpb)),
                  pl.BlockSpec((HEAD_DIM, tm), lambda i: (0, i % tpb))],
        out_specs=(tr_spec, nat_spec, tr_spec, nat_spec, nat_spec, nat_spec, nat_spec,
                   nrm_spec, nrm_spec),
        scratch_shapes=[pltpu.VMEM((tm, d), BF16)],
        compiler_params=pltpu.CompilerParams(
            dimension_semantics=("arbitrary",), vmem_limit_bytes=VMEM_LIMIT),
        name="in_proj",
    )(x2, n1, sc1, sh1, w_in_bf, gq, gk, cos_t, sin_t)


def _diff_attn_kernel(qt_ref, k_ref, vt_ref, qn_ref, kn_ref, lam_ref, sw_ref, o_ref, acc0, acc1,
                      *, tk, lambda_init):
    seq = k_ref.shape[0]
    tq = qt_ref.shape[1]
    qt = qt_ref[...]
    first = lax.broadcasted_iota(jnp.int32, qt.shape, 0) < HEAD_DIM
    zero = jnp.zeros_like(qt)
    qz = (jnp.where(first, qt, zero), jnp.where(first, zero, qt))
    accs = (acc0, acc1)
    acc0[...] = jnp.zeros_like(acc0)
    acc1[...] = jnp.zeros_like(acc1)
    n_chunks = seq // tk

    def load(j):
        off = pl.multiple_of(j * tk, tk)
        return k_ref[pl.ds(off, tk), :], vt_ref[:, pl.ds(off, tk)]

    def plain_body(j, carry):
        kk, vt = load(j)
        new = []
        for c in range(2):
            p = jnp.exp2(_dot(kk, qz[c]))
            new.append(carry[c] + jnp.sum(p, axis=0, keepdims=True))
            accs[c][...] += _dot(vt, p.astype(BF16))
        return tuple(new)

    def online_body(j, carry):
        kk, vt = load(j)
        new = []
        for c in range(2):
            m, l = carry[2 * c], carry[2 * c + 1]
            s = _dot(kk, qz[c])
            m_new = jnp.maximum(m, jnp.max(s, axis=0, keepdims=True))
            alpha = jnp.exp2(m - m_new)
            p = jnp.exp2(s - m_new)
            l = alpha * l + jnp.sum(p, axis=0, keepdims=True)
            accs[c][...] = alpha * accs[c][...] + _dot(vt, p.astype(BF16))
            new += [m_new, l]
        return tuple(new)

    m_init = jnp.full((1, tq), NEG_BIG, F32)
    l_init = jnp.zeros((1, tq), F32)

    def plain():
        return lax.fori_loop(0, n_chunks, plain_body, (l_init, l_init))

    def online():
        _, l0, _, l1 = lax.fori_loop(0, n_chunks, online_body, (m_init, l_init, m_init, l_init))
        return l0, l1

    bound2 = jnp.max(jnp.max(qn_ref[...], axis=-1, keepdims=True)
                     * jnp.max(kn_ref[...], axis=-1, keepdims=True))
    l0, l1 = lax.cond(bound2 <= SAFE_EXP2_BOUND * SAFE_EXP2_BOUND, plain, online)

    lq = lam_ref[...]
    lam = (jnp.exp(jnp.sum(lq[0:1] * lq[1:2], axis=-1, keepdims=True))
           - jnp.exp(jnp.sum(lq[2:3] * lq[3:4], axis=-1, keepdims=True)) + lambda_init)
    o = acc0[...] / l0 - lam * (acc1[...] / l1)
    o = o * lax.rsqrt(jnp.mean(o * o, axis=0, keepdims=True) + SUBLN_EPS)
    o = o * sw_ref[...] * (1.0 - lambda_init)
    o_ref[...] = o.T.astype(BF16)


def _diff_attn(qt, k3, vt, qn, kn, lam_vecs, subln_col, lambda_init):
    bsz, d, seq = qt.shape
    tq = min(seq, ATTN_Q_TILE)
    tk = min(seq, ATTN_K_TILE)
    kern = functools.partial(_diff_attn_kernel, tk=tk, lambda_init=lambda_init)
    return pl.pallas_call(
        kern,
        out_shape=jax.ShapeDtypeStruct((bsz, seq, d), BF16),
        grid=(bsz, N_HEADS, seq // tq),
        in_specs=[pl.BlockSpec((None, V_DIM, tq), lambda b, h, i: (b, h, i)),
                  pl.BlockSpec((None, seq, V_DIM), lambda b, h, i: (b, 0, h)),
                  pl.BlockSpec((None, V_DIM, seq), lambda b, h, i: (b, h, 0)),
                  pl.BlockSpec((None, None, 2, tq), lambda b, h, i: (b, h, 0, i)),
                  pl.BlockSpec((None, None, 2, seq), lambda b, h, i: (b, h, 0, 0)),
                  pl.BlockSpec((4, HEAD_DIM), lambda b, h, i: (0, 0)),
                  pl.BlockSpec((V_DIM, 1), lambda b, h, i: (0, 0))],
        out_specs=pl.BlockSpec((None, tq, V_DIM), lambda b, h, i: (b, i, h)),
        scratch_shapes=[pltpu.VMEM((V_DIM, tq), F32), pltpu.VMEM((V_DIM, tq), F32)],
        compiler_params=pltpu.CompilerParams(
            dimension_semantics=("arbitrary", "arbitrary", "arbitrary"),
            vmem_limit_bytes=VMEM_LIMIT),
        name="diff_attn",
    )(qt, k3, vt, qn, kn, lam_vecs, subln_col)


def _post_mix_kernel(o_ref, cb_ref, z_ref, zp_ref, zn_ref, sga_ref, sgc_ref, x_ref,
                     g1_ref, sc2_ref, sh2_ref, n2_ref, cw_ref, wao_ref, wco_ref, wout_ref,
                     wr_ref, br_ref,
                     x1_ref, h2_ref, route_ref, cnt_ref, *, tpb):
    i = pl.program_id(0)
    tm = x_ref.shape[0]

    @pl.when(i == 0)
    def _():
        cnt_ref[...] = jnp.zeros_like(cnt_ref)

    z = z_ref[...].astype(F32)
    rows = lax.broadcasted_iota(jnp.int32, z.shape, 0)
    halo_rows = zp_ref.shape[0]
    prev_row = zp_ref[halo_rows - 1:halo_rows, :].astype(F32)
    next_row = zn_ref[0:1, :].astype(F32)
    prev_row = jnp.where(i % tpb == 0, jnp.zeros_like(prev_row), prev_row)
    next_row = jnp.where(i % tpb == tpb - 1, jnp.zeros_like(next_row), next_row)
    z_m1 = jnp.where(rows == 0, prev_row, pltpu.roll(z, 1, 0))
    z_p1 = jnp.where(rows == tm - 1, next_row, pltpu.roll(z, tm - 1, 0))
    cw = cw_ref[...]
    conv = z_m1 * cw[0:1] + z * cw[1:2] + z_p1 * cw[2:3]
    u = (cb_ref[...].astype(F32) * conv).astype(BF16)

    y_attn = _dot(o_ref[...], wao_ref[...])
    y_conv = _dot(u, wco_ref[...])
    m = sga_ref[...].astype(F32) * y_attn + sgc_ref[...].astype(F32) * y_conv
    x1 = x_ref[...] + g1_ref[...] * _dot(m.astype(BF16), wout_ref[...])
    x1_ref[...] = x1

    h2 = x1 * lax.rsqrt(jnp.mean(x1 * x1, axis=-1, keepdims=True) + RMS_EPS) * n2_ref[...]
    h2 = h2 * (1.0 + sc2_ref[...]) + sh2_ref[...]
    _store_packed_rows(h2_ref, 0, h2)

    route, counts = _route_tile(h2, wr_ref, br_ref, cnt_ref[0:1, :])
    route_ref[...] = route
    cnt_ref[...] = jnp.broadcast_to(counts, cnt_ref.shape)


def _route_tile(h2, wr_ref, br_ref, counts):
    tm = h2.shape[0]
    h_hi = h2.astype(BF16)
    h_lo = (h2 - h_hi.astype(F32)).astype(BF16)
    hi_both = _dot(h_hi, wr_ref[...])
    logits = (hi_both[:, :LANES] + hi_both[:, LANES:] + _dot(h_lo, wr_ref[:, :LANES])
              + br_ref[...])

    lane = lax.broadcasted_iota(jnp.int32, logits.shape, 1)
    work = logits
    vals, idxs = [], []
    for _ in range(TOP_K):
        mx = jnp.max(work, axis=-1, keepdims=True)
        ix = jnp.min(jnp.where(work == mx, lane, LANES), axis=-1, keepdims=True)
        vals.append(mx)
        idxs.append(ix)
        work = jnp.where(lane == ix, 2.0 * NEG_BIG, work)
    exps = [jnp.exp(v - vals[0]) for v in vals]
    den = exps[0] + exps[1] + exps[2] + exps[3]

    sel = (work == 2.0 * NEG_BIG).astype(BF16)
    r_i = lax.broadcasted_iota(jnp.int32, (tm, tm), 0)
    c_i = lax.broadcasted_iota(jnp.int32, (tm, tm), 1)
    lower = (r_i > c_i).astype(BF16)
    before = _dot(lower, sel) + counts
    counts = counts + jnp.sum(sel.astype(F32), axis=0, keepdims=True)

    route = jnp.zeros(logits.shape, F32)
    for k in range(TOP_K):
        rank = jnp.sum(jnp.where(lane == idxs[k], before, 0.0), axis=-1, keepdims=True)
        route = jnp.where(lane == k, exps[k] / den, route)
        route = jnp.where(lane == TOP_K + k, idxs[k].astype(F32), route)
        route = jnp.where(lane == 2 * TOP_K + k, rank, route)
    return route, counts


def _post_mix(o2, cb, z, sga, sgc, x2, g1, sc2, sh2, n2, conv_w, wao, wco, wout,
              wr_split, br_pad, seq):
    n, d = x2.shape
    tm = min(seq, ROW_TILE)
    tpb = seq // tm
    halo = 16
    hb = tm // halo
    last_hb = n // halo - 1
    row = lambda i: (i, 0)
    const = lambda i: (0, 0)
    mod = lambda i: (i // tpb, 0, 0)
    wspec = pl.BlockSpec((d, d), const)
    kern = functools.partial(_post_mix_kernel, tpb=tpb)
    return pl.pallas_call(
        kern,
        out_shape=(jax.ShapeDtypeStruct((n, d), F32),
                   jax.ShapeDtypeStruct((n * TILE_ROW, LANES), jnp.uint32),
                   jax.ShapeDtypeStruct((n, LANES), F32), jax.ShapeDtypeStruct((8, LANES), F32)),
        grid=(n // tm,),
        in_specs=[pl.BlockSpec((tm, d), row), pl.BlockSpec((tm, d), row), pl.BlockSpec((tm, d), row),
                  pl.BlockSpec((halo, d), lambda i: (jnp.maximum(i * hb - 1, 0), 0)),
                  pl.BlockSpec((halo, d), lambda i: (jnp.minimum((i + 1) * hb, last_hb), 0)),
                  pl.BlockSpec((tm, d), row), pl.BlockSpec((tm, d), row), pl.BlockSpec((tm, d), row),
                  pl.BlockSpec((None, 1, d), mod), pl.BlockSpec((None, 1, d), mod),
                  pl.BlockSpec((None, 1, d), mod),
                  pl.BlockSpec((1, d), const), pl.BlockSpec((3, d), const),
                  wspec, wspec, wspec,
                  pl.BlockSpec((d, 2 * LANES), const), pl.BlockSpec((1, LANES), const)],
        out_specs=(pl.BlockSpec((tm, d), row), pl.BlockSpec((tm * TILE_ROW, LANES), row),
                   pl.BlockSpec((tm, LANES), row), pl.BlockSpec((8, LANES), const)),
        compiler_params=pltpu.CompilerParams(
            dimension_semantics=("arbitrary",), vmem_limit_bytes=VMEM_LIMIT),
        name="post_mix",
    )(o2, cb, z, z, z, sga, sgc, x2, g1, sc2, sh2, n2, conv_w, wao, wco, wout, wr_split, br_pad)


def _tile_rows(ref, row, n_rows=1):
    start = pl.multiple_of(row * TILE_ROW, TILE_ROW)
    return ref.at[pl.ds(start, n_rows * TILE_ROW)]


def _row_copy(src, dst, s, t, sem):
    return pltpu.make_async_copy(_tile_rows(src, s), _tile_rows(dst, t), sem)


def _dispatch_kernel(dest_ref, ends_ref, h2_ref, xs_hbm, zbuf, sem, zsem, *, tc, nblk):
    base = pl.program_id(0) * tc

    @pl.when(pl.program_id(0) == 0)
    def _():
        zbuf[...] = jnp.zeros_like(zbuf)

        def zero_block(row):
            return pltpu.make_async_copy(zbuf, _tile_rows(xs_hbm, row, MOE_ROWS), zsem)

        def nonempty(e):
            return ends_ref[e] > (ends_ref[e - 1] if e else 0)

        total = ends_ref[N_EXPERTS - 1]
        n_tail = nblk - total // MOE_ROWS

        def tail_start(b, carry):
            zero_block(total + b * MOE_ROWS).start()
            return carry

        def tail_wait(b, carry):
            zero_block(0).wait()
            return carry

        for e in range(N_EXPERTS):
            @pl.when(nonempty(e))
            def _(e=e):
                zero_block(ends_ref[e] - MOE_ROWS).start()
        lax.fori_loop(0, n_tail, tail_start, 0)
        for e in range(N_EXPERTS):
            @pl.when(nonempty(e))
            def _():
                zero_block(0).wait()
        lax.fori_loop(0, n_tail, tail_wait, 0)

    def issue(t, carry):
        for k in range(TOP_K):
            _row_copy(h2_ref, xs_hbm, t, dest_ref[(base + t) * TOP_K + k], sem).start(priority=k % 2)
        return carry

    lax.fori_loop(0, tc, issue, 0, unroll=4)

    for _ in range(TOP_K):
        pltpu.make_async_copy(h2_ref, _tile_rows(xs_hbm, 0, tc), sem).wait()


def _moe_dispatch(dest_flat, ends, h2_tiles, n, p_rows):
    tc = min(n, DISPATCH_TOKENS)
    kern = functools.partial(_dispatch_kernel, tc=tc, nblk=p_rows // MOE_ROWS)
    return pl.pallas_call(
        kern,
        out_shape=jax.ShapeDtypeStruct((p_rows * TILE_ROW, LANES), jnp.uint32),
        grid_spec=pltpu.PrefetchScalarGridSpec(
            num_scalar_prefetch=2, grid=(n // tc,),
            in_specs=[pl.BlockSpec((tc * TILE_ROW, LANES), lambda i, dr, en: (i, 0))],
            out_specs=pl.BlockSpec(memory_space=pl.ANY),
            scratch_shapes=[pltpu.VMEM((MOE_ROWS * TILE_ROW, LANES), jnp.uint32),
                            pltpu.SemaphoreType.DMA, pltpu.SemaphoreType.DMA]),
        compiler_params=pltpu.CompilerParams(dimension_semantics=("arbitrary",)),
        name="moe_dispatch",
    )(dest_flat, ends, h2_tiles)


_NT = (((1,), (1,)), ((), ()))


def _ffn_kernel(first_ref, count_ref, slot_ref, next_ref, misc_ref,
                xs_hbm, wgu_hbm, wd_hbm, bg_ref, bu_ref, bd_ref, ys_hbm,
                wgu_buf, wd_buf, wg_scr, wu_scr, wd_scr, xbuf, ybuf,
                wsem, xsem, ysem, *, nblk):
    e = pl.program_id(0)
    first = first_ref[e]
    count = count_ref[e]
    slot = slot_ref[e]
    first_expert, total_blocks = misc_ref[0], misc_ref[1]

    def weight_copies(ex, s):
        return (pltpu.make_async_copy(wgu_hbm.at[ex], wgu_buf.at[s], wsem.at[0, s]),
                pltpu.make_async_copy(wd_hbm.at[ex], wd_buf.at[s], wsem.at[1, s]))

    def x_copy(b, s):
        return pltpu.make_async_copy(_tile_rows(xs_hbm, b * MOE_ROWS, MOE_ROWS), xbuf.at[s], xsem.at[s])

    def y_copy(b):
        return pltpu.make_async_copy(ybuf.at[b % 2], _tile_rows(ys_hbm, b * MOE_ROWS, MOE_ROWS),
                                     ysem.at[b % 2])

    @pl.when(e == first_expert)
    def _():
        x_copy(first, 0).start()
        for cp in weight_copies(e, 0):
            cp.start()

    @pl.when(count > 0)
    def _():
        for cp in weight_copies(e, slot):
            cp.wait()

        @pl.when(next_ref[e] >= 0)
        def _():
            for cp in weight_copies(next_ref[e], 1 - slot):
                cp.start(priority=1)

        d = wgu_buf.shape[1]
        for c in range(d // LANES):
            cols = slice(c * LANES, (c + 1) * LANES)
            words = pltpu.bitcast(wgu_buf[slot, cols, :].astype(BF16).T, jnp.uint32)
            wg_scr[:, cols] = pltpu.bitcast(words << 16, F32).astype(BF16)
            wu_scr[:, cols] = pltpu.bitcast(words & jnp.uint32(_HIGH_HALF), F32).astype(BF16)
        wd_scr[...] = wd_buf[slot].astype(BF16)
        bg, bu, bd = bg_ref[e], bu_ref[e], bd_ref[e]

        def block(j, carry):
            s = j % 2
            b = first + j
            x_copy(b, s).wait()

            @pl.when(j + 1 < count)
            def _():
                x_copy(b + 1, 1 - s).start()

            @pl.when(b >= 2)
            def _():
                y_copy(b - 2).wait()

            x = jnp.concatenate(_load_packed_rows(xbuf.at[s], MOE_ROWS), axis=1).astype(BF16)
            gate = lax.dot_general(x, wg_scr[...], _NT, preferred_element_type=F32) + bg
            up = lax.dot_general(x, wu_scr[...], _NT, preferred_element_type=F32) + bu
            gate = jnp.minimum(gate, SWIGLU_LIMIT)
            up = jnp.clip(up, -SWIGLU_LIMIT, SWIGLU_LIMIT)
            glu = gate * jax.nn.sigmoid(gate * SWIGLU_ALPHA)
            mid = ((up + 1.0) * glu).astype(BF16)
            _store_packed_rows(ybuf.at[b % 2], 0, _dot(mid, wd_scr[...]) + bd)
            y_copy(b).start()
            return carry

        lax.fori_loop(0, count, block, 0)

        @pl.when(next_ref[e] >= 0)
        def _():
            x_copy(first + count, 0).start()

    @pl.when(e == pl.num_programs(0) - 1)
    def _():
        @pl.when(total_blocks >= 2)
        def _():
            y_copy(total_blocks - 2).wait()
        y_copy(total_blocks - 1).wait()

        ybuf[...] = jnp.zeros_like(ybuf)

        def tail_start(b, carry):
            y_copy(b).start()
            return carry

        def tail_wait(b, carry):
            y_copy(b).wait()
            return carry

        lax.fori_loop(total_blocks, nblk, tail_start, 0)
        lax.fori_loop(total_blocks, nblk, tail_wait, 0)


def _moe_ffn(first_blk, n_blk, slot, next_e, misc, xs, wgu, wd, bg, bu, bd):
    p_rows = xs.shape[0] // TILE_ROW
    n_exp, d, f2 = wgu.shape
    f = f2 // 2
    assert d == 2 * TILE_ROW * LANES
    whole = lambda shape: pl.BlockSpec(shape, lambda e, *_: (0,) * len(shape))
    tile = (MOE_ROWS * TILE_ROW, LANES)
    kern = functools.partial(_ffn_kernel, nblk=p_rows // MOE_ROWS)
    return pl.pallas_call(
        kern,
        out_shape=jax.ShapeDtypeStruct((p_rows * TILE_ROW, LANES), jnp.uint32),
        grid_spec=pltpu.PrefetchScalarGridSpec(
            num_scalar_prefetch=5, grid=(n_exp,),
            in_specs=[pl.BlockSpec(memory_space=pl.ANY),
                      pl.BlockSpec(memory_space=pl.ANY), pl.BlockSpec(memory_space=pl.ANY),
                      whole(bg.shape), whole(bu.shape), whole(bd.shape)],
            out_specs=pl.BlockSpec(memory_space=pl.ANY),
            scratch_shapes=[pltpu.VMEM((2, d, f2), F32), pltpu.VMEM((2, f, d), F32),
                            pltpu.VMEM((f, d), BF16), pltpu.VMEM((f, d), BF16),
                            pltpu.VMEM((f, d), BF16),
                            pltpu.VMEM((2,) + tile, jnp.uint32), pltpu.VMEM((2,) + tile, jnp.uint32),
                            pltpu.SemaphoreType.DMA((2, 2)), pltpu.SemaphoreType.DMA((2,)),
                            pltpu.SemaphoreType.DMA((2,))]),
        compiler_params=pltpu.CompilerParams(
            dimension_semantics=("arbitrary",), vmem_limit_bytes=VMEM_LIMIT),
        name="moe_ffn",
    )(first_blk, n_blk, slot, next_e, misc, xs, wgu, wd, bg, bu, bd)


def _combine_kernel(dest_ref, ys_hbm, x1_ref, route_ref, g2_ref, o_ref, buf, sem, *, tc, n_steps):
    i = pl.program_id(0)
    slot = i % 2

    def gather(step, s):
        def issue(t, carry):
            for k in range(TOP_K):
                row = dest_ref[(step * tc + t) * TOP_K + k]
                _row_copy(ys_hbm, buf.at[s, k], row, t, sem.at[s]).start(priority=k % 2)
            return carry

        lax.fori_loop(0, tc, issue, 0, unroll=4)

    @pl.when(i == 0)
    def _():
        gather(0, 0)

    @pl.when(i + 1 < n_steps)
    def _():
        gather(i + 1, 1 - slot)

    for k in range(TOP_K):
        pltpu.make_async_copy(_tile_rows(ys_hbm, 0, tc), buf.at[slot, k], sem.at[slot]).wait()

    route = route_ref[...]
    chunks = [_load_packed_rows(buf.at[slot, k], tc) for k in range(TOP_K)]
    for c in range(2 * TILE_ROW):
        cols = slice(c * LANES, (c + 1) * LANES)
        y = chunks[0][c] * route[:, 0:1]
        for k in range(1, TOP_K):
            y = y + chunks[k][c] * route[:, k:k + 1]
        o_ref[:, cols] = x1_ref[:, cols] + g2_ref[:, cols] * y


def _moe_combine(dest_flat, ys, x1, route, g2, seq):
    n, d = x1.shape
    assert d == 2 * TILE_ROW * LANES
    tc = min(seq, COMBINE_TOKENS)
    tpb = seq // tc
    kern = functools.partial(_combine_kernel, tc=tc, n_steps=n // tc)
    return pl.pallas_call(
        kern,
        out_shape=jax.ShapeDtypeStruct((n, d), F32),
        grid_spec=pltpu.PrefetchScalarGridSpec(
            num_scalar_prefetch=1, grid=(n // tc,),
            in_specs=[pl.BlockSpec(memory_space=pl.ANY),
                      pl.BlockSpec((tc, d), lambda i, dr: (i, 0)),
                      pl.BlockSpec((tc, LANES), lambda i, dr: (i, 0)),
                      pl.BlockSpec((None, 1, d), lambda i, dr: (i // tpb, 0, 0))],
            out_specs=pl.BlockSpec((tc, d), lambda i, dr: (i, 0)),
            scratch_shapes=[pltpu.VMEM((2, TOP_K, tc * TILE_ROW, LANES), jnp.uint32),
                            pltpu.SemaphoreType.DMA((2,))]),
        compiler_params=pltpu.CompilerParams(
            dimension_semantics=("arbitrary",), vmem_limit_bytes=VMEM_LIMIT),
        name="moe_combine",
    )(dest_flat, ys, x1, route, g2)


def _rope_tables(seq):
    inv_freq = (ROPE_THETA ** (-np.arange(0, HEAD_DIM, 2, dtype=np.float32) / HEAD_DIM)).astype(np.float32)
    ang = (inv_freq[:, None] * np.arange(seq, dtype=np.float32)[None, :]).astype(np.float32)
    cos, sin = np.cos(ang.astype(np.float64)), np.sin(ang.astype(np.float64))
    cos_t = np.concatenate([cos, cos], axis=0).astype(np.float32)
    sin_t = np.concatenate([-sin, sin], axis=0).astype(np.float32)
    return jnp.asarray(cos_t), jnp.asarray(sin_t)


def _layer(x, c, l, lambda_init, w_ada, b_ada, norm1_w, w_in, q_norm_w, k_norm_w, lambda_q1,
           lambda_k1, lambda_q2, lambda_k2, subln_w, w_attn_o, conv_w, w_conv_o, w_out, norm2_w,
           w_router, b_router, w_gate_up, b_gate_up, w_down, b_down):
    bsz, seq, d = x.shape
    n = bsz * seq
    x2 = x.reshape(n, d)

    mod = _ada_mod(c, w_ada[l], b_ada[l])
    sh1, sc1, g1, sh2, sc2, g2 = [m.reshape(bsz, 1, d) for m in jnp.split(mod, 6, axis=-1)]

    cos_t, sin_t = _rope_tables(seq)
    qt, k, vt, cb, z, sga, sgc, qn, kn = _in_proj(
        x2, norm1_w[l].reshape(1, d), sc1, sh1, w_in[l].astype(BF16),
        q_norm_w[l].reshape(HEAD_DIM, 1) * Q_SCALE, k_norm_w[l].reshape(HEAD_DIM, 1),
        cos_t, sin_t, bsz, seq)

    lam_vecs = jnp.stack([lambda_q1[l], lambda_k1[l], lambda_q2[l], lambda_k2[l]]).astype(F32)
    o = _diff_attn(qt, k.reshape(bsz, seq, d), vt, qn, kn, lam_vecs, subln_w[l].reshape(V_DIM, 1),
                   lambda_init)

    pad = LANES - N_EXPERTS
    wr = jnp.pad(w_router[l].astype(F32), ((0, 0), (0, pad)))
    wr_hi = wr.astype(BF16)
    wr_split = jnp.concatenate([wr_hi, (wr - wr_hi.astype(F32)).astype(BF16)], axis=1)
    br_pad = jnp.pad(b_router[l].astype(F32), (0, pad), constant_values=NEG_BIG).reshape(1, LANES)
    x1, h2, route, cnt = _post_mix(
        o.reshape(n, d), cb, z, sga, sgc, x2, g1, sc2, sh2, norm2_w[l].reshape(1, d), conv_w[l],
        w_attn_o[l].astype(BF16), w_conv_o[l].astype(BF16), w_out[l].astype(BF16),
        wr_split, br_pad, seq)

    counts = cnt[0, :N_EXPERTS].astype(jnp.int32)
    padded = ((counts + MOE_ROWS - 1) // MOE_ROWS) * MOE_ROWS
    ends = jnp.cumsum(padded)
    start = ends - padded
    p_rows = n * TOP_K + N_EXPERTS * MOE_ROWS
    top_e = route[:, TOP_K:2 * TOP_K].astype(jnp.int32)
    rank = route[:, 2 * TOP_K:3 * TOP_K].astype(jnp.int32)
    experts = jnp.arange(N_EXPERTS, dtype=jnp.int32)
    seg_start = jnp.sum(jnp.where(top_e[..., None] == experts, start, 0), axis=-1)
    dest = (seg_start + rank).reshape(-1)
    nonempty = padded > 0
    later = jnp.logical_and(nonempty[None, :], experts[None, :] > experts[:, None])
    next_e = jnp.min(jnp.where(later, experts[None, :], N_EXPERTS), axis=1)
    next_e = jnp.where(next_e == N_EXPERTS, -1, next_e).astype(jnp.int32)
    slot = ((jnp.cumsum(nonempty.astype(jnp.int32)) - 1) % 2).astype(jnp.int32)
    first_blk = (start // MOE_ROWS).astype(jnp.int32)
    n_blk = (padded // MOE_ROWS).astype(jnp.int32)
    misc = jnp.stack([jnp.argmax(nonempty).astype(jnp.int32), (ends[-1] // MOE_ROWS).astype(jnp.int32)])

    xs = _moe_dispatch(dest, ends.astype(jnp.int32), h2, n, p_rows)
    ys = _moe_ffn(first_blk, n_blk, slot, next_e, misc, xs, w_gate_up[l], w_down[l],
                  b_gate_up[l][:, None, 0::2], b_gate_up[l][:, None, 1::2], b_down[l][:, None, :])
    out = _moe_combine(dest, ys, x1, route, g2, seq)
    return out.reshape(bsz, seq, d)


def kernel(x, c, w_ada, b_ada, norm1_w, w_in, q_norm_w, k_norm_w, lambda_q1, lambda_k1, lambda_q2,
           lambda_k2, subln_w, w_attn_o, conv_w, w_conv_o, w_out, norm2_w, w_router, b_router,
           w_gate_up, b_gate_up, w_down, b_down):
    depth = w_ada.shape[0]
    for l in range(depth):
        lambda_init = 0.8 - 0.6 * math.exp(-0.3 * l)
        x = _layer(x, c, l, lambda_init, w_ada, b_ada, norm1_w, w_in, q_norm_w, k_norm_w,
                   lambda_q1, lambda_k1, lambda_q2, lambda_k2, subln_w, w_attn_o, conv_w,
                   w_conv_o, w_out, norm2_w, w_router, b_router, w_gate_up, b_gate_up,
                   w_down, b_down)
    return x
```
